```python
import math
import jax
import jax.numpy as jnp
from jax import lax
import numpy as np

D_MODEL = 1024
BATCH = 4
SEQ = 4096
DEPTH = 2
DEC_BATCH = 32
DEC_SEQ = 1
PAST_LEN = 8192
PAGE_SIZE = 128

HEAD_DIM = 64
H_RET = 8
H_SB = 8
RET_WIDTH = H_RET * HEAD_DIM
SB_WIDTH = H_SB * HEAD_DIM
IN0_WIDTH = 4 * RET_WIDTH + 4 * SB_WIDTH
RET_CHUNK = 128
ROPE_BASE = 10000.0
Q_BLOCK = 128
H_NSA = 16
G_NSA = 2
HPG = H_NSA // G_NSA
NSA_WIDTH = H_NSA * HEAD_DIM
KV_WIDTH = G_NSA * HEAD_DIM
IN1_SIZES = (NSA_WIDTH, KV_WIDTH, KV_WIDTH, KV_WIDTH, KV_WIDTH, KV_WIDTH, KV_WIDTH, NSA_WIDTH, 3 * H_NSA)
IN1_WIDTH = NSA_WIDTH * 2 + KV_WIDTH * 6 + 3 * H_NSA
CMP_BLOCK = 32
SEL_BLOCK = 64
TOP_N = 16
WINDOW = 512
FORCE_BONUS = 1000.0
NEG_INF = -1e30
EPS = 1e-6

kernel_name = 'hybrid_retention_stickbreak_nsa_step'


def _rmsnorm(x, g):
    xf = x.astype(jnp.float32)
    y = xf * lax.rsqrt(jnp.mean(xf * xf, axis=-1, keepdims=True) + EPS)
    return (y * g.astype(jnp.float32)).astype(x.dtype)


def _head_groupnorm(o, g):
    mu = jnp.mean(o, axis=-1, keepdims=True)
    oc = o - mu
    return oc * lax.rsqrt(jnp.mean(oc * oc, axis=-1, keepdims=True) + EPS) * g.astype(jnp.float32)


def _rope(x, pos):
    half = x.shape[-1] // 2
    inv = ROPE_BASE ** (-jnp.arange(half, dtype=jnp.float32) / half)
    ang = pos.astype(jnp.float32)[:, None] * inv[None, :]
    cos = jnp.cos(ang)[None, :, None, :]
    sin = jnp.sin(ang)[None, :, None, :]
    xf = x.astype(jnp.float32)
    x1, x2 = xf[..., :half], xf[..., half:]
    return jnp.concatenate([x1 * cos - x2 * sin, x1 * sin + x2 * cos], axis=-1).astype(x.dtype)


def _split_cols(z, sizes):
    cuts = [int(c) for c in np.cumsum(sizes)[:-1]]
    return jnp.split(z, cuts, axis=-1)


def _block_size(l):
    return Q_BLOCK if l % Q_BLOCK == 0 else l


def _gather_pages(pool, page_table):
    g = pool[page_table]
    return g.reshape(page_table.shape[0], page_table.shape[1] * pool.shape[1], pool.shape[2], pool.shape[3])


def _retention(q, k, v, s0):
    b, l, h, d = q.shape
    f32 = jnp.float32
    c = RET_CHUNK if l % RET_CHUNK == 0 else l
    n = l // c
    lg = jnp.log1p(-jnp.exp2(-5.0 - jnp.arange(h, dtype=f32)))
    idx = jnp.arange(c, dtype=f32)
    diff = idx[:, None] - idx[None, :]
    inner = jnp.where(diff >= 0, jnp.exp(lg[:, None, None] * jnp.maximum(diff, 0.0)), 0.0)
    q_dec = jnp.exp(lg[:, None] * (idx + 1.0))[None, :, :, None]
    k_dec = jnp.exp(lg[:, None] * (c - 1.0 - idx))[None, :, :, None]
    c_dec = jnp.exp(lg * c)[None, :, None, None]

    def chunks(t):
        return t.astype(f32).reshape(b, n, c, h, d).transpose(1, 0, 3, 2, 4)

    def step(s, inp):
        qi, ki, vi = inp
        att = jnp.einsum('bhnd,bhmd->bhnm', qi, ki) * inner
        o = jnp.einsum('bhnm,bhme->bhne', att, vi) + jnp.einsum('bhnd,bhde->bhne', qi, s) * q_dec
        s = s * c_dec + jnp.einsum('bhmd,bhme->bhde', ki * k_dec, vi)
        return s, o

    s, o = lax.scan(step, s0.astype(f32), (chunks(q), chunks(k) * d ** -0.5, chunks(v)))
    return o.transpose(1, 0, 3, 2, 4).reshape(b, l, h, d), s


def _sb_attention(q, k, v, q_pos, k_pos):
    b, lq, h, d = q.shape
    qb = _block_size(lq)
    nb = lq // qb
    qs = q.reshape(b, nb, qb, h, d).swapaxes(0, 1)
    ps = q_pos.reshape(nb, qb)

    def blk(args):
        qi, pi = args
        z = jnp.einsum('bqhd,bkhd->bhqk', qi, k).astype(jnp.float32) * d ** -0.5
        mask = k_pos[None, :] < pi[:, None]
        lm = jnp.where(mask, jax.nn.log_sigmoid(-z), 0.0)
        la = jax.nn.log_sigmoid(z) + lax.cumsum(lm, axis=3, reverse=True) - lm
        a = jnp.where(mask, jnp.exp(la), 0.0)
        return jnp.einsum('bhqk,bkhd->bqhd', a.astype(v.dtype), v)

    o = lax.map(blk, (qs, ps))
    return o.swapaxes(0, 1).reshape(b, lq, h, d)


def _compress(t, pe, w):
    b, lk, g, d = t.shape
    nb = lk // CMP_BLOCK
    tb = t[:, :nb * CMP_BLOCK].reshape(b, nb, CMP_BLOCK, g, d) + pe[None, None, :, None, :]
    return jnp.einsum('bnlgd,lde->bnge', tb, w)


def _nsa_block(qg, gates, q_pos, cmp_k, cmp_v, ksb, vsb, kw, vw, kw_pos):
    b, lq, g, hp, d = qg.shape
    scale = d ** -0.5
    nb = cmp_k.shape[1]
    nsb = ksb.shape[2]
    s = jnp.einsum('bqghd,bngd->bghqn', qg, cmp_k).astype(jnp.float32) * scale
    cmask = (jnp.arange(nb) * CMP_BLOCK + CMP_BLOCK - 1)[None, :] <= q_pos[:, None]
    p = jax.nn.softmax(jnp.where(cmask, s, NEG_INF), axis=-1) * cmask
    o_cmp = jnp.einsum('bghqn,bngd->bqghd', p.astype(cmp_v.dtype), cmp_v)
    ratio = SEL_BLOCK // CMP_BLOCK
    imp = jnp.pad(p.sum(axis=2), ((0, 0), (0, 0), (0, 0), (0, nsb * ratio - nb)))
    imp = imp.reshape(b, g, lq, nsb, ratio).sum(axis=-1)
    cur = q_pos // SEL_BLOCK
    blk = jnp.arange(nsb)[None, :]
    forced = (blk == 0) | (blk == cur[:, None]) | (blk == cur[:, None] - 1)
    valid = blk <= cur[:, None]
    score = jnp.where(valid, imp + FORCE_BONUS * forced, NEG_INF)
    n_sel = min(TOP_N, nsb)
    _, idx = lax.top_k(score, n_sel)
    bi = jnp.arange(b)[:, None, None, None]
    gi = jnp.arange(g)[None, :, None, None]
    ksel = ksb[bi, gi, idx].reshape(b, g, lq, n_sel * SEL_BLOCK, d)
    vsel = vsb[bi, gi, idx].reshape(b, g, lq, n_sel * SEL_BLOCK, d)
    kpos = (idx[..., None] * SEL_BLOCK + jnp.arange(SEL_BLOCK)).reshape(b, g, lq, n_sel * SEL_BLOCK)
    smask = (kpos <= q_pos[None, None, :, None])[:, :, None]
    ss = jnp.einsum('bqghd,bgqkd->bghqk', qg, ksel).astype(jnp.float32) * scale
    ps = jax.nn.softmax(jnp.where(smask, ss, NEG_INF), axis=-1)
    o_sel = jnp.einsum('bghqk,bgqkd->bqghd', ps.astype(vsel.dtype), vsel)
    sw = jnp.einsum('bqghd,bkgd->bghqk', qg, kw).astype(jnp.float32) * scale
    wmask = (kw_pos[None, :] <= q_pos[:, None]) & (kw_pos[None, :] > q_pos[:, None] - WINDOW) & (kw_pos[None, :] >= 0)
    pw = jax.nn.softmax(jnp.where(wmask, sw, NEG_INF), axis=-1)
    o_win = jnp.einsum('bghqk,bkgd->bqghd', pw.astype(vw.dtype), vw)
    return gates[..., 0:1] * o_cmp + gates[..., 1:2] * o_sel + gates[..., 2:3] * o_win


def _layer0(x, pos, s0, sb_past, norm_g, w_in, ret_gn_g, sb_qn_g, sb_kn_g, w_out):
    b, l, _ = x.shape
    z = _rmsnorm(x, norm_g) @ w_in
    rq, rk, rv, rg, sq, sk, sv, sg = jnp.split(z, 8, axis=-1)
    rq = _rope(rq.reshape(b, l, H_RET, HEAD_DIM), pos)
    rk = _rope(rk.reshape(b, l, H_RET, HEAD_DIM), pos)
    o_ret, s_new = _retention(rq, rk, rv.reshape(b, l, H_RET, HEAD_DIM), s0)
    o_ret = _head_groupnorm(o_ret, ret_gn_g).astype(x.dtype).reshape(b, l, RET_WIDTH)
    sq = _rmsnorm(sq.reshape(b, l, H_SB, HEAD_DIM), sb_qn_g)
    sk = _rmsnorm(sk.reshape(b, l, H_SB, HEAD_DIM), sb_kn_g)
    sv = sv.reshape(b, l, H_SB, HEAD_DIM)
    if sb_past is None:
        k_all, v_all = sk, sv
    else:
        k_all = jnp.concatenate([sb_past[0].astype(sk.dtype), sk], axis=1)
        v_all = jnp.concatenate([sb_past[1].astype(sv.dtype), sv], axis=1)
    k_pos = jnp.arange(k_all.shape[1], dtype=jnp.int32)
    o_sb = _sb_attention(sq, k_all, v_all, pos, k_pos).reshape(b, l, SB_WIDTH)
    mix = jnp.concatenate([o_ret * jax.nn.silu(rg), o_sb * jax.nn.silu(sg)], axis=-1)
    return x + mix @ w_out, s_new.astype(s0.dtype), sk, sv


def _layer1(x, pos, past, norm_g, w_in, qn_g, cmp_kn_g, sel_kn_g, win_kn_g, cmp_pe, w_cmp_k, w_cmp_v, w_out):
    b, l, _ = x.shape
    z = _rmsnorm(x, norm_g) @ w_in
    q, kc, vc, ks, vs, kw, vw, gate, bgate = _split_cols(z, IN1_SIZES)
    q = _rmsnorm(q.reshape(b, l, H_NSA, HEAD_DIM), qn_g).reshape(b, l, G_NSA, HPG, HEAD_DIM)
    kc = kc.reshape(b, l, G_NSA, HEAD_DIM)
    vc = vc.reshape(b, l, G_NSA, HEAD_DIM)
    ks = _rmsnorm(ks.reshape(b, l, G_NSA, HEAD_DIM), sel_kn_g)
    vs = vs.reshape(b, l, G_NSA, HEAD_DIM)
    kw = _rmsnorm(kw.reshape(b, l, G_NSA, HEAD_DIM), win_kn_g)
    vw = vw.reshape(b, l, G_NSA, HEAD_DIM)
    gates = jax.nn.sigmoid(bgate.astype(jnp.float32)).reshape(b, l, G_NSA, HPG, 3).astype(x.dtype)
    if past is None:
        kc_all, vc_all, ks_all, vs_all = kc, vc, ks, vs
    else:
        pkc, pvc, pks, pvs, wk_buf, wv_buf = past
        kc_all = jnp.concatenate([pkc.astype(kc.dtype), kc], axis=1)
        vc_all = jnp.concatenate([pvc.astype(vc.dtype), vc], axis=1)
        ks_all = jnp.concatenate([pks.astype(ks.dtype), ks], axis=1)
        vs_all = jnp.concatenate([pvs.astype(vs.dtype), vs], axis=1)
    lk = kc_all.shape[1]
    cmp_k = _rmsnorm(_compress(kc_all, cmp_pe, w_cmp_k), cmp_kn_g)
    cmp_v = _compress(vc_all, cmp_pe, w_cmp_v)
    nsb = -(-lk // SEL_BLOCK)

    def to_blocks(t):
        t = jnp.pad(t, ((0, 0), (0, nsb * SEL_BLOCK - lk), (0, 0), (0, 0)))
        return t.reshape(b, nsb, SEL_BLOCK, G_NSA, HEAD_DIM).transpose(0, 3, 1, 2, 4)

    ksb, vsb = to_blocks(ks_all), to_blocks(vs_all)
    if past is None:
        qb = _block_size(l)
        nqb = l // qb
        kw_pad = jnp.pad(kw, ((0, 0), (WINDOW, 0), (0, 0), (0, 0)))
        vw_pad = jnp.pad(vw, ((0, 0), (WINDOW, 0), (0, 0), (0, 0)))

        def body(args):
            qi, gi, st = args
            kwi = lax.dynamic_slice_in_dim(kw_pad, st, WINDOW + qb, axis=1)
            vwi = lax.dynamic_slice_in_dim(vw_pad, st, WINDOW + qb, axis=1)
            kwp = st - WINDOW + jnp.arange(WINDOW + qb, dtype=jnp.int32)
            return _nsa_block(qi, gi, st + jnp.arange(qb, dtype=jnp.int32), cmp_k, cmp_v, ksb, vsb, kwi, vwi, kwp)

        def qblocks(t):
            return t.reshape((b, nqb, qb) + t.shape[2:]).swapaxes(0, 1)

        o = lax.map(body, (qblocks(q), qblocks(gates), jnp.arange(nqb, dtype=jnp.int32) * qb))
        o = o.swapaxes(0, 1).reshape(b, l, NSA_WIDTH)
        keep = min(WINDOW, l)
        win_k_new, win_v_new = kw[:, l - keep:], vw[:, l - keep:]
    else:
        wbuf = wk_buf.shape[1]
        wk_all = jnp.concatenate([wk_buf.astype(kw.dtype), kw], axis=1)
        wv_all = jnp.concatenate([wv_buf.astype(vw.dtype), vw], axis=1)
        wpos = PAST_LEN - wbuf + jnp.arange(wk_all.shape[1], dtype=jnp.int32)
        o = _nsa_block(q, gates, pos, cmp_k, cmp_v, ksb, vsb, wk_all, wv_all, wpos).reshape(b, l, NSA_WIDTH)
        win_k_new, win_v_new = wk_all[:, l:], wv_all[:, l:]
    y = x + (o * jax.nn.silu(gate)) @ w_out
    return y, kc, vc, ks, vs, win_k_new, win_v_new


def setup_inputs(seed: int = 0) -> dict:
    key = jax.random.key(seed)
    ks = jax.random.split(key, 40)
    f32 = jnp.float32
    n_pages = PAST_LEN // PAGE_SIZE
    n_used = DEC_BATCH * n_pages
    n_phys = n_used + n_used // 4
    w_buf = min(WINDOW, PAST_LEN)

    def nrm(k, shape, s=1.0):
        return s * jax.random.normal(k, shape, f32)

    def gain(k, shape):
        return 1.0 + 0.02 * jax.random.normal(k, shape, f32)

    page_table = jax.random.permutation(ks[0], n_phys)[:n_used].reshape(DEC_BATCH, n_pages).astype(jnp.int32)
    return {
        'x_prompt': nrm(ks[1], (BATCH, SEQ, D_MODEL)),
        'x_sample': nrm(ks[2], (DEC_BATCH, DEC_SEQ, D_MODEL)),
        'state_ret': nrm(ks[3], (DEC_BATCH, H_RET, HEAD_DIM, HEAD_DIM), 0.1),
        'cache_sb_k': nrm(ks[4], (n_phys, PAGE_SIZE, H_SB, HEAD_DIM)),
        'cache_sb_v': nrm(ks[5], (n_phys, PAGE_SIZE, H_SB, HEAD_DIM)),
        'cache_cmp_k': nrm(ks[6], (n_phys, PAGE_SIZE, G_NSA, HEAD_DIM)),
        'cache_cmp_v': nrm(ks[7], (n_phys, PAGE_SIZE, G_NSA, HEAD_DIM)),
        'cache_sel_k': nrm(ks[8], (n_phys, PAGE_SIZE, G_NSA, HEAD_DIM)),
        'cache_sel_v': nrm(ks[9], (n_phys, PAGE_SIZE, G_NSA, HEAD_DIM)),
        'cache_win_k': nrm(ks[10], (DEC_BATCH, w_buf, G_NSA, HEAD_DIM)),
        'cache_win_v': nrm(ks[11], (DEC_BATCH, w_buf, G_NSA, HEAD_DIM)),
        'page_table': page_table,
        'norm0_g': gain(ks[12], (D_MODEL,)),
        'w_in0': nrm(ks[13], (D_MODEL, IN0_WIDTH), D_MODEL ** -0.5),
        'ret_gn_g': gain(ks[14], (H_RET, HEAD_DIM)),
        'sb_qn_g': gain(ks[15], (HEAD_DIM,)),
        'sb_kn_g': gain(ks[16], (HEAD_DIM,)),
        'w_out0': nrm(ks[17], (RET_WIDTH + SB_WIDTH, D_MODEL), (RET_WIDTH + SB_WIDTH) ** -0.5),
        'norm1_g': gain(ks[18], (D_MODEL,)),
        'w_in1': nrm(ks[19], (D_MODEL, IN1_WIDTH), D_MODEL ** -0.5),
        'nsa_qn_g': gain(ks[20], (HEAD_DIM,)),
        'cmp_kn_g': gain(ks[21], (HEAD_DIM,)),
        'sel_kn_g': gain(ks[22], (HEAD_DIM,)),
        'win_kn_g': gain(ks[23], (HEAD_DIM,)),
        'cmp_pe': nrm(ks[24], (CMP_BLOCK, HEAD_DIM), 0.1),
        'w_cmp_k': nrm(ks[25], (CMP_BLOCK, HEAD_DIM, HEAD_DIM), (CMP_BLOCK * HEAD_DIM) ** -0.5),
        'w_cmp_v': nrm(ks[26], (CMP_BLOCK, HEAD_DIM, HEAD_DIM), (CMP_BLOCK * HEAD_DIM) ** -0.5),
        'w_out1': nrm(ks[27], (NSA_WIDTH, D_MODEL), NSA_WIDTH ** -0.5),
    }


def reference(x_prompt, x_sample, state_ret, cache_sb_k, cache_sb_v, cache_cmp_k, cache_cmp_v,
              cache_sel_k, cache_sel_v, cache_win_k, cache_win_v, page_table,
              norm0_g, w_in0, ret_gn_g, sb_qn_g, sb_kn_g, w_out0,
              norm1_g, w_in1, nsa_qn_g, cmp_kn_g, sel_kn_g, win_kn_g, cmp_pe, w_cmp_k, w_cmp_v, w_out1):
    yp, ys = x_prompt, x_sample
    pos_p = jnp.arange(x_prompt.shape[1], dtype=jnp.int32)
    pos_s = PAST_LEN + jnp.arange(x_sample.shape[1], dtype=jnp.int32)
    for layer in range(DEPTH):
        if layer % 2 == 0:
            s0 = jnp.zeros((x_prompt.shape[0], H_RET, HEAD_DIM, HEAD_DIM), state_ret.dtype)
            yp, ret_p, sbk_p, sbv_p = _layer0(yp, pos_p, s0, None, norm0_g, w_in0, ret_gn_g, sb_qn_g, sb_kn_g, w_out0)
            sb_past = (_gather_pages(cache_sb_k, page_table), _gather_pages(cache_sb_v, page_table))
            ys, ret_s, sbk_s, sbv_s = _layer0(ys, pos_s, state_ret, sb_past, norm0_g, w_in0, ret_gn_g, sb_qn_g, sb_kn_g, w_out0)
        else:
            yp, ckp, cvp, skp, svp, wkp, wvp = _layer1(yp, pos_p, None, norm1_g, w_in1, nsa_qn_g, cmp_kn_g, sel_kn_g,
                                                     win_kn_g, cmp_pe, w_cmp_k, w_cmp_v, w_out1)
            nsa_past = (_gather_pages(cache_cmp_k, page_table), _gather_pages(cache_cmp_v, page_table),
                        _gather_pages(cache_sel_k, page_table), _gather_pages(cache_sel_v, page_table),
                        cache_win_k, cache_win_v)
            ys, cks, cvs, sks, svs, wks, wvs = _layer1(ys, pos_s, nsa_past, norm1_g, w_in1, nsa_qn_g, cmp_kn_g, sel_kn_g,
                                                     win_kn_g, cmp_pe, w_cmp_k, w_cmp_v, w_out1)
    return (yp, ys, ret_p, ret_s, sbk_p, sbv_p, sbk_s, sbv_s, ckp, cvp, skp, svp, wkp, wvp, cks, cvs, sks, svs, wks, wvs)
```

```python
import functools

import jax
import jax.numpy as jnp
import numpy as np
from jax import lax
from jax.experimental import pallas as pl
from jax.experimental.pallas import tpu as pltpu

F32 = jnp.float32
BF16 = jnp.bfloat16
I32 = jnp.int32

HEAD_DIM = 64
LANES = 128
H_RET = 8
H_SB = 8
RET_CHUNK = 128
ROPE_BASE = 10000.0
Q_BLOCK = 128
H_NSA = 16
G_NSA = 2
HPG = H_NSA // G_NSA
CMP_BLOCK = 32
SEL_BLOCK = 64
TOP_N = 16
WINDOW = 512
FORCE_BONUS = 1000.0
NEG_INF = -1e30
EPS = 1e-6
PAGE_SIZE = 128
SCALE = HEAD_DIM ** -0.5
VMEM_LIMIT = 56 * 1024 * 1024


def _dot(a, b):
    return jnp.dot(a, b, preferred_element_type=F32)


def _dot_nt(a, b):
    return lax.dot_general(a, b, (((1,), (1,)), ((), ())), preferred_element_type=F32)


def _dot_tn(a, b):
    return lax.dot_general(a, b, (((0,), (0,)), ((), ())), preferred_element_type=F32)


def _split_bf16(x):
    hi = x.astype(BF16)
    lo = (x - hi.astype(F32)).astype(BF16)
    return hi, lo


def _sigmoid(x):
    return 1.0 / (1.0 + jnp.exp(-x))


def _silu(x):
    return x * _sigmoid(x)


def _lane_lo(shape):
    return lax.broadcasted_iota(I32, shape, len(shape) - 1) < HEAD_DIM


def _pair_sum(x, lo):
    s_lo = jnp.sum(jnp.where(lo, x, 0.0), axis=-1, keepdims=True)
    s_hi = jnp.sum(jnp.where(lo, 0.0, x), axis=-1, keepdims=True)
    return jnp.where(lo, s_lo, s_hi)


def _pair_rmsnorm(x, g, lo):
    ms = _pair_sum(x * x, lo) * (1.0 / HEAD_DIM)
    return x * lax.rsqrt(ms + EPS) * g


def _rope_pair(x, cos, sin_signed):
    lane = lax.broadcasted_iota(I32, x.shape, 1)
    first = (lane & 32) == 0
    partner = jnp.where(first, pltpu.roll(x, 96, 1), pltpu.roll(x, 32, 1))
    return x * cos + partner * sin_signed


def _params(sem):
    return pltpu.CompilerParams(dimension_semantics=sem, vmem_limit_bytes=VMEM_LIMIT)


def _inproj0_kernel(x_ref, g_ref, w_ref, cos_ref, sin_ref, qg_ref, kg_ref,
                    ret_ref, sbq_ref, sbk_ref, sbv_ref, sbg_ref):
    x = x_ref[...]
    xn = x * lax.rsqrt(jnp.mean(x * x, axis=-1, keepdims=True) + EPS) * g_ref[...]
    xb = xn.astype(BF16)
    cos = cos_ref[...]
    sin = sin_ref[...]
    lo = _lane_lo((1, LANES))
    width = H_RET * HEAD_DIM

    def group(i):
        return _dot(xb, w_ref[:, i * width:(i + 1) * width])

    def tiles(a):
        return [a[:, t * LANES:(t + 1) * LANES] for t in range(width // LANES)]

    for t, v in enumerate(tiles(group(0))):
        ret_ref[:, t * LANES:(t + 1) * LANES] = _rope_pair(v, cos, sin)
    for t, v in enumerate(tiles(group(1))):
        ret_ref[:, width + t * LANES:width + (t + 1) * LANES] = _rope_pair(v, cos, sin) * SCALE
    ret_ref[:, 2 * width:3 * width] = group(2)
    ret_ref[:, 3 * width:4 * width] = group(3)
    for t, v in enumerate(tiles(group(4))):
        sbq_ref[:, t * LANES:(t + 1) * LANES] = (_pair_rmsnorm(v, qg_ref[...], lo) * SCALE).astype(BF16)
    for t, v in enumerate(tiles(group(5))):
        sbk_ref[:, t * LANES:(t + 1) * LANES] = _pair_rmsnorm(v, kg_ref[...], lo)
    sbv_ref[...] = group(6)
    sbg_ref[...] = _silu(group(7))


def _inproj0(x2d, norm_g, w_bf, cos, sin, qg, kg, tm, n_pos_blocks):
    rows, d = x2d.shape
    width = H_RET * HEAD_DIM
    n = w_bf.shape[1]
    grid = (rows // tm,)
    row = lambda i: (i, 0)
    fixed = lambda i: (0, 0)
    pos = lambda i: (i % n_pos_blocks, 0)
    return pl.pallas_call(
        _inproj0_kernel,
        grid=grid,
        in_specs=[
            pl.BlockSpec((tm, d), row),
            pl.BlockSpec((1, d), fixed),
            pl.BlockSpec((d, n), fixed),
            pl.BlockSpec((tm, LANES), pos),
            pl.BlockSpec((tm, LANES), pos),
            pl.BlockSpec((1, LANES), fixed),
            pl.BlockSpec((1, LANES), fixed),
        ],
        out_specs=[
            pl.BlockSpec((tm, 4 * width), row),
            pl.BlockSpec((tm, width), row),
            pl.BlockSpec((tm, width), row),
            pl.BlockSpec((tm, width), row),
            pl.BlockSpec((tm, width), row),
        ],
        out_shape=[
            jax.ShapeDtypeStruct((rows, 4 * width), F32),
            jax.ShapeDtypeStruct((rows, width), BF16),
            jax.ShapeDtypeStruct((rows, width), F32),
            jax.ShapeDtypeStruct((rows, width), F32),
            jax.ShapeDtypeStruct((rows, width), F32),
        ],
        compiler_params=_params(("parallel",)),
        name="inproj0",
    )(x2d, norm_g.reshape(1, d), w_bf, cos, sin, qg, kg)


def _ret_prompt_kernel(q_ref, k_ref, v_ref, g_ref, inner_ref, qdec_ref, kdec_ref, cdec_ref, gn_ref,
                       mix_ref, state_ref, s_scr):
    c = pl.program_id(2)

    @pl.when(c == 0)
    def _():
        s_scr[...] = jnp.zeros_like(s_scr)

    q = q_ref[...]
    k = k_ref[...]
    vb = v_ref[...].astype(BF16)
    qb = q.astype(BF16)
    kb = k.astype(BF16)
    lo = _lane_lo(q.shape)
    zero = jnp.zeros_like(qb)
    att_lo = _dot_nt(jnp.where(lo, qb, zero), kb) * inner_ref[0]
    att_hi = _dot_nt(jnp.where(lo, zero, qb), kb) * inner_ref[1]
    o = jnp.where(lo, _dot(att_lo.astype(BF16), vb), _dot(att_hi.astype(BF16), vb))
    s = s_scr[...]
    o = o + _dot(qb, s.astype(BF16)) * qdec_ref[0]
    kd = (k * kdec_ref[0]).astype(BF16)
    ri = lax.broadcasted_iota(I32, s.shape, 0) < HEAD_DIM
    ci = lax.broadcasted_iota(I32, s.shape, 1) < HEAD_DIM
    s_new = s * cdec_ref[0] + jnp.where(ri == ci, _dot_tn(kd, vb), 0.0)
    s_scr[...] = s_new
    state_ref[0, 0] = s_new
    mu = _pair_sum(o, lo) * (1.0 / HEAD_DIM)
    oc = o - mu
    var = _pair_sum(oc * oc, lo) * (1.0 / HEAD_DIM)
    on = oc * lax.rsqrt(var + EPS) * gn_ref[0]
    mix_ref[...] = on * _silu(g_ref[...])


def _ret_tables(chunk):
    h = jnp.arange(H_RET, dtype=F32)
    lg = jnp.log1p(-jnp.exp2(-5.0 - h))
    idx = jnp.arange(chunk, dtype=F32)
    diff = idx[:, None] - idx[None, :]
    inner = jnp.where(diff >= 0, jnp.exp(lg[:, None, None] * jnp.maximum(diff, 0.0)), 0.0)
    q_dec = jnp.exp(lg[:, None] * (idx + 1.0))
    k_dec = jnp.exp(lg[:, None] * (chunk - 1.0 - idx))
    c_dec = jnp.exp(lg * chunk)

    def lanes(t):
        t = jnp.repeat(t[:, :, None], HEAD_DIM, axis=2)
        return jnp.concatenate([t[0::2], t[1::2]], axis=2)

    cd = jnp.repeat(c_dec[:, None, None], HEAD_DIM, axis=2)
    cd = jnp.concatenate([cd[0::2], cd[1::2]], axis=2)
    return inner, lanes(q_dec), lanes(k_dec), cd


def _ret_prompt(ret, gn_g, batch, seq):
    width = H_RET * HEAD_DIM
    npair = H_RET // 2
    c = RET_CHUNK
    nc = seq // c
    inner, qdec, kdec, cdec = _ret_tables(c)
    gn = gn_g.reshape(npair, 1, LANES)

    def col(off):
        return lambda b, p, i: (b * nc + i, off + p)

    tab = lambda b, p, i: (p, 0, 0)
    return pl.pallas_call(
        _ret_prompt_kernel,
        grid=(batch, npair, nc),
        in_specs=[
            pl.BlockSpec((c, LANES), col(0)),
            pl.BlockSpec((c, LANES), col(npair)),
            pl.BlockSpec((c, LANES), col(2 * npair)),
            pl.BlockSpec((c, LANES), col(3 * npair)),
            pl.BlockSpec((2, c, c), tab),
            pl.BlockSpec((1, c, LANES), tab),
            pl.BlockSpec((1, c, LANES), tab),
            pl.BlockSpec((1, 1, LANES), tab),
            pl.BlockSpec((1, 1, LANES), tab),
        ],
        out_specs=[
            pl.BlockSpec((c, LANES), lambda b, p, i: (b * nc + i, p)),
            pl.BlockSpec((1, 1, LANES, LANES), lambda b, p, i: (b, p, 0, 0)),
        ],
        out_shape=[
            jax.ShapeDtypeStruct((batch * seq, width), F32),
            jax.ShapeDtypeStruct((batch, npair, LANES, LANES), F32),
        ],
        scratch_shapes=[pltpu.VMEM((LANES, LANES), F32)],
        compiler_params=_params(("parallel", "parallel", "arbitrary")),
        name="retention_prompt",
    )(ret, ret, ret, ret, inner, qdec, kdec, cdec, gn)


def _unpair_state(sp):
    a = sp[:, :, :HEAD_DIM, :HEAD_DIM]
    b = sp[:, :, HEAD_DIM:, HEAD_DIM:]
    return jnp.stack([a, b], axis=2).reshape(sp.shape[0], -1, HEAD_DIM, HEAD_DIM)


def _sb_tile(qm, kj, vj, r, acc, upper, diag_mask):
    z = _dot_nt(qm, kj)
    l1p = jnp.log1p(jnp.exp(-jnp.abs(z)))
    ls_pos = jnp.minimum(z, 0.0) - l1p
    lm = ls_pos - z
    if diag_mask is not None:
        lm = jnp.where(diag_mask, lm, 0.0)
    hi, lo_part = _split_bf16(lm)
    s_incl = _dot(hi, upper) + _dot(lo_part, upper)
    a = jnp.exp(ls_pos + (s_incl - lm) + r)
    if diag_mask is not None:
        a = jnp.where(diag_mask, a, 0.0)
    acc = acc + _dot(a.astype(BF16), vj)
    r = r + s_incl[:, 0:1]
    return r, acc


def _sb_prompt_kernel(q_ref, k_ref, v_ref, g_ref, o_ref):
    i = pl.program_id(2)
    tq = q_ref.shape[0]
    q = q_ref[...]
    lo = _lane_lo(q.shape)
    zero = jnp.zeros_like(q)
    q_lo = jnp.where(lo, q, zero)
    q_hi = jnp.where(lo, zero, q)
    rr = lax.broadcasted_iota(I32, (tq, tq), 0)
    cc = lax.broadcasted_iota(I32, (tq, tq), 1)
    upper = (rr >= cc).astype(BF16)
    diag = cc < rr

    def load(j):
        start = pl.multiple_of(j * tq, tq)
        return k_ref[pl.ds(start, tq), :].astype(BF16), v_ref[pl.ds(start, tq), :].astype(BF16)

    kj, vj = load(i)
    r0 = jnp.zeros((tq, 1), F32)
    a0 = jnp.zeros((tq, LANES), F32)
    r_lo, acc_lo = _sb_tile(q_lo, kj, vj, r0, a0, upper, diag)
    r_hi, acc_hi = _sb_tile(q_hi, kj, vj, r0, a0, upper, diag)

    def body(t, carry):
        r_lo, acc_lo, r_hi, acc_hi = carry
        kj, vj = load(i - 1 - t)
        r_lo, acc_lo = _sb_tile(q_lo, kj, vj, r_lo, acc_lo, upper, None)
        r_hi, acc_hi = _sb_tile(q_hi, kj, vj, r_hi, acc_hi, upper, None)
        return r_lo, acc_lo, r_hi, acc_hi

    _, acc_lo, _, acc_hi = lax.fori_loop(0, i, body, (r_lo, acc_lo, r_hi, acc_hi))
    o_ref[...] = jnp.where(lo, acc_lo, acc_hi) * g_ref[...]


def _sb_prompt(sbq, sbk, sbv, sbg, batch, seq):
    width = H_SB * HEAD_DIM
    npair = H_SB // 2
    tq = Q_BLOCK
    nq = seq // tq
    qmap = lambda b, p, i: (b * nq + i, p)
    kvmap = lambda b, p, i: (b, p)
    return pl.pallas_call(
        _sb_prompt_kernel,
        grid=(batch, npair, nq),
        in_specs=[
            pl.BlockSpec((tq, LANES), qmap),
            pl.BlockSpec((seq, LANES), kvmap),
            pl.BlockSpec((seq, LANES), kvmap),
            pl.BlockSpec((tq, LANES), qmap),
        ],
        out_specs=pl.BlockSpec((tq, LANES), qmap),
        out_shape=jax.ShapeDtypeStruct((batch * seq, width), F32),
        compiler_params=_params(("parallel", "parallel", "arbitrary")),
        name="sb_prompt",
    )(sbq, sbk, sbv, sbg)


def _outproj_kernel(*refs, n_mix):
    x_ref = refs[0]
    mix_refs = refs[1:1 + n_mix]
    w_refs = refs[1 + n_mix:1 + 2 * n_mix]
    y_ref = refs[1 + 2 * n_mix]
    acc = x_ref[...]
    for m_ref, w_ref in zip(mix_refs, w_refs):
        acc = acc + _dot(m_ref[...].astype(BF16), w_ref[...])
    y_ref[...] = acc


def _outproj(x2d, mixes, ws, tm):
    rows, d = x2d.shape
    n_mix = len(mixes)
    row = lambda i: (i, 0)
    fixed = lambda i: (0, 0)
    in_specs = [pl.BlockSpec((tm, d), row)]
    in_specs += [pl.BlockSpec((tm, m.shape[1]), row) for m in mixes]
    in_specs += [pl.BlockSpec(w.shape, fixed) for w in ws]
    return pl.pallas_call(
        functools.partial(_outproj_kernel, n_mix=n_mix),
        grid=(rows // tm,),
        in_specs=in_specs,
        out_specs=pl.BlockSpec((tm, d), row),
        out_shape=jax.ShapeDtypeStruct((rows, d), F32),
        compiler_params=_params(("parallel",)),
        name="outproj",
    )(x2d, *mixes, *ws)


def _rope_tables(pos):
    half = HEAD_DIM // 2
    inv = ROPE_BASE ** (-jnp.arange(half, dtype=F32) / half)
    ang = pos.astype(F32)[:, None] * inv[None, :]
    cos, sin = jnp.cos(ang), jnp.sin(ang)
    cos128 = jnp.tile(cos, (1, LANES // half))
    sin128 = jnp.tile(jnp.concatenate([-sin, sin], axis=1), (1, LANES // HEAD_DIM))
    return cos128, sin128


def _pair_gain(g):
    return jnp.tile(g.astype(F32), LANES // HEAD_DIM).reshape(1, LANES)


def _layer0_prompt(x2d, batch, seq, norm_g, w_in_bf, ret_gn_g, sb_qn_g, sb_kn_g, w_out_bf, tm):
    cos, sin = _rope_tables(jnp.arange(seq, dtype=I32))
    ret, sbq, sbk, sbv, sbg = _inproj0(x2d, norm_g, w_in_bf, cos, sin, _pair_gain(sb_qn_g),
                                       _pair_gain(sb_kn_g), tm, seq // tm)
    mix_ret, state_pairs = _ret_prompt(ret, ret_gn_g, batch, seq)
    mix_sb = _sb_prompt(sbq, sbk, sbv, sbg, batch, seq)
    half = w_out_bf.shape[0] // 2
    y = _outproj(x2d, [mix_ret, mix_sb], [w_out_bf[:half], w_out_bf[half:]], tm)
    return y, _unpair_state(state_pairs), sbk, sbv


NSA_W = H_NSA * HEAD_DIM
KV_W = G_NSA * HEAD_DIM
IN1_COLS = 2 * NSA_W + 6 * KV_W + LANES


def _dup_groups(x):
    lo = _lane_lo(x.shape)
    sw = pltpu.roll(x, HEAD_DIM, 1)
    return jnp.where(lo, x, sw), jnp.where(lo, sw, x)


def _inproj1_kernel(x_ref, g_ref, w_ref, qg_ref, skg_ref, wkg_ref,
                    q_ref, kc_ref, vc_ref, ks_ref, vs_ref, kw_ref, vw_ref,
                    ksd_ref, vsd_ref, kwd_ref, vwd_ref, gate_ref, bg_ref):
    x = x_ref[...]
    xn = x * lax.rsqrt(jnp.mean(x * x, axis=-1, keepdims=True) + EPS) * g_ref[...]
    xb = xn.astype(BF16)
    lo = _lane_lo((1, LANES))
    a = _dot(xb, w_ref[:, 0:NSA_W])
    for t in range(NSA_W // LANES):
        v = a[:, t * LANES:(t + 1) * LANES]
        q_ref[:, t * LANES:(t + 1) * LANES] = (_pair_rmsnorm(v, qg_ref[...], lo) * SCALE).astype(BF16)
    a = _dot(xb, w_ref[:, NSA_W:NSA_W + 6 * KV_W])
    kc, vc, ks, vs, kw, vw = [a[:, t * KV_W:(t + 1) * KV_W] for t in range(6)]
    ks = _pair_rmsnorm(ks, skg_ref[...], lo)
    kw = _pair_rmsnorm(kw, wkg_ref[...], lo)
    kc_ref[...] = kc
    vc_ref[...] = vc
    ks_ref[...] = ks
    vs_ref[...] = vs
    kw_ref[...] = kw
    vw_ref[...] = vw
    for src, dst in ((ks, ksd_ref), (vs, vsd_ref), (kw, kwd_ref), (vw, vwd_ref)):
        d0, d1 = _dup_groups(src)
        dst[:, 0:LANES] = d0.astype(BF16)
        dst[:, LANES:2 * LANES] = d1.astype(BF16)
    off = NSA_W + 6 * KV_W
    gate_ref[...] = _silu(_dot(xb, w_ref[:, off:off + NSA_W]))
    bg_ref[...] = _sigmoid(_dot(xb, w_ref[:, off + NSA_W:off + NSA_W + LANES]))


def _inproj1(x2d, norm_g, w_bf, qg, skg, wkg, tm):
    rows, d = x2d.shape
    row = lambda i: (i, 0)
    fixed = lambda i: (0, 0)
    kv = jax.ShapeDtypeStruct((rows, KV_W), F32)
    kvd = jax.ShapeDtypeStruct((rows, 2 * LANES), BF16)
    return pl.pallas_call(
        _inproj1_kernel,
        grid=(rows // tm,),
        in_specs=[
            pl.BlockSpec((tm, d), row),
            pl.BlockSpec((1, d), fixed),
            pl.BlockSpec((d, IN1_COLS), fixed),
            pl.BlockSpec((1, LANES), fixed),
            pl.BlockSpec((1, LANES), fixed),
            pl.BlockSpec((1, LANES), fixed),
        ],
        out_specs=[pl.BlockSpec((tm, NSA_W), row)] + [pl.BlockSpec((tm, KV_W), row)] * 6
        + [pl.BlockSpec((tm, 2 * LANES), row)] * 4 + [pl.BlockSpec((tm, NSA_W), row), pl.BlockSpec((tm, LANES), row)],
        out_shape=[jax.ShapeDtypeStruct((rows, NSA_W), BF16)] + [kv] * 6 + [kvd] * 4
        + [jax.ShapeDtypeStruct((rows, NSA_W), F32), jax.ShapeDtypeStruct((rows, LANES), F32)],
        compiler_params=_params(("parallel",)),
        name="inproj1",
    )(x2d, norm_g.reshape(1, d), w_bf, qg, skg, wkg)


def _pad_w_in1(w_in1):
    pad = IN1_COLS - w_in1.shape[1]
    return jnp.pad(w_in1, ((0, 0), (0, pad))).astype(BF16)


def _compress_rows(t_ref, pe_ref, w_ref, row0, nblk):
    acc = jnp.zeros((nblk, LANES), F32)
    for l in range(CMP_BLOCK):
        rows = t_ref[pl.ds(row0 + l, nblk, stride=CMP_BLOCK), :]
        acc = acc + _dot((rows + pe_ref[l:l + 1, :]).astype(BF16), w_ref[l])
    return acc


def _compress_prompt_kernel(kc_ref, vc_ref, pe_ref, wk_ref, wv_ref, kg_ref,
                            ck_ref, cv_ref, ckd_ref, cvd_ref):
    nblk = ck_ref.shape[0]
    lo = _lane_lo((1, LANES))
    ck = _pair_rmsnorm(_compress_rows(kc_ref, pe_ref, wk_ref, 0, nblk), kg_ref[...], lo)
    cv = _compress_rows(vc_ref, pe_ref, wv_ref, 0, nblk)
    ck_ref[...] = ck
    cv_ref[...] = cv
    for src, dst in ((ck, ckd_ref), (cv, cvd_ref)):
        d0, d1 = _dup_groups(src)
        dst[:, 0:LANES] = d0.astype(BF16)
        dst[:, LANES:2 * LANES] = d1.astype(BF16)


def _blockdiag_w(w):
    z = jnp.zeros_like(w)
    top = jnp.concatenate([w, z], axis=2)
    bot = jnp.concatenate([z, w], axis=2)
    return jnp.concatenate([top, bot], axis=1).astype(BF16)


def _compress_prompt(kc, vc, pe128, wk_bd, wv_bd, kg, batch, seq):
    nblk = seq // CMP_BLOCK
    rowb = lambda b: (b, 0)
    fixed2 = lambda b: (0, 0)
    fixed3 = lambda b: (0, 0, 0)
    o32 = jax.ShapeDtypeStruct((batch * nblk, LANES), F32)
    o16 = jax.ShapeDtypeStruct((batch * nblk, 2 * LANES), BF16)
    return pl.pallas_call(
        _compress_prompt_kernel,
        grid=(batch,),
        in_specs=[
            pl.BlockSpec((seq, LANES), rowb),
            pl.BlockSpec((seq, LANES), rowb),
            pl.BlockSpec((CMP_BLOCK, LANES), fixed2),
            pl.BlockSpec((CMP_BLOCK, LANES, LANES), fixed3),
            pl.BlockSpec((CMP_BLOCK, LANES, LANES), fixed3),
            pl.BlockSpec((1, LANES), fixed2),
        ],
        out_specs=[pl.BlockSpec((nblk, LANES), rowb)] * 2 + [pl.BlockSpec((nblk, 2 * LANES), rowb)] * 2,
        out_shape=[o32, o32, o16, o16],
        compiler_params=_params(("parallel",)),
        name="compress_prompt",
    )(kc, vc, pe128, wk_bd, wv_bd, kg)


def _softmax_tile_update(qm, kj, vj, mask, m, l, acc):
    s = _dot_nt(qm, kj)
    s = jnp.where(mask, s, NEG_INF)
    m_new = jnp.maximum(m, jnp.max(s, axis=-1, keepdims=True))
    alpha = jnp.exp(m - m_new)
    p = jnp.where(mask, jnp.exp(s - m_new), 0.0)
    l = alpha * l + jnp.sum(p, axis=-1, keepdims=True)
    acc = alpha * acc + _dot(p.astype(BF16), vj)
    return m_new, l, acc


def _nsa_prompt_kernel(q_ref, ck_ref, cv_ref, ks_ref, vs_ref, kw_ref, vw_ref, bg_ref, gate_ref, e_ref,
                       o_ref):
    i = pl.program_id(2)
    g = pl.program_id(1)
    tq = q_ref.shape[0]
    nb = ck_ref.shape[0]
    npair = HPG // 2
    lo = _lane_lo((tq, LANES))
    q_pos = i * tq + lax.broadcasted_iota(I32, (tq, 1), 0)

    heads = []
    for p in range(npair):
        qp = q_ref[:, p * LANES:(p + 1) * LANES]
        zero = jnp.zeros_like(qp)
        heads.append(jnp.where(lo, qp, zero))
        heads.append(jnp.where(lo, zero, qp))

    ck = ck_ref[...]
    cv = cv_ref[...]
    blk_n = lax.broadcasted_iota(I32, (tq, nb), 1)
    cmask = blk_n * CMP_BLOCK + (CMP_BLOCK - 1) <= q_pos
    imp = jnp.zeros((tq, nb), F32)
    o_cmp = []
    for qm in heads:
        s = jnp.where(cmask, _dot_nt(qm, ck), NEG_INF)
        m = jnp.max(s, axis=-1, keepdims=True)
        e = jnp.where(cmask, jnp.exp(s - m), 0.0)
        den = jnp.sum(e, axis=-1, keepdims=True)
        pr = e / jnp.where(den > 0.0, den, 1.0)
        imp = imp + pr
        o_cmp.append(_dot(pr.astype(BF16), cv))

    ratio = SEL_BLOCK // CMP_BLOCK
    even = (blk_n & 1) == 0
    pooled = imp + jnp.where(even, pltpu.roll(imp, nb - 1, 1), pltpu.roll(imp, 1, 1))
    sblk = blk_n // ratio
    cur = q_pos // SEL_BLOCK
    forced = (sblk == 0) | (sblk == cur) | (sblk == cur - 1)
    score = jnp.where(sblk <= cur, pooled + FORCE_BONUS * forced.astype(F32), NEG_INF)
    cnt = jnp.zeros((tq, nb), F32)
    for mblk in range(nb // ratio):
        c = score[:, ratio * mblk:ratio * mblk + 1]
        ahead = (c > score) | ((c == score) & (mblk < sblk))
        cnt = cnt + ahead.astype(F32)
    sel = (cnt < float(TOP_N)).astype(BF16)

    key_l = lax.broadcasted_iota(I32, (nb, tq), 1)
    lane_n = lax.broadcasted_iota(I32, (nb, tq), 0)
    k_in = lax.broadcasted_iota(I32, (tq, tq), 1)

    def flash(k_ref, v_ref, j0, j1, mask_fn):
        def body(j, carry):
            start = pl.multiple_of(j * tq, tq)
            kj = k_ref[pl.ds(start, tq), :]
            vj = v_ref[pl.ds(start, tq), :]
            mask = mask_fn(j)
            out = []
            for h, qm in enumerate(heads):
                out.extend(_softmax_tile_update(qm, kj, vj, mask, *carry[3 * h:3 * h + 3]))
            return tuple(out)

        init = []
        for _ in heads:
            init += [jnp.full((tq, 1), NEG_INF, F32), jnp.zeros((tq, 1), F32), jnp.zeros((tq, LANES), F32)]
        fin = lax.fori_loop(j0, j1, body, tuple(init))
        return [fin[3 * h + 2] / fin[3 * h + 1] for h in range(len(heads))]

    def sel_mask(j):
        want = (tq // SEL_BLOCK) * ratio * j + ratio * (key_l // SEL_BLOCK)
        expand = (lane_n == want).astype(BF16)
        chosen = _dot(sel, expand) > 0.5
        return chosen & (j * tq + k_in <= q_pos)

    def win_mask(j):
        k_pos = j * tq + k_in
        return (k_pos <= q_pos) & (k_pos > q_pos - WINDOW)

    o_sel = flash(ks_ref, vs_ref, 0, i + 1, sel_mask)
    o_win = flash(kw_ref, vw_ref, jnp.maximum(i - WINDOW // tq, 0), i + 1, win_mask)

    bg_hi, bg_lo = _split_bf16(bg_ref[...])
    gexp = [_dot(bg_hi, e_ref[0, br]) + _dot(bg_lo, e_ref[0, br]) for br in range(3)]
    for p in range(npair):
        sl = slice(p * LANES, (p + 1) * LANES)
        oc = jnp.where(lo, o_cmp[2 * p], o_cmp[2 * p + 1])
        osel = jnp.where(lo, o_sel[2 * p], o_sel[2 * p + 1])
        ow = jnp.where(lo, o_win[2 * p], o_win[2 * p + 1])
        mixed = gexp[0][:, sl] * oc + gexp[1][:, sl] * osel + gexp[2][:, sl] * ow
        o_ref[:, sl] = mixed * gate_ref[:, sl]


def _gate_expanders():
    e = np.zeros((G_NSA, 3, LANES, HPG * HEAD_DIM), np.float32)
    for g in range(G_NSA):
        for h in range(HPG):
            for br in range(3):
                e[g, br, g * HPG * 3 + h * 3 + br, h * HEAD_DIM:(h + 1) * HEAD_DIM] = 1.0
    return jnp.asarray(e, BF16)


def _nsa_prompt(qn, ckd, cvd, ksd, vsd, kwd, vwd, bg, gate, batch, seq):
    tq = Q_BLOCK
    nq = seq // tq
    nb = seq // CMP_BLOCK
    gw = HPG * HEAD_DIM
    qmap = lambda b, g, i: (b * nq + i, g)
    kvmap = lambda b, g, i: (b, g)
    return pl.pallas_call(
        _nsa_prompt_kernel,
        grid=(batch, G_NSA, nq),
        in_specs=[
            pl.BlockSpec((tq, gw), qmap),
            pl.BlockSpec((nb, LANES), kvmap),
            pl.BlockSpec((nb, LANES), kvmap),
            pl.BlockSpec((seq, LANES), kvmap),
            pl.BlockSpec((seq, LANES), kvmap),
            pl.BlockSpec((seq, LANES), kvmap),
            pl.BlockSpec((seq, LANES), kvmap),
            pl.BlockSpec((tq, LANES), lambda b, g, i: (b * nq + i, 0)),
            pl.BlockSpec((tq, gw), qmap),
            pl.BlockSpec((1, 3, LANES, gw), lambda b, g, i: (g, 0, 0, 0)),
        ],
        out_specs=pl.BlockSpec((tq, gw), qmap),
        out_shape=jax.ShapeDtypeStruct((batch * seq, NSA_W), F32),
        compiler_params=_params(("parallel", "parallel", "arbitrary")),
        name="nsa_prompt",
    )(qn, ckd, cvd, ksd, vsd, kwd, vwd, bg, gate, _gate_expanders())


def _layer1_prompt(x2d, batch, seq, norm_g, w_in_bf, qn_g, cmp_kn_g, sel_kn_g, win_kn_g, pe128, wk_bd, wv_bd,
                   w_out_bf, tm):
    (qn, kc, vc, ks, vs, kw, vw, ksd, vsd, kwd, vwd, gate, bg) = _inproj1(
        x2d, norm_g, w_in_bf, _pair_gain(qn_g), _pair_gain(sel_kn_g), _pair_gain(win_kn_g), tm)
    ck, cv, ckd, cvd = _compress_prompt(kc, vc, pe128, wk_bd, wv_bd, _pair_gain(cmp_kn_g), batch, seq)
    mix = _nsa_prompt(qn, ckd, cvd, ksd, vsd, kwd, vwd, bg, gate, batch, seq)
    y = _outproj(x2d, [mix], [w_out_bf], tm)
    return y, kc, vc, ks, vs, kw, vw


def _ret_sample_kernel(ret_ref, s_ref, qdec_ref, cdec_ref, gn_ref, mix_ref, so_ref):
    width = H_RET * HEAD_DIM
    rowi = lax.broadcasted_iota(I32, (8, HEAD_DIM), 0)
    eye = (lax.broadcasted_iota(I32, (HEAD_DIM, HEAD_DIM), 0)
           == lax.broadcasted_iota(I32, (HEAD_DIM, HEAD_DIM), 1)).astype(BF16)
    row = ret_ref[0]
    for h in range(H_RET):
        sl = slice(h * HEAD_DIM, (h + 1) * HEAD_DIM)
        qb = row[:, sl].astype(BF16)
        kb = row[:, width + h * HEAD_DIM:width + (h + 1) * HEAD_DIM].astype(BF16)
        vb = row[:, 2 * width + h * HEAD_DIM:2 * width + (h + 1) * HEAD_DIM].astype(BF16)
        gate = row[:, 3 * width + h * HEAD_DIM:3 * width + (h + 1) * HEAD_DIM]
        s = s_ref[0, h]
        qs = _dot(jnp.broadcast_to(qb.astype(F32), (8, HEAD_DIM)).astype(BF16), s.astype(BF16))[0:1]
        att = jnp.sum(qb.astype(F32) * kb.astype(F32), axis=-1, keepdims=True)
        o = att.astype(BF16).astype(F32) * vb.astype(F32) + qs * qdec_ref[h:h + 1, :]
        k8 = jnp.where(rowi == 0, jnp.broadcast_to(kb.astype(F32), (8, HEAD_DIM)), 0.0).astype(BF16)
        kcol = _dot_nt(eye, k8)[:, 0:1]
        so_ref[0, h] = s * cdec_ref[h:h + 1, :] + kcol * vb.astype(F32)
        mu = jnp.mean(o, axis=-1, keepdims=True)
        oc = o - mu
        var = jnp.mean(oc * oc, axis=-1, keepdims=True)
        on = oc * lax.rsqrt(var + EPS) * gn_ref[h:h + 1, :]
        mix_ref[0, :, sl] = on * _silu(gate)


def _ret_sample(ret, state, gn_g):
    nb, cols = ret.shape
    width = H_RET * HEAD_DIM
    h = jnp.arange(H_RET, dtype=F32)
    lg = jnp.log1p(-jnp.exp2(-5.0 - h))
    qdec = jnp.broadcast_to(jnp.exp(lg * 1.0)[:, None], (H_RET, HEAD_DIM))
    cdec = jnp.broadcast_to(jnp.exp(lg * 1.0)[:, None], (H_RET, HEAD_DIM))
    m3 = lambda b: (b, 0, 0)
    tab = pl.BlockSpec((H_RET, HEAD_DIM), lambda b: (0, 0))
    st = pl.BlockSpec((1, H_RET, HEAD_DIM, HEAD_DIM), lambda b: (b, 0, 0, 0))
    mix, state_new = pl.pallas_call(
        _ret_sample_kernel,
        grid=(nb,),
        in_specs=[pl.BlockSpec((1, 1, cols), m3), st, tab, tab, tab],
        out_specs=[pl.BlockSpec((1, 1, width), m3), st],
        out_shape=[jax.ShapeDtypeStruct((nb, 1, width), F32),
                   jax.ShapeDtypeStruct(state.shape, F32)],
        compiler_params=_params(("parallel",)),
        name="retention_sample",
    )(ret.reshape(nb, 1, cols), state, qdec, cdec, gn_g.astype(F32))
    return mix.reshape(nb, width), state_new


SB_PAGES_PER_STEP = 4


def _sb_sample_kernel(pt_ref, q_ref, g_ref, *refs, n_pp):
    k_refs = refs[:n_pp]
    v_refs = refs[n_pp:2 * n_pp]
    o_ref = refs[2 * n_pp]
    r_scr, acc_scr = refs[2 * n_pp + 1:]
    j = pl.program_id(1)

    @pl.when(j == 0)
    def _():
        r_scr[...] = jnp.zeros_like(r_scr)
        acc_scr[...] = jnp.zeros_like(acc_scr)

    q = q_ref[0].astype(F32)
    page = PAGE_SIZE
    rowq = lax.broadcasted_iota(I32, (H_SB, HEAD_DIM), 0)
    rowa = lax.broadcasted_iota(I32, (H_SB, page), 0)
    upper = (lax.broadcasted_iota(I32, (page, page), 0) >= lax.broadcasted_iota(I32, (page, page), 1)).astype(BF16)
    r = r_scr[:, 0:1]
    acc = acc_scr[...]
    for kref, vref in zip(k_refs, v_refs):
        z = jnp.zeros((H_SB, page), F32)
        for h in range(H_SB):
            kh = kref[pl.ds(h, page, stride=H_SB), :].astype(BF16)
            z = z + _dot_nt(jnp.where(rowq == h, q, 0.0).astype(BF16), kh)
        l1p = jnp.log1p(jnp.exp(-jnp.abs(z)))
        ls_pos = jnp.minimum(z, 0.0) - l1p
        lm = ls_pos - z
        hi, lo_part = _split_bf16(lm)
        s_incl = _dot(hi, upper) + _dot(lo_part, upper)
        a = jnp.exp(ls_pos + (s_incl - lm) + r)
        for h in range(H_SB):
            vh = vref[pl.ds(h, page, stride=H_SB), :].astype(BF16)
            acc = acc + _dot(jnp.where(rowa == h, a, 0.0).astype(BF16), vh)
        r = r + s_incl[:, 0:1]
    r_scr[...] = jnp.broadcast_to(r, r_scr.shape)
    acc_scr[...] = acc

    @pl.when(j == pl.num_programs(1) - 1)
    def _():
        o_ref[0] = acc * g_ref[0]


def _sb_sample(q8, g8, cache_k, cache_v, page_table):
    nb = q8.shape[0]
    n_pages = page_table.shape[1]
    n_pp = SB_PAGES_PER_STEP
    rows = PAGE_SIZE * H_SB

    def page_map(r):
        return lambda b, j, pt: (pt[b, n_pages - 1 - (j * n_pp + r)], 0, 0)

    head = lambda b, j, pt: (b, 0, 0)
    grid_spec = pltpu.PrefetchScalarGridSpec(
        num_scalar_prefetch=1,
        grid=(nb, n_pages // n_pp),
        in_specs=[pl.BlockSpec((1, H_SB, HEAD_DIM), head), pl.BlockSpec((1, H_SB, HEAD_DIM), head)]
        + [pl.BlockSpec((None, rows, HEAD_DIM), page_map(r)) for r in range(n_pp)] * 2,
        out_specs=pl.BlockSpec((1, H_SB, HEAD_DIM), head),
        scratch_shapes=[pltpu.VMEM((H_SB, LANES), F32), pltpu.VMEM((H_SB, HEAD_DIM), F32)],
    )
    return pl.pallas_call(
        functools.partial(_sb_sample_kernel, n_pp=n_pp),
        grid_spec=grid_spec,
        out_shape=jax.ShapeDtypeStruct((nb, H_SB, HEAD_DIM), F32),
        compiler_params=_params(("parallel", "arbitrary")),
        name="sb_sample",
    )(page_table, q8, g8, *([cache_k] * n_pp), *([cache_v] * n_pp))


CMP_PAGES_PER_STEP = 4


def _compress_sample_kernel(pt_ref, *refs, n_pp):
    k_refs = refs[:n_pp]
    v_refs = refs[n_pp:2 * n_pp]
    pe_ref, wk_ref, wv_ref, kg_ref, ck_ref, cv_ref, kbuf, vbuf = refs[2 * n_pp:]
    j = pl.program_id(1)
    for r in range(n_pp):
        kbuf[r * PAGE_SIZE:(r + 1) * PAGE_SIZE, :] = k_refs[r][0]
        vbuf[r * PAGE_SIZE:(r + 1) * PAGE_SIZE, :] = v_refs[r][0]
    nblk = n_pp * PAGE_SIZE // CMP_BLOCK
    lo = _lane_lo((1, LANES))
    start = pl.multiple_of(j * nblk, nblk)
    ck_ref[0, pl.ds(start, nblk), :] = _pair_rmsnorm(_compress_rows(kbuf, pe_ref, wk_ref, 0, nblk), kg_ref[...], lo)
    cv_ref[0, pl.ds(start, nblk), :] = _compress_rows(vbuf, pe_ref, wv_ref, 0, nblk)


def _compress_sample(cache_k, cache_v, page_table, pe128, wk_bd, wv_bd, kg):
    nb, n_pages = page_table.shape
    n_pp = CMP_PAGES_PER_STEP
    nblk_total = n_pages * PAGE_SIZE // CMP_BLOCK

    def page_map(r):
        return lambda b, j, pt: (pt[b, j * n_pp + r], 0, 0)

    fixed2 = lambda b, j, pt: (0, 0)
    fixed3 = lambda b, j, pt: (0, 0, 0)
    out_map = lambda b, j, pt: (b, 0, 0)
    grid_spec = pltpu.PrefetchScalarGridSpec(
        num_scalar_prefetch=1,
        grid=(nb, n_pages // n_pp),
        in_specs=[pl.BlockSpec((1, PAGE_SIZE, LANES), page_map(r)) for r in range(n_pp)] * 2
        + [pl.BlockSpec((CMP_BLOCK, LANES), fixed2),
           pl.BlockSpec((CMP_BLOCK, LANES, LANES), fixed3),
           pl.BlockSpec((CMP_BLOCK, LANES, LANES), fixed3),
           pl.BlockSpec((1, LANES), fixed2)],
        out_specs=[pl.BlockSpec((1, nblk_total, LANES), out_map)] * 2,
        scratch_shapes=[pltpu.VMEM((n_pp * PAGE_SIZE, LANES), F32)] * 2,
    )
    o = jax.ShapeDtypeStruct((nb, nblk_total, LANES), F32)
    return pl.pallas_call(
        functools.partial(_compress_sample_kernel, n_pp=n_pp),
        grid_spec=grid_spec,
        out_shape=[o, o],
        compiler_params=_params(("parallel", "arbitrary")),
        name="compress_sample",
    )(page_table, *([cache_k] * n_pp), *([cache_v] * n_pp), pe128, wk_bd, wv_bd, kg)


def _group_q(q, g):
    z = jnp.zeros_like(q)
    return jnp.concatenate([q, z], axis=1) if g == 0 else jnp.concatenate([z, q], axis=1)


def _nsa_sample_select_kernel(q_ref, ck_ref, cv_ref, ocmp_ref, idx_ref, *, q_pos):
    q = q_ref[0]
    ck = ck_ref[0].astype(BF16)
    cv = cv_ref[0].astype(BF16)
    nb = ck.shape[0]
    blk_n = lax.broadcasted_iota(I32, (1, nb), 1)
    cmask = blk_n * CMP_BLOCK + (CMP_BLOCK - 1) <= q_pos
    ratio = SEL_BLOCK // CMP_BLOCK
    sblk = blk_n // ratio
    cur = q_pos // SEL_BLOCK
    lane_f = blk_n.astype(F32)
    slot = lax.broadcasted_iota(I32, (1, LANES), 1)
    row8 = lax.broadcasted_iota(I32, (8, LANES), 0)
    idx_all = jnp.full((8, LANES), -1, I32)
    neg_inf = float("-inf")
    for g in range(G_NSA):
        qg = _group_q(q[g * HPG:(g + 1) * HPG], g)
        s = jnp.where(cmask, _dot_nt(qg, ck), NEG_INF)
        m = jnp.max(s, axis=-1, keepdims=True)
        e = jnp.where(cmask, jnp.exp(s - m), 0.0)
        den = jnp.sum(e, axis=-1, keepdims=True)
        pr = e / jnp.where(den > 0.0, den, 1.0)
        o = _dot(pr.astype(BF16), cv)
        ocmp_ref[0, g * HPG:(g + 1) * HPG, :] = o[:, g * HEAD_DIM:(g + 1) * HEAD_DIM]
        imp = jnp.sum(pr, axis=0, keepdims=True)
        even = (blk_n & 1) == 0
        pooled = imp + jnp.where(even, pltpu.roll(imp, nb - 1, 1), pltpu.roll(imp, 1, 1))
        forced = (sblk == 0) | (sblk == cur) | (sblk == cur - 1)
        score = jnp.where(sblk < cur, pooled + FORCE_BONUS * forced.astype(F32), neg_inf)
        idx_g = jnp.full((1, LANES), -1, I32)
        for t in range(TOP_N - 1):
            mx = jnp.max(score, axis=-1, keepdims=True)
            first = jnp.min(jnp.where(score == mx, lane_f, float(4 * nb)), axis=-1, keepdims=True)
            blk = first.astype(I32) // ratio
            idx_g = jnp.where((slot == t) & (mx > neg_inf), blk, idx_g)
            score = jnp.where(sblk == blk, neg_inf, score)
        idx_all = jnp.where(row8 == g, jnp.broadcast_to(idx_g, (8, LANES)), idx_all)
    idx_ref[0] = idx_all


def _nsa_sample_select(q16, ck, cv, q_pos):
    nb = q16.shape[0]
    nblk = ck.shape[1]
    m3 = lambda b: (b, 0, 0)
    return pl.pallas_call(
        functools.partial(_nsa_sample_select_kernel, q_pos=q_pos),
        grid=(nb,),
        in_specs=[pl.BlockSpec((1, H_NSA, HEAD_DIM), m3),
                  pl.BlockSpec((1, nblk, LANES), m3),
                  pl.BlockSpec((1, nblk, LANES), m3)],
        out_specs=[pl.BlockSpec((1, H_NSA, HEAD_DIM), m3), pl.BlockSpec((1, 8, LANES), m3)],
        out_shape=[jax.ShapeDtypeStruct((nb, H_NSA, HEAD_DIM), F32),
                   jax.ShapeDtypeStruct((nb, 8, LANES), I32)],
        compiler_params=_params(("parallel",)),
        name="nsa_sample_select",
    )(q16, ck, cv)


def _nsa_sample_attend_kernel(pt_ref, idx_ref, q_ref, k0_ref, k1_ref, v0_ref, v1_ref, kn_ref, vn_ref,
                              wk_ref, wv_ref, ocmp_ref, bg_ref, gate_ref, o_ref, m_scr, l_scr, acc_scr,
                              *, q_pos):
    b = pl.program_id(0)
    s = pl.program_id(1)

    @pl.when(s == 0)
    def _():
        m_scr[...] = jnp.full_like(m_scr, NEG_INF)
        l_scr[...] = jnp.zeros_like(l_scr)
        acc_scr[...] = jnp.zeros_like(acc_scr)

    q = q_ref[0]
    qgs = [_group_q(q[g * HPG:(g + 1) * HPG], g) for g in range(G_NSA)]
    for g, (kref, vref) in enumerate(((k0_ref, v0_ref), (k1_ref, v1_ref))):
        blk = idx_ref[b * G_NSA + g, s]
        valid = blk >= 0
        half = pl.multiple_of((jnp.maximum(blk, 0) % 2) * SEL_BLOCK, SEL_BLOCK)
        kb = kref[0, pl.ds(half, SEL_BLOCK), :].astype(BF16)
        vb = vref[0, pl.ds(half, SEL_BLOCK), :].astype(BF16)
        sc = jnp.where(valid, _dot_nt(qgs[g], kb), NEG_INF)
        m_old = m_scr[g][:, 0:1]
        m_new = jnp.maximum(m_old, jnp.max(sc, axis=-1, keepdims=True))
        alpha = jnp.exp(m_old - m_new)
        p = jnp.where(valid, jnp.exp(sc - m_new), 0.0)
        l_new = alpha * l_scr[g][:, 0:1] + jnp.sum(p, axis=-1, keepdims=True)
        acc_scr[g] = alpha * acc_scr[g] + _dot(p.astype(BF16), vb)
        m_scr[g] = jnp.broadcast_to(m_new, (HPG, LANES))
        l_scr[g] = jnp.broadcast_to(l_new, (HPG, LANES))

    @pl.when(s == pl.num_programs(1) - 1)
    def _():
        kn = kn_ref[0].astype(BF16).astype(F32)
        vn = vn_ref[0].astype(BF16).astype(F32)
        wk = wk_ref[0].astype(BF16)
        wv = wv_ref[0].astype(BF16)
        wlen = wk.shape[0]
        w_pos = q_pos - (wlen - 1) + lax.broadcasted_iota(I32, (1, wlen), 1)
        wmask = (w_pos <= q_pos) & (w_pos > q_pos - WINDOW) & (w_pos >= 0)
        bg = bg_ref[0]
        col = lax.broadcasted_iota(I32, (HPG, LANES), 1)
        hrow = lax.broadcasted_iota(I32, (HPG, LANES), 0)
        for g in range(G_NSA):
            qg = qgs[g]
            s_new = jnp.sum(qg.astype(F32) * kn, axis=-1, keepdims=True)
            m_old = m_scr[g][:, 0:1]
            m_new = jnp.maximum(m_old, s_new)
            alpha = jnp.exp(m_old - m_new)
            p = jnp.exp(s_new - m_new)
            l_new = alpha * l_scr[g][:, 0:1] + p
            acc = alpha * acc_scr[g] + p.astype(BF16).astype(F32) * vn
            o_sel = acc / l_new
            sw = jnp.where(wmask, _dot_nt(qg, wk), NEG_INF)
            mw = jnp.max(sw, axis=-1, keepdims=True)
            pw = jnp.where(wmask, jnp.exp(sw - mw), 0.0)
            o_win = _dot(pw.astype(BF16), wv) / jnp.sum(pw, axis=-1, keepdims=True)
            sl = slice(g * HEAD_DIM, (g + 1) * HEAD_DIM)
            gates = []
            for br in range(3):
                pick = col == g * HPG * 3 + hrow * 3 + br
                gates.append(jnp.sum(jnp.where(pick, jnp.broadcast_to(bg, (HPG, LANES)), 0.0), axis=-1, keepdims=True))
            rows = slice(g * HPG, (g + 1) * HPG)
            mixed = gates[0] * ocmp_ref[0, rows, :] + gates[1] * o_sel[:, sl] + gates[2] * o_win[:, sl]
            o_ref[0, rows, :] = mixed * gate_ref[0, rows, :]


def _nsa_sample_attend(page_table, idx, q16, sel_k, sel_v, ks_new, vs_new, wk_new, wv_new, ocmp, bg, gate16, q_pos):
    nb = q16.shape[0]
    n_slots = TOP_N - 1
    n_pages = page_table.shape[1]
    wlen = wk_new.shape[1]

    def page_map(g):
        def f(b, s, pt, ix):
            blk = jnp.maximum(ix[b * G_NSA + g, s], 0)
            return (pt[b, jnp.minimum(blk // 2, n_pages - 1)], 0, 0)
        return f

    m3 = lambda b, s, pt, ix: (b, 0, 0)
    grid_spec = pltpu.PrefetchScalarGridSpec(
        num_scalar_prefetch=2,
        grid=(nb, n_slots),
        in_specs=[pl.BlockSpec((1, H_NSA, HEAD_DIM), m3),
                  pl.BlockSpec((1, PAGE_SIZE, LANES), page_map(0)),
                  pl.BlockSpec((1, PAGE_SIZE, LANES), page_map(1)),
                  pl.BlockSpec((1, PAGE_SIZE, LANES), page_map(0)),
                  pl.BlockSpec((1, PAGE_SIZE, LANES), page_map(1)),
                  pl.BlockSpec((1, 1, LANES), m3),
                  pl.BlockSpec((1, 1, LANES), m3),
                  pl.BlockSpec((1, wlen, LANES), m3),
                  pl.BlockSpec((1, wlen, LANES), m3),
                  pl.BlockSpec((1, H_NSA, HEAD_DIM), m3),
                  pl.BlockSpec((1, 1, LANES), m3),
                  pl.BlockSpec((1, H_NSA, HEAD_DIM), m3)],
        out_specs=pl.BlockSpec((1, H_NSA, HEAD_DIM), m3),
        scratch_shapes=[pltpu.VMEM((G_NSA, HPG, LANES), F32)] * 3,
    )
    return pl.pallas_call(
        functools.partial(_nsa_sample_attend_kernel, q_pos=q_pos),
        grid_spec=grid_spec,
        out_shape=jax.ShapeDtypeStruct((nb, H_NSA, HEAD_DIM), F32),
        compiler_params=_params(("parallel", "arbitrary")),
        name="nsa_sample_attend",
    )(page_table, idx, q16, sel_k, sel_k, sel_v, sel_v, ks_new, vs_new, wk_new, wv_new, ocmp, bg, gate16)


def _layer0_sample(x2d, past_len, state, cache_k, cache_v, page_table, norm_g, w_in_bf, ret_gn_g, sb_qn_g, sb_kn_g,
                   w_out_bf):
    nb = x2d.shape[0]
    cos, sin = _rope_tables(jnp.full((nb,), past_len, I32))
    ret, sbq, sbk, sbv, sbg = _inproj0(x2d, norm_g, w_in_bf, cos, sin, _pair_gain(sb_qn_g), _pair_gain(sb_kn_g),
                                       nb, 1)
    mix_ret, state_new = _ret_sample(ret, state, ret_gn_g)
    n_phys = cache_k.shape[0]
    ck = cache_k.reshape(n_phys, PAGE_SIZE * H_SB, HEAD_DIM)
    cv = cache_v.reshape(n_phys, PAGE_SIZE * H_SB, HEAD_DIM)
    mix_sb = _sb_sample(sbq.reshape(nb, H_SB, HEAD_DIM), sbg.reshape(nb, H_SB, HEAD_DIM), ck, cv, page_table)
    mix_sb = mix_sb.reshape(nb, H_SB * HEAD_DIM)
    half = w_out_bf.shape[0] // 2
    y = _outproj(x2d, [mix_ret, mix_sb], [w_out_bf[:half], w_out_bf[half:]], nb)
    return y, state_new, sbk, sbv


def _layer1_sample(x2d, past_len, cmp_k, cmp_v, sel_k, sel_v, win_k, win_v, page_table, norm_g, w_in_bf, qn_g,
                   cmp_kn_g, sel_kn_g, win_kn_g, pe128, wk_bd, wv_bd, w_out_bf):
    nb = x2d.shape[0]
    (qn, kc, vc, ks, vs, kw, vw, _, _, _, _, gate, bg) = _inproj1(
        x2d, norm_g, w_in_bf, _pair_gain(qn_g), _pair_gain(sel_kn_g), _pair_gain(win_kn_g), nb)
    n_phys = cmp_k.shape[0]
    pool = lambda t: t.reshape(n_phys, PAGE_SIZE, LANES)
    ck, cv = _compress_sample(pool(cmp_k), pool(cmp_v), page_table, pe128, wk_bd, wv_bd, _pair_gain(cmp_kn_g))
    q16 = qn.reshape(nb, H_NSA, HEAD_DIM)
    ocmp, idx = _nsa_sample_select(q16, ck, cv, past_len)
    idx2 = idx[:, :G_NSA, :TOP_N].reshape(nb * G_NSA, TOP_N)
    wlen = win_k.shape[1]
    wk_new = jnp.concatenate([win_k.reshape(nb, wlen, LANES)[:, 1:], kw[:, None, :]], axis=1)
    wv_new = jnp.concatenate([win_v.reshape(nb, wlen, LANES)[:, 1:], vw[:, None, :]], axis=1)
    mix = _nsa_sample_attend(page_table, idx2, q16, pool(sel_k), pool(sel_v), ks[:, None, :], vs[:, None, :],
                             wk_new, wv_new, ocmp, bg[:, None, :], gate.reshape(nb, H_NSA, HEAD_DIM), past_len)
    y = _outproj(x2d, [mix.reshape(nb, NSA_W)], [w_out_bf], nb)
    return y, kc, vc, ks, vs, wk_new, wv_new


def kernel(x_prompt, x_sample, state_ret, cache_sb_k, cache_sb_v, cache_cmp_k, cache_cmp_v, cache_sel_k,
           cache_sel_v, cache_win_k, cache_win_v, page_table, norm0_g, w_in0, ret_gn_g, sb_qn_g, sb_kn_g,
           w_out0, norm1_g, w_in1, nsa_qn_g, cmp_kn_g, sel_kn_g, win_kn_g, cmp_pe, w_cmp_k, w_cmp_v, w_out1):
    batch, seq, d = x_prompt.shape
    nb = x_sample.shape[0]
    past_len = page_table.shape[1] * PAGE_SIZE
    tm = 256
    w_in0_bf = w_in0.astype(BF16)
    w_out0_bf = w_out0.astype(BF16)
    w_in1_bf = _pad_w_in1(w_in1)
    w_out1_bf = w_out1.astype(BF16)
    pe128 = jnp.tile(cmp_pe, (1, G_NSA))
    wk_bd = _blockdiag_w(w_cmp_k)
    wv_bd = _blockdiag_w(w_cmp_v)

    xp = x_prompt.reshape(batch * seq, d)
    y1p, ret_p, sbk_p, sbv_p = _layer0_prompt(xp, batch, seq, norm0_g, w_in0_bf, ret_gn_g, sb_qn_g, sb_kn_g,
                                              w_out0_bf, tm)
    y2p, ckp, cvp, skp, svp, wkp, wvp = _layer1_prompt(y1p, batch, seq, norm1_g, w_in1_bf, nsa_qn_g, cmp_kn_g,
                                                        sel_kn_g, win_kn_g, pe128, wk_bd, wv_bd, w_out1_bf, tm)
    xs = x_sample.reshape(nb, d)
    y1s, ret_s, sbk_s, sbv_s = _layer0_sample(xs, past_len, state_ret, cache_sb_k, cache_sb_v, page_table, norm0_g,
                                              w_in0_bf, ret_gn_g, sb_qn_g, sb_kn_g, w_out0_bf)
    y2s, cks, cvs, sks, svs, wks, wvs = _layer1_sample(y1s, past_len, cache_cmp_k, cache_cmp_v, cache_sel_k,
                                                        cache_sel_v, cache_win_k, cache_win_v, page_table, norm1_g,
                                                        w_in1_bf, nsa_qn_g, cmp_kn_g, sel_kn_g, win_kn_g, pe128,
                                                        wk_bd, wv_bd, w_out1_bf)
    keep = min(WINDOW, seq)
    hs = (batch, seq, H_SB, HEAD_DIM)
    gs = (batch, seq, G_NSA, HEAD_DIM)
    gd = (nb, 1, G_NSA, HEAD_DIM)
    wlen = cache_win_k.shape[1]
    return (y2p.reshape(batch, seq, d), y2s.reshape(nb, 1, d), ret_p, ret_s,
            sbk_p.reshape(hs), sbv_p.reshape(hs),
            sbk_s.reshape(nb, 1, H_SB, HEAD_DIM), sbv_s.reshape(nb, 1, H_SB, HEAD_DIM),
            ckp.reshape(gs), cvp.reshape(gs), skp.reshape(gs), svp.reshape(gs),
            wkp.reshape(gs)[:, seq - keep:], wvp.reshape(gs)[:, seq - keep:],
            cks.reshape(gd), cvs.reshape(gd), sks.reshape(gd), svs.reshape(gd),
            wks.reshape(nb, wlen, G_NSA, HEAD_DIM), wvs.reshape(nb, wlen, G_NSA, HEAD_DIM))
```

```python
import functools

import jax
import jax.numpy as jnp
import numpy as np
from jax import lax
from jax.experimental import pallas as pl
from jax.experimental.pallas import tpu as pltpu

F32 = jnp.float32
BF16 = jnp.bfloat16
I32 = jnp.int32

HEAD_DIM = 64
LANES = 128
H_RET = 8
H_SB = 8
RET_CHUNK = 128
ROPE_BASE = 10000.0
Q_BLOCK = 128
H_NSA = 16
G_NSA = 2
HPG = H_NSA // G_NSA
CMP_BLOCK = 32
SEL_BLOCK = 64
TOP_N = 16
WINDOW = 512
FORCE_BONUS = 1000.0
NEG_INF = -1e30
EPS = 1e-6
PAGE_SIZE = 128
SCALE = HEAD_DIM ** -0.5
VMEM_LIMIT = 56 * 1024 * 1024


def _dot(a, b):
    return jnp.dot(a, b, preferred_element_type=F32)


def _dot_nt(a, b):
    return lax.dot_general(a, b, (((1,), (1,)), ((), ())), preferred_element_type=F32)


def _dot_tn(a, b):
    return lax.dot_general(a, b, (((0,), (0,)), ((), ())), preferred_element_type=F32)


def _split_bf16(x):
    hi = x.astype(BF16)
    lo = (x - hi.astype(F32)).astype(BF16)
    return hi, lo


def _sigmoid(x):
    return 1.0 / (1.0 + jnp.exp(-x))


def _silu(x):
    return x * _sigmoid(x)


def _lane_lo(shape):
    return lax.broadcasted_iota(I32, shape, len(shape) - 1) < HEAD_DIM


def _pair_sum(x, lo):
    s_lo = jnp.sum(jnp.where(lo, x, 0.0), axis=-1, keepdims=True)
    s_hi = jnp.sum(jnp.where(lo, 0.0, x), axis=-1, keepdims=True)
    return jnp.where(lo, s_lo, s_hi)


def _pair_rmsnorm(x, g, lo):
    ms = _pair_sum(x * x, lo) * (1.0 / HEAD_DIM)
    return x * lax.rsqrt(ms + EPS) * g


def _rope_pair(x, cos, sin_signed):
    lane = lax.broadcasted_iota(I32, x.shape, 1)
    first = (lane & 32) == 0
    partner = jnp.where(first, pltpu.roll(x, 96, 1), pltpu.roll(x, 32, 1))
    return x * cos + partner * sin_signed


def _params(sem):
    return pltpu.CompilerParams(dimension_semantics=sem, vmem_limit_bytes=VMEM_LIMIT)


def _inproj0_kernel(x_ref, g_ref, w_ref, cos_ref, sin_ref, qg_ref, kg_ref,
                    ret_ref, sbq_ref, sbk_ref, sbv_ref, sbg_ref, *t_refs):
    x = x_ref[...]
    xn = x * lax.rsqrt(jnp.mean(x * x, axis=-1, keepdims=True) + EPS) * g_ref[...]
    xb = xn.astype(BF16)
    cos = cos_ref[...]
    sin = sin_ref[...]
    lo = _lane_lo((1, LANES))
    width = H_RET * HEAD_DIM

    def group(i):
        return _dot(xb, w_ref[:, i * width:(i + 1) * width])

    def tiles(a):
        return [a[:, t * LANES:(t + 1) * LANES] for t in range(width // LANES)]

    for t, v in enumerate(tiles(group(0))):
        ret_ref[:, t * LANES:(t + 1) * LANES] = _rope_pair(v, cos, sin)
    for t, v in enumerate(tiles(group(1))):
        ret_ref[:, width + t * LANES:width + (t + 1) * LANES] = _rope_pair(v, cos, sin) * SCALE
    ret_ref[:, 2 * width:3 * width] = group(2)
    ret_ref[:, 3 * width:4 * width] = group(3)
    for t, v in enumerate(tiles(group(4))):
        sbq_ref[:, t * LANES:(t + 1) * LANES] = (_pair_rmsnorm(v, qg_ref[...], lo) * SCALE).astype(BF16)
    for t, v in enumerate(tiles(group(5))):
        kn = _pair_rmsnorm(v, kg_ref[...], lo)
        sbk_ref[:, t * LANES:(t + 1) * LANES] = kn
        if t_refs:
            t_refs[0][:, t * LANES:(t + 1) * LANES] = kn.astype(BF16)
    sv = group(6)
    sbv_ref[...] = sv
    if t_refs:
        for r in range(sv.shape[0] // LANES):
            for t in range(width // LANES):
                t_refs[1][r, t] = sv[r * LANES:(r + 1) * LANES, t * LANES:(t + 1) * LANES].T.astype(BF16)
    sbg_ref[...] = _silu(group(7))


def _inproj0(x2d, norm_g, w_bf, cos, sin, qg, kg, tm, n_pos_blocks, with_attention_copies):
    rows, d = x2d.shape
    width = H_RET * HEAD_DIM
    n = w_bf.shape[1]
    grid = (rows // tm,)
    row = lambda i: (i, 0)
    fixed = lambda i: (0, 0)
    pos = lambda i: (i % n_pos_blocks, 0)
    extra_specs, extra_shapes = [], []
    if with_attention_copies:
        npair = width // LANES
        extra_specs = [pl.BlockSpec((tm, width), row),
                       pl.BlockSpec((tm // LANES, npair, LANES, LANES), lambda i: (i, 0, 0, 0))]
        extra_shapes = [jax.ShapeDtypeStruct((rows, width), BF16),
                        jax.ShapeDtypeStruct((rows // LANES, npair, LANES, LANES), BF16)]
    return pl.pallas_call(
        _inproj0_kernel,
        grid=grid,
        in_specs=[
            pl.BlockSpec((tm, d), row),
            pl.BlockSpec((1, d), fixed),
            pl.BlockSpec((d, n), fixed),
            pl.BlockSpec((tm, LANES), pos),
            pl.BlockSpec((tm, LANES), pos),
            pl.BlockSpec((1, LANES), fixed),
            pl.BlockSpec((1, LANES), fixed),
        ],
        out_specs=[
            pl.BlockSpec((tm, 4 * width), row),
            pl.BlockSpec((tm, width), row),
            pl.BlockSpec((tm, width), row),
            pl.BlockSpec((tm, width), row),
            pl.BlockSpec((tm, width), row),
        ] + extra_specs,
        out_shape=[
            jax.ShapeDtypeStruct((rows, 4 * width), F32),
            jax.ShapeDtypeStruct((rows, width), BF16),
            jax.ShapeDtypeStruct((rows, width), F32),
            jax.ShapeDtypeStruct((rows, width), F32),
            jax.ShapeDtypeStruct((rows, width), F32),
        ] + extra_shapes,
        compiler_params=_params(("parallel",)),
        name="inproj0",
    )(x2d, norm_g.reshape(1, d), w_bf, cos, sin, qg, kg)


def _ret_prompt_kernel(q_ref, k_ref, v_ref, g_ref, inner_ref, qdec_ref, kdec_ref, cdec_ref, gn_ref,
                       mix_ref, state_ref, s_scr):
    c = pl.program_id(2)

    @pl.when(c == 0)
    def _():
        s_scr[...] = jnp.zeros_like(s_scr)

    q = q_ref[...]
    k = k_ref[...]
    vb = v_ref[...].astype(BF16)
    qb = q.astype(BF16)
    kb = k.astype(BF16)
    lo = _lane_lo(q.shape)
    zero = jnp.zeros_like(qb)
    att_lo = _dot_nt(jnp.where(lo, qb, zero), kb) * inner_ref[0]
    att_hi = _dot_nt(jnp.where(lo, zero, qb), kb) * inner_ref[1]
    o = jnp.where(lo, _dot(att_lo.astype(BF16), vb), _dot(att_hi.astype(BF16), vb))
    s = s_scr[...]
    o = o + _dot(qb, s.astype(BF16)) * qdec_ref[0]
    kd = (k * kdec_ref[0]).astype(BF16)
    ri = lax.broadcasted_iota(I32, s.shape, 0) < HEAD_DIM
    ci = lax.broadcasted_iota(I32, s.shape, 1) < HEAD_DIM
    s_new = s * cdec_ref[0] + jnp.where(ri == ci, _dot_tn(kd, vb), 0.0)
    s_scr[...] = s_new
    state_ref[0, 0] = s_new
    mu = _pair_sum(o, lo) * (1.0 / HEAD_DIM)
    oc = o - mu
    var = _pair_sum(oc * oc, lo) * (1.0 / HEAD_DIM)
    on = oc * lax.rsqrt(var + EPS) * gn_ref[0]
    mix_ref[...] = on * _silu(g_ref[...])


def _ret_tables(chunk):
    h = jnp.arange(H_RET, dtype=F32)
    lg = jnp.log1p(-jnp.exp2(-5.0 - h))
    idx = jnp.arange(chunk, dtype=F32)
    diff = idx[:, None] - idx[None, :]
    inner = jnp.where(diff >= 0, jnp.exp(lg[:, None, None] * jnp.maximum(diff, 0.0)), 0.0)
    q_dec = jnp.exp(lg[:, None] * (idx + 1.0))
    k_dec = jnp.exp(lg[:, None] * (chunk - 1.0 - idx))
    c_dec = jnp.exp(lg * chunk)

    def lanes(t):
        t = jnp.repeat(t[:, :, None], HEAD_DIM, axis=2)
        return jnp.concatenate([t[0::2], t[1::2]], axis=2)

    cd = jnp.repeat(c_dec[:, None, None], HEAD_DIM, axis=2)
    cd = jnp.concatenate([cd[0::2], cd[1::2]], axis=2)
    return inner, lanes(q_dec), lanes(k_dec), cd


def _ret_prompt(ret, gn_g, batch, seq):
    width = H_RET * HEAD_DIM
    npair = H_RET // 2
    c = RET_CHUNK
    nc = seq // c
    inner, qdec, kdec, cdec = _ret_tables(c)
    gn = gn_g.reshape(npair, 1, LANES)

    def col(off):
        return lambda b, p, i: (b * nc + i, off + p)

    tab = lambda b, p, i: (p, 0, 0)
    return pl.pallas_call(
        _ret_prompt_kernel,
        grid=(batch, npair, nc),
        in_specs=[
            pl.BlockSpec((c, LANES), col(0)),
            pl.BlockSpec((c, LANES), col(npair)),
            pl.BlockSpec((c, LANES), col(2 * npair)),
            pl.BlockSpec((c, LANES), col(3 * npair)),
            pl.BlockSpec((2, c, c), tab),
            pl.BlockSpec((1, c, LANES), tab),
            pl.BlockSpec((1, c, LANES), tab),
            pl.BlockSpec((1, 1, LANES), tab),
            pl.BlockSpec((1, 1, LANES), tab),
        ],
        out_specs=[
            pl.BlockSpec((c, LANES), lambda b, p, i: (b * nc + i, p)),
            pl.BlockSpec((1, 1, LANES, LANES), lambda b, p, i: (b, p, 0, 0)),
        ],
        out_shape=[
            jax.ShapeDtypeStruct((batch * seq, width), F32),
            jax.ShapeDtypeStruct((batch, npair, LANES, LANES), F32),
        ],
        scratch_shapes=[pltpu.VMEM((LANES, LANES), F32)],
        compiler_params=_params(("parallel", "parallel", "arbitrary")),
        name="retention_prompt",
    )(ret, ret, ret, ret, inner, qdec, kdec, cdec, gn)


def _unpair_state(sp):
    a = sp[:, :, :HEAD_DIM, :HEAD_DIM]
    b = sp[:, :, HEAD_DIM:, HEAD_DIM:]
    return jnp.stack([a, b], axis=2).reshape(sp.shape[0], -1, HEAD_DIM, HEAD_DIM)


SB_LOG_ZERO = -104.0


def _sb_prompt_kernel(q_ref, k_ref, vt_ref, g_ref, o_ref, acc_scr):
    i = pl.program_id(1)
    tq = q_ref.shape[0]
    npair = H_SB // 2
    rr = lax.broadcasted_iota(I32, (tq, tq), 0)
    cc = lax.broadcasted_iota(I32, (tq, tq), 1)
    lower = (cc >= rr).astype(BF16)
    diag = rr < cc
    zeros = jnp.zeros((HEAD_DIM, tq), F32)
    qts = []
    for p in range(npair):
        t = q_ref[:, p * LANES:(p + 1) * LANES].astype(F32).T
        qts.append(jnp.concatenate([t[:HEAD_DIM], zeros], axis=0).astype(BF16))
        qts.append(jnp.concatenate([zeros, t[HEAD_DIM:]], axis=0).astype(BF16))

    def sweep(j, rs, diag_mask):
        start = pl.multiple_of(j * tq, tq)
        kjs = [k_ref[pl.ds(start, tq), p * LANES:(p + 1) * LANES] for p in range(npair)]
        vts = [vt_ref[j, p] for p in range(npair)]
        zs = [_dot(kjs[h // 2], qt) for h, qt in enumerate(qts)]
        ls_pos, lms, parts = [], [], []
        for z in zs:
            lsp = jnp.minimum(z, 0.0) - jnp.log1p(jnp.exp(-jnp.abs(z)))
            lm = lsp - z
            if diag_mask is not None:
                lm = jnp.where(diag_mask, lm, 0.0)
            ls_pos.append(lsp)
            lms.append(lm)
            parts.append(_split_bf16(lm))
        s_incl = [_dot(lower, hi) + _dot(lower, lo_part) for hi, lo_part in parts]
        probs = []
        for h in range(H_SB):
            a = jnp.exp(ls_pos[h] + (s_incl[h] - lms[h]) + rs[h])
            if diag_mask is not None:
                a = jnp.where(diag_mask, a, 0.0)
            probs.append(a.astype(BF16))
        for h in range(H_SB):
            half = h % 2
            pv = _dot(vts[h // 2][half * HEAD_DIM:(half + 1) * HEAD_DIM, :], probs[h])
            acc_scr[h] = pv if diag_mask is not None else acc_scr[h] + pv
        return [rs[h] + s_incl[h][0:1, :] for h in range(H_SB)]

    def alive_of(rs):
        worst = rs[0]
        for r in rs[1:]:
            worst = jnp.maximum(worst, r)
        return jnp.max(worst) > SB_LOG_ZERO

    rs = sweep(i, [jnp.zeros((1, tq), F32)] * H_SB, diag)

    def cond(carry):
        t, alive = carry[0], carry[1]
        return (t < i) & alive

    def body(carry):
        t = carry[0]
        rs = sweep(i - 1 - t, list(carry[2:]), None)
        return (t + 1, alive_of(rs)) + tuple(rs)

    lax.while_loop(cond, body, (jnp.int32(0), alive_of(rs)) + tuple(rs))
    for p in range(npair):
        both = jnp.concatenate([acc_scr[2 * p], acc_scr[2 * p + 1]], axis=0).T
        o_ref[:, p * LANES:(p + 1) * LANES] = both * g_ref[:, p * LANES:(p + 1) * LANES]


def _sb_prompt(sbq, sbk_b, sbvt, sbg, batch, seq):
    width = H_SB * HEAD_DIM
    npair = H_SB // 2
    tq = Q_BLOCK
    nq = seq // tq
    qmap = lambda b, i: (b * nq + i, 0)
    return pl.pallas_call(
        _sb_prompt_kernel,
        grid=(batch, nq),
        in_specs=[
            pl.BlockSpec((tq, width), qmap),
            pl.BlockSpec((seq, width), lambda b, i: (b, 0)),
            pl.BlockSpec((nq, npair, LANES, tq), lambda b, i: (b, 0, 0, 0)),
            pl.BlockSpec((tq, width), qmap),
        ],
        out_specs=pl.BlockSpec((tq, width), qmap),
        out_shape=jax.ShapeDtypeStruct((batch * seq, width), F32),
        scratch_shapes=[pltpu.VMEM((H_SB, HEAD_DIM, tq), F32)],
        compiler_params=_params(("parallel", "arbitrary")),
        name="sb_prompt",
    )(sbq, sbk_b, sbvt, sbg)


def _outproj_kernel(*refs, n_mix):
    x_ref = refs[0]
    mix_refs = refs[1:1 + n_mix]
    w_refs = refs[1 + n_mix:1 + 2 * n_mix]
    y_ref = refs[1 + 2 * n_mix]
    acc = x_ref[...]
    for m_ref, w_ref in zip(mix_refs, w_refs):
        acc = acc + _dot(m_ref[...].astype(BF16), w_ref[...])
    y_ref[...] = acc


def _outproj(x2d, mixes, ws, tm):
    rows, d = x2d.shape
    n_mix = len(mixes)
    row = lambda i: (i, 0)
    fixed = lambda i: (0, 0)
    in_specs = [pl.BlockSpec((tm, d), row)]
    in_specs += [pl.BlockSpec((tm, m.shape[1]), row) for m in mixes]
    in_specs += [pl.BlockSpec(w.shape, fixed) for w in ws]
    return pl.pallas_call(
        functools.partial(_outproj_kernel, n_mix=n_mix),
        grid=(rows // tm,),
        in_specs=in_specs,
        out_specs=pl.BlockSpec((tm, d), row),
        out_shape=jax.ShapeDtypeStruct((rows, d), F32),
        compiler_params=_params(("parallel",)),
        name="outproj",
    )(x2d, *mixes, *ws)


def _rope_tables(pos):
    half = HEAD_DIM // 2
    inv = ROPE_BASE ** (-jnp.arange(half, dtype=F32) / half)
    ang = pos.astype(F32)[:, None] * inv[None, :]
    cos, sin = jnp.cos(ang), jnp.sin(ang)
    cos128 = jnp.tile(cos, (1, LANES // half))
    sin128 = jnp.tile(jnp.concatenate([-sin, sin], axis=1), (1, LANES // HEAD_DIM))
    return cos128, sin128


def _pair_gain(g):
    return jnp.tile(g.astype(F32), LANES // HEAD_DIM).reshape(1, LANES)


def _layer0_prompt(x2d, batch, seq, norm_g, w_in_bf, ret_gn_g, sb_qn_g, sb_kn_g, w_out_bf, tm):
    cos, sin = _rope_tables(jnp.arange(seq, dtype=I32))
    ret, sbq, sbk, sbv, sbg, sbk_b, sbvt = _inproj0(x2d, norm_g, w_in_bf, cos, sin, _pair_gain(sb_qn_g),
                                                    _pair_gain(sb_kn_g), tm, seq // tm, True)
    mix_ret, state_pairs = _ret_prompt(ret, ret_gn_g, batch, seq)
    mix_sb = _sb_prompt(sbq, sbk_b, sbvt, sbg, batch, seq)
    half = w_out_bf.shape[0] // 2
    y = _outproj(x2d, [mix_ret, mix_sb], [w_out_bf[:half], w_out_bf[half:]], tm)
    return y, _unpair_state(state_pairs), sbk, sbv


NSA_W = H_NSA * HEAD_DIM
KV_W = G_NSA * HEAD_DIM
IN1_COLS = 2 * NSA_W + 6 * KV_W + LANES


def _inproj1_kernel(x_ref, g_ref, w_ref, qg_ref, skg_ref, wkg_ref,
                    q_ref, kc_ref, vc_ref, ks_ref, vs_ref, kw_ref, vw_ref, gate_ref, bg_ref, *t_refs):
    x = x_ref[...]
    xn = x * lax.rsqrt(jnp.mean(x * x, axis=-1, keepdims=True) + EPS) * g_ref[...]
    xb = xn.astype(BF16)
    lo = _lane_lo((1, LANES))
    a = _dot(xb, w_ref[:, 0:NSA_W])
    for t in range(NSA_W // LANES):
        v = a[:, t * LANES:(t + 1) * LANES]
        q_ref[:, t * LANES:(t + 1) * LANES] = (_pair_rmsnorm(v, qg_ref[...], lo) * SCALE).astype(BF16)
    a = _dot(xb, w_ref[:, NSA_W:NSA_W + 6 * KV_W])
    kc, vc, ks, vs, kw, vw = [a[:, t * KV_W:(t + 1) * KV_W] for t in range(6)]
    ks = _pair_rmsnorm(ks, skg_ref[...], lo)
    kw = _pair_rmsnorm(kw, wkg_ref[...], lo)
    kc_ref[...] = kc
    vc_ref[...] = vc
    ks_ref[...] = ks
    vs_ref[...] = vs
    kw_ref[...] = kw
    vw_ref[...] = vw
    if t_refs:
        ksb_ref, kwb_ref, vst_ref, vwt_ref = t_refs
        ksb_ref[...] = ks.astype(BF16)
        kwb_ref[...] = kw.astype(BF16)
        for src, dst in ((vs, vst_ref), (vw, vwt_ref)):
            for t in range(src.shape[0] // LANES):
                dst[t] = src[t * LANES:(t + 1) * LANES, :].T.astype(BF16)
    off = NSA_W + 6 * KV_W
    gate_ref[...] = _silu(_dot(xb, w_ref[:, off:off + NSA_W]))
    bg_ref[...] = _sigmoid(_dot(xb, w_ref[:, off + NSA_W:off + NSA_W + LANES]))


def _inproj1(x2d, norm_g, w_bf, qg, skg, wkg, tm, with_attention_copies):
    rows, d = x2d.shape
    row = lambda i: (i, 0)
    fixed = lambda i: (0, 0)
    kv = jax.ShapeDtypeStruct((rows, KV_W), F32)
    out_specs = ([pl.BlockSpec((tm, NSA_W), row)] + [pl.BlockSpec((tm, KV_W), row)] * 6
                 + [pl.BlockSpec((tm, NSA_W), row), pl.BlockSpec((tm, LANES), row)])
    out_shape = ([jax.ShapeDtypeStruct((rows, NSA_W), BF16)] + [kv] * 6
                 + [jax.ShapeDtypeStruct((rows, NSA_W), F32), jax.ShapeDtypeStruct((rows, LANES), F32)])
    if with_attention_copies:
        tiles = tm // LANES
        out_specs += [pl.BlockSpec((tm, KV_W), row)] * 2 + [pl.BlockSpec((tiles, LANES, LANES), lambda i: (i, 0, 0))] * 2
        out_shape += ([jax.ShapeDtypeStruct((rows, KV_W), BF16)] * 2
                      + [jax.ShapeDtypeStruct((rows // LANES, LANES, LANES), BF16)] * 2)
    return pl.pallas_call(
        _inproj1_kernel,
        grid=(rows // tm,),
        in_specs=[
            pl.BlockSpec((tm, d), row),
            pl.BlockSpec((1, d), fixed),
            pl.BlockSpec((d, IN1_COLS), fixed),
            pl.BlockSpec((1, LANES), fixed),
            pl.BlockSpec((1, LANES), fixed),
            pl.BlockSpec((1, LANES), fixed),
        ],
        out_specs=out_specs,
        out_shape=out_shape,
        compiler_params=_params(("parallel",)),
        name="inproj1",
    )(x2d, norm_g.reshape(1, d), w_bf, qg, skg, wkg)


def _pad_w_in1(w_in1):
    pad = IN1_COLS - w_in1.shape[1]
    return jnp.pad(w_in1, ((0, 0), (0, pad))).astype(BF16)


def _compress_rows(t_ref, pe_ref, w_ref, row0, nblk):
    acc = jnp.zeros((nblk, LANES), F32)
    for l in range(CMP_BLOCK):
        rows = t_ref[pl.ds(row0 + l, nblk, stride=CMP_BLOCK), :]
        acc = acc + _dot((rows + pe_ref[l:l + 1, :]).astype(BF16), w_ref[l])
    return acc


def _compress_prompt_kernel(kc_ref, vc_ref, pe_ref, wk_ref, wv_ref, kg_ref, ck_ref, cvt_ref):
    nblk = ck_ref.shape[0]
    lo = _lane_lo((1, LANES))
    ck = _pair_rmsnorm(_compress_rows(kc_ref, pe_ref, wk_ref, 0, nblk), kg_ref[...], lo)
    cv = _compress_rows(vc_ref, pe_ref, wv_ref, 0, nblk)
    ck_ref[...] = ck.astype(BF16)
    cvt_ref[0] = cv.T.astype(BF16)


def _blockdiag_w(w):
    z = jnp.zeros_like(w)
    top = jnp.concatenate([w, z], axis=2)
    bot = jnp.concatenate([z, w], axis=2)
    return jnp.concatenate([top, bot], axis=1).astype(BF16)


def _compress_prompt(kc, vc, pe128, wk_bd, wv_bd, kg, batch, seq):
    nblk = seq // CMP_BLOCK
    rowb = lambda b: (b, 0)
    fixed2 = lambda b: (0, 0)
    fixed3 = lambda b: (0, 0, 0)
    return pl.pallas_call(
        _compress_prompt_kernel,
        grid=(batch,),
        in_specs=[
            pl.BlockSpec((seq, LANES), rowb),
            pl.BlockSpec((seq, LANES), rowb),
            pl.BlockSpec((CMP_BLOCK, LANES), fixed2),
            pl.BlockSpec((CMP_BLOCK, LANES, LANES), fixed3),
            pl.BlockSpec((CMP_BLOCK, LANES, LANES), fixed3),
            pl.BlockSpec((1, LANES), fixed2),
        ],
        out_specs=[pl.BlockSpec((nblk, LANES), rowb), pl.BlockSpec((1, LANES, nblk), lambda b: (b, 0, 0))],
        out_shape=[jax.ShapeDtypeStruct((batch * nblk, LANES), BF16),
                   jax.ShapeDtypeStruct((batch, LANES, nblk), BF16)],
        compiler_params=_params(("parallel",)),
        name="compress_prompt",
    )(kc, vc, pe128, wk_bd, wv_bd, kg)


def _nsa_prompt_kernel(q_ref, ck_ref, cvt_ref, ks_ref, vst_ref, kw_ref, vwt_ref, bg_ref, gate_ref,
                       o_ref, m_scr, l_scr, acc_scr, imp_scr, score_scr, chosen_scr):
    i = pl.program_id(2)
    g = pl.program_id(1)
    tq = q_ref.shape[0]
    nb = ck_ref.shape[0]
    npair = HPG // 2
    ratio = SEL_BLOCK // CMP_BLOCK
    nsb = nb // ratio
    q_pos = i * tq + lax.broadcasted_iota(I32, (1, tq), 1)
    g_row = pl.multiple_of(g * HEAD_DIM, HEAD_DIM)

    zeros = jnp.zeros((HEAD_DIM, tq), F32)
    qts = []
    for p in range(npair):
        t = q_ref[:, p * LANES:(p + 1) * LANES].astype(F32).T
        for ht in (t[:HEAD_DIM], t[HEAD_DIM:]):
            in_g0 = jnp.concatenate([ht, zeros], axis=0)
            in_g1 = jnp.concatenate([zeros, ht], axis=0)
            qts.append(jnp.where(g == 0, in_g0, in_g1).astype(BF16))

    ck = ck_ref[...]
    cvt = cvt_ref[0, pl.ds(g_row, HEAD_DIM), :]
    blk_n = lax.broadcasted_iota(I32, (nb, tq), 0)
    cmask = blk_n * CMP_BLOCK + (CMP_BLOCK - 1) <= q_pos
    imp = jnp.zeros((nb, tq), F32)
    cmp_scores = [_dot(ck, qt) for qt in qts]
    cmp_probs = []
    for s in cmp_scores:
        s = jnp.where(cmask, s, NEG_INF)
        m = jnp.max(s, axis=0, keepdims=True)
        e = jnp.where(cmask, jnp.exp(s - m), 0.0)
        den = jnp.sum(e, axis=0, keepdims=True)
        pr = e / jnp.where(den > 0.0, den, 1.0)
        imp = imp + pr
        cmp_probs.append(pr.astype(BF16))
    o_cmp = [_dot(cvt, pr) for pr in cmp_probs]

    imp_scr[...] = imp
    pooled = imp_scr[pl.ds(0, nsb, stride=ratio), :]
    for r in range(1, ratio):
        pooled = pooled + imp_scr[pl.ds(r, nsb, stride=ratio), :]
    sblk = lax.broadcasted_iota(I32, (nsb, tq), 0)
    cur = q_pos // SEL_BLOCK
    forced = (sblk == 0) | (sblk == cur) | (sblk == cur - 1)
    score = jnp.where(sblk <= cur, pooled + FORCE_BONUS * forced.astype(F32), NEG_INF)
    score_scr[...] = score
    cnt = jnp.zeros((nsb, tq), F32)
    for mblk in range(nsb):
        c = score_scr[mblk:mblk + 1, :]
        ahead = (c > score) | ((c == score) & (mblk < sblk))
        cnt = cnt + ahead.astype(F32)
    chosen = (cnt < float(TOP_N)).astype(F32)
    for mblk in range(nsb):
        chosen_scr[mblk] = jnp.broadcast_to(chosen[mblk:mblk + 1, :], (8, tq))

    k_in = lax.broadcasted_iota(I32, (tq, tq), 0)
    blocks_per_tile = tq // SEL_BLOCK

    def flash(k_ref, vt_ref, slot, n_tiles, mask_fn):
        m_scr[slot] = jnp.full(m_scr.shape[1:], NEG_INF, F32)
        l_scr[slot] = jnp.zeros(l_scr.shape[1:], F32)
        acc_scr[slot] = jnp.zeros(acc_scr.shape[1:], F32)

        def body(t, carry):
            j = i - t
            kj = k_ref[pl.ds(pl.multiple_of(j * tq, tq), tq), :]
            vt = vt_ref[j, pl.ds(g_row, HEAD_DIM), :]
            mask = mask_fn(j)
            scores = [_dot(kj, qt) for qt in qts]
            alphas, probs = [], []
            for h, s in enumerate(scores):
                s = jnp.where(mask, s, NEG_INF)
                m_old = m_scr[slot, h]
                m_new = jnp.maximum(m_old, jnp.max(s, axis=0, keepdims=True))
                alpha = jnp.exp(m_old - m_new)
                p = jnp.exp(s - m_new[0:1])
                l_scr[slot, h] = alpha * l_scr[slot, h] + jnp.sum(p, axis=0, keepdims=True)
                m_scr[slot, h] = m_new
                alphas.append(alpha[0:1])
                probs.append(p.astype(BF16))
            for h in range(HPG):
                acc_scr[slot, h] = alphas[h] * acc_scr[slot, h] + _dot(vt, probs[h])
            return carry

        lax.fori_loop(0, n_tiles, body, 0)
        return [acc_scr[slot, h] / l_scr[slot, h][0:1] for h in range(HPG)]

    def sel_mask(j):
        rows = [jnp.broadcast_to(chosen_scr[blocks_per_tile * j + r][0:1], (SEL_BLOCK, tq))
                for r in range(blocks_per_tile)]
        picked = jnp.concatenate(rows, axis=0) > 0.5
        return picked & (j * tq + k_in <= q_pos)

    def win_mask(j):
        k_pos = j * tq + k_in
        return (k_pos <= q_pos) & (k_pos > q_pos - WINDOW)

    o_sel = flash(ks_ref, vst_ref, 0, i + 1, sel_mask)
    o_win = flash(kw_ref, vwt_ref, 1, jnp.minimum(i, WINDOW // tq) + 1, win_mask)

    bgt = bg_ref[...].T

    def gate_row(h, br):
        c = h * 3 + br
        n = HPG * 3
        return jnp.where(g == 0, bgt[c:c + 1, :], bgt[n + c:n + c + 1, :])

    for p in range(npair):
        halves = []
        for h in (2 * p, 2 * p + 1):
            halves.append(gate_row(h, 0) * o_cmp[h] + gate_row(h, 1) * o_sel[h] + gate_row(h, 2) * o_win[h])
        mixed = jnp.concatenate(halves, axis=0).T
        o_ref[:, p * LANES:(p + 1) * LANES] = mixed * gate_ref[:, p * LANES:(p + 1) * LANES]


def _nsa_prompt(qn, ck, cvt, ksb, vst, kwb, vwt, bg, gate, batch, seq):
    tq = Q_BLOCK
    nq = seq // tq
    nb = seq // CMP_BLOCK
    nsb = nb // (SEL_BLOCK // CMP_BLOCK)
    gw = HPG * HEAD_DIM
    qmap = lambda b, g, i: (b * nq + i, g)
    per_b2 = lambda b, g, i: (b, 0)
    per_b3 = lambda b, g, i: (b, 0, 0)
    return pl.pallas_call(
        _nsa_prompt_kernel,
        grid=(batch, G_NSA, nq),
        in_specs=[
            pl.BlockSpec((tq, gw), qmap),
            pl.BlockSpec((nb, LANES), per_b2),
            pl.BlockSpec((1, LANES, nb), per_b3),
            pl.BlockSpec((seq, LANES), per_b2),
            pl.BlockSpec((nq, LANES, tq), per_b3),
            pl.BlockSpec((seq, LANES), per_b2),
            pl.BlockSpec((nq, LANES, tq), per_b3),
            pl.BlockSpec((tq, LANES), lambda b, g, i: (b * nq + i, 0)),
            pl.BlockSpec((tq, gw), qmap),
        ],
        out_specs=pl.BlockSpec((tq, gw), qmap),
        out_shape=jax.ShapeDtypeStruct((batch * seq, NSA_W), F32),
        scratch_shapes=[
            pltpu.VMEM((2, HPG, 8, tq), F32),
            pltpu.VMEM((2, HPG, 8, tq), F32),
            pltpu.VMEM((2, HPG, HEAD_DIM, tq), F32),
            pltpu.VMEM((nb, tq), F32),
            pltpu.VMEM((nsb, tq), F32),
            pltpu.VMEM((nsb, 8, tq), F32),
        ],
        compiler_params=_params(("parallel", "parallel", "arbitrary")),
        name="nsa_prompt",
    )(qn, ck, cvt, ksb, vst, kwb, vwt, bg, gate)


def _layer1_prompt(x2d, batch, seq, norm_g, w_in_bf, qn_g, cmp_kn_g, sel_kn_g, win_kn_g, pe128, wk_bd, wv_bd,
                   w_out_bf, tm):
    (qn, kc, vc, ks, vs, kw, vw, gate, bg, ksb, kwb, vst, vwt) = _inproj1(
        x2d, norm_g, w_in_bf, _pair_gain(qn_g), _pair_gain(sel_kn_g), _pair_gain(win_kn_g), tm, True)
    ck, cvt = _compress_prompt(kc, vc, pe128, wk_bd, wv_bd, _pair_gain(cmp_kn_g), batch, seq)
    mix = _nsa_prompt(qn, ck, cvt, ksb, vst, kwb, vwt, bg, gate, batch, seq)
    y = _outproj(x2d, [mix], [w_out_bf], tm)
    return y, kc, vc, ks, vs, kw, vw


def _ret_sample_kernel(ret_ref, s_ref, qdec_ref, cdec_ref, gn_ref, mix_ref, so_ref):
    width = H_RET * HEAD_DIM
    rowi = lax.broadcasted_iota(I32, (8, HEAD_DIM), 0)
    eye = (lax.broadcasted_iota(I32, (HEAD_DIM, HEAD_DIM), 0)
           == lax.broadcasted_iota(I32, (HEAD_DIM, HEAD_DIM), 1)).astype(BF16)
    row = ret_ref[0]
    for h in range(H_RET):
        sl = slice(h * HEAD_DIM, (h + 1) * HEAD_DIM)
        qb = row[:, sl].astype(BF16)
        kb = row[:, width + h * HEAD_DIM:width + (h + 1) * HEAD_DIM].astype(BF16)
        vb = row[:, 2 * width + h * HEAD_DIM:2 * width + (h + 1) * HEAD_DIM].astype(BF16)
        gate = row[:, 3 * width + h * HEAD_DIM:3 * width + (h + 1) * HEAD_DIM]
        s = s_ref[0, h]
        qs = _dot(jnp.broadcast_to(qb.astype(F32), (8, HEAD_DIM)).astype(BF16), s.astype(BF16))[0:1]
        att = jnp.sum(qb.astype(F32) * kb.astype(F32), axis=-1, keepdims=True)
        o = att.astype(BF16).astype(F32) * vb.astype(F32) + qs * qdec_ref[h:h + 1, :]
        k8 = jnp.where(rowi == 0, jnp.broadcast_to(kb.astype(F32), (8, HEAD_DIM)), 0.0).astype(BF16)
        kcol = _dot_nt(eye, k8)[:, 0:1]
        so_ref[0, h] = s * cdec_ref[h:h + 1, :] + kcol * vb.astype(F32)
        mu = jnp.mean(o, axis=-1, keepdims=True)
        oc = o - mu
        var = jnp.mean(oc * oc, axis=-1, keepdims=True)
        on = oc * lax.rsqrt(var + EPS) * gn_ref[h:h + 1, :]
        mix_ref[0, :, sl] = on * _silu(gate)


def _ret_sample(ret, state, gn_g):
    nb, cols = ret.shape
    width = H_RET * HEAD_DIM
    h = jnp.arange(H_RET, dtype=F32)
    lg = jnp.log1p(-jnp.exp2(-5.0 - h))
    qdec = jnp.broadcast_to(jnp.exp(lg * 1.0)[:, None], (H_RET, HEAD_DIM))
    cdec = jnp.broadcast_to(jnp.exp(lg * 1.0)[:, None], (H_RET, HEAD_DIM))
    m3 = lambda b: (b, 0, 0)
    tab = pl.BlockSpec((H_RET, HEAD_DIM), lambda b: (0, 0))
    st = pl.BlockSpec((1, H_RET, HEAD_DIM, HEAD_DIM), lambda b: (b, 0, 0, 0))
    mix, state_new = pl.pallas_call(
        _ret_sample_kernel,
        grid=(nb,),
        in_specs=[pl.BlockSpec((1, 1, cols), m3), st, tab, tab, tab],
        out_specs=[pl.BlockSpec((1, 1, width), m3), st],
        out_shape=[jax.ShapeDtypeStruct((nb, 1, width), F32),
                   jax.ShapeDtypeStruct(state.shape, F32)],
        compiler_params=_params(("parallel",)),
        name="retention_sample",
    )(ret.reshape(nb, 1, cols), state, qdec, cdec, gn_g.astype(F32))
    return mix.reshape(nb, width), state_new


SB_PAGES_PER_STEP = 4


def _sb_sample_kernel(pt_ref, q_ref, g_ref, *refs, n_pp):
    k_refs = refs[:n_pp]
    v_refs = refs[n_pp:2 * n_pp]
    o_ref = refs[2 * n_pp]
    r_scr, acc_scr = refs[2 * n_pp + 1:]
    j = pl.program_id(1)

    @pl.when(j == 0)
    def _():
        r_scr[...] = jnp.zeros_like(r_scr)
        acc_scr[...] = jnp.zeros_like(acc_scr)

    q = q_ref[0].astype(F32)
    page = PAGE_SIZE
    rowq = lax.broadcasted_iota(I32, (H_SB, HEAD_DIM), 0)
    rowa = lax.broadcasted_iota(I32, (H_SB, page), 0)
    upper = (lax.broadcasted_iota(I32, (page, page), 0) >= lax.broadcasted_iota(I32, (page, page), 1)).astype(BF16)
    r = r_scr[:, 0:1]
    acc = acc_scr[...]
    for kref, vref in zip(k_refs, v_refs):
        z = jnp.zeros((H_SB, page), F32)
        for h in range(H_SB):
            kh = kref[pl.ds(h, page, stride=H_SB), :].astype(BF16)
            z = z + _dot_nt(jnp.where(rowq == h, q, 0.0).astype(BF16), kh)
        l1p = jnp.log1p(jnp.exp(-jnp.abs(z)))
        ls_pos = jnp.minimum(z, 0.0) - l1p
        lm = ls_pos - z
        hi, lo_part = _split_bf16(lm)
        s_incl = _dot(hi, upper) + _dot(lo_part, upper)
        a = jnp.exp(ls_pos + (s_incl - lm) + r)
        for h in range(H_SB):
            vh = vref[pl.ds(h, page, stride=H_SB), :].astype(BF16)
            acc = acc + _dot(jnp.where(rowa == h, a, 0.0).astype(BF16), vh)
        r = r + s_incl[:, 0:1]
    r_scr[...] = jnp.broadcast_to(r, r_scr.shape)
    acc_scr[...] = acc

    @pl.when(j == pl.num_programs(1) - 1)
    def _():
        o_ref[0] = acc * g_ref[0]


def _sb_sample(q8, g8, cache_k, cache_v, page_table):
    nb = q8.shape[0]
    n_pages = page_table.shape[1]
    n_pp = SB_PAGES_PER_STEP
    rows = PAGE_SIZE * H_SB

    def page_map(r):
        return lambda b, j, pt: (pt[b, n_pages - 1 - (j * n_pp + r)], 0, 0)

    head = lambda b, j, pt: (b, 0, 0)
    grid_spec = pltpu.PrefetchScalarGridSpec(
        num_scalar_prefetch=1,
        grid=(nb, n_pages // n_pp),
        in_specs=[pl.BlockSpec((1, H_SB, HEAD_DIM), head), pl.BlockSpec((1, H_SB, HEAD_DIM), head)]
        + [pl.BlockSpec((None, rows, HEAD_DIM), page_map(r)) for r in range(n_pp)] * 2,
        out_specs=pl.BlockSpec((1, H_SB, HEAD_DIM), head),
        scratch_shapes=[pltpu.VMEM((H_SB, LANES), F32), pltpu.VMEM((H_SB, HEAD_DIM), F32)],
    )
    return pl.pallas_call(
        functools.partial(_sb_sample_kernel, n_pp=n_pp),
        grid_spec=grid_spec,
        out_shape=jax.ShapeDtypeStruct((nb, H_SB, HEAD_DIM), F32),
        compiler_params=_params(("parallel", "arbitrary")),
        name="sb_sample",
    )(page_table, q8, g8, *([cache_k] * n_pp), *([cache_v] * n_pp))


CMP_PAGES_PER_STEP = 4


def _compress_sample_kernel(pt_ref, *refs, n_pp):
    k_refs = refs[:n_pp]
    v_refs = refs[n_pp:2 * n_pp]
    pe_ref, wk_ref, wv_ref, kg_ref, ck_ref, cv_ref, kbuf, vbuf = refs[2 * n_pp:]
    j = pl.program_id(1)
    for r in range(n_pp):
        kbuf[r * PAGE_SIZE:(r + 1) * PAGE_SIZE, :] = k_refs[r][0]
        vbuf[r * PAGE_SIZE:(r + 1) * PAGE_SIZE, :] = v_refs[r][0]
    nblk = n_pp * PAGE_SIZE // CMP_BLOCK
    lo = _lane_lo((1, LANES))
    start = pl.multiple_of(j * nblk, nblk)
    ck_ref[0, pl.ds(start, nblk), :] = _pair_rmsnorm(_compress_rows(kbuf, pe_ref, wk_ref, 0, nblk), kg_ref[...], lo)
    cv_ref[0, pl.ds(start, nblk), :] = _compress_rows(vbuf, pe_ref, wv_ref, 0, nblk)


def _compress_sample(cache_k, cache_v, page_table, pe128, wk_bd, wv_bd, kg):
    nb, n_pages = page_table.shape
    n_pp = CMP_PAGES_PER_STEP
    nblk_total = n_pages * PAGE_SIZE // CMP_BLOCK

    def page_map(r):
        return lambda b, j, pt: (pt[b, j * n_pp + r], 0, 0)

    fixed2 = lambda b, j, pt: (0, 0)
    fixed3 = lambda b, j, pt: (0, 0, 0)
    out_map = lambda b, j, pt: (b, 0, 0)
    grid_spec = pltpu.PrefetchScalarGridSpec(
        num_scalar_prefetch=1,
        grid=(nb, n_pages // n_pp),
        in_specs=[pl.BlockSpec((1, PAGE_SIZE, LANES), page_map(r)) for r in range(n_pp)] * 2
        + [pl.BlockSpec((CMP_BLOCK, LANES), fixed2),
           pl.BlockSpec((CMP_BLOCK, LANES, LANES), fixed3),
           pl.BlockSpec((CMP_BLOCK, LANES, LANES), fixed3),
           pl.BlockSpec((1, LANES), fixed2)],
        out_specs=[pl.BlockSpec((1, nblk_total, LANES), out_map)] * 2,
        scratch_shapes=[pltpu.VMEM((n_pp * PAGE_SIZE, LANES), F32)] * 2,
    )
    o = jax.ShapeDtypeStruct((nb, nblk_total, LANES), F32)
    return pl.pallas_call(
        functools.partial(_compress_sample_kernel, n_pp=n_pp),
        grid_spec=grid_spec,
        out_shape=[o, o],
        compiler_params=_params(("parallel", "arbitrary")),
        name="compress_sample",
    )(page_table, *([cache_k] * n_pp), *([cache_v] * n_pp), pe128, wk_bd, wv_bd, kg)


def _group_q(q, g):
    z = jnp.zeros_like(q)
    return jnp.concatenate([q, z], axis=1) if g == 0 else jnp.concatenate([z, q], axis=1)


def _nsa_sample_select_kernel(q_ref, ck_ref, cv_ref, ocmp_ref, idx_ref, *, q_pos):
    q = q_ref[0]
    ck = ck_ref[0].astype(BF16)
    cv = cv_ref[0].astype(BF16)
    nb = ck.shape[0]
    blk_n = lax.broadcasted_iota(I32, (1, nb), 1)
    cmask = blk_n * CMP_BLOCK + (CMP_BLOCK - 1) <= q_pos
    ratio = SEL_BLOCK // CMP_BLOCK
    sblk = blk_n // ratio
    cur = q_pos // SEL_BLOCK
    lane_f = blk_n.astype(F32)
    slot = lax.broadcasted_iota(I32, (1, LANES), 1)
    row8 = lax.broadcasted_iota(I32, (8, LANES), 0)
    idx_all = jnp.full((8, LANES), -1, I32)
    neg_inf = float("-inf")
    for g in range(G_NSA):
        qg = _group_q(q[g * HPG:(g + 1) * HPG], g)
        s = jnp.where(cmask, _dot_nt(qg, ck), NEG_INF)
        m = jnp.max(s, axis=-1, keepdims=True)
        e = jnp.where(cmask, jnp.exp(s - m), 0.0)
        den = jnp.sum(e, axis=-1, keepdims=True)
        pr = e / jnp.where(den > 0.0, den, 1.0)
        o = _dot(pr.astype(BF16), cv)
        ocmp_ref[0, g * HPG:(g + 1) * HPG, :] = o[:, g * HEAD_DIM:(g + 1) * HEAD_DIM]
        imp = jnp.sum(pr, axis=0, keepdims=True)
        even = (blk_n & 1) == 0
        pooled = imp + jnp.where(even, pltpu.roll(imp, nb - 1, 1), pltpu.roll(imp, 1, 1))
        forced = (sblk == 0) | (sblk == cur) | (sblk == cur - 1)
        score = jnp.where(sblk < cur, pooled + FORCE_BONUS * forced.astype(F32), neg_inf)
        idx_g = jnp.full((1, LANES), -1, I32)
        for t in range(TOP_N - 1):
            mx = jnp.max(score, axis=-1, keepdims=True)
            first = jnp.min(jnp.where(score == mx, lane_f, float(4 * nb)), axis=-1, keepdims=True)
            blk = first.astype(I32) // ratio
            idx_g = jnp.where((slot == t) & (mx > neg_inf), blk, idx_g)
            score = jnp.where(sblk == blk, neg_inf, score)
        idx_all = jnp.where(row8 == g, jnp.broadcast_to(idx_g, (8, LANES)), idx_all)
    idx_ref[0] = idx_all


def _nsa_sample_select(q16, ck, cv, q_pos):
    nb = q16.shape[0]
    nblk = ck.shape[1]
    m3 = lambda b: (b, 0, 0)
    return pl.pallas_call(
        functools.partial(_nsa_sample_select_kernel, q_pos=q_pos),
        grid=(nb,),
        in_specs=[pl.BlockSpec((1, H_NSA, HEAD_DIM), m3),
                  pl.BlockSpec((1, nblk, LANES), m3),
                  pl.BlockSpec((1, nblk, LANES), m3)],
        out_specs=[pl.BlockSpec((1, H_NSA, HEAD_DIM), m3), pl.BlockSpec((1, 8, LANES), m3)],
        out_shape=[jax.ShapeDtypeStruct((nb, H_NSA, HEAD_DIM), F32),
                   jax.ShapeDtypeStruct((nb, 8, LANES), I32)],
        compiler_params=_params(("parallel",)),
        name="nsa_sample_select",
    )(q16, ck, cv)


def _nsa_sample_attend_kernel(pt_ref, idx_ref, q_ref, k0_ref, k1_ref, v0_ref, v1_ref, kn_ref, vn_ref,
                              wk_ref, wv_ref, ocmp_ref, bg_ref, gate_ref, o_ref, m_scr, l_scr, acc_scr,
                              *, q_pos):
    b = pl.program_id(0)
    s = pl.program_id(1)

    @pl.when(s == 0)
    def _():
        m_scr[...] = jnp.full_like(m_scr, NEG_INF)
        l_scr[...] = jnp.zeros_like(l_scr)
        acc_scr[...] = jnp.zeros_like(acc_scr)

    q = q_ref[0]
    qgs = [_group_q(q[g * HPG:(g + 1) * HPG], g) for g in range(G_NSA)]
    for g, (kref, vref) in enumerate(((k0_ref, v0_ref), (k1_ref, v1_ref))):
        blk = idx_ref[b * G_NSA + g, s]
        valid = blk >= 0
        half = pl.multiple_of((jnp.maximum(blk, 0) % 2) * SEL_BLOCK, SEL_BLOCK)
        kb = kref[0, pl.ds(half, SEL_BLOCK), :].astype(BF16)
        vb = vref[0, pl.ds(half, SEL_BLOCK), :].astype(BF16)
        sc = jnp.where(valid, _dot_nt(qgs[g], kb), NEG_INF)
        m_old = m_scr[g][:, 0:1]
        m_new = jnp.maximum(m_old, jnp.max(sc, axis=-1, keepdims=True))
        alpha = jnp.exp(m_old - m_new)
        p = jnp.where(valid, jnp.exp(sc - m_new), 0.0)
        l_new = alpha * l_scr[g][:, 0:1] + jnp.sum(p, axis=-1, keepdims=True)
        acc_scr[g] = alpha * acc_scr[g] + _dot(p.astype(BF16), vb)
        m_scr[g] = jnp.broadcast_to(m_new, (HPG, LANES))
        l_scr[g] = jnp.broadcast_to(l_new, (HPG, LANES))

    @pl.when(s == pl.num_programs(1) - 1)
    def _():
        kn = kn_ref[0].astype(BF16).astype(F32)
        vn = vn_ref[0].astype(BF16).astype(F32)
        wk = wk_ref[0].astype(BF16)
        wv = wv_ref[0].astype(BF16)
        wlen = wk.shape[0]
        w_pos = q_pos - (wlen - 1) + lax.broadcasted_iota(I32, (1, wlen), 1)
        wmask = (w_pos <= q_pos) & (w_pos > q_pos - WINDOW) & (w_pos >= 0)
        bg = bg_ref[0]
        col = lax.broadcasted_iota(I32, (HPG, LANES), 1)
        hrow = lax.broadcasted_iota(I32, (HPG, LANES), 0)
        for g in range(G_NSA):
            qg = qgs[g]
            s_new = jnp.sum(qg.astype(F32) * kn, axis=-1, keepdims=True)
            m_old = m_scr[g][:, 0:1]
            m_new = jnp.maximum(m_old, s_new)
            alpha = jnp.exp(m_old - m_new)
            p = jnp.exp(s_new - m_new)
            l_new = alpha * l_scr[g][:, 0:1] + p
            acc = alpha * acc_scr[g] + p.astype(BF16).astype(F32) * vn
            o_sel = acc / l_new
            sw = jnp.where(wmask, _dot_nt(qg, wk), NEG_INF)
            mw = jnp.max(sw, axis=-1, keepdims=True)
            pw = jnp.where(wmask, jnp.exp(sw - mw), 0.0)
            o_win = _dot(pw.astype(BF16), wv) / jnp.sum(pw, axis=-1, keepdims=True)
            sl = slice(g * HEAD_DIM, (g + 1) * HEAD_DIM)
            gates = []
            for br in range(3):
                pick = col == g * HPG * 3 + hrow * 3 + br
                gates.append(jnp.sum(jnp.where(pick, jnp.broadcast_to(bg, (HPG, LANES)), 0.0), axis=-1, keepdims=True))
            rows = slice(g * HPG, (g + 1) * HPG)
            mixed = gates[0] * ocmp_ref[0, rows, :] + gates[1] * o_sel[:, sl] + gates[2] * o_win[:, sl]
            o_ref[0, rows, :] = mixed * gate_ref[0, rows, :]


def _nsa_sample_attend(page_table, idx, q16, sel_k, sel_v, ks_new, vs_new, wk_new, wv_new, ocmp, bg, gate16, q_pos):
    nb = q16.shape[0]
    n_slots = TOP_N - 1
    n_pages = page_table.shape[1]
    wlen = wk_new.shape[1]

    def page_map(g):
        def f(b, s, pt, ix):
            blk = jnp.maximum(ix[b * G_NSA + g, s], 0)
            return (pt[b, jnp.minimum(blk // 2, n_pages - 1)], 0, 0)
        return f

    m3 = lambda b, s, pt, ix: (b, 0, 0)
    grid_spec = pltpu.PrefetchScalarGridSpec(
        num_scalar_prefetch=2,
        grid=(nb, n_slots),
        in_specs=[pl.BlockSpec((1, H_NSA, HEAD_DIM), m3),
                  pl.BlockSpec((1, PAGE_SIZE, LANES), page_map(0)),
                  pl.BlockSpec((1, PAGE_SIZE, LANES), page_map(1)),
                  pl.BlockSpec((1, PAGE_SIZE, LANES), page_map(0)),
                  pl.BlockSpec((1, PAGE_SIZE, LANES), page_map(1)),
                  pl.BlockSpec((1, 1, LANES), m3),
                  pl.BlockSpec((1, 1, LANES), m3),
                  pl.BlockSpec((1, wlen, LANES), m3),
                  pl.BlockSpec((1, wlen, LANES), m3),
                  pl.BlockSpec((1, H_NSA, HEAD_DIM), m3),
                  pl.BlockSpec((1, 1, LANES), m3),
                  pl.BlockSpec((1, H_NSA, HEAD_DIM), m3)],
        out_specs=pl.BlockSpec((1, H_NSA, HEAD_DIM), m3),
        scratch_shapes=[pltpu.VMEM((G_NSA, HPG, LANES), F32)] * 3,
    )
    return pl.pallas_call(
        functools.partial(_nsa_sample_attend_kernel, q_pos=q_pos),
        grid_spec=grid_spec,
        out_shape=jax.ShapeDtypeStruct((nb, H_NSA, HEAD_DIM), F32),
        compiler_params=_params(("parallel", "arbitrary")),
        name="nsa_sample_attend",
    )(page_table, idx, q16, sel_k, sel_k, sel_v, sel_v, ks_new, vs_new, wk_new, wv_new, ocmp, bg, gate16)


def _layer0_sample(x2d, past_len, state, cache_k, cache_v, page_table, norm_g, w_in_bf, ret_gn_g, sb_qn_g, sb_kn_g,
                   w_out_bf):
    nb = x2d.shape[0]
    cos, sin = _rope_tables(jnp.full((nb,), past_len, I32))
    ret, sbq, sbk, sbv, sbg = _inproj0(x2d, norm_g, w_in_bf, cos, sin, _pair_gain(sb_qn_g), _pair_gain(sb_kn_g),
                                       nb, 1, False)
    mix_ret, state_new = _ret_sample(ret, state, ret_gn_g)
    n_phys = cache_k.shape[0]
    ck = cache_k.reshape(n_phys, PAGE_SIZE * H_SB, HEAD_DIM)
    cv = cache_v.reshape(n_phys, PAGE_SIZE * H_SB, HEAD_DIM)
    mix_sb = _sb_sample(sbq.reshape(nb, H_SB, HEAD_DIM), sbg.reshape(nb, H_SB, HEAD_DIM), ck, cv, page_table)
    mix_sb = mix_sb.reshape(nb, H_SB * HEAD_DIM)
    half = w_out_bf.shape[0] // 2
    y = _outproj(x2d, [mix_ret, mix_sb], [w_out_bf[:half], w_out_bf[half:]], nb)
    return y, state_new, sbk, sbv


def _layer1_sample(x2d, past_len, cmp_k, cmp_v, sel_k, sel_v, win_k, win_v, page_table, norm_g, w_in_bf, qn_g,
                   cmp_kn_g, sel_kn_g, win_kn_g, pe128, wk_bd, wv_bd, w_out_bf):
    nb = x2d.shape[0]
    (qn, kc, vc, ks, vs, kw, vw, gate, bg) = _inproj1(
        x2d, norm_g, w_in_bf, _pair_gain(qn_g), _pair_gain(sel_kn_g), _pair_gain(win_kn_g), nb, False)
    n_phys = cmp_k.shape[0]
    pool = lambda t: t.reshape(n_phys, PAGE_SIZE, LANES)
    ck, cv = _compress_sample(pool(cmp_k), pool(cmp_v), page_table, pe128, wk_bd, wv_bd, _pair_gain(cmp_kn_g))
    q16 = qn.reshape(nb, H_NSA, HEAD_DIM)
    ocmp, idx = _nsa_sample_select(q16, ck, cv, past_len)
    idx2 = idx[:, :G_NSA, :TOP_N].reshape(nb * G_NSA, TOP_N)
    wlen = win_k.shape[1]
    wk_new = jnp.concatenate([win_k.reshape(nb, wlen, LANES)[:, 1:], kw[:, None, :]], axis=1)
    wv_new = jnp.concatenate([win_v.reshape(nb, wlen, LANES)[:, 1:], vw[:, None, :]], axis=1)
    mix = _nsa_sample_attend(page_table, idx2, q16, pool(sel_k), pool(sel_v), ks[:, None, :], vs[:, None, :],
                             wk_new, wv_new, ocmp, bg[:, None, :], gate.reshape(nb, H_NSA, HEAD_DIM), past_len)
    y = _outproj(x2d, [mix.reshape(nb, NSA_W)], [w_out_bf], nb)
    return y, kc, vc, ks, vs, wk_new, wv_new


def kernel(x_prompt, x_sample, state_ret, cache_sb_k, cache_sb_v, cache_cmp_k, cache_cmp_v, cache_sel_k,
           cache_sel_v, cache_win_k, cache_win_v, page_table, norm0_g, w_in0, ret_gn_g, sb_qn_g, sb_kn_g,
           w_out0, norm1_g, w_in1, nsa_qn_g, cmp_kn_g, sel_kn_g, win_kn_g, cmp_pe, w_cmp_k, w_cmp_v, w_out1):
    batch, seq, d = x_prompt.shape
    nb = x_sample.shape[0]
    past_len = page_table.shape[1] * PAGE_SIZE
    tm = 256
    w_in0_bf = w_in0.astype(BF16)
    w_out0_bf = w_out0.astype(BF16)
    w_in1_bf = _pad_w_in1(w_in1)
    w_out1_bf = w_out1.astype(BF16)
    pe128 = jnp.tile(cmp_pe, (1, G_NSA))
    wk_bd = _blockdiag_w(w_cmp_k)
    wv_bd = _blockdiag_w(w_cmp_v)

    xp = x_prompt.reshape(batch * seq, d)
    y1p, ret_p, sbk_p, sbv_p = _layer0_prompt(xp, batch, seq, norm0_g, w_in0_bf, ret_gn_g, sb_qn_g, sb_kn_g,
                                              w_out0_bf, tm)
    y2p, ckp, cvp, skp, svp, wkp, wvp = _layer1_prompt(y1p, batch, seq, norm1_g, w_in1_bf, nsa_qn_g, cmp_kn_g,
                                                        sel_kn_g, win_kn_g, pe128, wk_bd, wv_bd, w_out1_bf, tm)
    xs = x_sample.reshape(nb, d)
    y1s, ret_s, sbk_s, sbv_s = _layer0_sample(xs, past_len, state_ret, cache_sb_k, cache_sb_v, page_table, norm0_g,
                                              w_in0_bf, ret_gn_g, sb_qn_g, sb_kn_g, w_out0_bf)
    y2s, cks, cvs, sks, svs, wks, wvs = _layer1_sample(y1s, past_len, cache_cmp_k, cache_cmp_v, cache_sel_k,
                                                        cache_sel_v, cache_win_k, cache_win_v, page_table, norm1_g,
                                                        w_in1_bf, nsa_qn_g, cmp_kn_g, sel_kn_g, win_kn_g, pe128,
                                                        wk_bd, wv_bd, w_out1_bf)
    keep = min(WINDOW, seq)
    hs = (batch, seq, H_SB, HEAD_DIM)
    gs = (batch, seq, G_NSA, HEAD_DIM)
    gd = (nb, 1, G_NSA, HEAD_DIM)
    wlen = cache_win_k.shape[1]
    return (y2p.reshape(batch, seq, d), y2s.reshape(nb, 1, d), ret_p, ret_s,
            sbk_p.reshape(hs), sbv_p.reshape(hs),
            sbk_s.reshape(nb, 1, H_SB, HEAD_DIM), sbv_s.reshape(nb, 1, H_SB, HEAD_DIM),
            ckp.reshape(gs), cvp.reshape(gs), skp.reshape(gs), svp.reshape(gs),
            wkp.reshape(gs)[:, seq - keep:], wvp.reshape(gs)[:, seq - keep:],
            cks.reshape(gd), cvs.reshape(gd), sks.reshape(gd), svs.reshape(gd),
            wks.reshape(nb, wlen, G_NSA, HEAD_DIM), wvs.reshape(nb, wlen, G_NSA, HEAD_DIM))
```

```python
import functools

import jax
import jax.numpy as jnp
import numpy as np
from jax import lax
from jax.experimental import pallas as pl
from jax.experimental.pallas import tpu as pltpu

F32 = jnp.float32
BF16 = jnp.bfloat16
I32 = jnp.int32

HEAD_DIM = 64
LANES = 128
H_RET = 8
H_SB = 8
RET_CHUNK = 128
ROPE_BASE = 10000.0
Q_BLOCK = 128
H_NSA = 16
G_NSA = 2
HPG = H_NSA // G_NSA
CMP_BLOCK = 32
SEL_BLOCK = 64
TOP_N = 16
WINDOW = 512
FORCE_BONUS = 1000.0
NEG_INF = -1e30
EPS = 1e-6
PAGE_SIZE = 128
SCALE = HEAD_DIM ** -0.5
VMEM_LIMIT = 56 * 1024 * 1024


def _dot(a, b):
    return jnp.dot(a, b, preferred_element_type=F32)


def _dot_nt(a, b):
    return lax.dot_general(a, b, (((1,), (1,)), ((), ())), preferred_element_type=F32)


def _dot_tn(a, b):
    return lax.dot_general(a, b, (((0,), (0,)), ((), ())), preferred_element_type=F32)


def _split_bf16(x):
    hi = x.astype(BF16)
    lo = (x - hi.astype(F32)).astype(BF16)
    return hi, lo


def _sigmoid(x):
    return 1.0 / (1.0 + jnp.exp(-x))


def _silu(x):
    return x * _sigmoid(x)


def _lane_lo(shape):
    return lax.broadcasted_iota(I32, shape, len(shape) - 1) < HEAD_DIM


def _pair_sum(x, lo):
    s_lo = jnp.sum(jnp.where(lo, x, 0.0), axis=-1, keepdims=True)
    s_hi = jnp.sum(jnp.where(lo, 0.0, x), axis=-1, keepdims=True)
    return jnp.where(lo, s_lo, s_hi)


def _pair_rmsnorm(x, g, lo):
    ms = _pair_sum(x * x, lo) * (1.0 / HEAD_DIM)
    return x * lax.rsqrt(ms + EPS) * g


def _rope_pair(x, cos, sin_signed):
    lane = lax.broadcasted_iota(I32, x.shape, 1)
    first = (lane & 32) == 0
    partner = jnp.where(first, pltpu.roll(x, 96, 1), pltpu.roll(x, 32, 1))
    return x * cos + partner * sin_signed


def _params(sem):
    return pltpu.CompilerParams(dimension_semantics=sem, vmem_limit_bytes=VMEM_LIMIT)


def _store_head_major(dst_ref, t, pair, r):
    cols = slice(r * LANES, (r + 1) * LANES)
    dst_ref[0, 2 * pair, :, cols] = t[:HEAD_DIM]
    dst_ref[0, 2 * pair + 1, :, cols] = t[HEAD_DIM:]


def _inproj0_kernel(x_ref, g_ref, w_ref, cos_ref, sin_ref, qg_ref, kg_ref,
                    ret_ref, sbq_ref, sbg_ref, *kv_refs, prompt):
    x = x_ref[...]
    xn = x * lax.rsqrt(jnp.mean(x * x, axis=-1, keepdims=True) + EPS) * g_ref[...]
    xb = xn.astype(BF16)
    cos = cos_ref[...]
    sin = sin_ref[...]
    lo = _lane_lo((1, LANES))
    width = H_RET * HEAD_DIM

    def group(i):
        return _dot(xb, w_ref[:, i * width:(i + 1) * width])

    def tiles(a):
        return [a[:, t * LANES:(t + 1) * LANES] for t in range(width // LANES)]

    for t, v in enumerate(tiles(group(0))):
        ret_ref[:, t * LANES:(t + 1) * LANES] = _rope_pair(v, cos, sin)
    for t, v in enumerate(tiles(group(1))):
        ret_ref[:, width + t * LANES:width + (t + 1) * LANES] = _rope_pair(v, cos, sin) * SCALE
    ret_ref[:, 2 * width:3 * width] = group(2)
    ret_ref[:, 3 * width:4 * width] = group(3)
    for t, v in enumerate(tiles(group(4))):
        sbq_ref[:, t * LANES:(t + 1) * LANES] = (_pair_rmsnorm(v, qg_ref[...], lo) * SCALE).astype(BF16)
    sk = [_pair_rmsnorm(v, kg_ref[...], lo) for v in tiles(group(5))]
    sv = tiles(group(6))
    if prompt:
        kb_ref, vt_ref, kt_out, vt_out = kv_refs
        for t in range(width // LANES):
            kb_ref[:, t * LANES:(t + 1) * LANES] = sk[t].astype(BF16)
            for r in range(x.shape[0] // LANES):
                rows = slice(r * LANES, (r + 1) * LANES)
                vt = sv[t][rows].T
                _store_head_major(kt_out, sk[t][rows].T, t, r)
                _store_head_major(vt_out, vt, t, r)
                vt_ref[r, t] = vt.astype(BF16)
    else:
        k_ref, v_ref = kv_refs
        for t in range(width // LANES):
            k_ref[:, t * LANES:(t + 1) * LANES] = sk[t]
            v_ref[:, t * LANES:(t + 1) * LANES] = sv[t]
    sbg_ref[...] = _silu(group(7))


def _inproj0(x2d, norm_g, w_bf, cos, sin, qg, kg, tm, seq, prompt):
    rows, d = x2d.shape
    width = H_RET * HEAD_DIM
    n = w_bf.shape[1]
    grid = (rows // tm,)
    per_seq = seq // tm
    row = lambda i: (i, 0)
    fixed = lambda i: (0, 0)
    pos = lambda i: (i % per_seq, 0)
    if prompt:
        npair = width // LANES
        head_major = pl.BlockSpec((1, H_SB, HEAD_DIM, tm), lambda i: (i // per_seq, 0, 0, i % per_seq))
        hm_shape = jax.ShapeDtypeStruct((rows // seq, H_SB, HEAD_DIM, seq), F32)
        extra_specs = [pl.BlockSpec((tm, width), row),
                       pl.BlockSpec((tm // LANES, npair, LANES, LANES), lambda i: (i, 0, 0, 0)),
                       head_major, head_major]
        extra_shapes = [jax.ShapeDtypeStruct((rows, width), BF16),
                        jax.ShapeDtypeStruct((rows // LANES, npair, LANES, LANES), BF16),
                        hm_shape, hm_shape]
    else:
        extra_specs = [pl.BlockSpec((tm, width), row)] * 2
        extra_shapes = [jax.ShapeDtypeStruct((rows, width), F32)] * 2
    return pl.pallas_call(
        functools.partial(_inproj0_kernel, prompt=prompt),
        grid=grid,
        in_specs=[
            pl.BlockSpec((tm, d), row),
            pl.BlockSpec((1, d), fixed),
            pl.BlockSpec((d, n), fixed),
            pl.BlockSpec((tm, LANES), pos),
            pl.BlockSpec((tm, LANES), pos),
            pl.BlockSpec((1, LANES), fixed),
            pl.BlockSpec((1, LANES), fixed),
        ],
        out_specs=[
            pl.BlockSpec((tm, 4 * width), row),
            pl.BlockSpec((tm, width), row),
            pl.BlockSpec((tm, width), row),
        ] + extra_specs,
        out_shape=[
            jax.ShapeDtypeStruct((rows, 4 * width), F32),
            jax.ShapeDtypeStruct((rows, width), BF16),
            jax.ShapeDtypeStruct((rows, width), F32),
        ] + extra_shapes,
        compiler_params=_params(("parallel",)),
        name="inproj0",
    )(x2d, norm_g.reshape(1, d), w_bf, cos, sin, qg, kg)


def _ret_prompt_kernel(q_ref, k_ref, v_ref, g_ref, inner_ref, qdec_ref, kdec_ref, cdec_ref, gn_ref,
                       mix_ref, state_ref, s_scr):
    c = pl.program_id(2)

    @pl.when(c == 0)
    def _():
        s_scr[...] = jnp.zeros_like(s_scr)

    q = q_ref[...]
    k = k_ref[...]
    vb = v_ref[...].astype(BF16)
    qb = q.astype(BF16)
    kb = k.astype(BF16)
    lo = _lane_lo(q.shape)
    zero = jnp.zeros_like(qb)
    att_lo = _dot_nt(jnp.where(lo, qb, zero), kb) * inner_ref[0]
    att_hi = _dot_nt(jnp.where(lo, zero, qb), kb) * inner_ref[1]
    o = jnp.where(lo, _dot(att_lo.astype(BF16), vb), _dot(att_hi.astype(BF16), vb))
    s = s_scr[...]
    o = o + _dot(qb, s.astype(BF16)) * qdec_ref[0]
    kd = (k * kdec_ref[0]).astype(BF16)
    ri = lax.broadcasted_iota(I32, s.shape, 0) < HEAD_DIM
    ci = lax.broadcasted_iota(I32, s.shape, 1) < HEAD_DIM
    s_new = s * cdec_ref[0] + jnp.where(ri == ci, _dot_tn(kd, vb), 0.0)
    s_scr[...] = s_new
    state_ref[0, 0] = s_new
    mu = _pair_sum(o, lo) * (1.0 / HEAD_DIM)
    oc = o - mu
    var = _pair_sum(oc * oc, lo) * (1.0 / HEAD_DIM)
    on = oc * lax.rsqrt(var + EPS) * gn_ref[0]
    mix_ref[...] = on * _silu(g_ref[...])


def _ret_tables(chunk):
    h = jnp.arange(H_RET, dtype=F32)
    lg = jnp.log1p(-jnp.exp2(-5.0 - h))
    idx = jnp.arange(chunk, dtype=F32)
    diff = idx[:, None] - idx[None, :]
    inner = jnp.where(diff >= 0, jnp.exp(lg[:, None, None] * jnp.maximum(diff, 0.0)), 0.0)
    q_dec = jnp.exp(lg[:, None] * (idx + 1.0))
    k_dec = jnp.exp(lg[:, None] * (chunk - 1.0 - idx))
    c_dec = jnp.exp(lg * chunk)

    def lanes(t):
        t = jnp.repeat(t[:, :, None], HEAD_DIM, axis=2)
        return jnp.concatenate([t[0::2], t[1::2]], axis=2)

    cd = jnp.repeat(c_dec[:, None, None], HEAD_DIM, axis=2)
    cd = jnp.concatenate([cd[0::2], cd[1::2]], axis=2)
    return inner, lanes(q_dec), lanes(k_dec), cd


def _ret_prompt(ret, gn_g, batch, seq):
    width = H_RET * HEAD_DIM
    npair = H_RET // 2
    c = RET_CHUNK
    nc = seq // c
    inner, qdec, kdec, cdec = _ret_tables(c)
    gn = gn_g.reshape(npair, 1, LANES)

    def col(off):
        return lambda b, p, i: (b * nc + i, off + p)

    tab = lambda b, p, i: (p, 0, 0)
    return pl.pallas_call(
        _ret_prompt_kernel,
        grid=(batch, npair, nc),
        in_specs=[
            pl.BlockSpec((c, LANES), col(0)),
            pl.BlockSpec((c, LANES), col(npair)),
            pl.BlockSpec((c, LANES), col(2 * npair)),
            pl.BlockSpec((c, LANES), col(3 * npair)),
            pl.BlockSpec((2, c, c), tab),
            pl.BlockSpec((1, c, LANES), tab),
            pl.BlockSpec((1, c, LANES), tab),
            pl.BlockSpec((1, 1, LANES), tab),
            pl.BlockSpec((1, 1, LANES), tab),
        ],
        out_specs=[
            pl.BlockSpec((c, LANES), lambda b, p, i: (b * nc + i, p)),
            pl.BlockSpec((1, 1, LANES, LANES), lambda b, p, i: (b, p, 0, 0)),
        ],
        out_shape=[
            jax.ShapeDtypeStruct((batch * seq, width), F32),
            jax.ShapeDtypeStruct((batch, npair, LANES, LANES), F32),
        ],
        scratch_shapes=[pltpu.VMEM((LANES, LANES), F32)],
        compiler_params=_params(("parallel", "parallel", "arbitrary")),
        name="retention_prompt",
    )(ret, ret, ret, ret, inner, qdec, kdec, cdec, gn)


def _unpair_state(sp):
    a = sp[:, :, :HEAD_DIM, :HEAD_DIM]
    b = sp[:, :, HEAD_DIM:, HEAD_DIM:]
    return jnp.stack([a, b], axis=2).reshape(sp.shape[0], -1, HEAD_DIM, HEAD_DIM)


SB_LOG_ZERO = -104.0


def _sb_prompt_kernel(q_ref, k_ref, vt_ref, g_ref, o_ref, acc_scr):
    i = pl.program_id(1)
    tq = q_ref.shape[0]
    npair = H_SB // 2
    rr = lax.broadcasted_iota(I32, (tq, tq), 0)
    cc = lax.broadcasted_iota(I32, (tq, tq), 1)
    lower = (cc >= rr).astype(BF16)
    diag = rr < cc
    zeros = jnp.zeros((HEAD_DIM, tq), F32)
    qts = []
    for p in range(npair):
        t = q_ref[:, p * LANES:(p + 1) * LANES].astype(F32).T
        qts.append(jnp.concatenate([t[:HEAD_DIM], zeros], axis=0).astype(BF16))
        qts.append(jnp.concatenate([zeros, t[HEAD_DIM:]], axis=0).astype(BF16))

    def sweep(j, rs, diag_mask):
        start = pl.multiple_of(j * tq, tq)
        kjs = [k_ref[pl.ds(start, tq), p * LANES:(p + 1) * LANES] for p in range(npair)]
        vts = [vt_ref[j, p] for p in range(npair)]
        zs = [_dot(kjs[h // 2], qt) for h, qt in enumerate(qts)]
        ls_pos, lms, parts = [], [], []
        for z in zs:
            lsp = jnp.minimum(z, 0.0) - jnp.log1p(jnp.exp(-jnp.abs(z)))
            lm = lsp - z
            if diag_mask is not None:
                lm = jnp.where(diag_mask, lm, 0.0)
            ls_pos.append(lsp)
            lms.append(lm)
            parts.append(_split_bf16(lm))
        s_incl = [_dot(lower, hi) + _dot(lower, lo_part) for hi, lo_part in parts]
        probs = []
        for h in range(H_SB):
            a = jnp.exp(ls_pos[h] + (s_incl[h] - lms[h]) + rs[h])
            if diag_mask is not None:
                a = jnp.where(diag_mask, a, 0.0)
            probs.append(a.astype(BF16))
        for h in range(H_SB):
            half = h % 2
            pv = _dot(vts[h // 2][half * HEAD_DIM:(half + 1) * HEAD_DIM, :], probs[h])
            acc_scr[h] = pv if diag_mask is not None else acc_scr[h] + pv
        return [rs[h] + s_incl[h][0:1, :] for h in range(H_SB)]

    def alive_of(rs):
        worst = rs[0]
        for r in rs[1:]:
            worst = jnp.maximum(worst, r)
        return jnp.max(worst) > SB_LOG_ZERO

    rs = sweep(i, [jnp.zeros((1, tq), F32)] * H_SB, diag)

    def cond(carry):
        t, alive = carry[0], carry[1]
        return (t < i) & alive

    def body(carry):
        t = carry[0]
        rs = sweep(i - 1 - t, list(carry[2:]), None)
        return (t + 1, alive_of(rs)) + tuple(rs)

    lax.while_loop(cond, body, (jnp.int32(0), alive_of(rs)) + tuple(rs))
    for p in range(npair):
        both = jnp.concatenate([acc_scr[2 * p], acc_scr[2 * p + 1]], axis=0).T
        o_ref[:, p * LANES:(p + 1) * LANES] = both * g_ref[:, p * LANES:(p + 1) * LANES]


def _sb_prompt(sbq, sbk_b, sbvt, sbg, batch, seq):
    width = H_SB * HEAD_DIM
    npair = H_SB // 2
    tq = Q_BLOCK
    nq = seq // tq
    qmap = lambda b, i: (b * nq + i, 0)
    return pl.pallas_call(
        _sb_prompt_kernel,
        grid=(batch, nq),
        in_specs=[
            pl.BlockSpec((tq, width), qmap),
            pl.BlockSpec((seq, width), lambda b, i: (b, 0)),
            pl.BlockSpec((nq, npair, LANES, tq), lambda b, i: (b, 0, 0, 0)),
            pl.BlockSpec((tq, width), qmap),
        ],
        out_specs=pl.BlockSpec((tq, width), qmap),
        out_shape=jax.ShapeDtypeStruct((batch * seq, width), F32),
        scratch_shapes=[pltpu.VMEM((H_SB, HEAD_DIM, tq), F32)],
        compiler_params=_params(("parallel", "arbitrary")),
        name="sb_prompt",
    )(sbq, sbk_b, sbvt, sbg)


def _outproj_kernel(*refs, n_mix):
    x_ref = refs[0]
    mix_refs = refs[1:1 + n_mix]
    w_refs = refs[1 + n_mix:1 + 2 * n_mix]
    y_ref = refs[1 + 2 * n_mix]
    acc = x_ref[...]
    for m_ref, w_ref in zip(mix_refs, w_refs):
        acc = acc + _dot(m_ref[...].astype(BF16), w_ref[...])
    y_ref[...] = acc


def _outproj(x2d, mixes, ws, tm):
    rows, d = x2d.shape
    n_mix = len(mixes)
    row = lambda i: (i, 0)
    fixed = lambda i: (0, 0)
    in_specs = [pl.BlockSpec((tm, d), row)]
    in_specs += [pl.BlockSpec((tm, m.shape[1]), row) for m in mixes]
    in_specs += [pl.BlockSpec(w.shape, fixed) for w in ws]
    return pl.pallas_call(
        functools.partial(_outproj_kernel, n_mix=n_mix),
        grid=(rows // tm,),
        in_specs=in_specs,
        out_specs=pl.BlockSpec((tm, d), row),
        out_shape=jax.ShapeDtypeStruct((rows, d), F32),
        compiler_params=_params(("parallel",)),
        name="outproj",
    )(x2d, *mixes, *ws)


def _rope_tables(pos):
    half = HEAD_DIM // 2
    inv = ROPE_BASE ** (-jnp.arange(half, dtype=F32) / half)
    ang = pos.astype(F32)[:, None] * inv[None, :]
    cos, sin = jnp.cos(ang), jnp.sin(ang)
    cos128 = jnp.tile(cos, (1, LANES // half))
    sin128 = jnp.tile(jnp.concatenate([-sin, sin], axis=1), (1, LANES // HEAD_DIM))
    return cos128, sin128


def _pair_gain(g):
    return jnp.tile(g.astype(F32), LANES // HEAD_DIM).reshape(1, LANES)


def _layer0_prompt(x2d, batch, seq, norm_g, w_in_bf, ret_gn_g, sb_qn_g, sb_kn_g, w_out_bf, tm):
    cos, sin = _rope_tables(jnp.arange(seq, dtype=I32))
    ret, sbq, sbg, sbk_b, sbvt, sbk_hm, sbv_hm = _inproj0(x2d, norm_g, w_in_bf, cos, sin, _pair_gain(sb_qn_g),
                                                          _pair_gain(sb_kn_g), tm, seq, True)
    mix_ret, state_pairs = _ret_prompt(ret, ret_gn_g, batch, seq)
    mix_sb = _sb_prompt(sbq, sbk_b, sbvt, sbg, batch, seq)
    half = w_out_bf.shape[0] // 2
    y = _outproj(x2d, [mix_ret, mix_sb], [w_out_bf[:half], w_out_bf[half:]], tm)
    return y, _unpair_state(state_pairs), sbk_hm, sbv_hm


NSA_W = H_NSA * HEAD_DIM
KV_W = G_NSA * HEAD_DIM
IN1_COLS = 2 * NSA_W + 6 * KV_W + LANES


def _inproj1_kernel(x_ref, g_ref, w_ref, qg_ref, skg_ref, wkg_ref, q_ref, gate_ref, bg_ref, *kv_refs, prompt):
    x = x_ref[...]
    xn = x * lax.rsqrt(jnp.mean(x * x, axis=-1, keepdims=True) + EPS) * g_ref[...]
    xb = xn.astype(BF16)
    lo = _lane_lo((1, LANES))
    a = _dot(xb, w_ref[:, 0:NSA_W])
    for t in range(NSA_W // LANES):
        v = a[:, t * LANES:(t + 1) * LANES]
        q_ref[:, t * LANES:(t + 1) * LANES] = (_pair_rmsnorm(v, qg_ref[...], lo) * SCALE).astype(BF16)
    a = _dot(xb, w_ref[:, NSA_W:NSA_W + 6 * KV_W])
    kc, vc, ks, vs, kw, vw = [a[:, t * KV_W:(t + 1) * KV_W] for t in range(6)]
    ks = _pair_rmsnorm(ks, skg_ref[...], lo)
    kw = _pair_rmsnorm(kw, wkg_ref[...], lo)
    six = (kc, vc, ks, vs, kw, vw)
    if prompt:
        t_out = kv_refs[:6]
        kc_ref, vc_ref, ksb_ref, kwb_ref, vst_ref, vwt_ref = kv_refs[6:]
        kc_ref[...] = kc
        vc_ref[...] = vc
        ksb_ref[...] = ks.astype(BF16)
        kwb_ref[...] = kw.astype(BF16)
        for r in range(x.shape[0] // LANES):
            rows = slice(r * LANES, (r + 1) * LANES)
            for src, dst in zip(six, t_out):
                t = src[rows].T
                dst[0, :, rows] = t
                if src is vs:
                    vst_ref[r] = t.astype(BF16)
                if src is vw:
                    vwt_ref[r] = t.astype(BF16)
    else:
        for src, dst in zip(six, kv_refs):
            dst[...] = src
    off = NSA_W + 6 * KV_W
    gate_ref[...] = _silu(_dot(xb, w_ref[:, off:off + NSA_W]))
    bg_ref[...] = _sigmoid(_dot(xb, w_ref[:, off + NSA_W:off + NSA_W + LANES]))


def _inproj1(x2d, norm_g, w_bf, qg, skg, wkg, tm, seq, prompt):
    rows, d = x2d.shape
    row = lambda i: (i, 0)
    fixed = lambda i: (0, 0)
    out_specs = [pl.BlockSpec((tm, NSA_W), row), pl.BlockSpec((tm, NSA_W), row), pl.BlockSpec((tm, LANES), row)]
    out_shape = [jax.ShapeDtypeStruct((rows, NSA_W), BF16), jax.ShapeDtypeStruct((rows, NSA_W), F32),
                 jax.ShapeDtypeStruct((rows, LANES), F32)]
    if prompt:
        per_seq = seq // tm
        tiles = tm // LANES
        out_specs += ([pl.BlockSpec((1, KV_W, tm), lambda i: (i // per_seq, 0, i % per_seq))] * 6
                      + [pl.BlockSpec((tm, KV_W), row)] * 4
                      + [pl.BlockSpec((tiles, LANES, LANES), lambda i: (i, 0, 0))] * 2)
        out_shape += ([jax.ShapeDtypeStruct((rows // seq, KV_W, seq), F32)] * 6
                      + [jax.ShapeDtypeStruct((rows, KV_W), F32)] * 2
                      + [jax.ShapeDtypeStruct((rows, KV_W), BF16)] * 2
                      + [jax.ShapeDtypeStruct((rows // LANES, LANES, LANES), BF16)] * 2)
    else:
        out_specs += [pl.BlockSpec((tm, KV_W), row)] * 6
        out_shape += [jax.ShapeDtypeStruct((rows, KV_W), F32)] * 6
    return pl.pallas_call(
        functools.partial(_inproj1_kernel, prompt=prompt),
        grid=(rows // tm,),
        in_specs=[
            pl.BlockSpec((tm, d), row),
            pl.BlockSpec((1, d), fixed),
            pl.BlockSpec((d, IN1_COLS), fixed),
            pl.BlockSpec((1, LANES), fixed),
            pl.BlockSpec((1, LANES), fixed),
            pl.BlockSpec((1, LANES), fixed),
        ],
        out_specs=out_specs,
        out_shape=out_shape,
        compiler_params=_params(("parallel",)),
        name="inproj1",
    )(x2d, norm_g.reshape(1, d), w_bf, qg, skg, wkg)


def _pad_w_in1(w_in1):
    pad = IN1_COLS - w_in1.shape[1]
    return jnp.pad(w_in1, ((0, 0), (0, pad))).astype(BF16)


def _compress_rows(t_ref, pe_ref, w_ref, row0, nblk):
    acc = jnp.zeros((nblk, LANES), F32)
    for l in range(CMP_BLOCK):
        rows = t_ref[pl.ds(row0 + l, nblk, stride=CMP_BLOCK), :]
        acc = acc + _dot((rows + pe_ref[l:l + 1, :]).astype(BF16), w_ref[l])
    return acc


def _compress_prompt_kernel(kc_ref, vc_ref, pe_ref, wk_ref, wv_ref, kg_ref, ck_ref, cvt_ref):
    nblk = ck_ref.shape[0]
    lo = _lane_lo((1, LANES))
    ck = _pair_rmsnorm(_compress_rows(kc_ref, pe_ref, wk_ref, 0, nblk), kg_ref[...], lo)
    cv = _compress_rows(vc_ref, pe_ref, wv_ref, 0, nblk)
    ck_ref[...] = ck.astype(BF16)
    cvt_ref[0] = cv.T.astype(BF16)


def _blockdiag_w(w):
    z = jnp.zeros_like(w)
    top = jnp.concatenate([w, z], axis=2)
    bot = jnp.concatenate([z, w], axis=2)
    return jnp.concatenate([top, bot], axis=1).astype(BF16)


def _compress_prompt(kc, vc, pe128, wk_bd, wv_bd, kg, batch, seq):
    nblk = seq // CMP_BLOCK
    rowb = lambda b: (b, 0)
    fixed2 = lambda b: (0, 0)
    fixed3 = lambda b: (0, 0, 0)
    return pl.pallas_call(
        _compress_prompt_kernel,
        grid=(batch,),
        in_specs=[
            pl.BlockSpec((seq, LANES), rowb),
            pl.BlockSpec((seq, LANES), rowb),
            pl.BlockSpec((CMP_BLOCK, LANES), fixed2),
            pl.BlockSpec((CMP_BLOCK, LANES, LANES), fixed3),
            pl.BlockSpec((CMP_BLOCK, LANES, LANES), fixed3),
            pl.BlockSpec((1, LANES), fixed2),
        ],
        out_specs=[pl.BlockSpec((nblk, LANES), rowb), pl.BlockSpec((1, LANES, nblk), lambda b: (b, 0, 0))],
        out_shape=[jax.ShapeDtypeStruct((batch * nblk, LANES), BF16),
                   jax.ShapeDtypeStruct((batch, LANES, nblk), BF16)],
        compiler_params=_params(("parallel",)),
        name="compress_prompt",
    )(kc, vc, pe128, wk_bd, wv_bd, kg)


def _nsa_prompt_kernel(q_ref, ck_ref, cvt_ref, ks_ref, vst_ref, kw_ref, vwt_ref, bg_ref, gate_ref,
                       o_ref, m_scr, l_scr, acc_scr, imp_scr, score_scr, chosen_scr):
    i = pl.program_id(2)
    g = pl.program_id(1)
    tq = q_ref.shape[0]
    nb = ck_ref.shape[0]
    npair = HPG // 2
    ratio = SEL_BLOCK // CMP_BLOCK
    nsb = nb // ratio
    q_pos = i * tq + lax.broadcasted_iota(I32, (1, tq), 1)
    g_row = pl.multiple_of(g * HEAD_DIM, HEAD_DIM)

    zeros = jnp.zeros((HEAD_DIM, tq), F32)
    qts = []
    for p in range(npair):
        t = q_ref[:, p * LANES:(p + 1) * LANES].astype(F32).T
        for ht in (t[:HEAD_DIM], t[HEAD_DIM:]):
            in_g0 = jnp.concatenate([ht, zeros], axis=0)
            in_g1 = jnp.concatenate([zeros, ht], axis=0)
            qts.append(jnp.where(g == 0, in_g0, in_g1).astype(BF16))

    ck = ck_ref[...]
    cvt = cvt_ref[0, pl.ds(g_row, HEAD_DIM), :]
    blk_n = lax.broadcasted_iota(I32, (nb, tq), 0)
    cmask = blk_n * CMP_BLOCK + (CMP_BLOCK - 1) <= q_pos
    imp = jnp.zeros((nb, tq), F32)
    cmp_scores = [_dot(ck, qt) for qt in qts]
    cmp_probs = []
    for s in cmp_scores:
        s = jnp.where(cmask, s, NEG_INF)
        m = jnp.max(s, axis=0, keepdims=True)
        e = jnp.where(cmask, jnp.exp(s - m), 0.0)
        den = jnp.sum(e, axis=0, keepdims=True)
        pr = e / jnp.where(den > 0.0, den, 1.0)
        imp = imp + pr
        cmp_probs.append(pr.astype(BF16))
    o_cmp = [_dot(cvt, pr) for pr in cmp_probs]

    imp_scr[...] = imp
    pooled = imp_scr[pl.ds(0, nsb, stride=ratio), :]
    for r in range(1, ratio):
        pooled = pooled + imp_scr[pl.ds(r, nsb, stride=ratio), :]
    sblk = lax.broadcasted_iota(I32, (nsb, tq), 0)
    cur = q_pos // SEL_BLOCK
    forced = (sblk == 0) | (sblk == cur) | (sblk == cur - 1)
    score = jnp.where(sblk <= cur, pooled + FORCE_BONUS * forced.astype(F32), NEG_INF)
    score_scr[...] = score
    cnt = jnp.zeros((nsb, tq), F32)
    for mblk in range(nsb):
        c = score_scr[mblk:mblk + 1, :]
        ahead = (c > score) | ((c == score) & (mblk < sblk))
        cnt = cnt + ahead.astype(F32)
    chosen = (cnt < float(TOP_N)).astype(F32)
    for mblk in range(nsb):
        chosen_scr[mblk] = jnp.broadcast_to(chosen[mblk:mblk + 1, :], (8, tq))

    k_in = lax.broadcasted_iota(I32, (tq, tq), 0)
    blocks_per_tile = tq // SEL_BLOCK

    def flash(k_ref, vt_ref, slot, n_tiles, mask_fn):
        m_scr[slot] = jnp.full(m_scr.shape[1:], NEG_INF, F32)
        l_scr[slot] = jnp.zeros(l_scr.shape[1:], F32)
        acc_scr[slot] = jnp.zeros(acc_scr.shape[1:], F32)

        def body(t, carry):
            j = i - t
            kj = k_ref[pl.ds(pl.multiple_of(j * tq, tq), tq), :]
            vt = vt_ref[j, pl.ds(g_row, HEAD_DIM), :]
            mask = mask_fn(j)
            scores = [_dot(kj, qt) for qt in qts]
            alphas, probs = [], []
            for h, s in enumerate(scores):
                s = jnp.where(mask, s, NEG_INF)
                m_old = m_scr[slot, h]
                m_new = jnp.maximum(m_old, jnp.max(s, axis=0, keepdims=True))
                alpha = jnp.exp(m_old - m_new)
                p = jnp.exp(s - m_new[0:1])
                l_scr[slot, h] = alpha * l_scr[slot, h] + jnp.sum(p, axis=0, keepdims=True)
                m_scr[slot, h] = m_new
                alphas.append(alpha[0:1])
                probs.append(p.astype(BF16))
            for h in range(HPG):
                acc_scr[slot, h] = alphas[h] * acc_scr[slot, h] + _dot(vt, probs[h])
            return carry

        lax.fori_loop(0, n_tiles, body, 0)
        return [acc_scr[slot, h] / l_scr[slot, h][0:1] for h in range(HPG)]

    def sel_mask(j):
        rows = [jnp.broadcast_to(chosen_scr[blocks_per_tile * j + r][0:1], (SEL_BLOCK, tq))
                for r in range(blocks_per_tile)]
        picked = jnp.concatenate(rows, axis=0) > 0.5
        return picked & (j * tq + k_in <= q_pos)

    def win_mask(j):
        k_pos = j * tq + k_in
        return (k_pos <= q_pos) & (k_pos > q_pos - WINDOW)

    o_sel = flash(ks_ref, vst_ref, 0, i + 1, sel_mask)
    o_win = flash(kw_ref, vwt_ref, 1, jnp.minimum(i, WINDOW // tq) + 1, win_mask)

    bgt = bg_ref[...].T

    def gate_row(h, br):
        c = h * 3 + br
        n = HPG * 3
        return jnp.where(g == 0, bgt[c:c + 1, :], bgt[n + c:n + c + 1, :])

    for p in range(npair):
        halves = []
        for h in (2 * p, 2 * p + 1):
            halves.append(gate_row(h, 0) * o_cmp[h] + gate_row(h, 1) * o_sel[h] + gate_row(h, 2) * o_win[h])
        mixed = jnp.concatenate(halves, axis=0).T
        o_ref[:, p * LANES:(p + 1) * LANES] = mixed * gate_ref[:, p * LANES:(p + 1) * LANES]


def _nsa_prompt(qn, ck, cvt, ksb, vst, kwb, vwt, bg, gate, batch, seq):
    tq = Q_BLOCK
    nq = seq // tq
    nb = seq // CMP_BLOCK
    nsb = nb // (SEL_BLOCK // CMP_BLOCK)
    gw = HPG * HEAD_DIM
    qmap = lambda b, g, i: (b * nq + i, g)
    per_b2 = lambda b, g, i: (b, 0)
    per_b3 = lambda b, g, i: (b, 0, 0)
    return pl.pallas_call(
        _nsa_prompt_kernel,
        grid=(batch, G_NSA, nq),
        in_specs=[
            pl.BlockSpec((tq, gw), qmap),
            pl.BlockSpec((nb, LANES), per_b2),
            pl.BlockSpec((1, LANES, nb), per_b3),
            pl.BlockSpec((seq, LANES), per_b2),
            pl.BlockSpec((nq, LANES, tq), per_b3),
            pl.BlockSpec((seq, LANES), per_b2),
            pl.BlockSpec((nq, LANES, tq), per_b3),
            pl.BlockSpec((tq, LANES), lambda b, g, i: (b * nq + i, 0)),
            pl.BlockSpec((tq, gw), qmap),
        ],
        out_specs=pl.BlockSpec((tq, gw), qmap),
        out_shape=jax.ShapeDtypeStruct((batch * seq, NSA_W), F32),
        scratch_shapes=[
            pltpu.VMEM((2, HPG, 8, tq), F32),
            pltpu.VMEM((2, HPG, 8, tq), F32),
            pltpu.VMEM((2, HPG, HEAD_DIM, tq), F32),
            pltpu.VMEM((nb, tq), F32),
            pltpu.VMEM((nsb, tq), F32),
            pltpu.VMEM((nsb, 8, tq), F32),
        ],
        compiler_params=_params(("parallel", "parallel", "arbitrary")),
        name="nsa_prompt",
    )(qn, ck, cvt, ksb, vst, kwb, vwt, bg, gate)


def _layer1_prompt(x2d, batch, seq, norm_g, w_in_bf, qn_g, cmp_kn_g, sel_kn_g, win_kn_g, pe128, wk_bd, wv_bd,
                   w_out_bf, tm):
    (qn, gate, bg, kc_t, vc_t, ks_t, vs_t, kw_t, vw_t, kc, vc, ksb, kwb, vst, vwt) = _inproj1(
        x2d, norm_g, w_in_bf, _pair_gain(qn_g), _pair_gain(sel_kn_g), _pair_gain(win_kn_g), tm, seq, True)
    ck, cvt = _compress_prompt(kc, vc, pe128, wk_bd, wv_bd, _pair_gain(cmp_kn_g), batch, seq)
    mix = _nsa_prompt(qn, ck, cvt, ksb, vst, kwb, vwt, bg, gate, batch, seq)
    y = _outproj(x2d, [mix], [w_out_bf], tm)
    return y, kc_t, vc_t, ks_t, vs_t, kw_t, vw_t


def _ret_sample_kernel(ret_ref, s_ref, qdec_ref, cdec_ref, gn_ref, mix_ref, so_ref):
    width = H_RET * HEAD_DIM
    rowi = lax.broadcasted_iota(I32, (8, HEAD_DIM), 0)
    eye = (lax.broadcasted_iota(I32, (HEAD_DIM, HEAD_DIM), 0)
           == lax.broadcasted_iota(I32, (HEAD_DIM, HEAD_DIM), 1)).astype(BF16)
    row = ret_ref[0]
    for h in range(H_RET):
        sl = slice(h * HEAD_DIM, (h + 1) * HEAD_DIM)
        qb = row[:, sl].astype(BF16)
        kb = row[:, width + h * HEAD_DIM:width + (h + 1) * HEAD_DIM].astype(BF16)
        vb = row[:, 2 * width + h * HEAD_DIM:2 * width + (h + 1) * HEAD_DIM].astype(BF16)
        gate = row[:, 3 * width + h * HEAD_DIM:3 * width + (h + 1) * HEAD_DIM]
        s = s_ref[0, h]
        qs = _dot(jnp.broadcast_to(qb.astype(F32), (8, HEAD_DIM)).astype(BF16), s.astype(BF16))[0:1]
        att = jnp.sum(qb.astype(F32) * kb.astype(F32), axis=-1, keepdims=True)
        o = att.astype(BF16).astype(F32) * vb.astype(F32) + qs * qdec_ref[h:h + 1, :]
        k8 = jnp.where(rowi == 0, jnp.broadcast_to(kb.astype(F32), (8, HEAD_DIM)), 0.0).astype(BF16)
        kcol = _dot_nt(eye, k8)[:, 0:1]
        so_ref[0, h] = s * cdec_ref[h:h + 1, :] + kcol * vb.astype(F32)
        mu = jnp.mean(o, axis=-1, keepdims=True)
        oc = o - mu
        var = jnp.mean(oc * oc, axis=-1, keepdims=True)
        on = oc * lax.rsqrt(var + EPS) * gn_ref[h:h + 1, :]
        mix_ref[0, :, sl] = on * _silu(gate)


def _ret_sample(ret, state, gn_g):
    nb, cols = ret.shape
    width = H_RET * HEAD_DIM
    h = jnp.arange(H_RET, dtype=F32)
    lg = jnp.log1p(-jnp.exp2(-5.0 - h))
    qdec = jnp.broadcast_to(jnp.exp(lg * 1.0)[:, None], (H_RET, HEAD_DIM))
    cdec = jnp.broadcast_to(jnp.exp(lg * 1.0)[:, None], (H_RET, HEAD_DIM))
    m3 = lambda b: (b, 0, 0)
    tab = pl.BlockSpec((H_RET, HEAD_DIM), lambda b: (0, 0))
    st = pl.BlockSpec((1, H_RET, HEAD_DIM, HEAD_DIM), lambda b: (b, 0, 0, 0))
    mix, state_new = pl.pallas_call(
        _ret_sample_kernel,
        grid=(nb,),
        in_specs=[pl.BlockSpec((1, 1, cols), m3), st, tab, tab, tab],
        out_specs=[pl.BlockSpec((1, 1, width), m3), st],
        out_shape=[jax.ShapeDtypeStruct((nb, 1, width), F32),
                   jax.ShapeDtypeStruct(state.shape, F32)],
        compiler_params=_params(("parallel",)),
        name="retention_sample",
    )(ret.reshape(nb, 1, cols), state, qdec, cdec, gn_g.astype(F32))
    return mix.reshape(nb, width), state_new


def _sb_sample_kernel(pt_ref, q_ref, g_ref, k_hbm, v_hbm, o_ref, kbuf, vbuf, sem):
    b = pl.program_id(0)
    n_pages = pt_ref.shape[1]
    page = PAGE_SIZE

    def copies(t, slot):
        pid = pt_ref[b, n_pages - 1 - t]
        return (pltpu.make_async_copy(k_hbm.at[pid], kbuf.at[slot], sem.at[0, slot]),
                pltpu.make_async_copy(v_hbm.at[pid], vbuf.at[slot], sem.at[1, slot]))

    def start(t, slot):
        for c in copies(t, slot):
            c.start()

    def wait(t, slot):
        for c in copies(t, slot):
            c.wait()

    q = q_ref[0].astype(F32)
    rowq = lax.broadcasted_iota(I32, (H_SB, HEAD_DIM), 0)
    rowa = lax.broadcasted_iota(I32, (H_SB, page), 0)
    upper = (lax.broadcasted_iota(I32, (page, page), 0) >= lax.broadcasted_iota(I32, (page, page), 1)).astype(BF16)
    q_rows = [jnp.where(rowq == h, q, 0.0).astype(BF16) for h in range(H_SB)]

    start(0, 0)

    def cond(carry):
        t, alive = carry[0], carry[1]
        return (t < n_pages) & alive

    def body(carry):
        t, _, r, acc = carry
        slot = t % 2
        wait(t, slot)

        @pl.when(t + 1 < n_pages)
        def _():
            start(t + 1, 1 - slot)

        z = jnp.zeros((H_SB, page), F32)
        for h in range(H_SB):
            z = z + _dot(q_rows[h], kbuf[slot, h].astype(BF16))
        ls_pos = jnp.minimum(z, 0.0) - jnp.log1p(jnp.exp(-jnp.abs(z)))
        lm = ls_pos - z
        hi, lo_part = _split_bf16(lm)
        s_incl = _dot(hi, upper) + _dot(lo_part, upper)
        a = jnp.exp(ls_pos + (s_incl - lm) + r)
        for h in range(H_SB):
            acc = acc + _dot_nt(jnp.where(rowa == h, a, 0.0).astype(BF16), vbuf[slot, h].astype(BF16))
        r = r + s_incl[:, 0:1]
        return t + 1, jnp.max(r) > SB_LOG_ZERO, r, acc

    init = (jnp.int32(0), jnp.bool_(True), jnp.zeros((H_SB, 1), F32), jnp.zeros((H_SB, HEAD_DIM), F32))
    t_end, _, _, acc = lax.while_loop(cond, body, init)

    @pl.when(t_end < n_pages)
    def _():
        wait(t_end, t_end % 2)

    o_ref[0] = acc * g_ref[0]


def _sb_sample(q8, g8, cache_kt, cache_vt, page_table):
    nb = q8.shape[0]
    head = lambda b, pt: (b, 0, 0)
    grid_spec = pltpu.PrefetchScalarGridSpec(
        num_scalar_prefetch=1,
        grid=(nb,),
        in_specs=[pl.BlockSpec((1, H_SB, HEAD_DIM), head), pl.BlockSpec((1, H_SB, HEAD_DIM), head),
                  pl.BlockSpec(memory_space=pl.ANY), pl.BlockSpec(memory_space=pl.ANY)],
        out_specs=pl.BlockSpec((1, H_SB, HEAD_DIM), head),
        scratch_shapes=[pltpu.VMEM((2, H_SB, HEAD_DIM, PAGE_SIZE), F32),
                        pltpu.VMEM((2, H_SB, HEAD_DIM, PAGE_SIZE), F32),
                        pltpu.SemaphoreType.DMA((2, 2))],
    )
    return pl.pallas_call(
        _sb_sample_kernel,
        grid_spec=grid_spec,
        out_shape=jax.ShapeDtypeStruct((nb, H_SB, HEAD_DIM), F32),
        compiler_params=_params(("arbitrary",)),
        name="sb_sample",
    )(page_table, q8, g8, cache_kt, cache_vt)


CMP_PAGES_PER_STEP = 8


def _compress_sample_kernel(pt_ref, *refs, n_pp):
    k_refs = refs[:n_pp]
    v_refs = refs[n_pp:2 * n_pp]
    pe_ref, wk_ref, wv_ref, kg_ref, ck_ref, cv_ref, kbuf, vbuf = refs[2 * n_pp:]
    j = pl.program_id(1)
    for r in range(n_pp):
        kbuf[r * PAGE_SIZE:(r + 1) * PAGE_SIZE, :] = k_refs[r][0].T
        vbuf[r * PAGE_SIZE:(r + 1) * PAGE_SIZE, :] = v_refs[r][0].T
    nblk = n_pp * PAGE_SIZE // CMP_BLOCK
    lo = _lane_lo((1, LANES))
    start = pl.multiple_of(j * nblk, nblk)
    ck_ref[0, pl.ds(start, nblk), :] = _pair_rmsnorm(_compress_rows(kbuf, pe_ref, wk_ref, 0, nblk), kg_ref[...], lo)
    cv_ref[0, pl.ds(start, nblk), :] = _compress_rows(vbuf, pe_ref, wv_ref, 0, nblk)


def _compress_sample(cache_k, cache_v, page_table, pe128, wk_bd, wv_bd, kg):
    nb, n_pages = page_table.shape
    n_pp = CMP_PAGES_PER_STEP
    nblk_total = n_pages * PAGE_SIZE // CMP_BLOCK

    def page_map(r):
        return lambda b, j, pt: (pt[b, j * n_pp + r], 0, 0)

    fixed2 = lambda b, j, pt: (0, 0)
    fixed3 = lambda b, j, pt: (0, 0, 0)
    out_map = lambda b, j, pt: (b, 0, 0)
    grid_spec = pltpu.PrefetchScalarGridSpec(
        num_scalar_prefetch=1,
        grid=(nb, n_pages // n_pp),
        in_specs=[pl.BlockSpec((1, PAGE_SIZE, LANES), page_map(r)) for r in range(n_pp)] * 2
        + [pl.BlockSpec((CMP_BLOCK, LANES), fixed2),
           pl.BlockSpec((CMP_BLOCK, LANES, LANES), fixed3),
           pl.BlockSpec((CMP_BLOCK, LANES, LANES), fixed3),
           pl.BlockSpec((1, LANES), fixed2)],
        out_specs=[pl.BlockSpec((1, nblk_total, LANES), out_map)] * 2,
        scratch_shapes=[pltpu.VMEM((n_pp * PAGE_SIZE, LANES), F32)] * 2,
    )
    o = jax.ShapeDtypeStruct((nb, nblk_total, LANES), F32)
    return pl.pallas_call(
        functools.partial(_compress_sample_kernel, n_pp=n_pp),
        grid_spec=grid_spec,
        out_shape=[o, o],
        compiler_params=_params(("parallel", "arbitrary")),
        name="compress_sample",
    )(page_table, *([cache_k] * n_pp), *([cache_v] * n_pp), pe128, wk_bd, wv_bd, kg)


def _pages_t(cache):
    return cache.transpose(0, 2, 3, 1)


def _group_q(q, g):
    z = jnp.zeros_like(q)
    return jnp.concatenate([q, z], axis=1) if g == 0 else jnp.concatenate([z, q], axis=1)


def _nsa_sample_select_kernel(q_ref, ck_ref, cv_ref, ocmp_ref, idx_ref, *, q_pos):
    q = q_ref[0]
    ck = ck_ref[0].astype(BF16)
    cv = cv_ref[0].astype(BF16)
    nb = ck.shape[0]
    blk_n = lax.broadcasted_iota(I32, (1, nb), 1)
    cmask = blk_n * CMP_BLOCK + (CMP_BLOCK - 1) <= q_pos
    ratio = SEL_BLOCK // CMP_BLOCK
    sblk = blk_n // ratio
    cur = q_pos // SEL_BLOCK
    lane_f = blk_n.astype(F32)
    slot = lax.broadcasted_iota(I32, (1, LANES), 1)
    row8 = lax.broadcasted_iota(I32, (8, LANES), 0)
    idx_all = jnp.full((8, LANES), -1, I32)
    neg_inf = float("-inf")
    for g in range(G_NSA):
        qg = _group_q(q[g * HPG:(g + 1) * HPG], g)
        s = jnp.where(cmask, _dot_nt(qg, ck), NEG_INF)
        m = jnp.max(s, axis=-1, keepdims=True)
        e = jnp.where(cmask, jnp.exp(s - m), 0.0)
        den = jnp.sum(e, axis=-1, keepdims=True)
        pr = e / jnp.where(den > 0.0, den, 1.0)
        o = _dot(pr.astype(BF16), cv)
        ocmp_ref[0, g * HPG:(g + 1) * HPG, :] = o[:, g * HEAD_DIM:(g + 1) * HEAD_DIM]
        imp = jnp.sum(pr, axis=0, keepdims=True)
        even = (blk_n & 1) == 0
        pooled = imp + jnp.where(even, pltpu.roll(imp, nb - 1, 1), pltpu.roll(imp, 1, 1))
        forced = (sblk == 0) | (sblk == cur) | (sblk == cur - 1)
        score = jnp.where(sblk < cur, pooled + FORCE_BONUS * forced.astype(F32), neg_inf)
        idx_g = jnp.full((1, LANES), -1, I32)
        for t in range(TOP_N - 1):
            mx = jnp.max(score, axis=-1, keepdims=True)
            first = jnp.min(jnp.where(score == mx, lane_f, float(4 * nb)), axis=-1, keepdims=True)
            blk = first.astype(I32) // ratio
            idx_g = jnp.where((slot == t) & (mx > neg_inf), blk, idx_g)
            score = jnp.where(sblk == blk, neg_inf, score)
        idx_all = jnp.where(row8 == g, jnp.broadcast_to(idx_g, (8, LANES)), idx_all)
    idx_ref[0] = idx_all


def _nsa_sample_select(q16, ck, cv, q_pos):
    nb = q16.shape[0]
    nblk = ck.shape[1]
    m3 = lambda b: (b, 0, 0)
    return pl.pallas_call(
        functools.partial(_nsa_sample_select_kernel, q_pos=q_pos),
        grid=(nb,),
        in_specs=[pl.BlockSpec((1, H_NSA, HEAD_DIM), m3),
                  pl.BlockSpec((1, nblk, LANES), m3),
                  pl.BlockSpec((1, nblk, LANES), m3)],
        out_specs=[pl.BlockSpec((1, H_NSA, HEAD_DIM), m3), pl.BlockSpec((1, 8, LANES), m3)],
        out_shape=[jax.ShapeDtypeStruct((nb, H_NSA, HEAD_DIM), F32),
                   jax.ShapeDtypeStruct((nb, 8, LANES), I32)],
        compiler_params=_params(("parallel",)),
        name="nsa_sample_select",
    )(q16, ck, cv)


def _nsa_sample_attend_kernel(pt_ref, idx_ref, q_ref, k_hbm, v_hbm, kn_ref, vn_ref, kwn_ref, vwn_ref,
                              wk_ref, wv_ref, ocmp_ref, bg_ref, gate_ref, o_ref, kbuf, vbuf, sem,
                              *, q_pos):
    b = pl.program_id(0)
    n_slots = kbuf.shape[1]
    n_pages = pt_ref.shape[1]

    def copies(g, s):
        blk = jnp.maximum(idx_ref[b * G_NSA + g, s], 0)
        pid = pt_ref[b, jnp.minimum(blk // 2, n_pages - 1)]
        rows = pl.ds(g * HEAD_DIM, HEAD_DIM)
        return (pltpu.make_async_copy(k_hbm.at[pid, rows], kbuf.at[g, s], sem.at[0, g]),
                pltpu.make_async_copy(v_hbm.at[pid, rows], vbuf.at[g, s], sem.at[1, g]))

    for g in range(G_NSA):
        for s in range(n_slots):
            for c in copies(g, s):
                c.start()

    q = q_ref[0]
    bg = bg_ref[0]
    col = lax.broadcasted_iota(I32, (HPG, LANES), 1)
    hrow = lax.broadcasted_iota(I32, (HPG, LANES), 0)
    key_half = lax.broadcasted_iota(I32, (1, PAGE_SIZE), 1) // SEL_BLOCK
    wlen = wk_ref.shape[3]
    w_pos = q_pos - wlen + lax.broadcasted_iota(I32, (1, wlen), 1)
    wmask = (w_pos > q_pos - WINDOW) & (w_pos >= 0)

    def new_token(x_ref, g):
        return x_ref[0][:, g * HEAD_DIM:(g + 1) * HEAD_DIM].astype(BF16).astype(F32)

    for g in range(G_NSA):
        qg = q[g * HPG:(g + 1) * HPG]
        qf = qg.astype(F32)
        sw = jnp.where(wmask, _dot(qg, wk_ref[0, g].astype(BF16)), NEG_INF)
        sw_new = jnp.sum(qf * new_token(kwn_ref, g), axis=-1, keepdims=True)
        mw = jnp.maximum(jnp.max(sw, axis=-1, keepdims=True), sw_new)
        pw = jnp.where(wmask, jnp.exp(sw - mw), 0.0)
        pw_new = jnp.exp(sw_new - mw)
        o_win = _dot_nt(pw.astype(BF16), wv_ref[0, g].astype(BF16))
        o_win = o_win + pw_new.astype(BF16).astype(F32) * new_token(vwn_ref, g)
        o_win = o_win / (jnp.sum(pw, axis=-1, keepdims=True) + pw_new)
        for s in range(n_slots):
            for c in copies(g, s):
                c.wait()
        scores, masks = [], []
        for s in range(n_slots):
            blk = idx_ref[b * G_NSA + g, s]
            ok = (key_half == jnp.maximum(blk, 0) % 2) & (blk >= 0)
            scores.append(jnp.where(ok, _dot(qg, kbuf[g, s].astype(BF16)), NEG_INF))
            masks.append(ok)
        s_new = jnp.sum(qf * new_token(kn_ref, g), axis=-1, keepdims=True)
        m = s_new
        for sc in scores:
            m = jnp.maximum(m, jnp.max(sc, axis=-1, keepdims=True))
        p_new = jnp.exp(s_new - m)
        den = p_new
        acc = p_new.astype(BF16).astype(F32) * new_token(vn_ref, g)
        for s in range(n_slots):
            p = jnp.where(masks[s], jnp.exp(scores[s] - m), 0.0)
            den = den + jnp.sum(p, axis=-1, keepdims=True)
            acc = acc + _dot_nt(p.astype(BF16), vbuf[g, s].astype(BF16))
        o_sel = acc / den
        gates = []
        for br in range(3):
            pick = col == g * HPG * 3 + hrow * 3 + br
            gates.append(jnp.sum(jnp.where(pick, jnp.broadcast_to(bg, (HPG, LANES)), 0.0), axis=-1, keepdims=True))
        rows = slice(g * HPG, (g + 1) * HPG)
        mixed = gates[0] * ocmp_ref[0, rows, :] + gates[1] * o_sel + gates[2] * o_win
        o_ref[0, rows, :] = mixed * gate_ref[0, rows, :]


def _nsa_sample_attend(page_table, idx, q16, sel_kt, sel_vt, ks_new, vs_new, kw_new, vw_new, win_kt, win_vt, ocmp,
                       bg, gate16, q_pos):
    nb = q16.shape[0]
    n_slots = TOP_N - 1
    wlen = win_kt.shape[3]
    m3 = lambda b, pt, ix: (b, 0, 0)
    m4 = lambda b, pt, ix: (b, 0, 0, 0)
    row = pl.BlockSpec((1, 1, LANES), m3)
    heads = pl.BlockSpec((1, H_NSA, HEAD_DIM), m3)
    win = pl.BlockSpec((1, G_NSA, HEAD_DIM, wlen), m4)
    grid_spec = pltpu.PrefetchScalarGridSpec(
        num_scalar_prefetch=2,
        grid=(nb,),
        in_specs=[heads, pl.BlockSpec(memory_space=pl.ANY), pl.BlockSpec(memory_space=pl.ANY),
                  row, row, row, row, win, win, heads, row, heads],
        out_specs=heads,
        scratch_shapes=[pltpu.VMEM((G_NSA, n_slots, HEAD_DIM, PAGE_SIZE), F32),
                        pltpu.VMEM((G_NSA, n_slots, HEAD_DIM, PAGE_SIZE), F32),
                        pltpu.SemaphoreType.DMA((2, G_NSA))],
    )
    return pl.pallas_call(
        functools.partial(_nsa_sample_attend_kernel, q_pos=q_pos),
        grid_spec=grid_spec,
        out_shape=jax.ShapeDtypeStruct((nb, H_NSA, HEAD_DIM), F32),
        compiler_params=_params(("arbitrary",)),
        name="nsa_sample_attend",
    )(page_table, idx, q16, sel_kt, sel_vt, ks_new, vs_new, kw_new, vw_new, win_kt, win_vt, ocmp, bg, gate16)


def _layer0_sample(x2d, past_len, state, cache_k, cache_v, page_table, norm_g, w_in_bf, ret_gn_g, sb_qn_g, sb_kn_g,
                   w_out_bf):
    nb = x2d.shape[0]
    cos, sin = _rope_tables(jnp.full((nb,), past_len, I32))
    ret, sbq, sbg, sbk, sbv = _inproj0(x2d, norm_g, w_in_bf, cos, sin, _pair_gain(sb_qn_g), _pair_gain(sb_kn_g),
                                       nb, nb, False)
    mix_ret, state_new = _ret_sample(ret, state, ret_gn_g)
    mix_sb = _sb_sample(sbq.reshape(nb, H_SB, HEAD_DIM), sbg.reshape(nb, H_SB, HEAD_DIM),
                        _pages_t(cache_k), _pages_t(cache_v), page_table)
    mix_sb = mix_sb.reshape(nb, H_SB * HEAD_DIM)
    half = w_out_bf.shape[0] // 2
    y = _outproj(x2d, [mix_ret, mix_sb], [w_out_bf[:half], w_out_bf[half:]], nb)
    return y, state_new, sbk, sbv


def _layer1_sample(x2d, past_len, cmp_k, cmp_v, sel_k, sel_v, win_k, win_v, page_table, norm_g, w_in_bf, qn_g,
                   cmp_kn_g, sel_kn_g, win_kn_g, pe128, wk_bd, wv_bd, w_out_bf):
    nb = x2d.shape[0]
    (qn, gate, bg, kc, vc, ks, vs, kw, vw) = _inproj1(
        x2d, norm_g, w_in_bf, _pair_gain(qn_g), _pair_gain(sel_kn_g), _pair_gain(win_kn_g), nb, nb, False)
    n_phys = cmp_k.shape[0]
    pool = lambda t: _pages_t(t).reshape(n_phys, KV_W, PAGE_SIZE)
    ck, cv = _compress_sample(pool(cmp_k), pool(cmp_v), page_table, pe128, wk_bd, wv_bd, _pair_gain(cmp_kn_g))
    q16 = qn.reshape(nb, H_NSA, HEAD_DIM)
    ocmp, idx = _nsa_sample_select(q16, ck, cv, past_len)
    idx2 = idx[:, :G_NSA, :TOP_N].reshape(nb * G_NSA, TOP_N)
    mix = _nsa_sample_attend(page_table, idx2, q16, pool(sel_k), pool(sel_v), ks[:, None, :], vs[:, None, :],
                             kw[:, None, :], vw[:, None, :], _pages_t(win_k), _pages_t(win_v), ocmp,
                             bg[:, None, :], gate.reshape(nb, H_NSA, HEAD_DIM), past_len)
    y = _outproj(x2d, [mix.reshape(nb, NSA_W)], [w_out_bf], nb)
    gd = (nb, 1, G_NSA, HEAD_DIM)
    wk_new = jnp.concatenate([win_k[:, 1:], kw.reshape(gd)], axis=1)
    wv_new = jnp.concatenate([win_v[:, 1:], vw.reshape(gd)], axis=1)
    return y, kc, vc, ks, vs, wk_new, wv_new


def kernel(x_prompt, x_sample, state_ret, cache_sb_k, cache_sb_v, cache_cmp_k, cache_cmp_v, cache_sel_k,
           cache_sel_v, cache_win_k, cache_win_v, page_table, norm0_g, w_in0, ret_gn_g, sb_qn_g, sb_kn_g,
           w_out0, norm1_g, w_in1, nsa_qn_g, cmp_kn_g, sel_kn_g, win_kn_g, cmp_pe, w_cmp_k, w_cmp_v, w_out1):
    batch, seq, d = x_prompt.shape
    nb = x_sample.shape[0]
    past_len = page_table.shape[1] * PAGE_SIZE
    tm = 256
    w_in0_bf = w_in0.astype(BF16)
    w_out0_bf = w_out0.astype(BF16)
    w_in1_bf = _pad_w_in1(w_in1)
    w_out1_bf = w_out1.astype(BF16)
    pe128 = jnp.tile(cmp_pe, (1, G_NSA))
    wk_bd = _blockdiag_w(w_cmp_k)
    wv_bd = _blockdiag_w(w_cmp_v)

    xp = x_prompt.reshape(batch * seq, d)
    y1p, ret_p, sbk_p, sbv_p = _layer0_prompt(xp, batch, seq, norm0_g, w_in0_bf, ret_gn_g, sb_qn_g, sb_kn_g,
                                              w_out0_bf, tm)
    y2p, ckp, cvp, skp, svp, wkp, wvp = _layer1_prompt(y1p, batch, seq, norm1_g, w_in1_bf, nsa_qn_g, cmp_kn_g,
                                                        sel_kn_g, win_kn_g, pe128, wk_bd, wv_bd, w_out1_bf, tm)
    xs = x_sample.reshape(nb, d)
    y1s, ret_s, sbk_s, sbv_s = _layer0_sample(xs, past_len, state_ret, cache_sb_k, cache_sb_v, page_table, norm0_g,
                                              w_in0_bf, ret_gn_g, sb_qn_g, sb_kn_g, w_out0_bf)
    y2s, cks, cvs, sks, svs, wks, wvs = _layer1_sample(y1s, past_len, cache_cmp_k, cache_cmp_v, cache_sel_k,
                                                        cache_sel_v, cache_win_k, cache_win_v, page_table, norm1_g,
                                                        w_in1_bf, nsa_qn_g, cmp_kn_g, sel_kn_g, win_kn_g, pe128,
                                                        wk_bd, wv_bd, w_out1_bf)
    keep = min(WINDOW, seq)
    gd = (nb, 1, G_NSA, HEAD_DIM)

    def rows_major(t):
        return t.transpose(0, 3, 1, 2)

    def groups(t):
        return rows_major(t.reshape(batch, G_NSA, HEAD_DIM, seq))

    return (y2p.reshape(batch, seq, d), y2s.reshape(nb, 1, d), ret_p, ret_s,
            rows_major(sbk_p), rows_major(sbv_p),
            sbk_s.reshape(nb, 1, H_SB, HEAD_DIM), sbv_s.reshape(nb, 1, H_SB, HEAD_DIM),
            groups(ckp), groups(cvp), groups(skp), groups(svp),
            groups(wkp)[:, seq - keep:], groups(wvp)[:, seq - keep:],
            cks.reshape(gd), cvs.reshape(gd), sks.reshape(gd), svs.reshape(gd), wks, wvs)
```

```python
import functools

import jax
import jax.numpy as jnp
import numpy as np
from jax import lax
from jax.experimental import pallas as pl
from jax.experimental.pallas import tpu as pltpu

F32 = jnp.float32
BF16 = jnp.bfloat16
I32 = jnp.int32

HEAD_DIM = 64
LANES = 128
H_RET = 8
H_SB = 8
RET_CHUNK = 128
ROPE_BASE = 10000.0
Q_BLOCK = 128
H_NSA = 16
G_NSA = 2
HPG = H_NSA // G_NSA
CMP_BLOCK = 32
SEL_BLOCK = 64
TOP_N = 16
WINDOW = 512
FORCE_BONUS = 1000.0
NEG_INF = -1e30
EPS = 1e-6
PAGE_SIZE = 128
NSA_KEY_TILE = 256
SCALE = HEAD_DIM ** -0.5
VMEM_LIMIT = 56 * 1024 * 1024


def _dot(a, b):
    return jnp.dot(a, b, preferred_element_type=F32)


def _dot_nt(a, b):
    return lax.dot_general(a, b, (((1,), (1,)), ((), ())), preferred_element_type=F32)


def _dot_tn(a, b):
    return lax.dot_general(a, b, (((0,), (0,)), ((), ())), preferred_element_type=F32)


def _split_bf16(x):
    hi = x.astype(BF16)
    lo = (x - hi.astype(F32)).astype(BF16)
    return hi, lo


def _sigmoid(x):
    return 1.0 / (1.0 + jnp.exp(-x))


def _silu(x):
    return x * _sigmoid(x)


def _lane_lo(shape):
    return lax.broadcasted_iota(I32, shape, len(shape) - 1) < HEAD_DIM


def _pair_sum(x, lo):
    s_lo = jnp.sum(jnp.where(lo, x, 0.0), axis=-1, keepdims=True)
    s_hi = jnp.sum(jnp.where(lo, 0.0, x), axis=-1, keepdims=True)
    return jnp.where(lo, s_lo, s_hi)


def _pair_rmsnorm(x, g, lo):
    ms = _pair_sum(x * x, lo) * (1.0 / HEAD_DIM)
    return x * lax.rsqrt(ms + EPS) * g


def _rope_pair(x, cos, sin_signed):
    lane = lax.broadcasted_iota(I32, x.shape, 1)
    first = (lane & 32) == 0
    partner = jnp.where(first, pltpu.roll(x, 96, 1), pltpu.roll(x, 32, 1))
    return x * cos + partner * sin_signed


def _params(sem):
    return pltpu.CompilerParams(dimension_semantics=sem, vmem_limit_bytes=VMEM_LIMIT)


def _store_head_major(dst_ref, t, pair, r):
    cols = slice(r * LANES, (r + 1) * LANES)
    dst_ref[0, 2 * pair, :, cols] = t[:HEAD_DIM]
    dst_ref[0, 2 * pair + 1, :, cols] = t[HEAD_DIM:]


def _inproj0_kernel(x_ref, g_ref, w_ref, cos_ref, sin_ref, qg_ref, kg_ref,
                    ret_ref, sbq_ref, sbg_ref, *kv_refs, prompt):
    x = x_ref[...]
    xn = x * lax.rsqrt(jnp.mean(x * x, axis=-1, keepdims=True) + EPS) * g_ref[...]
    xb = xn.astype(BF16)
    cos = cos_ref[...]
    sin = sin_ref[...]
    lo = _lane_lo((1, LANES))
    width = H_RET * HEAD_DIM

    def group(i):
        return _dot(xb, w_ref[:, i * width:(i + 1) * width])

    def tiles(a):
        return [a[:, t * LANES:(t + 1) * LANES] for t in range(width // LANES)]

    for t, v in enumerate(tiles(group(0))):
        ret_ref[:, t * LANES:(t + 1) * LANES] = _rope_pair(v, cos, sin)
    for t, v in enumerate(tiles(group(1))):
        ret_ref[:, width + t * LANES:width + (t + 1) * LANES] = _rope_pair(v, cos, sin) * SCALE
    ret_ref[:, 2 * width:3 * width] = group(2)
    ret_ref[:, 3 * width:4 * width] = group(3)
    for t, v in enumerate(tiles(group(4))):
        sbq_ref[:, t * LANES:(t + 1) * LANES] = (_pair_rmsnorm(v, qg_ref[...], lo) * SCALE).astype(BF16)
    sk = [_pair_rmsnorm(v, kg_ref[...], lo) for v in tiles(group(5))]
    sv = tiles(group(6))
    if prompt:
        kb_ref, vt_ref, kt_out, vt_out = kv_refs
        for t in range(width // LANES):
            kb_ref[:, t * LANES:(t + 1) * LANES] = sk[t].astype(BF16)
            for r in range(x.shape[0] // LANES):
                rows = slice(r * LANES, (r + 1) * LANES)
                vt = sv[t][rows].T
                _store_head_major(kt_out, sk[t][rows].T, t, r)
                _store_head_major(vt_out, vt, t, r)
                vt_ref[r, t] = vt.astype(BF16)
    else:
        k_ref, v_ref = kv_refs
        for t in range(width // LANES):
            k_ref[:, t * LANES:(t + 1) * LANES] = sk[t]
            v_ref[:, t * LANES:(t + 1) * LANES] = sv[t]
    sbg_ref[...] = _silu(group(7))


def _inproj0(x2d, norm_g, w_bf, cos, sin, qg, kg, tm, seq, prompt):
    rows, d = x2d.shape
    width = H_RET * HEAD_DIM
    n = w_bf.shape[1]
    grid = (rows // tm,)
    per_seq = seq // tm
    row = lambda i: (i, 0)
    fixed = lambda i: (0, 0)
    pos = lambda i: (i % per_seq, 0)
    if prompt:
        npair = width // LANES
        head_major = pl.BlockSpec((1, H_SB, HEAD_DIM, tm), lambda i: (i // per_seq, 0, 0, i % per_seq))
        hm_shape = jax.ShapeDtypeStruct((rows // seq, H_SB, HEAD_DIM, seq), F32)
        extra_specs = [pl.BlockSpec((tm, width), row),
                       pl.BlockSpec((tm // LANES, npair, LANES, LANES), lambda i: (i, 0, 0, 0)),
                       head_major, head_major]
        extra_shapes = [jax.ShapeDtypeStruct((rows, width), BF16),
                        jax.ShapeDtypeStruct((rows // LANES, npair, LANES, LANES), BF16),
                        hm_shape, hm_shape]
    else:
        extra_specs = [pl.BlockSpec((tm, width), row)] * 2
        extra_shapes = [jax.ShapeDtypeStruct((rows, width), F32)] * 2
    return pl.pallas_call(
        functools.partial(_inproj0_kernel, prompt=prompt),
        grid=grid,
        in_specs=[
            pl.BlockSpec((tm, d), row),
            pl.BlockSpec((1, d), fixed),
            pl.BlockSpec((d, n), fixed),
            pl.BlockSpec((tm, LANES), pos),
            pl.BlockSpec((tm, LANES), pos),
            pl.BlockSpec((1, LANES), fixed),
            pl.BlockSpec((1, LANES), fixed),
        ],
        out_specs=[
            pl.BlockSpec((tm, 4 * width), row),
            pl.BlockSpec((tm, width), row),
            pl.BlockSpec((tm, width), row),
        ] + extra_specs,
        out_shape=[
            jax.ShapeDtypeStruct((rows, 4 * width), F32),
            jax.ShapeDtypeStruct((rows, width), BF16),
            jax.ShapeDtypeStruct((rows, width), F32),
        ] + extra_shapes,
        compiler_params=_params(("parallel",)),
        name="inproj0",
    )(x2d, norm_g.reshape(1, d), w_bf, cos, sin, qg, kg)


def _ret_prompt_kernel(ret_ref, inner_ref, qdec_ref, kdec_ref, cdec_ref, gn_ref, mix_ref, state_ref, s_scr):
    c = pl.program_id(1)
    npair = H_RET // 2
    width = H_RET * HEAD_DIM

    @pl.when(c == 0)
    def _():
        s_scr[...] = jnp.zeros_like(s_scr)

    def tile(group, p):
        return ret_ref[:, group * width + p * LANES:group * width + (p + 1) * LANES]

    lo = _lane_lo((ret_ref.shape[0], LANES))
    same_head = ((lax.broadcasted_iota(I32, (LANES, LANES), 0) < HEAD_DIM)
                 == (lax.broadcasted_iota(I32, (LANES, LANES), 1) < HEAD_DIM))
    qs = [tile(0, p).astype(BF16) for p in range(npair)]
    ks = [tile(1, p) for p in range(npair)]
    kbs = [k.astype(BF16) for k in ks]
    vbs = [tile(2, p).astype(BF16) for p in range(npair)]
    states = [s_scr[p] for p in range(npair)]
    zero = jnp.zeros_like(qs[0])
    att_lo = [_dot_nt(jnp.where(lo, qs[p], zero), kbs[p]) for p in range(npair)]
    att_hi = [_dot_nt(jnp.where(lo, zero, qs[p]), kbs[p]) for p in range(npair)]
    cross = [_dot(qs[p], states[p].astype(BF16)) for p in range(npair)]
    kv = [_dot_tn((ks[p] * kdec_ref[p]).astype(BF16), vbs[p]) for p in range(npair)]
    a_lo = [(att_lo[p] * inner_ref[2 * p]).astype(BF16) for p in range(npair)]
    a_hi = [(att_hi[p] * inner_ref[2 * p + 1]).astype(BF16) for p in range(npair)]
    o_lo = [_dot(a_lo[p], vbs[p]) for p in range(npair)]
    o_hi = [_dot(a_hi[p], vbs[p]) for p in range(npair)]
    for p in range(npair):
        s_new = states[p] * cdec_ref[p] + jnp.where(same_head, kv[p], 0.0)
        s_scr[p] = s_new
        state_ref[0, p] = s_new
        o = jnp.where(lo, o_lo[p], o_hi[p]) + cross[p] * qdec_ref[p]
        mu = _pair_sum(o, lo) * (1.0 / HEAD_DIM)
        oc = o - mu
        var = _pair_sum(oc * oc, lo) * (1.0 / HEAD_DIM)
        on = oc * lax.rsqrt(var + EPS) * gn_ref[p]
        mix_ref[:, p * LANES:(p + 1) * LANES] = on * _silu(tile(3, p))


def _ret_tables(chunk):
    h = jnp.arange(H_RET, dtype=F32)
    lg = jnp.log1p(-jnp.exp2(-5.0 - h))
    idx = jnp.arange(chunk, dtype=F32)
    diff = idx[:, None] - idx[None, :]
    inner = jnp.where(diff >= 0, jnp.exp(lg[:, None, None] * jnp.maximum(diff, 0.0)), 0.0)
    q_dec = jnp.exp(lg[:, None] * (idx + 1.0))
    k_dec = jnp.exp(lg[:, None] * (chunk - 1.0 - idx))
    c_dec = jnp.exp(lg * chunk)

    def lanes(t):
        t = jnp.repeat(t[:, :, None], HEAD_DIM, axis=2)
        return jnp.concatenate([t[0::2], t[1::2]], axis=2)

    cd = jnp.repeat(c_dec[:, None, None], HEAD_DIM, axis=2)
    cd = jnp.concatenate([cd[0::2], cd[1::2]], axis=2)
    return inner, lanes(q_dec), lanes(k_dec), cd


def _ret_prompt(ret, gn_g, batch, seq):
    width = H_RET * HEAD_DIM
    npair = H_RET // 2
    c = RET_CHUNK
    nc = seq // c
    inner, qdec, kdec, cdec = _ret_tables(c)
    gn = gn_g.reshape(npair, 1, LANES)
    tab = lambda b, i: (0, 0, 0)
    return pl.pallas_call(
        _ret_prompt_kernel,
        grid=(batch, nc),
        in_specs=[
            pl.BlockSpec((c, 4 * width), lambda b, i: (b * nc + i, 0)),
            pl.BlockSpec((H_RET, c, c), tab),
            pl.BlockSpec((npair, c, LANES), tab),
            pl.BlockSpec((npair, c, LANES), tab),
            pl.BlockSpec((npair, 1, LANES), tab),
            pl.BlockSpec((npair, 1, LANES), tab),
        ],
        out_specs=[
            pl.BlockSpec((c, width), lambda b, i: (b * nc + i, 0)),
            pl.BlockSpec((1, npair, LANES, LANES), lambda b, i: (b, 0, 0, 0)),
        ],
        out_shape=[
            jax.ShapeDtypeStruct((batch * seq, width), F32),
            jax.ShapeDtypeStruct((batch, npair, LANES, LANES), F32),
        ],
        scratch_shapes=[pltpu.VMEM((npair, LANES, LANES), F32)],
        compiler_params=_params(("parallel", "arbitrary")),
        name="retention_prompt",
    )(ret, inner, qdec, kdec, cdec, gn)


def _unpair_state(sp):
    a = sp[:, :, :HEAD_DIM, :HEAD_DIM]
    b = sp[:, :, HEAD_DIM:, HEAD_DIM:]
    return jnp.stack([a, b], axis=2).reshape(sp.shape[0], -1, HEAD_DIM, HEAD_DIM)


SB_LOG_ZERO = -104.0


def _sb_prompt_kernel(q_ref, k_ref, vt_ref, g_ref, o_ref, acc_scr):
    i = pl.program_id(1)
    tq = q_ref.shape[0]
    npair = H_SB // 2
    rr = lax.broadcasted_iota(I32, (tq, tq), 0)
    cc = lax.broadcasted_iota(I32, (tq, tq), 1)
    lower = (cc >= rr).astype(BF16)
    diag = rr < cc
    zeros = jnp.zeros((HEAD_DIM, tq), F32)
    qts = []
    for p in range(npair):
        t = q_ref[:, p * LANES:(p + 1) * LANES].astype(F32).T
        qts.append(jnp.concatenate([t[:HEAD_DIM], zeros], axis=0).astype(BF16))
        qts.append(jnp.concatenate([zeros, t[HEAD_DIM:]], axis=0).astype(BF16))

    def sweep(j, rs, diag_mask):
        start = pl.multiple_of(j * tq, tq)
        kjs = [k_ref[pl.ds(start, tq), p * LANES:(p + 1) * LANES] for p in range(npair)]
        vts = [vt_ref[j, p] for p in range(npair)]
        zs = [_dot(kjs[h // 2], qt) for h, qt in enumerate(qts)]
        ls_pos, lms, parts = [], [], []
        for z in zs:
            lsp = jnp.minimum(z, 0.0) - jnp.log1p(jnp.exp(-jnp.abs(z)))
            lm = lsp - z
            if diag_mask is not None:
                lm = jnp.where(diag_mask, lm, 0.0)
            ls_pos.append(lsp)
            lms.append(lm)
            parts.append(_split_bf16(lm))
        s_incl = [_dot(lower, hi) + _dot(lower, lo_part) for hi, lo_part in parts]
        probs = []
        for h in range(H_SB):
            a = jnp.exp(ls_pos[h] + (s_incl[h] - lms[h]) + rs[h])
            if diag_mask is not None:
                a = jnp.where(diag_mask, a, 0.0)
            probs.append(a.astype(BF16))
        for h in range(H_SB):
            half = h % 2
            pv = _dot(vts[h // 2][half * HEAD_DIM:(half + 1) * HEAD_DIM, :], probs[h])
            acc_scr[h] = pv if diag_mask is not None else acc_scr[h] + pv
        return [rs[h] + s_incl[h][0:1, :] for h in range(H_SB)]

    def alive_of(rs):
        worst = rs[0]
        for r in rs[1:]:
            worst = jnp.maximum(worst, r)
        return jnp.max(worst) > SB_LOG_ZERO

    rs = sweep(i, [jnp.zeros((1, tq), F32)] * H_SB, diag)

    def cond(carry):
        t, alive = carry[0], carry[1]
        return (t < i) & alive

    def body(carry):
        t = carry[0]
        rs = sweep(i - 1 - t, list(carry[2:]), None)
        return (t + 1, alive_of(rs)) + tuple(rs)

    lax.while_loop(cond, body, (jnp.int32(0), alive_of(rs)) + tuple(rs))
    for p in range(npair):
        both = jnp.concatenate([acc_scr[2 * p], acc_scr[2 * p + 1]], axis=0).T
        o_ref[:, p * LANES:(p + 1) * LANES] = both * g_ref[:, p * LANES:(p + 1) * LANES]


def _sb_prompt(sbq, sbk_b, sbvt, sbg, batch, seq):
    width = H_SB * HEAD_DIM
    npair = H_SB // 2
    tq = Q_BLOCK
    nq = seq // tq
    qmap = lambda b, i: (b * nq + i, 0)
    return pl.pallas_call(
        _sb_prompt_kernel,
        grid=(batch, nq),
        in_specs=[
            pl.BlockSpec((tq, width), qmap),
            pl.BlockSpec((seq, width), lambda b, i: (b, 0)),
            pl.BlockSpec((nq, npair, LANES, tq), lambda b, i: (b, 0, 0, 0)),
            pl.BlockSpec((tq, width), qmap),
        ],
        out_specs=pl.BlockSpec((tq, width), qmap),
        out_shape=jax.ShapeDtypeStruct((batch * seq, width), F32),
        scratch_shapes=[pltpu.VMEM((H_SB, HEAD_DIM, tq), F32)],
        compiler_params=_params(("parallel", "arbitrary")),
        name="sb_prompt",
    )(sbq, sbk_b, sbvt, sbg)


def _outproj_kernel(*refs, n_mix):
    x_ref = refs[0]
    mix_refs = refs[1:1 + n_mix]
    w_refs = refs[1 + n_mix:1 + 2 * n_mix]
    y_ref = refs[1 + 2 * n_mix]
    acc = x_ref[...]
    for m_ref, w_ref in zip(mix_refs, w_refs):
        acc = acc + _dot(m_ref[...].astype(BF16), w_ref[...])
    y_ref[...] = acc


def _outproj(x2d, mixes, ws, tm):
    rows, d = x2d.shape
    n_mix = len(mixes)
    row = lambda i: (i, 0)
    fixed = lambda i: (0, 0)
    in_specs = [pl.BlockSpec((tm, d), row)]
    in_specs += [pl.BlockSpec((tm, m.shape[1]), row) for m in mixes]
    in_specs += [pl.BlockSpec(w.shape, fixed) for w in ws]
    return pl.pallas_call(
        functools.partial(_outproj_kernel, n_mix=n_mix),
        grid=(rows // tm,),
        in_specs=in_specs,
        out_specs=pl.BlockSpec((tm, d), row),
        out_shape=jax.ShapeDtypeStruct((rows, d), F32),
        compiler_params=_params(("parallel",)),
        name="outproj",
    )(x2d, *mixes, *ws)


def _rope_tables(pos):
    half = HEAD_DIM // 2
    inv = ROPE_BASE ** (-jnp.arange(half, dtype=F32) / half)
    ang = pos.astype(F32)[:, None] * inv[None, :]
    cos, sin = jnp.cos(ang), jnp.sin(ang)
    cos128 = jnp.tile(cos, (1, LANES // half))
    sin128 = jnp.tile(jnp.concatenate([-sin, sin], axis=1), (1, LANES // HEAD_DIM))
    return cos128, sin128


def _pair_gain(g):
    return jnp.tile(g.astype(F32), LANES // HEAD_DIM).reshape(1, LANES)


def _layer0_prompt(x2d, batch, seq, norm_g, w_in_bf, ret_gn_g, sb_qn_g, sb_kn_g, w_out_bf, tm):
    cos, sin = _rope_tables(jnp.arange(seq, dtype=I32))
    ret, sbq, sbg, sbk_b, sbvt, sbk_hm, sbv_hm = _inproj0(x2d, norm_g, w_in_bf, cos, sin, _pair_gain(sb_qn_g),
                                                          _pair_gain(sb_kn_g), tm, seq, True)
    mix_ret, state_pairs = _ret_prompt(ret, ret_gn_g, batch, seq)
    mix_sb = _sb_prompt(sbq, sbk_b, sbvt, sbg, batch, seq)
    half = w_out_bf.shape[0] // 2
    y = _outproj(x2d, [mix_ret, mix_sb], [w_out_bf[:half], w_out_bf[half:]], tm)
    return y, _unpair_state(state_pairs), sbk_hm, sbv_hm


NSA_W = H_NSA * HEAD_DIM
KV_W = G_NSA * HEAD_DIM
IN1_COLS = 2 * NSA_W + 6 * KV_W + LANES


def _inproj1_kernel(x_ref, g_ref, w_ref, qg_ref, skg_ref, wkg_ref, q_ref, gate_ref, bg_ref, *kv_refs, prompt):
    x = x_ref[...]
    xn = x * lax.rsqrt(jnp.mean(x * x, axis=-1, keepdims=True) + EPS) * g_ref[...]
    xb = xn.astype(BF16)
    lo = _lane_lo((1, LANES))
    a = _dot(xb, w_ref[:, 0:NSA_W])
    for t in range(NSA_W // LANES):
        v = a[:, t * LANES:(t + 1) * LANES]
        q_ref[:, t * LANES:(t + 1) * LANES] = (_pair_rmsnorm(v, qg_ref[...], lo) * SCALE).astype(BF16)
    a = _dot(xb, w_ref[:, NSA_W:NSA_W + 6 * KV_W])
    kc, vc, ks, vs, kw, vw = [a[:, t * KV_W:(t + 1) * KV_W] for t in range(6)]
    ks = _pair_rmsnorm(ks, skg_ref[...], lo)
    kw = _pair_rmsnorm(kw, wkg_ref[...], lo)
    six = (kc, vc, ks, vs, kw, vw)
    if prompt:
        t_out = kv_refs[:6]
        kc_ref, vc_ref, ksb_ref, kwb_ref, vst_ref, vwt_ref = kv_refs[6:]
        kc_ref[...] = kc
        vc_ref[...] = vc
        ksb_ref[...] = ks.astype(BF16)
        kwb_ref[...] = kw.astype(BF16)
        per_tile = NSA_KEY_TILE // LANES
        for r in range(x.shape[0] // LANES):
            rows = slice(r * LANES, (r + 1) * LANES)
            cols = slice((r % per_tile) * LANES, (r % per_tile + 1) * LANES)
            for src, dst in zip(six, t_out):
                t = src[rows].T
                dst[0, :, rows] = t
                if src is vs:
                    vst_ref[r // per_tile, :, cols] = t.astype(BF16)
                if src is vw:
                    vwt_ref[r // per_tile, :, cols] = t.astype(BF16)
    else:
        for src, dst in zip(six, kv_refs):
            dst[...] = src
    off = NSA_W + 6 * KV_W
    gate_ref[...] = _silu(_dot(xb, w_ref[:, off:off + NSA_W]))
    bg_ref[...] = _sigmoid(_dot(xb, w_ref[:, off + NSA_W:off + NSA_W + LANES]))


def _inproj1(x2d, norm_g, w_bf, qg, skg, wkg, tm, seq, prompt):
    rows, d = x2d.shape
    row = lambda i: (i, 0)
    fixed = lambda i: (0, 0)
    out_specs = [pl.BlockSpec((tm, NSA_W), row), pl.BlockSpec((tm, NSA_W), row), pl.BlockSpec((tm, LANES), row)]
    out_shape = [jax.ShapeDtypeStruct((rows, NSA_W), BF16), jax.ShapeDtypeStruct((rows, NSA_W), F32),
                 jax.ShapeDtypeStruct((rows, LANES), F32)]
    if prompt:
        per_seq = seq // tm
        kt = NSA_KEY_TILE
        out_specs += ([pl.BlockSpec((1, KV_W, tm), lambda i: (i // per_seq, 0, i % per_seq))] * 6
                      + [pl.BlockSpec((tm, KV_W), row)] * 4
                      + [pl.BlockSpec((tm // kt, LANES, kt), lambda i: (i, 0, 0))] * 2)
        out_shape += ([jax.ShapeDtypeStruct((rows // seq, KV_W, seq), F32)] * 6
                      + [jax.ShapeDtypeStruct((rows, KV_W), F32)] * 2
                      + [jax.ShapeDtypeStruct((rows, KV_W), BF16)] * 2
                      + [jax.ShapeDtypeStruct((rows // kt, LANES, kt), BF16)] * 2)
    else:
        out_specs += [pl.BlockSpec((tm, KV_W), row)] * 6
        out_shape += [jax.ShapeDtypeStruct((rows, KV_W), F32)] * 6
    return pl.pallas_call(
        functools.partial(_inproj1_kernel, prompt=prompt),
        grid=(rows // tm,),
        in_specs=[
            pl.BlockSpec((tm, d), row),
            pl.BlockSpec((1, d), fixed),
            pl.BlockSpec((d, IN1_COLS), fixed),
            pl.BlockSpec((1, LANES), fixed),
            pl.BlockSpec((1, LANES), fixed),
            pl.BlockSpec((1, LANES), fixed),
        ],
        out_specs=out_specs,
        out_shape=out_shape,
        compiler_params=_params(("parallel",)),
        name="inproj1",
    )(x2d, norm_g.reshape(1, d), w_bf, qg, skg, wkg)


def _pad_w_in1(w_in1):
    pad = IN1_COLS - w_in1.shape[1]
    return jnp.pad(w_in1, ((0, 0), (0, pad))).astype(BF16)


def _compress_rows(t_ref, pe_ref, w_ref, row0, nblk):
    acc = jnp.zeros((nblk, LANES), F32)
    for l in range(CMP_BLOCK):
        rows = t_ref[pl.ds(row0 + l, nblk, stride=CMP_BLOCK), :]
        acc = acc + _dot((rows + pe_ref[l:l + 1, :]).astype(BF16), w_ref[l])
    return acc


def _compress_prompt_kernel(kc_ref, vc_ref, pe_ref, wk_ref, wv_ref, kg_ref, ck_ref, cvt_ref):
    nblk = ck_ref.shape[0]
    lo = _lane_lo((1, LANES))
    ck = _pair_rmsnorm(_compress_rows(kc_ref, pe_ref, wk_ref, 0, nblk), kg_ref[...], lo)
    cv = _compress_rows(vc_ref, pe_ref, wv_ref, 0, nblk)
    ck_ref[...] = ck.astype(BF16)
    cvt_ref[0] = cv.T.astype(BF16)


def _blockdiag_w(w):
    z = jnp.zeros_like(w)
    top = jnp.concatenate([w, z], axis=2)
    bot = jnp.concatenate([z, w], axis=2)
    return jnp.concatenate([top, bot], axis=1).astype(BF16)


def _compress_prompt(kc, vc, pe128, wk_bd, wv_bd, kg, batch, seq):
    nblk = seq // CMP_BLOCK
    rowb = lambda b: (b, 0)
    fixed2 = lambda b: (0, 0)
    fixed3 = lambda b: (0, 0, 0)
    return pl.pallas_call(
        _compress_prompt_kernel,
        grid=(batch,),
        in_specs=[
            pl.BlockSpec((seq, LANES), rowb),
            pl.BlockSpec((seq, LANES), rowb),
            pl.BlockSpec((CMP_BLOCK, LANES), fixed2),
            pl.BlockSpec((CMP_BLOCK, LANES, LANES), fixed3),
            pl.BlockSpec((CMP_BLOCK, LANES, LANES), fixed3),
            pl.BlockSpec((1, LANES), fixed2),
        ],
        out_specs=[pl.BlockSpec((nblk, LANES), rowb), pl.BlockSpec((1, LANES, nblk), lambda b: (b, 0, 0))],
        out_shape=[jax.ShapeDtypeStruct((batch * nblk, LANES), BF16),
                   jax.ShapeDtypeStruct((batch, LANES, nblk), BF16)],
        compiler_params=_params(("parallel",)),
        name="compress_prompt",
    )(kc, vc, pe128, wk_bd, wv_bd, kg)


def _nsa_prompt_kernel(q_ref, ck_ref, cvt_ref, ks_ref, vst_ref, kw_ref, vwt_ref, bg_ref, gate_ref,
                       o_ref, m_scr, acc_scr, imp_scr, score_scr, chosen_scr):
    i = pl.program_id(1)
    tq = q_ref.shape[0]
    nb = ck_ref.shape[0]
    kt = NSA_KEY_TILE
    ratio = SEL_BLOCK // CMP_BLOCK
    nsb = nb // ratio
    q_pos = i * tq + lax.broadcasted_iota(I32, (1, tq), 1)

    zeros = jnp.zeros((HEAD_DIM, tq), F32)
    qts = []
    for p in range(H_NSA // 2):
        t = q_ref[:, p * LANES:(p + 1) * LANES].astype(F32).T
        for ht in (t[:HEAD_DIM], t[HEAD_DIM:]):
            parts = [zeros] * G_NSA
            parts[(2 * p) // HPG] = ht
            qts.append(jnp.concatenate(parts, axis=0).astype(BF16))
    group_of = [h // HPG for h in range(H_NSA)]

    ck = ck_ref[...]
    blk_n = lax.broadcasted_iota(I32, (nb, tq), 0)
    cmask = blk_n * CMP_BLOCK + (CMP_BLOCK - 1) <= q_pos
    cmp_scores = [_dot(ck, qt) for qt in qts]
    cmp_probs = []
    imps = [jnp.zeros((nb, tq), F32) for _ in range(G_NSA)]
    for h, s in enumerate(cmp_scores):
        s = jnp.where(cmask, s, NEG_INF)
        m = jnp.max(s, axis=0, keepdims=True)
        e = jnp.where(cmask, jnp.exp(s - m), 0.0)
        den = jnp.sum(e, axis=0, keepdims=True)
        pr = e / jnp.where(den > 0.0, den, 1.0)
        imps[group_of[h]] = imps[group_of[h]] + pr
        cmp_probs.append(pr.astype(BF16))
    o_cmp = [_dot(cvt_ref[0, group_of[h] * HEAD_DIM:(group_of[h] + 1) * HEAD_DIM, :], pr)
             for h, pr in enumerate(cmp_probs)]

    sblk = lax.broadcasted_iota(I32, (nsb, tq), 0)
    cur = q_pos // SEL_BLOCK
    forced = (sblk == 0) | (sblk == cur) | (sblk == cur - 1)
    for g in range(G_NSA):
        imp_scr[...] = imps[g]
        pooled = imp_scr[pl.ds(0, nsb, stride=ratio), :]
        for r in range(1, ratio):
            pooled = pooled + imp_scr[pl.ds(r, nsb, stride=ratio), :]
        score = jnp.where(sblk <= cur, pooled + FORCE_BONUS * forced.astype(F32), NEG_INF)
        score_scr[...] = score
        cnt = jnp.zeros((nsb, tq), F32)
        for mblk in range(nsb):
            c = score_scr[mblk:mblk + 1, :]
            ahead = (c > score) | ((c == score) & (mblk < sblk))
            cnt = cnt + ahead.astype(F32)
        chosen = (cnt < float(TOP_N)).astype(F32)
        for mblk in range(nsb):
            chosen_scr[g, mblk] = jnp.broadcast_to(chosen[mblk:mblk + 1, :], (8, tq))

    k_in = lax.broadcasted_iota(I32, (kt, tq), 0)
    blocks_per_tile = kt // SEL_BLOCK
    j_diag = (i * tq) // kt

    ones_rows = jnp.ones((acc_scr.shape[2] - HEAD_DIM, kt), BF16)

    def flash(k_ref, vt_ref, slot, n_tiles, mask_fn):
        m_scr[slot] = jnp.full(m_scr.shape[1:], NEG_INF, F32)
        acc_scr[slot] = jnp.zeros(acc_scr.shape[1:], F32)

        def body(t, carry):
            j = j_diag - t
            kj = k_ref[pl.ds(pl.multiple_of(j * kt, kt), kt), :]
            vt = vt_ref[j]
            vts = [jnp.concatenate([vt[g * HEAD_DIM:(g + 1) * HEAD_DIM, :], ones_rows], axis=0)
                   for g in range(G_NSA)]
            masks = [mask_fn(g, j) for g in range(G_NSA)]
            scores = [_dot(kj, qt) for qt in qts]
            alphas, probs = [], []
            for h, s in enumerate(scores):
                s = jnp.where(masks[group_of[h]], s, NEG_INF)
                m_old = m_scr[slot, h]
                m_new = jnp.maximum(m_old, jnp.max(s, axis=0, keepdims=True))
                alphas.append(jnp.exp(m_old - m_new)[0:1])
                probs.append(jnp.exp(s - m_new[0:1]).astype(BF16))
                m_scr[slot, h] = m_new
            for h in range(H_NSA):
                acc_scr[slot, h] = alphas[h] * acc_scr[slot, h] + _dot(vts[group_of[h]], probs[h])
            return carry

        lax.fori_loop(0, n_tiles, body, 0)
        outs = []
        for h in range(H_NSA):
            acc = acc_scr[slot, h]
            outs.append(acc[:HEAD_DIM] / acc[HEAD_DIM:HEAD_DIM + 1])
        return outs

    def causal(j):
        return j * kt + k_in <= q_pos

    def sel_mask(g, j):
        rows = [jnp.broadcast_to(chosen_scr[g, blocks_per_tile * j + r][0:1], (SEL_BLOCK, tq))
                for r in range(blocks_per_tile)]
        return (jnp.concatenate(rows, axis=0) > 0.5) & causal(j)

    def win_mask(g, j):
        return causal(j) & (j * kt + k_in > q_pos - WINDOW)

    o_sel = flash(ks_ref, vst_ref, 0, j_diag + 1, sel_mask)
    j_low = jnp.maximum(i * tq - (WINDOW - 1), 0) // kt
    o_win = flash(kw_ref, vwt_ref, 1, j_diag - j_low + 1, win_mask)

    bgt = bg_ref[...].T

    def gate_row(h, br):
        c = h * 3 + br
        return bgt[c:c + 1, :]

    for p in range(H_NSA // 2):
        halves = []
        for h in (2 * p, 2 * p + 1):
            halves.append(gate_row(h, 0) * o_cmp[h] + gate_row(h, 1) * o_sel[h] + gate_row(h, 2) * o_win[h])
        mixed = jnp.concatenate(halves, axis=0).T
        o_ref[:, p * LANES:(p + 1) * LANES] = mixed * gate_ref[:, p * LANES:(p + 1) * LANES]


def _nsa_prompt(qn, ck, cvt, ksb, vst, kwb, vwt, bg, gate, batch, seq):
    tq = Q_BLOCK
    kt = NSA_KEY_TILE
    nq = seq // tq
    nb = seq // CMP_BLOCK
    nsb = nb // (SEL_BLOCK // CMP_BLOCK)
    qmap = lambda b, i: (b * nq + i, 0)
    per_b2 = lambda b, i: (b, 0)
    per_b3 = lambda b, i: (b, 0, 0)
    return pl.pallas_call(
        _nsa_prompt_kernel,
        grid=(batch, nq),
        in_specs=[
            pl.BlockSpec((tq, NSA_W), qmap),
            pl.BlockSpec((nb, LANES), per_b2),
            pl.BlockSpec((1, LANES, nb), per_b3),
            pl.BlockSpec((seq, LANES), per_b2),
            pl.BlockSpec((seq // kt, LANES, kt), per_b3),
            pl.BlockSpec((seq, LANES), per_b2),
            pl.BlockSpec((seq // kt, LANES, kt), per_b3),
            pl.BlockSpec((tq, LANES), qmap),
            pl.BlockSpec((tq, NSA_W), qmap),
        ],
        out_specs=pl.BlockSpec((tq, NSA_W), qmap),
        out_shape=jax.ShapeDtypeStruct((batch * seq, NSA_W), F32),
        scratch_shapes=[
            pltpu.VMEM((2, H_NSA, 8, tq), F32),
            pltpu.VMEM((2, H_NSA, HEAD_DIM + 16, tq), F32),
            pltpu.VMEM((nb, tq), F32),
            pltpu.VMEM((nsb, tq), F32),
            pltpu.VMEM((G_NSA, nsb, 8, tq), F32),
        ],
        compiler_params=_params(("parallel", "arbitrary")),
        name="nsa_prompt",
    )(qn, ck, cvt, ksb, vst, kwb, vwt, bg, gate)


def _layer1_prompt(x2d, batch, seq, norm_g, w_in_bf, qn_g, cmp_kn_g, sel_kn_g, win_kn_g, pe128, wk_bd, wv_bd,
                   w_out_bf, tm):
    (qn, gate, bg, kc_t, vc_t, ks_t, vs_t, kw_t, vw_t, kc, vc, ksb, kwb, vst, vwt) = _inproj1(
        x2d, norm_g, w_in_bf, _pair_gain(qn_g), _pair_gain(sel_kn_g), _pair_gain(win_kn_g), tm, seq, True)
    ck, cvt = _compress_prompt(kc, vc, pe128, wk_bd, wv_bd, _pair_gain(cmp_kn_g), batch, seq)
    mix = _nsa_prompt(qn, ck, cvt, ksb, vst, kwb, vwt, bg, gate, batch, seq)
    y = _outproj(x2d, [mix], [w_out_bf], tm)
    return y, kc_t, vc_t, ks_t, vs_t, kw_t, vw_t


def _ret_sample_kernel(ret_ref, s_ref, qdec_ref, cdec_ref, gn_ref, mix_ref, so_ref):
    width = H_RET * HEAD_DIM
    rowi = lax.broadcasted_iota(I32, (8, HEAD_DIM), 0)
    eye = (lax.broadcasted_iota(I32, (HEAD_DIM, HEAD_DIM), 0)
           == lax.broadcasted_iota(I32, (HEAD_DIM, HEAD_DIM), 1)).astype(BF16)
    row = ret_ref[0]
    for h in range(H_RET):
        sl = slice(h * HEAD_DIM, (h + 1) * HEAD_DIM)
        qb = row[:, sl].astype(BF16)
        kb = row[:, width + h * HEAD_DIM:width + (h + 1) * HEAD_DIM].astype(BF16)
        vb = row[:, 2 * width + h * HEAD_DIM:2 * width + (h + 1) * HEAD_DIM].astype(BF16)
        gate = row[:, 3 * width + h * HEAD_DIM:3 * width + (h + 1) * HEAD_DIM]
        s = s_ref[0, h]
        qs = _dot(jnp.broadcast_to(qb.astype(F32), (8, HEAD_DIM)).astype(BF16), s.astype(BF16))[0:1]
        att = jnp.sum(qb.astype(F32) * kb.astype(F32), axis=-1, keepdims=True)
        o = att.astype(BF16).astype(F32) * vb.astype(F32) + qs * qdec_ref[h:h + 1, :]
        k8 = jnp.where(rowi == 0, jnp.broadcast_to(kb.astype(F32), (8, HEAD_DIM)), 0.0).astype(BF16)
        kcol = _dot_nt(eye, k8)[:, 0:1]
        so_ref[0, h] = s * cdec_ref[h:h + 1, :] + kcol * vb.astype(F32)
        mu = jnp.mean(o, axis=-1, keepdims=True)
        oc = o - mu
        var = jnp.mean(oc * oc, axis=-1, keepdims=True)
        on = oc * lax.rsqrt(var + EPS) * gn_ref[h:h + 1, :]
        mix_ref[0, :, sl] = on * _silu(gate)


def _ret_sample(ret, state, gn_g):
    nb, cols = ret.shape
    width = H_RET * HEAD_DIM
    h = jnp.arange(H_RET, dtype=F32)
    lg = jnp.log1p(-jnp.exp2(-5.0 - h))
    qdec = jnp.broadcast_to(jnp.exp(lg * 1.0)[:, None], (H_RET, HEAD_DIM))
    cdec = jnp.broadcast_to(jnp.exp(lg * 1.0)[:, None], (H_RET, HEAD_DIM))
    m3 = lambda b: (b, 0, 0)
    tab = pl.BlockSpec((H_RET, HEAD_DIM), lambda b: (0, 0))
    st = pl.BlockSpec((1, H_RET, HEAD_DIM, HEAD_DIM), lambda b: (b, 0, 0, 0))
    mix, state_new = pl.pallas_call(
        _ret_sample_kernel,
        grid=(nb,),
        in_specs=[pl.BlockSpec((1, 1, cols), m3), st, tab, tab, tab],
        out_specs=[pl.BlockSpec((1, 1, width), m3), st],
        out_shape=[jax.ShapeDtypeStruct((nb, 1, width), F32),
                   jax.ShapeDtypeStruct(state.shape, F32)],
        compiler_params=_params(("parallel",)),
        name="retention_sample",
    )(ret.reshape(nb, 1, cols), state, qdec, cdec, gn_g.astype(F32))
    return mix.reshape(nb, width), state_new


def _sb_sample_kernel(pt_ref, q_ref, g_ref, k_hbm, v_hbm, o_ref, kbuf, vbuf, sem):
    b = pl.program_id(0)
    n_pages = pt_ref.shape[1]
    page = PAGE_SIZE

    def copies(t, slot):
        pid = pt_ref[b, n_pages - 1 - t]
        return (pltpu.make_async_copy(k_hbm.at[pid], kbuf.at[slot], sem.at[0, slot]),
                pltpu.make_async_copy(v_hbm.at[pid], vbuf.at[slot], sem.at[1, slot]))

    def start(t, slot):
        for c in copies(t, slot):
            c.start()

    def wait(t, slot):
        for c in copies(t, slot):
            c.wait()

    q = q_ref[0].astype(F32)
    rowq = lax.broadcasted_iota(I32, (H_SB, HEAD_DIM), 0)
    rowa = lax.broadcasted_iota(I32, (H_SB, page), 0)
    upper = (lax.broadcasted_iota(I32, (page, page), 0) >= lax.broadcasted_iota(I32, (page, page), 1)).astype(BF16)
    q_rows = [jnp.where(rowq == h, q, 0.0).astype(BF16) for h in range(H_SB)]

    start(0, 0)

    def cond(carry):
        t, alive = carry[0], carry[1]
        return (t < n_pages) & alive

    def body(carry):
        t, _, r, acc = carry
        slot = t % 2
        wait(t, slot)

        @pl.when(t + 1 < n_pages)
        def _():
            start(t + 1, 1 - slot)

        z = jnp.zeros((H_SB, page), F32)
        for h in range(H_SB):
            z = z + _dot(q_rows[h], kbuf[slot, h].astype(BF16))
        ls_pos = jnp.minimum(z, 0.0) - jnp.log1p(jnp.exp(-jnp.abs(z)))
        lm = ls_pos - z
        hi, lo_part = _split_bf16(lm)
        s_incl = _dot(hi, upper) + _dot(lo_part, upper)
        a = jnp.exp(ls_pos + (s_incl - lm) + r)
        for h in range(H_SB):
            acc = acc + _dot_nt(jnp.where(rowa == h, a, 0.0).astype(BF16), vbuf[slot, h].astype(BF16))
        r = r + s_incl[:, 0:1]
        return t + 1, jnp.max(r) > SB_LOG_ZERO, r, acc

    init = (jnp.int32(0), jnp.bool_(True), jnp.zeros((H_SB, 1), F32), jnp.zeros((H_SB, HEAD_DIM), F32))
    t_end, _, _, acc = lax.while_loop(cond, body, init)

    @pl.when(t_end < n_pages)
    def _():
        wait(t_end, t_end % 2)

    o_ref[0] = acc * g_ref[0]


def _sb_sample(q8, g8, cache_kt, cache_vt, page_table):
    nb = q8.shape[0]
    head = lambda b, pt: (b, 0, 0)
    grid_spec = pltpu.PrefetchScalarGridSpec(
        num_scalar_prefetch=1,
        grid=(nb,),
        in_specs=[pl.BlockSpec((1, H_SB, HEAD_DIM), head), pl.BlockSpec((1, H_SB, HEAD_DIM), head),
                  pl.BlockSpec(memory_space=pl.ANY), pl.BlockSpec(memory_space=pl.ANY)],
        out_specs=pl.BlockSpec((1, H_SB, HEAD_DIM), head),
        scratch_shapes=[pltpu.VMEM((2, H_SB, HEAD_DIM, PAGE_SIZE), F32),
                        pltpu.VMEM((2, H_SB, HEAD_DIM, PAGE_SIZE), F32),
                        pltpu.SemaphoreType.DMA((2, 2))],
    )
    return pl.pallas_call(
        _sb_sample_kernel,
        grid_spec=grid_spec,
        out_shape=jax.ShapeDtypeStruct((nb, H_SB, HEAD_DIM), F32),
        compiler_params=_params(("arbitrary",)),
        name="sb_sample",
    )(page_table, q8, g8, cache_kt, cache_vt)


CMP_PAGES_PER_STEP = 8


def _compress_sample_kernel(pt_ref, *refs, n_pp):
    k_refs = refs[:n_pp]
    v_refs = refs[n_pp:2 * n_pp]
    pe_ref, wk_ref, wv_ref, kg_ref, ck_ref, cv_ref, kbuf, vbuf = refs[2 * n_pp:]
    j = pl.program_id(1)
    for r in range(n_pp):
        kbuf[r * PAGE_SIZE:(r + 1) * PAGE_SIZE, :] = k_refs[r][0].T
        vbuf[r * PAGE_SIZE:(r + 1) * PAGE_SIZE, :] = v_refs[r][0].T
    nblk = n_pp * PAGE_SIZE // CMP_BLOCK
    lo = _lane_lo((1, LANES))
    start = pl.multiple_of(j * nblk, nblk)
    ck_ref[0, pl.ds(start, nblk), :] = _pair_rmsnorm(_compress_rows(kbuf, pe_ref, wk_ref, 0, nblk), kg_ref[...], lo)
    cv_ref[0, pl.ds(start, nblk), :] = _compress_rows(vbuf, pe_ref, wv_ref, 0, nblk)


def _compress_sample(cache_k, cache_v, page_table, pe128, wk_bd, wv_bd, kg):
    nb, n_pages = page_table.shape
    n_pp = CMP_PAGES_PER_STEP
    nblk_total = n_pages * PAGE_SIZE // CMP_BLOCK

    def page_map(r):
        return lambda b, j, pt: (pt[b, j * n_pp + r], 0, 0)

    fixed2 = lambda b, j, pt: (0, 0)
    fixed3 = lambda b, j, pt: (0, 0, 0)
    out_map = lambda b, j, pt: (b, 0, 0)
    grid_spec = pltpu.PrefetchScalarGridSpec(
        num_scalar_prefetch=1,
        grid=(nb, n_pages // n_pp),
        in_specs=[pl.BlockSpec((1, PAGE_SIZE, LANES), page_map(r)) for r in range(n_pp)] * 2
        + [pl.BlockSpec((CMP_BLOCK, LANES), fixed2),
           pl.BlockSpec((CMP_BLOCK, LANES, LANES), fixed3),
           pl.BlockSpec((CMP_BLOCK, LANES, LANES), fixed3),
           pl.BlockSpec((1, LANES), fixed2)],
        out_specs=[pl.BlockSpec((1, nblk_total, LANES), out_map)] * 2,
        scratch_shapes=[pltpu.VMEM((n_pp * PAGE_SIZE, LANES), F32)] * 2,
    )
    o = jax.ShapeDtypeStruct((nb, nblk_total, LANES), F32)
    return pl.pallas_call(
        functools.partial(_compress_sample_kernel, n_pp=n_pp),
        grid_spec=grid_spec,
        out_shape=[o, o],
        compiler_params=_params(("parallel", "arbitrary")),
        name="compress_sample",
    )(page_table, *([cache_k] * n_pp), *([cache_v] * n_pp), pe128, wk_bd, wv_bd, kg)


def _pages_t(cache):
    return cache.transpose(0, 2, 3, 1)


def _group_q(q, g):
    z = jnp.zeros_like(q)
    return jnp.concatenate([q, z], axis=1) if g == 0 else jnp.concatenate([z, q], axis=1)


def _nsa_sample_select_kernel(q_ref, ck_ref, cv_ref, ocmp_ref, idx_ref, *, q_pos):
    q = q_ref[0]
    ck = ck_ref[0].astype(BF16)
    cv = cv_ref[0].astype(BF16)
    nb = ck.shape[0]
    blk_n = lax.broadcasted_iota(I32, (1, nb), 1)
    cmask = blk_n * CMP_BLOCK + (CMP_BLOCK - 1) <= q_pos
    ratio = SEL_BLOCK // CMP_BLOCK
    sblk = blk_n // ratio
    cur = q_pos // SEL_BLOCK
    lane_f = blk_n.astype(F32)
    slot = lax.broadcasted_iota(I32, (1, LANES), 1)
    row8 = lax.broadcasted_iota(I32, (8, LANES), 0)
    idx_all = jnp.full((8, LANES), -1, I32)
    neg_inf = float("-inf")
    for g in range(G_NSA):
        qg = _group_q(q[g * HPG:(g + 1) * HPG], g)
        s = jnp.where(cmask, _dot_nt(qg, ck), NEG_INF)
        m = jnp.max(s, axis=-1, keepdims=True)
        e = jnp.where(cmask, jnp.exp(s - m), 0.0)
        den = jnp.sum(e, axis=-1, keepdims=True)
        pr = e / jnp.where(den > 0.0, den, 1.0)
        o = _dot(pr.astype(BF16), cv)
        ocmp_ref[0, g * HPG:(g + 1) * HPG, :] = o[:, g * HEAD_DIM:(g + 1) * HEAD_DIM]
        imp = jnp.sum(pr, axis=0, keepdims=True)
        even = (blk_n & 1) == 0
        pooled = imp + jnp.where(even, pltpu.roll(imp, nb - 1, 1), pltpu.roll(imp, 1, 1))
        forced = (sblk == 0) | (sblk == cur) | (sblk == cur - 1)
        score = jnp.where(sblk < cur, pooled + FORCE_BONUS * forced.astype(F32), neg_inf)
        idx_g = jnp.full((1, LANES), -1, I32)
        for t in range(TOP_N - 1):
            mx = jnp.max(score, axis=-1, keepdims=True)
            first = jnp.min(jnp.where(score == mx, lane_f, float(4 * nb)), axis=-1, keepdims=True)
            blk = first.astype(I32) // ratio
            idx_g = jnp.where((slot == t) & (mx > neg_inf), blk, idx_g)
            score = jnp.where(sblk == blk, neg_inf, score)
        idx_all = jnp.where(row8 == g, jnp.broadcast_to(idx_g, (8, LANES)), idx_all)
    idx_ref[0] = idx_all


def _nsa_sample_select(q16, ck, cv, q_pos):
    nb = q16.shape[0]
    nblk = ck.shape[1]
    m3 = lambda b: (b, 0, 0)
    return pl.pallas_call(
        functools.partial(_nsa_sample_select_kernel, q_pos=q_pos),
        grid=(nb,),
        in_specs=[pl.BlockSpec((1, H_NSA, HEAD_DIM), m3),
                  pl.BlockSpec((1, nblk, LANES), m3),
                  pl.BlockSpec((1, nblk, LANES), m3)],
        out_specs=[pl.BlockSpec((1, H_NSA, HEAD_DIM), m3), pl.BlockSpec((1, 8, LANES), m3)],
        out_shape=[jax.ShapeDtypeStruct((nb, H_NSA, HEAD_DIM), F32),
                   jax.ShapeDtypeStruct((nb, 8, LANES), I32)],
        compiler_params=_params(("parallel",)),
        name="nsa_sample_select",
    )(q16, ck, cv)


def _nsa_sample_attend_kernel(pt_ref, idx_ref, q_ref, k_hbm, v_hbm, kn_ref, vn_ref, kwn_ref, vwn_ref,
                              wk_ref, wv_ref, ocmp_ref, bg_ref, gate_ref, o_ref, kbuf, vbuf, sem,
                              *, q_pos):
    b = pl.program_id(0)
    n_slots = kbuf.shape[1]
    n_pages = pt_ref.shape[1]

    def copies(g, s):
        blk = jnp.maximum(idx_ref[b * G_NSA + g, s], 0)
        pid = pt_ref[b, jnp.minimum(blk // 2, n_pages - 1)]
        rows = pl.ds(g * HEAD_DIM, HEAD_DIM)
        return (pltpu.make_async_copy(k_hbm.at[pid, rows], kbuf.at[g, s], sem.at[0, g]),
                pltpu.make_async_copy(v_hbm.at[pid, rows], vbuf.at[g, s], sem.at[1, g]))

    for g in range(G_NSA):
        for s in range(n_slots):
            for c in copies(g, s):
                c.start()

    q = q_ref[0]
    bg = bg_ref[0]
    col = lax.broadcasted_iota(I32, (HPG, LANES), 1)
    hrow = lax.broadcasted_iota(I32, (HPG, LANES), 0)
    key_half = lax.broadcasted_iota(I32, (1, PAGE_SIZE), 1) // SEL_BLOCK
    wlen = wk_ref.shape[3]
    w_pos = q_pos - wlen + lax.broadcasted_iota(I32, (1, wlen), 1)
    wmask = (w_pos > q_pos - WINDOW) & (w_pos >= 0)

    def new_token(x_ref, g):
        return x_ref[0][:, g * HEAD_DIM:(g + 1) * HEAD_DIM].astype(BF16).astype(F32)

    for g in range(G_NSA):
        qg = q[g * HPG:(g + 1) * HPG]
        qf = qg.astype(F32)
        sw = jnp.where(wmask, _dot(qg, wk_ref[0, g].astype(BF16)), NEG_INF)
        sw_new = jnp.sum(qf * new_token(kwn_ref, g), axis=-1, keepdims=True)
        mw = jnp.maximum(jnp.max(sw, axis=-1, keepdims=True), sw_new)
        pw = jnp.where(wmask, jnp.exp(sw - mw), 0.0)
        pw_new = jnp.exp(sw_new - mw)
        o_win = _dot_nt(pw.astype(BF16), wv_ref[0, g].astype(BF16))
        o_win = o_win + pw_new.astype(BF16).astype(F32) * new_token(vwn_ref, g)
        o_win = o_win / (jnp.sum(pw, axis=-1, keepdims=True) + pw_new)
        for s in range(n_slots):
            for c in copies(g, s):
                c.wait()
        scores, masks = [], []
        for s in range(n_slots):
            blk = idx_ref[b * G_NSA + g, s]
            ok = (key_half == jnp.maximum(blk, 0) % 2) & (blk >= 0)
            scores.append(jnp.where(ok, _dot(qg, kbuf[g, s].astype(BF16)), NEG_INF))
            masks.append(ok)
        s_new = jnp.sum(qf * new_token(kn_ref, g), axis=-1, keepdims=True)
        m = s_new
        for sc in scores:
            m = jnp.maximum(m, jnp.max(sc, axis=-1, keepdims=True))
        p_new = jnp.exp(s_new - m)
        den = p_new
        acc = p_new.astype(BF16).astype(F32) * new_token(vn_ref, g)
        for s in range(n_slots):
            p = jnp.where(masks[s], jnp.exp(scores[s] - m), 0.0)
            den = den + jnp.sum(p, axis=-1, keepdims=True)
            acc = acc + _dot_nt(p.astype(BF16), vbuf[g, s].astype(BF16))
        o_sel = acc / den
        gates = []
        for br in range(3):
            pick = col == g * HPG * 3 + hrow * 3 + br
            gates.append(jnp.sum(jnp.where(pick, jnp.broadcast_to(bg, (HPG, LANES)), 0.0), axis=-1, keepdims=True))
        rows = slice(g * HPG, (g + 1) * HPG)
        mixed = gates[0] * ocmp_ref[0, rows, :] + gates[1] * o_sel + gates[2] * o_win
        o_ref[0, rows, :] = mixed * gate_ref[0, rows, :]


def _nsa_sample_attend(page_table, idx, q16, sel_kt, sel_vt, ks_new, vs_new, kw_new, vw_new, win_kt, win_vt, ocmp,
                       bg, gate16, q_pos):
    nb = q16.shape[0]
    n_slots = TOP_N - 1
    wlen = win_kt.shape[3]
    m3 = lambda b, pt, ix: (b, 0, 0)
    m4 = lambda b, pt, ix: (b, 0, 0, 0)
    row = pl.BlockSpec((1, 1, LANES), m3)
    heads = pl.BlockSpec((1, H_NSA, HEAD_DIM), m3)
    win = pl.BlockSpec((1, G_NSA, HEAD_DIM, wlen), m4)
    grid_spec = pltpu.PrefetchScalarGridSpec(
        num_scalar_prefetch=2,
        grid=(nb,),
        in_specs=[heads, pl.BlockSpec(memory_space=pl.ANY), pl.BlockSpec(memory_space=pl.ANY),
                  row, row, row, row, win, win, heads, row, heads],
        out_specs=heads,
        scratch_shapes=[pltpu.VMEM((G_NSA, n_slots, HEAD_DIM, PAGE_SIZE), F32),
                        pltpu.VMEM((G_NSA, n_slots, HEAD_DIM, PAGE_SIZE), F32),
                        pltpu.SemaphoreType.DMA((2, G_NSA))],
    )
    return pl.pallas_call(
        functools.partial(_nsa_sample_attend_kernel, q_pos=q_pos),
        grid_spec=grid_spec,
        out_shape=jax.ShapeDtypeStruct((nb, H_NSA, HEAD_DIM), F32),
        compiler_params=_params(("arbitrary",)),
        name="nsa_sample_attend",
    )(page_table, idx, q16, sel_kt, sel_vt, ks_new, vs_new, kw_new, vw_new, win_kt, win_vt, ocmp, bg, gate16)


def _layer0_sample(x2d, past_len, state, cache_k, cache_v, page_table, norm_g, w_in_bf, ret_gn_g, sb_qn_g, sb_kn_g,
                   w_out_bf):
    nb = x2d.shape[0]
    cos, sin = _rope_tables(jnp.full((nb,), past_len, I32))
    ret, sbq, sbg, sbk, sbv = _inproj0(x2d, norm_g, w_in_bf, cos, sin, _pair_gain(sb_qn_g), _pair_gain(sb_kn_g),
                                       nb, nb, False)
    mix_ret, state_new = _ret_sample(ret, state, ret_gn_g)
    mix_sb = _sb_sample(sbq.reshape(nb, H_SB, HEAD_DIM), sbg.reshape(nb, H_SB, HEAD_DIM),
                        _pages_t(cache_k), _pages_t(cache_v), page_table)
    mix_sb = mix_sb.reshape(nb, H_SB * HEAD_DIM)
    half = w_out_bf.shape[0] // 2
    y = _outproj(x2d, [mix_ret, mix_sb], [w_out_bf[:half], w_out_bf[half:]], nb)
    return y, state_new, sbk, sbv


def _layer1_sample(x2d, past_len, cmp_k, cmp_v, sel_k, sel_v, win_k, win_v, page_table, norm_g, w_in_bf, qn_g,
                   cmp_kn_g, sel_kn_g, win_kn_g, pe128, wk_bd, wv_bd, w_out_bf):
    nb = x2d.shape[0]
    (qn, gate, bg, kc, vc, ks, vs, kw, vw) = _inproj1(
        x2d, norm_g, w_in_bf, _pair_gain(qn_g), _pair_gain(sel_kn_g), _pair_gain(win_kn_g), nb, nb, False)
    n_phys = cmp_k.shape[0]
    pool = lambda t: _pages_t(t).reshape(n_phys, KV_W, PAGE_SIZE)
    ck, cv = _compress_sample(pool(cmp_k), pool(cmp_v), page_table, pe128, wk_bd, wv_bd, _pair_gain(cmp_kn_g))
    q16 = qn.reshape(nb, H_NSA, HEAD_DIM)
    ocmp, idx = _nsa_sample_select(q16, ck, cv, past_len)
    idx2 = idx[:, :G_NSA, :TOP_N].reshape(nb * G_NSA, TOP_N)
    mix = _nsa_sample_attend(page_table, idx2, q16, pool(sel_k), pool(sel_v), ks[:, None, :], vs[:, None, :],
                             kw[:, None, :], vw[:, None, :], _pages_t(win_k), _pages_t(win_v), ocmp,
                             bg[:, None, :], gate.reshape(nb, H_NSA, HEAD_DIM), past_len)
    y = _outproj(x2d, [mix.reshape(nb, NSA_W)], [w_out_bf], nb)
    gd = (nb, 1, G_NSA, HEAD_DIM)
    wk_new = jnp.concatenate([win_k[:, 1:], kw.reshape(gd)], axis=1)
    wv_new = jnp.concatenate([win_v[:, 1:], vw.reshape(gd)], axis=1)
    return y, kc, vc, ks, vs, wk_new, wv_new


def kernel(x_prompt, x_sample, state_ret, cache_sb_k, cache_sb_v, cache_cmp_k, cache_cmp_v, cache_sel_k,
           cache_sel_v, cache_win_k, cache_win_v, page_table, norm0_g, w_in0, ret_gn_g, sb_qn_g, sb_kn_g,
           w_out0, norm1_g, w_in1, nsa_qn_g, cmp_kn_g, sel_kn_g, win_kn_g, cmp_pe, w_cmp_k, w_cmp_v, w_out1):
    batch, seq, d = x_prompt.shape
    nb = x_sample.shape[0]
    past_len = page_table.shape[1] * PAGE_SIZE
    tm = 256
    w_in0_bf = w_in0.astype(BF16)
    w_out0_bf = w_out0.astype(BF16)
    w_in1_bf = _pad_w_in1(w_in1)
    w_out1_bf = w_out1.astype(BF16)
    pe128 = jnp.tile(cmp_pe, (1, G_NSA))
    wk_bd = _blockdiag_w(w_cmp_k)
    wv_bd = _blockdiag_w(w_cmp_v)

    xp = x_prompt.reshape(batch * seq, d)
    y1p, ret_p, sbk_p, sbv_p = _layer0_prompt(xp, batch, seq, norm0_g, w_in0_bf, ret_gn_g, sb_qn_g, sb_kn_g,
                                              w_out0_bf, tm)
    y2p, ckp, cvp, skp, svp, wkp, wvp = _layer1_prompt(y1p, batch, seq, norm1_g, w_in1_bf, nsa_qn_g, cmp_kn_g,
                                                        sel_kn_g, win_kn_g, pe128, wk_bd, wv_bd, w_out1_bf, tm)
    xs = x_sample.reshape(nb, d)
    y1s, ret_s, sbk_s, sbv_s = _layer0_sample(xs, past_len, state_ret, cache_sb_k, cache_sb_v, page_table, norm0_g,
                                              w_in0_bf, ret_gn_g, sb_qn_g, sb_kn_g, w_out0_bf)
    y2s, cks, cvs, sks, svs, wks, wvs = _layer1_sample(y1s, past_len, cache_cmp_k, cache_cmp_v, cache_sel_k,
                                                        cache_sel_v, cache_win_k, cache_win_v, page_table, norm1_g,
                                                        w_in1_bf, nsa_qn_g, cmp_kn_g, sel_kn_g, win_kn_g, pe128,
                                                        wk_bd, wv_bd, w_out1_bf)
    keep = min(WINDOW, seq)
    gd = (nb, 1, G_NSA, HEAD_DIM)

    def rows_major(t):
        return t.transpose(0, 3, 1, 2)

    def groups(t):
        return rows_major(t.reshape(batch, G_NSA, HEAD_DIM, seq))

    return (y2p.reshape(batch, seq, d), y2s.reshape(nb, 1, d), ret_p, ret_s,
            rows_major(sbk_p), rows_major(sbv_p),
            sbk_s.reshape(nb, 1, H_SB, HEAD_DIM), sbv_s.reshape(nb, 1, H_SB, HEAD_DIM),
            groups(ckp), groups(cvp), groups(skp), groups(svp),
            groups(wkp)[:, seq - keep:], groups(wvp)[:, seq - keep:],
            cks.reshape(gd), cvs.reshape(gd), sks.reshape(gd), svs.reshape(gd), wks, wvs)
```

```python
import functools

import jax
import jax.numpy as jnp
import numpy as np
from jax import lax
from jax.experimental import pallas as pl
from jax.experimental.pallas import tpu as pltpu

F32 = jnp.float32
BF16 = jnp.bfloat16
I32 = jnp.int32

HEAD_DIM = 64
LANES = 128
H_RET = 8
H_SB = 8
RET_CHUNK = 128
ROPE_BASE = 10000.0
Q_BLOCK = 128
H_NSA = 16
G_NSA = 2
HPG = H_NSA // G_NSA
CMP_BLOCK = 32
SEL_BLOCK = 64
TOP_N = 16
WINDOW = 512
FORCE_BONUS = 1000.0
NEG_INF = -1e30
EPS = 1e-6
PAGE_SIZE = 128
NSA_KEY_TILE = 128
NSA_Q_TILE = 256
SCALE = HEAD_DIM ** -0.5
LOG2E = 1.4426950408889634
VMEM_LIMIT = 56 * 1024 * 1024


def _dot(a, b):
    return jnp.dot(a, b, preferred_element_type=F32)


def _dot_nt(a, b):
    return lax.dot_general(a, b, (((1,), (1,)), ((), ())), preferred_element_type=F32)


def _dot_tn(a, b):
    return lax.dot_general(a, b, (((0,), (0,)), ((), ())), preferred_element_type=F32)


def _split_bf16(x):
    hi = x.astype(BF16)
    lo = (x - hi.astype(F32)).astype(BF16)
    return hi, lo


def _sigmoid(x):
    return 1.0 / (1.0 + jnp.exp(-x))


def _silu(x):
    return x * _sigmoid(x)


def _lane_lo(shape):
    return lax.broadcasted_iota(I32, shape, len(shape) - 1) < HEAD_DIM


def _pair_sum(x, lo):
    s_lo = jnp.sum(jnp.where(lo, x, 0.0), axis=-1, keepdims=True)
    s_hi = jnp.sum(jnp.where(lo, 0.0, x), axis=-1, keepdims=True)
    return jnp.where(lo, s_lo, s_hi)


def _pair_rmsnorm(x, g, lo):
    ms = _pair_sum(x * x, lo) * (1.0 / HEAD_DIM)
    return x * lax.rsqrt(ms + EPS) * g


def _rope_pair(x, cos, sin_signed):
    lane = lax.broadcasted_iota(I32, x.shape, 1)
    first = (lane & 32) == 0
    partner = jnp.where(first, pltpu.roll(x, 96, 1), pltpu.roll(x, 32, 1))
    return x * cos + partner * sin_signed


def _transpose_tiles(x):
    r, c = x.shape
    if r <= LANES and c <= LANES:
        return x.T
    rows = [jnp.concatenate([x[ri * LANES:(ri + 1) * LANES, cj * LANES:(cj + 1) * LANES].T
                             for ri in range(r // LANES)], axis=1) for cj in range(c // LANES)]
    return jnp.concatenate(rows, axis=0)


def _params(sem):
    return pltpu.CompilerParams(dimension_semantics=sem, vmem_limit_bytes=VMEM_LIMIT)


def _store_head_major(dst_ref, t, pair, r):
    cols = slice(r * LANES, (r + 1) * LANES)
    dst_ref[0, 2 * pair, :, cols] = t[:HEAD_DIM]
    dst_ref[0, 2 * pair + 1, :, cols] = t[HEAD_DIM:]


def _inproj0_kernel(x_ref, g_ref, w_ref, cos_ref, sin_ref, qg_ref, kg_ref,
                    ret_ref, sbq_ref, sbg_ref, *kv_refs, prompt):
    x = x_ref[...]
    xn = x * lax.rsqrt(jnp.mean(x * x, axis=-1, keepdims=True) + EPS) * g_ref[...]
    xb = xn.astype(BF16)
    cos = cos_ref[...]
    sin = sin_ref[...]
    lo = _lane_lo((1, LANES))
    width = H_RET * HEAD_DIM

    def group(i):
        return _dot(xb, w_ref[:, i * width:(i + 1) * width])

    def tiles(a):
        return [a[:, t * LANES:(t + 1) * LANES] for t in range(width // LANES)]

    for t, v in enumerate(tiles(group(0))):
        ret_ref[:, t * LANES:(t + 1) * LANES] = _rope_pair(v, cos, sin)
    for t, v in enumerate(tiles(group(1))):
        ret_ref[:, width + t * LANES:width + (t + 1) * LANES] = _rope_pair(v, cos, sin) * SCALE
    ret_ref[:, 2 * width:3 * width] = group(2)
    ret_ref[:, 3 * width:4 * width] = group(3)
    for t, v in enumerate(tiles(group(4))):
        sbq_ref[:, t * LANES:(t + 1) * LANES] = (_pair_rmsnorm(v, qg_ref[...], lo) * SCALE).astype(BF16)
    sk = [_pair_rmsnorm(v, kg_ref[...], lo) for v in tiles(group(5))]
    sv = tiles(group(6))
    if prompt:
        kb_ref, vt_ref, kt_out, vt_out = kv_refs
        for t in range(width // LANES):
            kb_ref[:, t * LANES:(t + 1) * LANES] = sk[t].astype(BF16)
            for r in range(x.shape[0] // LANES):
                rows = slice(r * LANES, (r + 1) * LANES)
                vt = sv[t][rows].T
                _store_head_major(kt_out, sk[t][rows].T, t, r)
                _store_head_major(vt_out, vt, t, r)
                vt_ref[r, t] = vt.astype(BF16)
    else:
        k_ref, v_ref = kv_refs
        for t in range(width // LANES):
            k_ref[:, t * LANES:(t + 1) * LANES] = sk[t]
            v_ref[:, t * LANES:(t + 1) * LANES] = sv[t]
    sbg_ref[...] = _silu(group(7))


def _inproj0(x2d, norm_g, w_bf, cos, sin, qg, kg, tm, seq, prompt):
    rows, d = x2d.shape
    width = H_RET * HEAD_DIM
    n = w_bf.shape[1]
    grid = (rows // tm,)
    per_seq = seq // tm
    row = lambda i: (i, 0)
    fixed = lambda i: (0, 0)
    pos = lambda i: (i % per_seq, 0)
    if prompt:
        npair = width // LANES
        head_major = pl.BlockSpec((1, H_SB, HEAD_DIM, tm), lambda i: (i // per_seq, 0, 0, i % per_seq))
        hm_shape = jax.ShapeDtypeStruct((rows // seq, H_SB, HEAD_DIM, seq), F32)
        extra_specs = [pl.BlockSpec((tm, width), row),
                       pl.BlockSpec((tm // LANES, npair, LANES, LANES), lambda i: (i, 0, 0, 0)),
                       head_major, head_major]
        extra_shapes = [jax.ShapeDtypeStruct((rows, width), BF16),
                        jax.ShapeDtypeStruct((rows // LANES, npair, LANES, LANES), BF16),
                        hm_shape, hm_shape]
    else:
        extra_specs = [pl.BlockSpec((tm, width), row)] * 2
        extra_shapes = [jax.ShapeDtypeStruct((rows, width), F32)] * 2
    return pl.pallas_call(
        functools.partial(_inproj0_kernel, prompt=prompt),
        grid=grid,
        in_specs=[
            pl.BlockSpec((tm, d), row),
            pl.BlockSpec((1, d), fixed),
            pl.BlockSpec((d, n), fixed),
            pl.BlockSpec((tm, LANES), pos),
            pl.BlockSpec((tm, LANES), pos),
            pl.BlockSpec((1, LANES), fixed),
            pl.BlockSpec((1, LANES), fixed),
        ],
        out_specs=[
            pl.BlockSpec((tm, 4 * width), row),
            pl.BlockSpec((tm, width), row),
            pl.BlockSpec((tm, width), row),
        ] + extra_specs,
        out_shape=[
            jax.ShapeDtypeStruct((rows, 4 * width), F32),
            jax.ShapeDtypeStruct((rows, width), BF16),
            jax.ShapeDtypeStruct((rows, width), F32),
        ] + extra_shapes,
        compiler_params=_params(("parallel",)),
        name="inproj0",
    )(x2d, norm_g.reshape(1, d), w_bf, cos, sin, qg, kg)


def _ret_prompt_kernel(ret_ref, inner_ref, qdec_ref, kdec_ref, cdec_ref, gn_ref, mix_ref, state_ref, s_scr):
    c = pl.program_id(1)
    npair = H_RET // 2
    width = H_RET * HEAD_DIM

    @pl.when(c == 0)
    def _():
        s_scr[...] = jnp.zeros_like(s_scr)

    def tile(group, p):
        return ret_ref[:, group * width + p * LANES:group * width + (p + 1) * LANES]

    lo = _lane_lo((ret_ref.shape[0], LANES))
    same_head = ((lax.broadcasted_iota(I32, (LANES, LANES), 0) < HEAD_DIM)
                 == (lax.broadcasted_iota(I32, (LANES, LANES), 1) < HEAD_DIM))
    qs = [tile(0, p).astype(BF16) for p in range(npair)]
    ks = [tile(1, p) for p in range(npair)]
    kbs = [k.astype(BF16) for k in ks]
    vbs = [tile(2, p).astype(BF16) for p in range(npair)]
    states = [s_scr[p] for p in range(npair)]
    zero = jnp.zeros_like(qs[0])
    att_lo = [_dot_nt(jnp.where(lo, qs[p], zero), kbs[p]) for p in range(npair)]
    att_hi = [_dot_nt(jnp.where(lo, zero, qs[p]), kbs[p]) for p in range(npair)]
    cross = [_dot(qs[p], states[p].astype(BF16)) for p in range(npair)]
    kv = [_dot_tn((ks[p] * kdec_ref[p]).astype(BF16), vbs[p]) for p in range(npair)]
    a_lo = [(att_lo[p] * inner_ref[2 * p]).astype(BF16) for p in range(npair)]
    a_hi = [(att_hi[p] * inner_ref[2 * p + 1]).astype(BF16) for p in range(npair)]
    o_lo = [_dot(a_lo[p], vbs[p]) for p in range(npair)]
    o_hi = [_dot(a_hi[p], vbs[p]) for p in range(npair)]
    for p in range(npair):
        s_new = states[p] * cdec_ref[p] + jnp.where(same_head, kv[p], 0.0)
        s_scr[p] = s_new
        state_ref[0, p] = s_new
        o = jnp.where(lo, o_lo[p], o_hi[p]) + cross[p] * qdec_ref[p]
        mu = _pair_sum(o, lo) * (1.0 / HEAD_DIM)
        oc = o - mu
        var = _pair_sum(oc * oc, lo) * (1.0 / HEAD_DIM)
        on = oc * lax.rsqrt(var + EPS) * gn_ref[p]
        mix_ref[:, p * LANES:(p + 1) * LANES] = on * _silu(tile(3, p))


def _ret_tables(chunk):
    h = jnp.arange(H_RET, dtype=F32)
    lg = jnp.log1p(-jnp.exp2(-5.0 - h))
    idx = jnp.arange(chunk, dtype=F32)
    diff = idx[:, None] - idx[None, :]
    inner = jnp.where(diff >= 0, jnp.exp(lg[:, None, None] * jnp.maximum(diff, 0.0)), 0.0)
    q_dec = jnp.exp(lg[:, None] * (idx + 1.0))
    k_dec = jnp.exp(lg[:, None] * (chunk - 1.0 - idx))
    c_dec = jnp.exp(lg * chunk)

    def lanes(t):
        t = jnp.repeat(t[:, :, None], HEAD_DIM, axis=2)
        return jnp.concatenate([t[0::2], t[1::2]], axis=2)

    cd = jnp.repeat(c_dec[:, None, None], HEAD_DIM, axis=2)
    cd = jnp.concatenate([cd[0::2], cd[1::2]], axis=2)
    return inner, lanes(q_dec), lanes(k_dec), cd


def _ret_prompt(ret, gn_g, batch, seq):
    width = H_RET * HEAD_DIM
    npair = H_RET // 2
    c = RET_CHUNK
    nc = seq // c
    inner, qdec, kdec, cdec = _ret_tables(c)
    gn = gn_g.reshape(npair, 1, LANES)
    tab = lambda b, i: (0, 0, 0)
    return pl.pallas_call(
        _ret_prompt_kernel,
        grid=(batch, nc),
        in_specs=[
            pl.BlockSpec((c, 4 * width), lambda b, i: (b * nc + i, 0)),
            pl.BlockSpec((H_RET, c, c), tab),
            pl.BlockSpec((npair, c, LANES), tab),
            pl.BlockSpec((npair, c, LANES), tab),
            pl.BlockSpec((npair, 1, LANES), tab),
            pl.BlockSpec((npair, 1, LANES), tab),
        ],
        out_specs=[
            pl.BlockSpec((c, width), lambda b, i: (b * nc + i, 0)),
            pl.BlockSpec((1, npair, LANES, LANES), lambda b, i: (b, 0, 0, 0)),
        ],
        out_shape=[
            jax.ShapeDtypeStruct((batch * seq, width), F32),
            jax.ShapeDtypeStruct((batch, npair, LANES, LANES), F32),
        ],
        scratch_shapes=[pltpu.VMEM((npair, LANES, LANES), F32)],
        compiler_params=_params(("parallel", "arbitrary")),
        name="retention_prompt",
    )(ret, inner, qdec, kdec, cdec, gn)


def _unpair_state(sp):
    a = sp[:, :, :HEAD_DIM, :HEAD_DIM]
    b = sp[:, :, HEAD_DIM:, HEAD_DIM:]
    return jnp.stack([a, b], axis=2).reshape(sp.shape[0], -1, HEAD_DIM, HEAD_DIM)


SB_LOG_ZERO = -104.0


def _sb_prompt_kernel(q_ref, k_ref, vt_ref, g_ref, o_ref, acc_scr):
    i = pl.program_id(1)
    tq = q_ref.shape[0]
    npair = H_SB // 2
    rr = lax.broadcasted_iota(I32, (tq, tq), 0)
    cc = lax.broadcasted_iota(I32, (tq, tq), 1)
    lower = (cc >= rr).astype(BF16)
    diag = rr < cc
    zeros = jnp.zeros((HEAD_DIM, tq), F32)
    qts = []
    for p in range(npair):
        t = q_ref[:, p * LANES:(p + 1) * LANES].astype(F32).T
        qts.append(jnp.concatenate([t[:HEAD_DIM], zeros], axis=0).astype(BF16))
        qts.append(jnp.concatenate([zeros, t[HEAD_DIM:]], axis=0).astype(BF16))

    def sweep(j, rs, diag_mask):
        start = pl.multiple_of(j * tq, tq)
        kjs = [k_ref[pl.ds(start, tq), p * LANES:(p + 1) * LANES] for p in range(npair)]
        vts = [vt_ref[j, p] for p in range(npair)]
        zs = [_dot(kjs[h // 2], qt) for h, qt in enumerate(qts)]
        ls_pos, lms, parts = [], [], []
        for z in zs:
            lsp = jnp.minimum(z, 0.0) - jnp.log1p(jnp.exp(-jnp.abs(z)))
            lm = lsp - z
            if diag_mask is not None:
                lm = jnp.where(diag_mask, lm, 0.0)
            ls_pos.append(lsp)
            lms.append(lm)
            parts.append(_split_bf16(lm))
        s_incl = [_dot(lower, hi) + _dot(lower, lo_part) for hi, lo_part in parts]
        probs = []
        for h in range(H_SB):
            a = jnp.exp(ls_pos[h] + (s_incl[h] - lms[h]) + rs[h])
            if diag_mask is not None:
                a = jnp.where(diag_mask, a, 0.0)
            probs.append(a.astype(BF16))
        for h in range(H_SB):
            half = h % 2
            pv = _dot(vts[h // 2][half * HEAD_DIM:(half + 1) * HEAD_DIM, :], probs[h])
            acc_scr[h] = pv if diag_mask is not None else acc_scr[h] + pv
        return [rs[h] + s_incl[h][0:1, :] for h in range(H_SB)]

    def alive_of(rs):
        worst = rs[0]
        for r in rs[1:]:
            worst = jnp.maximum(worst, r)
        return jnp.max(worst) > SB_LOG_ZERO

    rs = sweep(i, [jnp.zeros((1, tq), F32)] * H_SB, diag)

    def cond(carry):
        t, alive = carry[0], carry[1]
        return (t < i) & alive

    def body(carry):
        t = carry[0]
        rs = sweep(i - 1 - t, list(carry[2:]), None)
        return (t + 1, alive_of(rs)) + tuple(rs)

    lax.while_loop(cond, body, (jnp.int32(0), alive_of(rs)) + tuple(rs))
    for p in range(npair):
        both = jnp.concatenate([acc_scr[2 * p], acc_scr[2 * p + 1]], axis=0).T
        o_ref[:, p * LANES:(p + 1) * LANES] = both * g_ref[:, p * LANES:(p + 1) * LANES]


def _sb_prompt(sbq, sbk_b, sbvt, sbg, batch, seq):
    width = H_SB * HEAD_DIM
    npair = H_SB // 2
    tq = Q_BLOCK
    nq = seq // tq
    qmap = lambda b, i: (b * nq + i, 0)
    return pl.pallas_call(
        _sb_prompt_kernel,
        grid=(batch, nq),
        in_specs=[
            pl.BlockSpec((tq, width), qmap),
            pl.BlockSpec((seq, width), lambda b, i: (b, 0)),
            pl.BlockSpec((nq, npair, LANES, tq), lambda b, i: (b, 0, 0, 0)),
            pl.BlockSpec((tq, width), qmap),
        ],
        out_specs=pl.BlockSpec((tq, width), qmap),
        out_shape=jax.ShapeDtypeStruct((batch * seq, width), F32),
        scratch_shapes=[pltpu.VMEM((H_SB, HEAD_DIM, tq), F32)],
        compiler_params=_params(("parallel", "arbitrary")),
        name="sb_prompt",
    )(sbq, sbk_b, sbvt, sbg)


def _outproj_kernel(*refs, n_mix):
    x_ref = refs[0]
    mix_refs = refs[1:1 + n_mix]
    w_refs = refs[1 + n_mix:1 + 2 * n_mix]
    y_ref = refs[1 + 2 * n_mix]
    acc = x_ref[...]
    for m_ref, w_ref in zip(mix_refs, w_refs):
        acc = acc + _dot(m_ref[...].astype(BF16), w_ref[...])
    y_ref[...] = acc


def _outproj(x2d, mixes, ws, tm):
    rows, d = x2d.shape
    n_mix = len(mixes)
    row = lambda i: (i, 0)
    fixed = lambda i: (0, 0)
    in_specs = [pl.BlockSpec((tm, d), row)]
    in_specs += [pl.BlockSpec((tm, m.shape[1]), row) for m in mixes]
    in_specs += [pl.BlockSpec(w.shape, fixed) for w in ws]
    return pl.pallas_call(
        functools.partial(_outproj_kernel, n_mix=n_mix),
        grid=(rows // tm,),
        in_specs=in_specs,
        out_specs=pl.BlockSpec((tm, d), row),
        out_shape=jax.ShapeDtypeStruct((rows, d), F32),
        compiler_params=_params(("parallel",)),
        name="outproj",
    )(x2d, *mixes, *ws)


def _rope_tables(pos):
    half = HEAD_DIM // 2
    inv = ROPE_BASE ** (-jnp.arange(half, dtype=F32) / half)
    ang = pos.astype(F32)[:, None] * inv[None, :]
    cos, sin = jnp.cos(ang), jnp.sin(ang)
    cos128 = jnp.tile(cos, (1, LANES // half))
    sin128 = jnp.tile(jnp.concatenate([-sin, sin], axis=1), (1, LANES // HEAD_DIM))
    return cos128, sin128


def _pair_gain(g):
    return jnp.tile(g.astype(F32), LANES // HEAD_DIM).reshape(1, LANES)


def _layer0_prompt(x2d, batch, seq, norm_g, w_in_bf, ret_gn_g, sb_qn_g, sb_kn_g, w_out_bf, tm):
    cos, sin = _rope_tables(jnp.arange(seq, dtype=I32))
    ret, sbq, sbg, sbk_b, sbvt, sbk_hm, sbv_hm = _inproj0(x2d, norm_g, w_in_bf, cos, sin, _pair_gain(sb_qn_g),
                                                          _pair_gain(sb_kn_g), tm, seq, True)
    mix_ret, state_pairs = _ret_prompt(ret, ret_gn_g, batch, seq)
    mix_sb = _sb_prompt(sbq, sbk_b, sbvt, sbg, batch, seq)
    half = w_out_bf.shape[0] // 2
    y = _outproj(x2d, [mix_ret, mix_sb], [w_out_bf[:half], w_out_bf[half:]], tm)
    return y, _unpair_state(state_pairs), sbk_hm, sbv_hm


NSA_W = H_NSA * HEAD_DIM
KV_W = G_NSA * HEAD_DIM
IN1_COLS = 2 * NSA_W + 6 * KV_W + LANES


def _inproj1_kernel(x_ref, g_ref, w_ref, qg_ref, skg_ref, wkg_ref, q_ref, gate_ref, bg_ref, *kv_refs, prompt):
    x = x_ref[...]
    xn = x * lax.rsqrt(jnp.mean(x * x, axis=-1, keepdims=True) + EPS) * g_ref[...]
    xb = xn.astype(BF16)
    lo = _lane_lo((1, LANES))
    a = _dot(xb, w_ref[:, 0:NSA_W])
    for t in range(NSA_W // LANES):
        v = a[:, t * LANES:(t + 1) * LANES]
        q_ref[:, t * LANES:(t + 1) * LANES] = (_pair_rmsnorm(v, qg_ref[...], lo) * (SCALE * LOG2E)).astype(BF16)
    a = _dot(xb, w_ref[:, NSA_W:NSA_W + 6 * KV_W])
    kc, vc, ks, vs, kw, vw = [a[:, t * KV_W:(t + 1) * KV_W] for t in range(6)]
    ks = _pair_rmsnorm(ks, skg_ref[...], lo)
    kw = _pair_rmsnorm(kw, wkg_ref[...], lo)
    six = (kc, vc, ks, vs, kw, vw)
    if prompt:
        t_out = kv_refs[:6]
        kc_ref, vc_ref, ksb_ref, kwb_ref, vst_ref, vwt_ref = kv_refs[6:]
        kc_ref[...] = kc
        vc_ref[...] = vc
        ksb_ref[...] = ks.astype(BF16)
        kwb_ref[...] = kw.astype(BF16)
        per_tile = NSA_KEY_TILE // LANES
        for r in range(x.shape[0] // LANES):
            rows = slice(r * LANES, (r + 1) * LANES)
            cols = slice((r % per_tile) * LANES, (r % per_tile + 1) * LANES)
            for src, dst in zip(six, t_out):
                t = src[rows].T
                dst[0, :, rows] = t
                if src is vs:
                    vst_ref[r // per_tile, :, cols] = t.astype(BF16)
                if src is vw:
                    vwt_ref[r // per_tile, :, cols] = t.astype(BF16)
    else:
        for src, dst in zip(six, kv_refs):
            dst[...] = src
    off = NSA_W + 6 * KV_W
    gate_ref[...] = _silu(_dot(xb, w_ref[:, off:off + NSA_W]))
    bg_ref[...] = _sigmoid(_dot(xb, w_ref[:, off + NSA_W:off + NSA_W + LANES]))


def _inproj1(x2d, norm_g, w_bf, qg, skg, wkg, tm, seq, prompt):
    rows, d = x2d.shape
    row = lambda i: (i, 0)
    fixed = lambda i: (0, 0)
    out_specs = [pl.BlockSpec((tm, NSA_W), row), pl.BlockSpec((tm, NSA_W), row), pl.BlockSpec((tm, LANES), row)]
    out_shape = [jax.ShapeDtypeStruct((rows, NSA_W), BF16), jax.ShapeDtypeStruct((rows, NSA_W), F32),
                 jax.ShapeDtypeStruct((rows, LANES), F32)]
    if prompt:
        per_seq = seq // tm
        kt = NSA_KEY_TILE
        out_specs += ([pl.BlockSpec((1, KV_W, tm), lambda i: (i // per_seq, 0, i % per_seq))] * 6
                      + [pl.BlockSpec((tm, KV_W), row)] * 4
                      + [pl.BlockSpec((tm // kt, LANES, kt), lambda i: (i, 0, 0))] * 2)
        out_shape += ([jax.ShapeDtypeStruct((rows // seq, KV_W, seq), F32)] * 6
                      + [jax.ShapeDtypeStruct((rows, KV_W), F32)] * 2
                      + [jax.ShapeDtypeStruct((rows, KV_W), BF16)] * 2
                      + [jax.ShapeDtypeStruct((rows // kt, LANES, kt), BF16)] * 2)
    else:
        out_specs += [pl.BlockSpec((tm, KV_W), row)] * 6
        out_shape += [jax.ShapeDtypeStruct((rows, KV_W), F32)] * 6
    return pl.pallas_call(
        functools.partial(_inproj1_kernel, prompt=prompt),
        grid=(rows // tm,),
        in_specs=[
            pl.BlockSpec((tm, d), row),
            pl.BlockSpec((1, d), fixed),
            pl.BlockSpec((d, IN1_COLS), fixed),
            pl.BlockSpec((1, LANES), fixed),
            pl.BlockSpec((1, LANES), fixed),
            pl.BlockSpec((1, LANES), fixed),
        ],
        out_specs=out_specs,
        out_shape=out_shape,
        compiler_params=_params(("parallel",)),
        name="inproj1",
    )(x2d, norm_g.reshape(1, d), w_bf, qg, skg, wkg)


def _pad_w_in1(w_in1):
    pad = IN1_COLS - w_in1.shape[1]
    return jnp.pad(w_in1, ((0, 0), (0, pad))).astype(BF16)


def _compress_rows(t_ref, pe_ref, w_ref, row0, nblk):
    acc = jnp.zeros((nblk, LANES), F32)
    for l in range(CMP_BLOCK):
        rows = t_ref[pl.ds(row0 + l, nblk, stride=CMP_BLOCK), :]
        acc = acc + _dot((rows + pe_ref[l:l + 1, :]).astype(BF16), w_ref[l])
    return acc


def _compress_prompt_kernel(kc_ref, vc_ref, pe_ref, wk_ref, wv_ref, kg_ref, ck_ref, cvt_ref):
    nblk = ck_ref.shape[0]
    lo = _lane_lo((1, LANES))
    ck = _pair_rmsnorm(_compress_rows(kc_ref, pe_ref, wk_ref, 0, nblk), kg_ref[...], lo)
    cv = _compress_rows(vc_ref, pe_ref, wv_ref, 0, nblk)
    ck_ref[...] = ck.astype(BF16)
    cvt_ref[0] = cv.T.astype(BF16)


def _blockdiag_w(w):
    z = jnp.zeros_like(w)
    top = jnp.concatenate([w, z], axis=2)
    bot = jnp.concatenate([z, w], axis=2)
    return jnp.concatenate([top, bot], axis=1).astype(BF16)


def _compress_prompt(kc, vc, pe128, wk_bd, wv_bd, kg, batch, seq):
    nblk = seq // CMP_BLOCK
    rowb = lambda b: (b, 0)
    fixed2 = lambda b: (0, 0)
    fixed3 = lambda b: (0, 0, 0)
    return pl.pallas_call(
        _compress_prompt_kernel,
        grid=(batch,),
        in_specs=[
            pl.BlockSpec((seq, LANES), rowb),
            pl.BlockSpec((seq, LANES), rowb),
            pl.BlockSpec((CMP_BLOCK, LANES), fixed2),
            pl.BlockSpec((CMP_BLOCK, LANES, LANES), fixed3),
            pl.BlockSpec((CMP_BLOCK, LANES, LANES), fixed3),
            pl.BlockSpec((1, LANES), fixed2),
        ],
        out_specs=[pl.BlockSpec((nblk, LANES), rowb), pl.BlockSpec((1, LANES, nblk), lambda b: (b, 0, 0))],
        out_shape=[jax.ShapeDtypeStruct((batch * nblk, LANES), BF16),
                   jax.ShapeDtypeStruct((batch, LANES, nblk), BF16)],
        compiler_params=_params(("parallel",)),
        name="compress_prompt",
    )(kc, vc, pe128, wk_bd, wv_bd, kg)


def _nsa_prompt_kernel(q_ref, ck_ref, cvt_ref, ks_ref, vst_ref, kw_ref, vwt_ref, bg_ref, gate_ref,
                       o_ref, m_scr, acc_scr, imp_scr, score_scr, chosen_scr, sc_scr):
    i = pl.program_id(1)
    tq = q_ref.shape[0]
    nb = ck_ref.shape[0]
    kt = NSA_KEY_TILE
    ratio = SEL_BLOCK // CMP_BLOCK
    nsb = nb // ratio
    q_pos = i * tq + lax.broadcasted_iota(I32, (1, tq), 1)

    zeros = jnp.zeros((HEAD_DIM, tq), F32)
    qts = []
    for p in range(H_NSA // 2):
        t = _transpose_tiles(q_ref[:, p * LANES:(p + 1) * LANES].astype(F32))
        for ht in (t[:HEAD_DIM], t[HEAD_DIM:]):
            parts = [zeros] * G_NSA
            parts[(2 * p) // HPG] = ht
            qts.append(jnp.concatenate(parts, axis=0).astype(BF16))
    group_of = [h // HPG for h in range(H_NSA)]

    ck = ck_ref[...]
    blk_n = lax.broadcasted_iota(I32, (nb, tq), 0)
    cmask = blk_n * CMP_BLOCK + (CMP_BLOCK - 1) <= q_pos
    cmp_scores = [_dot(ck, qt) for qt in qts]
    cmp_probs = []
    imps = [jnp.zeros((nb, tq), F32) for _ in range(G_NSA)]
    for h, s in enumerate(cmp_scores):
        s = jnp.where(cmask, s, NEG_INF)
        m = jnp.max(s, axis=0, keepdims=True)
        e = jnp.where(cmask, jnp.exp2(s - m), 0.0)
        den = jnp.sum(e, axis=0, keepdims=True)
        pr = e / jnp.where(den > 0.0, den, 1.0)
        imps[group_of[h]] = imps[group_of[h]] + pr
        cmp_probs.append(pr.astype(BF16))
    o_cmp = [_dot(cvt_ref[0, group_of[h] * HEAD_DIM:(group_of[h] + 1) * HEAD_DIM, :], pr)
             for h, pr in enumerate(cmp_probs)]

    sblk = lax.broadcasted_iota(I32, (nsb, tq), 0)
    cur = q_pos // SEL_BLOCK
    forced = (sblk == 0) | (sblk == cur) | (sblk == cur - 1)
    for g in range(G_NSA):
        pooled_parts = []
        for c in range(tq // LANES):
            part_ref = imp_scr.at[c]
            part_ref[...] = imps[g][:, c * LANES:(c + 1) * LANES]
            part = part_ref[pl.ds(0, nsb, stride=ratio), :]
            for r in range(1, ratio):
                part = part + part_ref[pl.ds(r, nsb, stride=ratio), :]
            pooled_parts.append(part)
        pooled = jnp.concatenate(pooled_parts, axis=1)
        score = jnp.where(sblk <= cur, pooled + FORCE_BONUS * forced.astype(F32), NEG_INF)
        score_scr[...] = score
        slab = 8
        cnts = []
        for r0 in range(0, nsb, slab):
            rows = score_scr[r0:min(r0 + slab, nsb), :]
            idx = r0 + lax.broadcasted_iota(I32, rows.shape, 0)
            cnt = jnp.zeros(rows.shape, F32)
            for mblk in range(nsb):
                c = score_scr[mblk:mblk + 1, :]
                if mblk < r0:
                    ahead = c >= rows
                elif mblk >= r0 + rows.shape[0]:
                    ahead = c > rows
                else:
                    ahead = (c > rows) | ((c == rows) & (mblk < idx))
                cnt = cnt + jnp.where(ahead, 1.0, 0.0)
            cnts.append(cnt)
        chosen = (jnp.concatenate(cnts, axis=0) < float(TOP_N)).astype(F32)
        for mblk in range(nsb):
            chosen_scr[g, mblk] = jnp.broadcast_to(chosen[mblk:mblk + 1, :], (8, tq))

    k_in = lax.broadcasted_iota(I32, (kt, tq), 0)
    blocks_per_tile = kt // SEL_BLOCK
    j_diag = ((i + 1) * tq - 1) // kt

    ones_rows = jnp.ones((acc_scr.shape[2] - HEAD_DIM, kt), BF16)

    heads_a = list(range(HPG))
    heads_b = list(range(HPG, H_NSA))

    def flash(k_ref, vt_ref, slot, n_tiles, mask_fn):
        m_scr[slot] = jnp.full(m_scr.shape[1:], NEG_INF, F32)
        acc_scr[slot] = jnp.zeros(acc_scr.shape[1:], F32)

        def load_k(j):
            return k_ref[pl.ds(pl.multiple_of(j * kt, kt), kt), :]

        def softmax_step(h, s, mask):
            s = jnp.where(mask, s, NEG_INF)
            m_old = m_scr[slot, h]
            m_new = jnp.maximum(m_old, jnp.max(s, axis=0, keepdims=True))
            m_scr[slot, h] = m_new
            return jnp.exp2(m_old - m_new)[0:1], jnp.exp2(s - m_new[0:1]).astype(BF16)

        def accumulate(h, alpha, pv):
            acc_scr[slot, h] = alpha * acc_scr[slot, h] + pv

        k_first = load_k(j_diag)
        for n, h in enumerate(heads_b):
            sc_scr[n] = _dot(k_first, qts[h])

        def body(t, carry):
            j = j_diag - t
            kj = load_k(j)
            k_next = load_k(jnp.maximum(j - 1, 0))
            vt = vt_ref[j]
            vts = [jnp.concatenate([vt[g * HEAD_DIM:(g + 1) * HEAD_DIM, :], ones_rows], axis=0)
                   for g in range(G_NSA)]
            scores_a = [_dot(kj, qts[h]) for h in heads_a]
            mask_b = mask_fn(1, j)
            soft_b = [softmax_step(h, sc_scr[n], mask_b) for n, h in enumerate(heads_b)]
            pv_b = [_dot(vts[1], p) for _, p in soft_b]
            next_b = [_dot(k_next, qts[h]) for h in heads_b]
            mask_a = mask_fn(0, j)
            soft_a = [softmax_step(h, scores_a[n], mask_a) for n, h in enumerate(heads_a)]
            pv_a = [_dot(vts[0], p) for _, p in soft_a]
            for n, h in enumerate(heads_b):
                accumulate(h, soft_b[n][0], pv_b[n])
                sc_scr[n] = next_b[n]
            for n, h in enumerate(heads_a):
                accumulate(h, soft_a[n][0], pv_a[n])
            return carry

        lax.fori_loop(0, n_tiles, body, 0)
        outs = []
        for h in range(H_NSA):
            acc = acc_scr[slot, h]
            outs.append(acc[:HEAD_DIM] / acc[HEAD_DIM:HEAD_DIM + 1])
        return outs

    def causal(j):
        return j * kt + k_in <= q_pos

    def sel_mask(g, j):
        rows = [jnp.broadcast_to(chosen_scr[g, blocks_per_tile * j + r][0:1], (SEL_BLOCK, tq))
                for r in range(blocks_per_tile)]
        return (jnp.concatenate(rows, axis=0) > 0.5) & causal(j)

    def win_mask(g, j):
        return causal(j) & (j * kt + k_in > q_pos - WINDOW)

    o_sel = flash(ks_ref, vst_ref, 0, j_diag + 1, sel_mask)
    j_low = jnp.maximum(i * tq - (WINDOW - 1), 0) // kt
    o_win = flash(kw_ref, vwt_ref, 1, j_diag - j_low + 1, win_mask)

    bgt = _transpose_tiles(bg_ref[...])

    def gate_row(h, br):
        c = h * 3 + br
        return bgt[c:c + 1, :]

    for p in range(H_NSA // 2):
        halves = []
        for h in (2 * p, 2 * p + 1):
            halves.append(gate_row(h, 0) * o_cmp[h] + gate_row(h, 1) * o_sel[h] + gate_row(h, 2) * o_win[h])
        mixed = _transpose_tiles(jnp.concatenate(halves, axis=0))
        o_ref[:, p * LANES:(p + 1) * LANES] = mixed * gate_ref[:, p * LANES:(p + 1) * LANES]


def _nsa_prompt(qn, ck, cvt, ksb, vst, kwb, vwt, bg, gate, batch, seq):
    tq = NSA_Q_TILE
    kt = NSA_KEY_TILE
    nq = seq // tq
    nb = seq // CMP_BLOCK
    nsb = nb // (SEL_BLOCK // CMP_BLOCK)
    qmap = lambda b, i: (b * nq + i, 0)
    per_b2 = lambda b, i: (b, 0)
    per_b3 = lambda b, i: (b, 0, 0)
    return pl.pallas_call(
        _nsa_prompt_kernel,
        grid=(batch, nq),
        in_specs=[
            pl.BlockSpec((tq, NSA_W), qmap),
            pl.BlockSpec((nb, LANES), per_b2),
            pl.BlockSpec((1, LANES, nb), per_b3),
            pl.BlockSpec((seq, LANES), per_b2),
            pl.BlockSpec((seq // kt, LANES, kt), per_b3),
            pl.BlockSpec((seq, LANES), per_b2),
            pl.BlockSpec((seq // kt, LANES, kt), per_b3),
            pl.BlockSpec((tq, LANES), qmap),
            pl.BlockSpec((tq, NSA_W), qmap),
        ],
        out_specs=pl.BlockSpec((tq, NSA_W), qmap),
        out_shape=jax.ShapeDtypeStruct((batch * seq, NSA_W), F32),
        scratch_shapes=[
            pltpu.VMEM((2, H_NSA, 8, tq), F32),
            pltpu.VMEM((2, H_NSA, HEAD_DIM + 16, tq), F32),
            pltpu.VMEM((tq // LANES, nb, LANES), F32),
            pltpu.VMEM((nsb, tq), F32),
            pltpu.VMEM((G_NSA, nsb, 8, tq), F32),
            pltpu.VMEM((HPG, kt, tq), F32),
        ],
        compiler_params=_params(("parallel", "arbitrary")),
        name="nsa_prompt",
    )(qn, ck, cvt, ksb, vst, kwb, vwt, bg, gate)


def _layer1_prompt(x2d, batch, seq, norm_g, w_in_bf, qn_g, cmp_kn_g, sel_kn_g, win_kn_g, pe128, wk_bd, wv_bd,
                   w_out_bf, tm):
    (qn, gate, bg, kc_t, vc_t, ks_t, vs_t, kw_t, vw_t, kc, vc, ksb, kwb, vst, vwt) = _inproj1(
        x2d, norm_g, w_in_bf, _pair_gain(qn_g), _pair_gain(sel_kn_g), _pair_gain(win_kn_g), tm, seq, True)
    ck, cvt = _compress_prompt(kc, vc, pe128, wk_bd, wv_bd, _pair_gain(cmp_kn_g), batch, seq)
    mix = _nsa_prompt(qn, ck, cvt, ksb, vst, kwb, vwt, bg, gate, batch, seq)
    y = _outproj(x2d, [mix], [w_out_bf], tm)
    return y, kc_t, vc_t, ks_t, vs_t, kw_t, vw_t


def _ret_sample_kernel(ret_ref, s_ref, qdec_ref, cdec_ref, gn_ref, mix_ref, so_ref):
    width = H_RET * HEAD_DIM
    rowi = lax.broadcasted_iota(I32, (8, HEAD_DIM), 0)
    eye = (lax.broadcasted_iota(I32, (HEAD_DIM, HEAD_DIM), 0)
           == lax.broadcasted_iota(I32, (HEAD_DIM, HEAD_DIM), 1)).astype(BF16)
    row = ret_ref[0]
    for h in range(H_RET):
        sl = slice(h * HEAD_DIM, (h + 1) * HEAD_DIM)
        qb = row[:, sl].astype(BF16)
        kb = row[:, width + h * HEAD_DIM:width + (h + 1) * HEAD_DIM].astype(BF16)
        vb = row[:, 2 * width + h * HEAD_DIM:2 * width + (h + 1) * HEAD_DIM].astype(BF16)
        gate = row[:, 3 * width + h * HEAD_DIM:3 * width + (h + 1) * HEAD_DIM]
        s = s_ref[0, h]
        qs = _dot(jnp.broadcast_to(qb.astype(F32), (8, HEAD_DIM)).astype(BF16), s.astype(BF16))[0:1]
        att = jnp.sum(qb.astype(F32) * kb.astype(F32), axis=-1, keepdims=True)
        o = att.astype(BF16).astype(F32) * vb.astype(F32) + qs * qdec_ref[h:h + 1, :]
        k8 = jnp.where(rowi == 0, jnp.broadcast_to(kb.astype(F32), (8, HEAD_DIM)), 0.0).astype(BF16)
        kcol = _dot_nt(eye, k8)[:, 0:1]
        so_ref[0, h] = s * cdec_ref[h:h + 1, :] + kcol * vb.astype(F32)
        mu = jnp.mean(o, axis=-1, keepdims=True)
        oc = o - mu
        var = jnp.mean(oc * oc, axis=-1, keepdims=True)
        on = oc * lax.rsqrt(var + EPS) * gn_ref[h:h + 1, :]
        mix_ref[0, :, sl] = on * _silu(gate)


def _ret_sample(ret, state, gn_g):
    nb, cols = ret.shape
    width = H_RET * HEAD_DIM
    h = jnp.arange(H_RET, dtype=F32)
    lg = jnp.log1p(-jnp.exp2(-5.0 - h))
    qdec = jnp.broadcast_to(jnp.exp(lg * 1.0)[:, None], (H_RET, HEAD_DIM))
    cdec = jnp.broadcast_to(jnp.exp(lg * 1.0)[:, None], (H_RET, HEAD_DIM))
    m3 = lambda b: (b, 0, 0)
    tab = pl.BlockSpec((H_RET, HEAD_DIM), lambda b: (0, 0))
    st = pl.BlockSpec((1, H_RET, HEAD_DIM, HEAD_DIM), lambda b: (b, 0, 0, 0))
    mix, state_new = pl.pallas_call(
        _ret_sample_kernel,
        grid=(nb,),
        in_specs=[pl.BlockSpec((1, 1, cols), m3), st, tab, tab, tab],
        out_specs=[pl.BlockSpec((1, 1, width), m3), st],
        out_shape=[jax.ShapeDtypeStruct((nb, 1, width), F32),
                   jax.ShapeDtypeStruct(state.shape, F32)],
        compiler_params=_params(("parallel",)),
        name="retention_sample",
    )(ret.reshape(nb, 1, cols), state, qdec, cdec, gn_g.astype(F32))
    return mix.reshape(nb, width), state_new


def _sb_sample_kernel(pt_ref, q_ref, g_ref, k_hbm, v_hbm, o_ref, kbuf, vbuf, sem):
    b = pl.program_id(0)
    n_pages = pt_ref.shape[1]
    page = PAGE_SIZE

    def copies(t, slot):
        pid = pt_ref[b, n_pages - 1 - t]
        return (pltpu.make_async_copy(k_hbm.at[pid], kbuf.at[slot], sem.at[0, slot]),
                pltpu.make_async_copy(v_hbm.at[pid], vbuf.at[slot], sem.at[1, slot]))

    def start(t, slot):
        for c in copies(t, slot):
            c.start()

    def wait(t, slot):
        for c in copies(t, slot):
            c.wait()

    q = q_ref[0].astype(F32)
    rowq = lax.broadcasted_iota(I32, (H_SB, HEAD_DIM), 0)
    rowa = lax.broadcasted_iota(I32, (H_SB, page), 0)
    upper = (lax.broadcasted_iota(I32, (page, page), 0) >= lax.broadcasted_iota(I32, (page, page), 1)).astype(BF16)
    q_rows = [jnp.where(rowq == h, q, 0.0).astype(BF16) for h in range(H_SB)]

    start(0, 0)

    def cond(carry):
        t, alive = carry[0], carry[1]
        return (t < n_pages) & alive

    def body(carry):
        t, _, r, acc = carry
        slot = t % 2
        wait(t, slot)

        @pl.when(t + 1 < n_pages)
        def _():
            start(t + 1, 1 - slot)

        z = jnp.zeros((H_SB, page), F32)
        for h in range(H_SB):
            z = z + _dot(q_rows[h], kbuf[slot, h].astype(BF16))
        ls_pos = jnp.minimum(z, 0.0) - jnp.log1p(jnp.exp(-jnp.abs(z)))
        lm = ls_pos - z
        hi, lo_part = _split_bf16(lm)
        s_incl = _dot(hi, upper) + _dot(lo_part, upper)
        a = jnp.exp(ls_pos + (s_incl - lm) + r)
        for h in range(H_SB):
            acc = acc + _dot_nt(jnp.where(rowa == h, a, 0.0).astype(BF16), vbuf[slot, h].astype(BF16))
        r = r + s_incl[:, 0:1]
        return t + 1, jnp.max(r) > SB_LOG_ZERO, r, acc

    init = (jnp.int32(0), jnp.bool_(True), jnp.zeros((H_SB, 1), F32), jnp.zeros((H_SB, HEAD_DIM), F32))
    t_end, _, _, acc = lax.while_loop(cond, body, init)

    @pl.when(t_end < n_pages)
    def _():
        wait(t_end, t_end % 2)

    o_ref[0] = acc * g_ref[0]


def _sb_sample(q8, g8, cache_kt, cache_vt, page_table):
    nb = q8.shape[0]
    head = lambda b, pt: (b, 0, 0)
    grid_spec = pltpu.PrefetchScalarGridSpec(
        num_scalar_prefetch=1,
        grid=(nb,),
        in_specs=[pl.BlockSpec((1, H_SB, HEAD_DIM), head), pl.BlockSpec((1, H_SB, HEAD_DIM), head),
                  pl.BlockSpec(memory_space=pl.ANY), pl.BlockSpec(memory_space=pl.ANY)],
        out_specs=pl.BlockSpec((1, H_SB, HEAD_DIM), head),
        scratch_shapes=[pltpu.VMEM((2, H_SB, HEAD_DIM, PAGE_SIZE), F32),
                        pltpu.VMEM((2, H_SB, HEAD_DIM, PAGE_SIZE), F32),
                        pltpu.SemaphoreType.DMA((2, 2))],
    )
    return pl.pallas_call(
        _sb_sample_kernel,
        grid_spec=grid_spec,
        out_shape=jax.ShapeDtypeStruct((nb, H_SB, HEAD_DIM), F32),
        compiler_params=_params(("arbitrary",)),
        name="sb_sample",
    )(page_table, q8, g8, cache_kt, cache_vt)


CMP_PAGES_PER_STEP = 8


def _compress_sample_kernel(pt_ref, *refs, n_pp):
    k_refs = refs[:n_pp]
    v_refs = refs[n_pp:2 * n_pp]
    pe_ref, wk_ref, wv_ref, kg_ref, ck_ref, cv_ref, kbuf, vbuf = refs[2 * n_pp:]
    j = pl.program_id(1)
    for r in range(n_pp):
        kbuf[r * PAGE_SIZE:(r + 1) * PAGE_SIZE, :] = k_refs[r][0].T
        vbuf[r * PAGE_SIZE:(r + 1) * PAGE_SIZE, :] = v_refs[r][0].T
    nblk = n_pp * PAGE_SIZE // CMP_BLOCK
    lo = _lane_lo((1, LANES))
    start = pl.multiple_of(j * nblk, nblk)
    ck_ref[0, pl.ds(start, nblk), :] = _pair_rmsnorm(_compress_rows(kbuf, pe_ref, wk_ref, 0, nblk), kg_ref[...], lo)
    cv_ref[0, pl.ds(start, nblk), :] = _compress_rows(vbuf, pe_ref, wv_ref, 0, nblk)


def _compress_sample(cache_k, cache_v, page_table, pe128, wk_bd, wv_bd, kg):
    nb, n_pages = page_table.shape
    n_pp = CMP_PAGES_PER_STEP
    nblk_total = n_pages * PAGE_SIZE // CMP_BLOCK

    def page_map(r):
        return lambda b, j, pt: (pt[b, j * n_pp + r], 0, 0)

    fixed2 = lambda b, j, pt: (0, 0)
    fixed3 = lambda b, j, pt: (0, 0, 0)
    out_map = lambda b, j, pt: (b, 0, 0)
    grid_spec = pltpu.PrefetchScalarGridSpec(
        num_scalar_prefetch=1,
        grid=(nb, n_pages // n_pp),
        in_specs=[pl.BlockSpec((1, PAGE_SIZE, LANES), page_map(r)) for r in range(n_pp)] * 2
        + [pl.BlockSpec((CMP_BLOCK, LANES), fixed2),
           pl.BlockSpec((CMP_BLOCK, LANES, LANES), fixed3),
           pl.BlockSpec((CMP_BLOCK, LANES, LANES), fixed3),
           pl.BlockSpec((1, LANES), fixed2)],
        out_specs=[pl.BlockSpec((1, nblk_total, LANES), out_map)] * 2,
        scratch_shapes=[pltpu.VMEM((n_pp * PAGE_SIZE, LANES), F32)] * 2,
    )
    o = jax.ShapeDtypeStruct((nb, nblk_total, LANES), F32)
    return pl.pallas_call(
        functools.partial(_compress_sample_kernel, n_pp=n_pp),
        grid_spec=grid_spec,
        out_shape=[o, o],
        compiler_params=_params(("parallel", "arbitrary")),
        name="compress_sample",
    )(page_table, *([cache_k] * n_pp), *([cache_v] * n_pp), pe128, wk_bd, wv_bd, kg)


def _pages_t(cache):
    return cache.transpose(0, 2, 3, 1)


def _group_q(q, g):
    z = jnp.zeros_like(q)
    return jnp.concatenate([q, z], axis=1) if g == 0 else jnp.concatenate([z, q], axis=1)


def _nsa_sample_select_kernel(q_ref, ck_ref, cv_ref, ocmp_ref, idx_ref, *, q_pos):
    q = q_ref[0]
    ck = ck_ref[0].astype(BF16)
    cv = cv_ref[0].astype(BF16)
    nb = ck.shape[0]
    blk_n = lax.broadcasted_iota(I32, (1, nb), 1)
    cmask = blk_n * CMP_BLOCK + (CMP_BLOCK - 1) <= q_pos
    ratio = SEL_BLOCK // CMP_BLOCK
    sblk = blk_n // ratio
    cur = q_pos // SEL_BLOCK
    lane_f = blk_n.astype(F32)
    slot = lax.broadcasted_iota(I32, (1, LANES), 1)
    row8 = lax.broadcasted_iota(I32, (8, LANES), 0)
    idx_all = jnp.full((8, LANES), -1, I32)
    neg_inf = float("-inf")
    for g in range(G_NSA):
        qg = _group_q(q[g * HPG:(g + 1) * HPG], g)
        s = jnp.where(cmask, _dot_nt(qg, ck), NEG_INF)
        m = jnp.max(s, axis=-1, keepdims=True)
        e = jnp.where(cmask, jnp.exp2(s - m), 0.0)
        den = jnp.sum(e, axis=-1, keepdims=True)
        pr = e / jnp.where(den > 0.0, den, 1.0)
        o = _dot(pr.astype(BF16), cv)
        ocmp_ref[0, g * HPG:(g + 1) * HPG, :] = o[:, g * HEAD_DIM:(g + 1) * HEAD_DIM]
        imp = jnp.sum(pr, axis=0, keepdims=True)
        even = (blk_n & 1) == 0
        pooled = imp + jnp.where(even, pltpu.roll(imp, nb - 1, 1), pltpu.roll(imp, 1, 1))
        forced = (sblk == 0) | (sblk == cur) | (sblk == cur - 1)
        score = jnp.where(sblk < cur, pooled + FORCE_BONUS * forced.astype(F32), neg_inf)
        idx_g = jnp.full((1, LANES), -1, I32)
        for t in range(TOP_N - 1):
            mx = jnp.max(score, axis=-1, keepdims=True)
            first = jnp.min(jnp.where(score == mx, lane_f, float(4 * nb)), axis=-1, keepdims=True)
            blk = first.astype(I32) // ratio
            idx_g = jnp.where((slot == t) & (mx > neg_inf), blk, idx_g)
            score = jnp.where(sblk == blk, neg_inf, score)
        idx_all = jnp.where(row8 == g, jnp.broadcast_to(idx_g, (8, LANES)), idx_all)
    idx_ref[0] = idx_all


def _nsa_sample_select(q16, ck, cv, q_pos):
    nb = q16.shape[0]
    nblk = ck.shape[1]
    m3 = lambda b: (b, 0, 0)
    return pl.pallas_call(
        functools.partial(_nsa_sample_select_kernel, q_pos=q_pos),
        grid=(nb,),
        in_specs=[pl.BlockSpec((1, H_NSA, HEAD_DIM), m3),
                  pl.BlockSpec((1, nblk, LANES), m3),
                  pl.BlockSpec((1, nblk, LANES), m3)],
        out_specs=[pl.BlockSpec((1, H_NSA, HEAD_DIM), m3), pl.BlockSpec((1, 8, LANES), m3)],
        out_shape=[jax.ShapeDtypeStruct((nb, H_NSA, HEAD_DIM), F32),
                   jax.ShapeDtypeStruct((nb, 8, LANES), I32)],
        compiler_params=_params(("parallel",)),
        name="nsa_sample_select",
    )(q16, ck, cv)


def _nsa_sample_attend_kernel(pt_ref, idx_ref, q_ref, k_hbm, v_hbm, kn_ref, vn_ref, kwn_ref, vwn_ref,
                              wk_ref, wv_ref, ocmp_ref, bg_ref, gate_ref, o_ref, kbuf, vbuf, sem,
                              *, q_pos):
    b = pl.program_id(0)
    n_slots = kbuf.shape[1]
    n_pages = pt_ref.shape[1]

    def copies(g, s):
        blk = jnp.maximum(idx_ref[b * G_NSA + g, s], 0)
        pid = pt_ref[b, jnp.minimum(blk // 2, n_pages - 1)]
        rows = pl.ds(g * HEAD_DIM, HEAD_DIM)
        return (pltpu.make_async_copy(k_hbm.at[pid, rows], kbuf.at[g, s], sem.at[0, g]),
                pltpu.make_async_copy(v_hbm.at[pid, rows], vbuf.at[g, s], sem.at[1, g]))

    for g in range(G_NSA):
        for s in range(n_slots):
            for c in copies(g, s):
                c.start()

    q = q_ref[0]
    bg = bg_ref[0]
    col = lax.broadcasted_iota(I32, (HPG, LANES), 1)
    hrow = lax.broadcasted_iota(I32, (HPG, LANES), 0)
    key_half = lax.broadcasted_iota(I32, (1, PAGE_SIZE), 1) // SEL_BLOCK
    wlen = wk_ref.shape[3]
    w_pos = q_pos - wlen + lax.broadcasted_iota(I32, (1, wlen), 1)
    wmask = (w_pos > q_pos - WINDOW) & (w_pos >= 0)

    def new_token(x_ref, g):
        return x_ref[0][:, g * HEAD_DIM:(g + 1) * HEAD_DIM].astype(BF16).astype(F32)

    for g in range(G_NSA):
        qg = q[g * HPG:(g + 1) * HPG]
        qf = qg.astype(F32)
        sw = jnp.where(wmask, _dot(qg, wk_ref[0, g].astype(BF16)), NEG_INF)
        sw_new = jnp.sum(qf * new_token(kwn_ref, g), axis=-1, keepdims=True)
        mw = jnp.maximum(jnp.max(sw, axis=-1, keepdims=True), sw_new)
        pw = jnp.where(wmask, jnp.exp2(sw - mw), 0.0)
        pw_new = jnp.exp2(sw_new - mw)
        o_win = _dot_nt(pw.astype(BF16), wv_ref[0, g].astype(BF16))
        o_win = o_win + pw_new.astype(BF16).astype(F32) * new_token(vwn_ref, g)
        o_win = o_win / (jnp.sum(pw, axis=-1, keepdims=True) + pw_new)
        for s in range(n_slots):
            for c in copies(g, s):
                c.wait()
        scores, masks = [], []
        for s in range(n_slots):
            blk = idx_ref[b * G_NSA + g, s]
            ok = (key_half == jnp.maximum(blk, 0) % 2) & (blk >= 0)
            scores.append(jnp.where(ok, _dot(qg, kbuf[g, s].astype(BF16)), NEG_INF))
            masks.append(ok)
        s_new = jnp.sum(qf * new_token(kn_ref, g), axis=-1, keepdims=True)
        m = s_new
        for sc in scores:
            m = jnp.maximum(m, jnp.max(sc, axis=-1, keepdims=True))
        p_new = jnp.exp2(s_new - m)
        den = p_new
        acc = p_new.astype(BF16).astype(F32) * new_token(vn_ref, g)
        for s in range(n_slots):
            p = jnp.where(masks[s], jnp.exp2(scores[s] - m), 0.0)
            den = den + jnp.sum(p, axis=-1, keepdims=True)
            acc = acc + _dot_nt(p.astype(BF16), vbuf[g, s].astype(BF16))
        o_sel = acc / den
        gates = []
        for br in range(3):
            pick = col == g * HPG * 3 + hrow * 3 + br
            gates.append(jnp.sum(jnp.where(pick, jnp.broadcast_to(bg, (HPG, LANES)), 0.0), axis=-1, keepdims=True))
        rows = slice(g * HPG, (g + 1) * HPG)
        mixed = gates[0] * ocmp_ref[0, rows, :] + gates[1] * o_sel + gates[2] * o_win
        o_ref[0, rows, :] = mixed * gate_ref[0, rows, :]


def _nsa_sample_attend(page_table, idx, q16, sel_kt, sel_vt, ks_new, vs_new, kw_new, vw_new, win_kt, win_vt, ocmp,
                       bg, gate16, q_pos):
    nb = q16.shape[0]
    n_slots = TOP_N - 1
    wlen = win_kt.shape[3]
    m3 = lambda b, pt, ix: (b, 0, 0)
    m4 = lambda b, pt, ix: (b, 0, 0, 0)
    row = pl.BlockSpec((1, 1, LANES), m3)
    heads = pl.BlockSpec((1, H_NSA, HEAD_DIM), m3)
    win = pl.BlockSpec((1, G_NSA, HEAD_DIM, wlen), m4)
    grid_spec = pltpu.PrefetchScalarGridSpec(
        num_scalar_prefetch=2,
        grid=(nb,),
        in_specs=[heads, pl.BlockSpec(memory_space=pl.ANY), pl.BlockSpec(memory_space=pl.ANY),
                  row, row, row, row, win, win, heads, row, heads],
        out_specs=heads,
        scratch_shapes=[pltpu.VMEM((G_NSA, n_slots, HEAD_DIM, PAGE_SIZE), F32),
                        pltpu.VMEM((G_NSA, n_slots, HEAD_DIM, PAGE_SIZE), F32),
                        pltpu.SemaphoreType.DMA((2, G_NSA))],
    )
    return pl.pallas_call(
        functools.partial(_nsa_sample_attend_kernel, q_pos=q_pos),
        grid_spec=grid_spec,
        out_shape=jax.ShapeDtypeStruct((nb, H_NSA, HEAD_DIM), F32),
        compiler_params=_params(("arbitrary",)),
        name="nsa_sample_attend",
    )(page_table, idx, q16, sel_kt, sel_vt, ks_new, vs_new, kw_new, vw_new, win_kt, win_vt, ocmp, bg, gate16)


def _layer0_sample(x2d, past_len, state, cache_k, cache_v, page_table, norm_g, w_in_bf, ret_gn_g, sb_qn_g, sb_kn_g,
                   w_out_bf):
    nb = x2d.shape[0]
    cos, sin = _rope_tables(jnp.full((nb,), past_len, I32))
    ret, sbq, sbg, sbk, sbv = _inproj0(x2d, norm_g, w_in_bf, cos, sin, _pair_gain(sb_qn_g), _pair_gain(sb_kn_g),
                                       nb, nb, False)
    mix_ret, state_new = _ret_sample(ret, state, ret_gn_g)
    mix_sb = _sb_sample(sbq.reshape(nb, H_SB, HEAD_DIM), sbg.reshape(nb, H_SB, HEAD_DIM),
                        _pages_t(cache_k), _pages_t(cache_v), page_table)
    mix_sb = mix_sb.reshape(nb, H_SB * HEAD_DIM)
    half = w_out_bf.shape[0] // 2
    y = _outproj(x2d, [mix_ret, mix_sb], [w_out_bf[:half], w_out_bf[half:]], nb)
    return y, state_new, sbk, sbv


def _layer1_sample(x2d, past_len, cmp_k, cmp_v, sel_k, sel_v, win_k, win_v, page_table, norm_g, w_in_bf, qn_g,
                   cmp_kn_g, sel_kn_g, win_kn_g, pe128, wk_bd, wv_bd, w_out_bf):
    nb = x2d.shape[0]
    (qn, gate, bg, kc, vc, ks, vs, kw, vw) = _inproj1(
        x2d, norm_g, w_in_bf, _pair_gain(qn_g), _pair_gain(sel_kn_g), _pair_gain(win_kn_g), nb, nb, False)
    n_phys = cmp_k.shape[0]
    pool = lambda t: _pages_t(t).reshape(n_phys, KV_W, PAGE_SIZE)
    ck, cv = _compress_sample(pool(cmp_k), pool(cmp_v), page_table, pe128, wk_bd, wv_bd, _pair_gain(cmp_kn_g))
    q16 = qn.reshape(nb, H_NSA, HEAD_DIM)
    ocmp, idx = _nsa_sample_select(q16, ck, cv, past_len)
    idx2 = idx[:, :G_NSA, :TOP_N].reshape(nb * G_NSA, TOP_N)
    mix = _nsa_sample_attend(page_table, idx2, q16, pool(sel_k), pool(sel_v), ks[:, None, :], vs[:, None, :],
                             kw[:, None, :], vw[:, None, :], _pages_t(win_k), _pages_t(win_v), ocmp,
                             bg[:, None, :], gate.reshape(nb, H_NSA, HEAD_DIM), past_len)
    y = _outproj(x2d, [mix.reshape(nb, NSA_W)], [w_out_bf], nb)
    gd = (nb, 1, G_NSA, HEAD_DIM)
    wk_new = jnp.concatenate([win_k[:, 1:], kw.reshape(gd)], axis=1)
    wv_new = jnp.concatenate([win_v[:, 1:], vw.reshape(gd)], axis=1)
    return y, kc, vc, ks, vs, wk_new, wv_new


def kernel(x_prompt, x_sample, state_ret, cache_sb_k, cache_sb_v, cache_cmp_k, cache_cmp_v, cache_sel_k,
           cache_sel_v, cache_win_k, cache_win_v, page_table, norm0_g, w_in0, ret_gn_g, sb_qn_g, sb_kn_g,
           w_out0, norm1_g, w_in1, nsa_qn_g, cmp_kn_g, sel_kn_g, win_kn_g, cmp_pe, w_cmp_k, w_cmp_v, w_out1):
    batch, seq, d = x_prompt.shape
    nb = x_sample.shape[0]
    past_len = page_table.shape[1] * PAGE_SIZE
    tm = 256
    w_in0_bf = w_in0.astype(BF16)
    w_out0_bf = w_out0.astype(BF16)
    w_in1_bf = _pad_w_in1(w_in1)
    w_out1_bf = w_out1.astype(BF16)
    pe128 = jnp.tile(cmp_pe, (1, G_NSA))
    wk_bd = _blockdiag_w(w_cmp_k)
    wv_bd = _blockdiag_w(w_cmp_v)

    xp = x_prompt.reshape(batch * seq, d)
    y1p, ret_p, sbk_p, sbv_p = _layer0_prompt(xp, batch, seq, norm0_g, w_in0_bf, ret_gn_g, sb_qn_g, sb_kn_g,
                                              w_out0_bf, tm)
    y2p, ckp, cvp, skp, svp, wkp, wvp = _layer1_prompt(y1p, batch, seq, norm1_g, w_in1_bf, nsa_qn_g, cmp_kn_g,
                                                        sel_kn_g, win_kn_g, pe128, wk_bd, wv_bd, w_out1_bf, tm)
    xs = x_sample.reshape(nb, d)
    y1s, ret_s, sbk_s, sbv_s = _layer0_sample(xs, past_len, state_ret, cache_sb_k, cache_sb_v, page_table, norm0_g,
                                              w_in0_bf, ret_gn_g, sb_qn_g, sb_kn_g, w_out0_bf)
    y2s, cks, cvs, sks, svs, wks, wvs = _layer1_sample(y1s, past_len, cache_cmp_k, cache_cmp_v, cache_sel_k,
                                                        cache_sel_v, cache_win_k, cache_win_v, page_table, norm1_g,
                                                        w_in1_bf, nsa_qn_g, cmp_kn_g, sel_kn_g, win_kn_g, pe128,
                                                        wk_bd, wv_bd, w_out1_bf)
    keep = min(WINDOW, seq)
    gd = (nb, 1, G_NSA, HEAD_DIM)

    def rows_major(t):
        return t.transpose(0, 3, 1, 2)

    def groups(t):
        return rows_major(t.reshape(batch, G_NSA, HEAD_DIM, seq))

    return (y2p.reshape(batch, seq, d), y2s.reshape(nb, 1, d), ret_p, ret_s,
            rows_major(sbk_p), rows_major(sbv_p),
            sbk_s.reshape(nb, 1, H_SB, HEAD_DIM), sbv_s.reshape(nb, 1, H_SB, HEAD_DIM),
            groups(ckp), groups(cvp), groups(skp), groups(svp),
            groups(wkp)[:, seq - keep:], groups(wvp)[:, seq - keep:],
            cks.reshape(gd), cvs.reshape(gd), sks.reshape(gd), svs.reshape(gd), wks, wvs)
```

```python
import functools

import jax
import jax.numpy as jnp
import numpy as np
from jax import lax
from jax.experimental import pallas as pl
from jax.experimental.pallas import tpu as pltpu

F32 = jnp.float32
BF16 = jnp.bfloat16
I32 = jnp.int32

HEAD_DIM = 64
LANES = 128
H_RET = 8
H_SB = 8
RET_CHUNK = 128
ROPE_BASE = 10000.0
H_NSA = 16
G_NSA = 2
HPG = H_NSA // G_NSA
CMP_BLOCK = 32
SEL_BLOCK = 64
TOP_N = 16
WINDOW = 512
FORCE_BONUS = 1000.0
NEG_INF = -1e30
EPS = 1e-6
PAGE_SIZE = 128
NSA_KEY_TILE = 128
NSA_Q_TILE = 256
SCALE = HEAD_DIM ** -0.5
LOG2E = 1.4426950408889634
VMEM_LIMIT = 56 * 1024 * 1024


def _dot(a, b):
    return jnp.dot(a, b, preferred_element_type=F32)


def _dot_nt(a, b):
    return lax.dot_general(a, b, (((1,), (1,)), ((), ())), preferred_element_type=F32)


def _dot_tn(a, b):
    return lax.dot_general(a, b, (((0,), (0,)), ((), ())), preferred_element_type=F32)


def _split_bf16(x):
    hi = x.astype(BF16)
    lo = (x - hi.astype(F32)).astype(BF16)
    return hi, lo


def _sigmoid(x):
    return 1.0 / (1.0 + jnp.exp(-x))


def _silu(x):
    return x * _sigmoid(x)


def _lane_lo(shape):
    return lax.broadcasted_iota(I32, shape, len(shape) - 1) < HEAD_DIM


def _pair_sum(x, lo):
    s_lo = jnp.sum(jnp.where(lo, x, 0.0), axis=-1, keepdims=True)
    s_hi = jnp.sum(jnp.where(lo, 0.0, x), axis=-1, keepdims=True)
    return jnp.where(lo, s_lo, s_hi)


def _pair_rmsnorm(x, g, lo):
    ms = _pair_sum(x * x, lo) * (1.0 / HEAD_DIM)
    return x * lax.rsqrt(ms + EPS) * g


def _rope_pair(x, cos, sin_signed):
    lane = lax.broadcasted_iota(I32, x.shape, 1)
    first = (lane & 32) == 0
    partner = jnp.where(first, pltpu.roll(x, 96, 1), pltpu.roll(x, 32, 1))
    return x * cos + partner * sin_signed


def _transpose_tiles(x):
    r, c = x.shape
    if r <= LANES and c <= LANES:
        return x.T
    rows = [jnp.concatenate([x[ri * LANES:(ri + 1) * LANES, cj * LANES:(cj + 1) * LANES].T
                             for ri in range(r // LANES)], axis=1) for cj in range(c // LANES)]
    return jnp.concatenate(rows, axis=0)


def _params(sem):
    return pltpu.CompilerParams(dimension_semantics=sem, vmem_limit_bytes=VMEM_LIMIT)


def _store_head_major(dst_ref, t, pair, r):
    cols = slice(r * LANES, (r + 1) * LANES)
    dst_ref[0, 2 * pair, :, cols] = t[:HEAD_DIM]
    dst_ref[0, 2 * pair + 1, :, cols] = t[HEAD_DIM:]


def _inproj0_kernel(x_ref, g_ref, w_ref, cos_ref, sin_ref, qg_ref, kg_ref,
                    ret_ref, sbq_ref, sbg_ref, *kv_refs, prompt):
    x = x_ref[...]
    xn = x * lax.rsqrt(jnp.mean(x * x, axis=-1, keepdims=True) + EPS) * g_ref[...]
    xb = xn.astype(BF16)
    cos = cos_ref[...]
    sin = sin_ref[...]
    lo = _lane_lo((1, LANES))
    width = H_RET * HEAD_DIM

    def group(i):
        return _dot(xb, w_ref[:, i * width:(i + 1) * width])

    def tiles(a):
        return [a[:, t * LANES:(t + 1) * LANES] for t in range(width // LANES)]

    for t, v in enumerate(tiles(group(0))):
        ret_ref[:, t * LANES:(t + 1) * LANES] = _rope_pair(v, cos, sin)
    for t, v in enumerate(tiles(group(1))):
        ret_ref[:, width + t * LANES:width + (t + 1) * LANES] = _rope_pair(v, cos, sin) * SCALE
    ret_ref[:, 2 * width:3 * width] = group(2)
    ret_ref[:, 3 * width:4 * width] = group(3)
    for t, v in enumerate(tiles(group(4))):
        sbq_ref[:, t * LANES:(t + 1) * LANES] = (_pair_rmsnorm(v, qg_ref[...], lo) * SCALE).astype(BF16)
    sk = [_pair_rmsnorm(v, kg_ref[...], lo) for v in tiles(group(5))]
    sv = tiles(group(6))
    if prompt:
        kb_ref, vt_ref, kt_out, vt_out = kv_refs
        for t in range(width // LANES):
            kb_ref[:, t * LANES:(t + 1) * LANES] = sk[t].astype(BF16)
            for r in range(x.shape[0] // LANES):
                rows = slice(r * LANES, (r + 1) * LANES)
                vt = sv[t][rows].T
                _store_head_major(kt_out, sk[t][rows].T, t, r)
                _store_head_major(vt_out, vt, t, r)
                vt_ref[r, t] = vt.astype(BF16)
    else:
        k_ref, v_ref = kv_refs
        for t in range(width // LANES):
            k_ref[:, t * LANES:(t + 1) * LANES] = sk[t]
            v_ref[:, t * LANES:(t + 1) * LANES] = sv[t]
    sbg_ref[...] = _silu(group(7))


def _inproj0(x2d, norm_g, w_bf, cos, sin, qg, kg, tm, seq, prompt):
    rows, d = x2d.shape
    width = H_RET * HEAD_DIM
    n = w_bf.shape[1]
    grid = (rows // tm,)
    per_seq = seq // tm
    row = lambda i: (i, 0)
    fixed = lambda i: (0, 0)
    pos = lambda i: (i % per_seq, 0)
    if prompt:
        npair = width // LANES
        head_major = pl.BlockSpec((1, H_SB, HEAD_DIM, tm), lambda i: (i // per_seq, 0, 0, i % per_seq))
        hm_shape = jax.ShapeDtypeStruct((rows // seq, H_SB, HEAD_DIM, seq), F32)
        extra_specs = [pl.BlockSpec((tm, width), row),
                       pl.BlockSpec((tm // LANES, npair, LANES, LANES), lambda i: (i, 0, 0, 0)),
                       head_major, head_major]
        extra_shapes = [jax.ShapeDtypeStruct((rows, width), BF16),
                        jax.ShapeDtypeStruct((rows // LANES, npair, LANES, LANES), BF16),
                        hm_shape, hm_shape]
    else:
        extra_specs = [pl.BlockSpec((tm, width), row)] * 2
        extra_shapes = [jax.ShapeDtypeStruct((rows, width), F32)] * 2
    return pl.pallas_call(
        functools.partial(_inproj0_kernel, prompt=prompt),
        grid=grid,
        in_specs=[
            pl.BlockSpec((tm, d), row),
            pl.BlockSpec((1, d), fixed),
            pl.BlockSpec((d, n), fixed),
            pl.BlockSpec((tm, LANES), pos),
            pl.BlockSpec((tm, LANES), pos),
            pl.BlockSpec((1, LANES), fixed),
            pl.BlockSpec((1, LANES), fixed),
        ],
        out_specs=[
            pl.BlockSpec((tm, 4 * width), row),
            pl.BlockSpec((tm, width), row),
            pl.BlockSpec((tm, width), row),
        ] + extra_specs,
        out_shape=[
            jax.ShapeDtypeStruct((rows, 4 * width), F32),
            jax.ShapeDtypeStruct((rows, width), BF16),
            jax.ShapeDtypeStruct((rows, width), F32),
        ] + extra_shapes,
        compiler_params=_params(("parallel",)),
        name="inproj0",
    )(x2d, norm_g.reshape(1, d), w_bf, cos, sin, qg, kg)


def _ret_prompt_kernel(ret_ref, inner_ref, qdec_ref, kdec_ref, cdec_ref, gn_ref, mix_ref, state_ref, s_scr):
    c = pl.program_id(1)
    npair = H_RET // 2
    width = H_RET * HEAD_DIM

    @pl.when(c == 0)
    def _():
        s_scr[...] = jnp.zeros_like(s_scr)

    def tile(group, p):
        return ret_ref[:, group * width + p * LANES:group * width + (p + 1) * LANES]

    lo = _lane_lo((ret_ref.shape[0], LANES))
    same_head = ((lax.broadcasted_iota(I32, (LANES, LANES), 0) < HEAD_DIM)
                 == (lax.broadcasted_iota(I32, (LANES, LANES), 1) < HEAD_DIM))
    qs = [tile(0, p).astype(BF16) for p in range(npair)]
    ks = [tile(1, p) for p in range(npair)]
    kbs = [k.astype(BF16) for k in ks]
    vbs = [tile(2, p).astype(BF16) for p in range(npair)]
    states = [s_scr[p] for p in range(npair)]
    zero = jnp.zeros_like(qs[0])
    att_lo = [_dot_nt(jnp.where(lo, qs[p], zero), kbs[p]) for p in range(npair)]
    att_hi = [_dot_nt(jnp.where(lo, zero, qs[p]), kbs[p]) for p in range(npair)]
    cross = [_dot(qs[p], states[p].astype(BF16)) for p in range(npair)]
    kv = [_dot_tn((ks[p] * kdec_ref[p]).astype(BF16), vbs[p]) for p in range(npair)]
    a_lo = [(att_lo[p] * inner_ref[2 * p]).astype(BF16) for p in range(npair)]
    a_hi = [(att_hi[p] * inner_ref[2 * p + 1]).astype(BF16) for p in range(npair)]
    o_lo = [_dot(a_lo[p], vbs[p]) for p in range(npair)]
    o_hi = [_dot(a_hi[p], vbs[p]) for p in range(npair)]
    for p in range(npair):
        s_new = states[p] * cdec_ref[p] + jnp.where(same_head, kv[p], 0.0)
        s_scr[p] = s_new
        state_ref[0, p] = s_new
        o = jnp.where(lo, o_lo[p], o_hi[p]) + cross[p] * qdec_ref[p]
        mu = _pair_sum(o, lo) * (1.0 / HEAD_DIM)
        oc = o - mu
        var = _pair_sum(oc * oc, lo) * (1.0 / HEAD_DIM)
        on = oc * lax.rsqrt(var + EPS) * gn_ref[p]
        mix_ref[:, p * LANES:(p + 1) * LANES] = on * _silu(tile(3, p))


def _ret_tables(chunk):
    h = jnp.arange(H_RET, dtype=F32)
    lg = jnp.log1p(-jnp.exp2(-5.0 - h))
    idx = jnp.arange(chunk, dtype=F32)
    diff = idx[:, None] - idx[None, :]
    inner = jnp.where(diff >= 0, jnp.exp(lg[:, None, None] * jnp.maximum(diff, 0.0)), 0.0)
    q_dec = jnp.exp(lg[:, None] * (idx + 1.0))
    k_dec = jnp.exp(lg[:, None] * (chunk - 1.0 - idx))
    c_dec = jnp.exp(lg * chunk)

    def lanes(t):
        t = jnp.repeat(t[:, :, None], HEAD_DIM, axis=2)
        return jnp.concatenate([t[0::2], t[1::2]], axis=2)

    cd = jnp.repeat(c_dec[:, None, None], HEAD_DIM, axis=2)
    cd = jnp.concatenate([cd[0::2], cd[1::2]], axis=2)
    return inner, lanes(q_dec), lanes(k_dec), cd


def _ret_prompt(ret, gn_g, batch, seq):
    width = H_RET * HEAD_DIM
    npair = H_RET // 2
    c = RET_CHUNK
    nc = seq // c
    inner, qdec, kdec, cdec = _ret_tables(c)
    gn = gn_g.reshape(npair, 1, LANES)
    tab = lambda b, i: (0, 0, 0)
    return pl.pallas_call(
        _ret_prompt_kernel,
        grid=(batch, nc),
        in_specs=[
            pl.BlockSpec((c, 4 * width), lambda b, i: (b * nc + i, 0)),
            pl.BlockSpec((H_RET, c, c), tab),
            pl.BlockSpec((npair, c, LANES), tab),
            pl.BlockSpec((npair, c, LANES), tab),
            pl.BlockSpec((npair, 1, LANES), tab),
            pl.BlockSpec((npair, 1, LANES), tab),
        ],
        out_specs=[
            pl.BlockSpec((c, width), lambda b, i: (b * nc + i, 0)),
            pl.BlockSpec((1, npair, LANES, LANES), lambda b, i: (b, 0, 0, 0)),
        ],
        out_shape=[
            jax.ShapeDtypeStruct((batch * seq, width), F32),
            jax.ShapeDtypeStruct((batch, npair, LANES, LANES), F32),
        ],
        scratch_shapes=[pltpu.VMEM((npair, LANES, LANES), F32)],
        compiler_params=_params(("parallel", "arbitrary")),
        name="retention_prompt",
    )(ret, inner, qdec, kdec, cdec, gn)


def _unpair_state(sp):
    a = sp[:, :, :HEAD_DIM, :HEAD_DIM]
    b = sp[:, :, HEAD_DIM:, HEAD_DIM:]
    return jnp.stack([a, b], axis=2).reshape(sp.shape[0], -1, HEAD_DIM, HEAD_DIM)


SB_LOG_ZERO = -104.0
SB_Q_TILE = 256
SB_KEY_TILE = 128


def _sb_prompt_kernel(q_ref, k_ref, vt_ref, g_ref, o_ref, acc_scr):
    i = pl.program_id(1)
    tq = q_ref.shape[0]
    kt = SB_KEY_TILE
    npair = H_SB // 2
    lower = (lax.broadcasted_iota(I32, (kt, kt), 1)
             >= lax.broadcasted_iota(I32, (kt, kt), 0)).astype(BF16)
    k_in = lax.broadcasted_iota(I32, (kt, tq), 0)
    q_pos = i * tq + lax.broadcasted_iota(I32, (kt, tq), 1)
    zeros = jnp.zeros((HEAD_DIM, tq), F32)
    qts = []
    for p in range(npair):
        t = _transpose_tiles(q_ref[:, p * LANES:(p + 1) * LANES].astype(F32))
        qts.append(jnp.concatenate([t[:HEAD_DIM], zeros], axis=0).astype(BF16))
        qts.append(jnp.concatenate([zeros, t[HEAD_DIM:]], axis=0).astype(BF16))
    acc_scr[...] = jnp.zeros_like(acc_scr)

    def sweep(j, rs, masked):
        start = pl.multiple_of(j * kt, kt)
        diag_mask = (j * kt + k_in < q_pos) if masked else None
        kjs = [k_ref[pl.ds(start, kt), p * LANES:(p + 1) * LANES] for p in range(npair)]
        vts = [vt_ref[j, p] for p in range(npair)]
        zs = [_dot(kjs[h // 2], qt) for h, qt in enumerate(qts)]
        ls_pos, lms, parts = [], [], []
        for z in zs:
            lsp = jnp.minimum(z, 0.0) - jnp.log1p(jnp.exp(-jnp.abs(z)))
            lm = lsp - z
            if diag_mask is not None:
                lm = jnp.where(diag_mask, lm, 0.0)
            ls_pos.append(lsp)
            lms.append(lm)
            parts.append(_split_bf16(lm))
        s_incl = [_dot(lower, hi) + _dot(lower, lo_part) for hi, lo_part in parts]
        probs = []
        for h in range(H_SB):
            a = jnp.exp(ls_pos[h] + (s_incl[h] - lms[h]) + rs[h])
            if diag_mask is not None:
                a = jnp.where(diag_mask, a, 0.0)
            probs.append(a.astype(BF16))
        for h in range(H_SB):
            half = h % 2
            acc_scr[h] = acc_scr[h] + _dot(vts[h // 2][half * HEAD_DIM:(half + 1) * HEAD_DIM, :], probs[h])
        return [rs[h] + s_incl[h][0:1, :] for h in range(H_SB)]

    def alive_of(rs):
        worst = rs[0]
        for r in rs[1:]:
            worst = jnp.maximum(worst, r)
        return jnp.max(worst) > SB_LOG_ZERO

    n_diag = tq // kt
    j_top = (i + 1) * n_diag - 1
    rs = [jnp.zeros((1, tq), F32)] * H_SB
    for d in range(n_diag):
        rs = sweep(j_top - d, rs, True)
    n_old = i * n_diag

    def cond(carry):
        t, alive = carry[0], carry[1]
        return (t < n_old) & alive

    def body(carry):
        t = carry[0]
        rs = sweep(n_old - 1 - t, list(carry[2:]), False)
        return (t + 1, alive_of(rs)) + tuple(rs)

    lax.while_loop(cond, body, (jnp.int32(0), alive_of(rs)) + tuple(rs))
    for p in range(npair):
        both = _transpose_tiles(jnp.concatenate([acc_scr[2 * p], acc_scr[2 * p + 1]], axis=0))
        o_ref[:, p * LANES:(p + 1) * LANES] = both * g_ref[:, p * LANES:(p + 1) * LANES]


def _sb_prompt(sbq, sbk_b, sbvt, sbg, batch, seq):
    width = H_SB * HEAD_DIM
    npair = H_SB // 2
    tq = SB_Q_TILE
    kt = SB_KEY_TILE
    nq = seq // tq
    qmap = lambda b, i: (b * nq + i, 0)
    return pl.pallas_call(
        _sb_prompt_kernel,
        grid=(batch, nq),
        in_specs=[
            pl.BlockSpec((tq, width), qmap),
            pl.BlockSpec((seq, width), lambda b, i: (b, 0)),
            pl.BlockSpec((seq // kt, npair, LANES, kt), lambda b, i: (b, 0, 0, 0)),
            pl.BlockSpec((tq, width), qmap),
        ],
        out_specs=pl.BlockSpec((tq, width), qmap),
        out_shape=jax.ShapeDtypeStruct((batch * seq, width), F32),
        scratch_shapes=[pltpu.VMEM((H_SB, HEAD_DIM, tq), F32)],
        compiler_params=_params(("parallel", "arbitrary")),
        name="sb_prompt",
    )(sbq, sbk_b, sbvt, sbg)


def _outproj_kernel(*refs, n_mix):
    x_ref = refs[0]
    mix_refs = refs[1:1 + n_mix]
    w_refs = refs[1 + n_mix:1 + 2 * n_mix]
    y_ref = refs[1 + 2 * n_mix]
    acc = x_ref[...]
    for m_ref, w_ref in zip(mix_refs, w_refs):
        acc = acc + _dot(m_ref[...].astype(BF16), w_ref[...])
    y_ref[...] = acc


def _outproj(x2d, mixes, ws, tm):
    rows, d = x2d.shape
    n_mix = len(mixes)
    row = lambda i: (i, 0)
    fixed = lambda i: (0, 0)
    in_specs = [pl.BlockSpec((tm, d), row)]
    in_specs += [pl.BlockSpec((tm, m.shape[1]), row) for m in mixes]
    in_specs += [pl.BlockSpec(w.shape, fixed) for w in ws]
    return pl.pallas_call(
        functools.partial(_outproj_kernel, n_mix=n_mix),
        grid=(rows // tm,),
        in_specs=in_specs,
        out_specs=pl.BlockSpec((tm, d), row),
        out_shape=jax.ShapeDtypeStruct((rows, d), F32),
        compiler_params=_params(("parallel",)),
        name="outproj",
    )(x2d, *mixes, *ws)


def _rope_tables(pos):
    half = HEAD_DIM // 2
    inv = ROPE_BASE ** (-jnp.arange(half, dtype=F32) / half)
    ang = pos.astype(F32)[:, None] * inv[None, :]
    cos, sin = jnp.cos(ang), jnp.sin(ang)
    cos128 = jnp.tile(cos, (1, LANES // half))
    sin128 = jnp.tile(jnp.concatenate([-sin, sin], axis=1), (1, LANES // HEAD_DIM))
    return cos128, sin128


def _pair_gain(g):
    return jnp.tile(g.astype(F32), LANES // HEAD_DIM).reshape(1, LANES)


def _layer0_prompt(x2d, batch, seq, norm_g, w_in_bf, ret_gn_g, sb_qn_g, sb_kn_g, w_out_bf, tm):
    cos, sin = _rope_tables(jnp.arange(seq, dtype=I32))
    ret, sbq, sbg, sbk_b, sbvt, sbk_hm, sbv_hm = _inproj0(x2d, norm_g, w_in_bf, cos, sin, _pair_gain(sb_qn_g),
                                                          _pair_gain(sb_kn_g), tm, seq, True)
    mix_ret, state_pairs = _ret_prompt(ret, ret_gn_g, batch, seq)
    mix_sb = _sb_prompt(sbq, sbk_b, sbvt, sbg, batch, seq)
    half = w_out_bf.shape[0] // 2
    y = _outproj(x2d, [mix_ret, mix_sb], [w_out_bf[:half], w_out_bf[half:]], tm)
    return y, _unpair_state(state_pairs), sbk_hm, sbv_hm


NSA_W = H_NSA * HEAD_DIM
KV_W = G_NSA * HEAD_DIM
IN1_COLS = 2 * NSA_W + 6 * KV_W + LANES


def _inproj1_kernel(x_ref, g_ref, w_ref, qg_ref, skg_ref, wkg_ref, q_ref, gate_ref, bg_ref, *kv_refs, prompt):
    x = x_ref[...]
    xn = x * lax.rsqrt(jnp.mean(x * x, axis=-1, keepdims=True) + EPS) * g_ref[...]
    xb = xn.astype(BF16)
    lo = _lane_lo((1, LANES))
    a = _dot(xb, w_ref[:, 0:NSA_W])
    for t in range(NSA_W // LANES):
        v = a[:, t * LANES:(t + 1) * LANES]
        q_ref[:, t * LANES:(t + 1) * LANES] = (_pair_rmsnorm(v, qg_ref[...], lo) * (SCALE * LOG2E)).astype(BF16)
    a = _dot(xb, w_ref[:, NSA_W:NSA_W + 6 * KV_W])
    kc, vc, ks, vs, kw, vw = [a[:, t * KV_W:(t + 1) * KV_W] for t in range(6)]
    ks = _pair_rmsnorm(ks, skg_ref[...], lo)
    kw = _pair_rmsnorm(kw, wkg_ref[...], lo)
    six = (kc, vc, ks, vs, kw, vw)
    if prompt:
        t_out = kv_refs[:6]
        kc_ref, vc_ref, ksb_ref, kwb_ref, vst_ref, vwt_ref = kv_refs[6:]
        kc_ref[...] = kc
        vc_ref[...] = vc
        ksb_ref[...] = ks.astype(BF16)
        kwb_ref[...] = kw.astype(BF16)
        per_tile = NSA_KEY_TILE // LANES
        for r in range(x.shape[0] // LANES):
            rows = slice(r * LANES, (r + 1) * LANES)
            cols = slice((r % per_tile) * LANES, (r % per_tile + 1) * LANES)
            for src, dst in zip(six, t_out):
                t = src[rows].T
                dst[0, :, rows] = t
                if src is vs:
                    vst_ref[r // per_tile, :, cols] = t.astype(BF16)
                if src is vw:
                    vwt_ref[r // per_tile, :, cols] = t.astype(BF16)
    else:
        for src, dst in zip(six, kv_refs):
            dst[...] = src
    off = NSA_W + 6 * KV_W
    gate_ref[...] = _silu(_dot(xb, w_ref[:, off:off + NSA_W]))
    bg_ref[...] = _sigmoid(_dot(xb, w_ref[:, off + NSA_W:off + NSA_W + LANES]))


def _inproj1(x2d, norm_g, w_bf, qg, skg, wkg, tm, seq, prompt):
    rows, d = x2d.shape
    row = lambda i: (i, 0)
    fixed = lambda i: (0, 0)
    out_specs = [pl.BlockSpec((tm, NSA_W), row), pl.BlockSpec((tm, NSA_W), row), pl.BlockSpec((tm, LANES), row)]
    out_shape = [jax.ShapeDtypeStruct((rows, NSA_W), BF16), jax.ShapeDtypeStruct((rows, NSA_W), F32),
                 jax.ShapeDtypeStruct((rows, LANES), F32)]
    if prompt:
        per_seq = seq // tm
        kt = NSA_KEY_TILE
        out_specs += ([pl.BlockSpec((1, KV_W, tm), lambda i: (i // per_seq, 0, i % per_seq))] * 6
                      + [pl.BlockSpec((tm, KV_W), row)] * 4
                      + [pl.BlockSpec((tm // kt, LANES, kt), lambda i: (i, 0, 0))] * 2)
        out_shape += ([jax.ShapeDtypeStruct((rows // seq, KV_W, seq), F32)] * 6
                      + [jax.ShapeDtypeStruct((rows, KV_W), F32)] * 2
                      + [jax.ShapeDtypeStruct((rows, KV_W), BF16)] * 2
                      + [jax.ShapeDtypeStruct((rows // kt, LANES, kt), BF16)] * 2)
    else:
        out_specs += [pl.BlockSpec((tm, KV_W), row)] * 6
        out_shape += [jax.ShapeDtypeStruct((rows, KV_W), F32)] * 6
    return pl.pallas_call(
        functools.partial(_inproj1_kernel, prompt=prompt),
        grid=(rows // tm,),
        in_specs=[
            pl.BlockSpec((tm, d), row),
            pl.BlockSpec((1, d), fixed),
            pl.BlockSpec((d, IN1_COLS), fixed),
            pl.BlockSpec((1, LANES), fixed),
            pl.BlockSpec((1, LANES), fixed),
            pl.BlockSpec((1, LANES), fixed),
        ],
        out_specs=out_specs,
        out_shape=out_shape,
        compiler_params=_params(("parallel",)),
        name="inproj1",
    )(x2d, norm_g.reshape(1, d), w_bf, qg, skg, wkg)


def _pad_w_in1(w_in1):
    pad = IN1_COLS - w_in1.shape[1]
    return jnp.pad(w_in1, ((0, 0), (0, pad))).astype(BF16)


def _compress_rows(t_ref, pe_ref, w_ref, row0, nblk):
    acc = jnp.zeros((nblk, LANES), F32)
    for l in range(CMP_BLOCK):
        rows = t_ref[pl.ds(row0 + l, nblk, stride=CMP_BLOCK), :]
        acc = acc + _dot((rows + pe_ref[l:l + 1, :]).astype(BF16), w_ref[l])
    return acc


def _compress_prompt_kernel(kc_ref, vc_ref, pe_ref, wk_ref, wv_ref, kg_ref, ck_ref, cvt_ref):
    nblk = ck_ref.shape[0]
    lo = _lane_lo((1, LANES))
    ck = _pair_rmsnorm(_compress_rows(kc_ref, pe_ref, wk_ref, 0, nblk), kg_ref[...], lo)
    cv = _compress_rows(vc_ref, pe_ref, wv_ref, 0, nblk)
    ck_ref[...] = ck.astype(BF16)
    cvt_ref[0] = cv.T.astype(BF16)


def _blockdiag_w(w):
    z = jnp.zeros_like(w)
    top = jnp.concatenate([w, z], axis=2)
    bot = jnp.concatenate([z, w], axis=2)
    return jnp.concatenate([top, bot], axis=1).astype(BF16)


def _compress_prompt(kc, vc, pe128, wk_bd, wv_bd, kg, batch, seq):
    nblk = seq // CMP_BLOCK
    rowb = lambda b: (b, 0)
    fixed2 = lambda b: (0, 0)
    fixed3 = lambda b: (0, 0, 0)
    return pl.pallas_call(
        _compress_prompt_kernel,
        grid=(batch,),
        in_specs=[
            pl.BlockSpec((seq, LANES), rowb),
            pl.BlockSpec((seq, LANES), rowb),
            pl.BlockSpec((CMP_BLOCK, LANES), fixed2),
            pl.BlockSpec((CMP_BLOCK, LANES, LANES), fixed3),
            pl.BlockSpec((CMP_BLOCK, LANES, LANES), fixed3),
            pl.BlockSpec((1, LANES), fixed2),
        ],
        out_specs=[pl.BlockSpec((nblk, LANES), rowb), pl.BlockSpec((1, LANES, nblk), lambda b: (b, 0, 0))],
        out_shape=[jax.ShapeDtypeStruct((batch * nblk, LANES), BF16),
                   jax.ShapeDtypeStruct((batch, LANES, nblk), BF16)],
        compiler_params=_params(("parallel",)),
        name="compress_prompt",
    )(kc, vc, pe128, wk_bd, wv_bd, kg)


def _nsa_prompt_kernel(q_ref, ck_ref, cvt_ref, ks_ref, vst_ref, kw_ref, vwt_ref, bg_ref, gate_ref,
                       o_ref, m_scr, acc_scr, imp_scr, score_scr, chosen_scr, sc_scr):
    i = pl.program_id(1)
    tq = q_ref.shape[0]
    nb = ck_ref.shape[0]
    kt = NSA_KEY_TILE
    ratio = SEL_BLOCK // CMP_BLOCK
    nsb = nb // ratio
    q_pos = i * tq + lax.broadcasted_iota(I32, (1, tq), 1)

    zeros = jnp.zeros((HEAD_DIM, tq), F32)
    qts = []
    for p in range(H_NSA // 2):
        t = _transpose_tiles(q_ref[:, p * LANES:(p + 1) * LANES].astype(F32))
        for ht in (t[:HEAD_DIM], t[HEAD_DIM:]):
            parts = [zeros] * G_NSA
            parts[(2 * p) // HPG] = ht
            qts.append(jnp.concatenate(parts, axis=0).astype(BF16))
    group_of = [h // HPG for h in range(H_NSA)]

    ck = ck_ref[...]
    blk_n = lax.broadcasted_iota(I32, (nb, tq), 0)
    cmask = blk_n * CMP_BLOCK + (CMP_BLOCK - 1) <= q_pos
    cmp_scores = [_dot(ck, qt) for qt in qts]
    cmp_probs = []
    imps = [jnp.zeros((nb, tq), F32) for _ in range(G_NSA)]
    for h, s in enumerate(cmp_scores):
        s = jnp.where(cmask, s, NEG_INF)
        m = jnp.max(s, axis=0, keepdims=True)
        e = jnp.where(cmask, jnp.exp2(s - m), 0.0)
        den = jnp.sum(e, axis=0, keepdims=True)
        pr = e / jnp.where(den > 0.0, den, 1.0)
        imps[group_of[h]] = imps[group_of[h]] + pr
        cmp_probs.append(pr.astype(BF16))
    o_cmp = [_dot(cvt_ref[0, group_of[h] * HEAD_DIM:(group_of[h] + 1) * HEAD_DIM, :], pr)
             for h, pr in enumerate(cmp_probs)]

    sblk = lax.broadcasted_iota(I32, (nsb, tq), 0)
    cur = q_pos // SEL_BLOCK
    forced = (sblk == 0) | (sblk == cur) | (sblk == cur - 1)
    for g in range(G_NSA):
        pooled_parts = []
        for c in range(tq // LANES):
            part_ref = imp_scr.at[c]
            part_ref[...] = imps[g][:, c * LANES:(c + 1) * LANES]
            part = part_ref[pl.ds(0, nsb, stride=ratio), :]
            for r in range(1, ratio):
                part = part + part_ref[pl.ds(r, nsb, stride=ratio), :]
            pooled_parts.append(part)
        pooled = jnp.concatenate(pooled_parts, axis=1)
        score = jnp.where(sblk <= cur, pooled + FORCE_BONUS * forced.astype(F32), NEG_INF)
        score_scr[...] = score
        slab = 8
        cnts = []
        for r0 in range(0, nsb, slab):
            rows = score_scr[r0:min(r0 + slab, nsb), :]
            idx = r0 + lax.broadcasted_iota(I32, rows.shape, 0)
            cnt = jnp.zeros(rows.shape, F32)
            for mblk in range(nsb):
                c = score_scr[mblk:mblk + 1, :]
                if mblk < r0:
                    ahead = c >= rows
                elif mblk >= r0 + rows.shape[0]:
                    ahead = c > rows
                else:
                    ahead = (c > rows) | ((c == rows) & (mblk < idx))
                cnt = cnt + jnp.where(ahead, 1.0, 0.0)
            cnts.append(cnt)
        chosen = (jnp.concatenate(cnts, axis=0) < float(TOP_N)).astype(F32)
        for mblk in range(nsb):
            chosen_scr[g, mblk] = jnp.broadcast_to(chosen[mblk:mblk + 1, :], (8, tq))

    k_in = lax.broadcasted_iota(I32, (kt, tq), 0)
    blocks_per_tile = kt // SEL_BLOCK
    j_diag = ((i + 1) * tq - 1) // kt

    ones_rows = jnp.ones((acc_scr.shape[2] - HEAD_DIM, kt), BF16)

    heads_a = list(range(HPG))
    heads_b = list(range(HPG, H_NSA))

    def flash(k_ref, vt_ref, slot, n_tiles, mask_fn):
        m_scr[slot] = jnp.full(m_scr.shape[1:], NEG_INF, F32)
        acc_scr[slot] = jnp.zeros(acc_scr.shape[1:], F32)

        def load_k(j):
            return k_ref[pl.ds(pl.multiple_of(j * kt, kt), kt), :]

        def softmax_step(h, s, mask):
            s = jnp.where(mask, s, NEG_INF)
            m_old = m_scr[slot, h]
            m_new = jnp.maximum(m_old, jnp.max(s, axis=0, keepdims=True))
            m_scr[slot, h] = m_new
            return jnp.exp2(m_old - m_new)[0:1], jnp.exp2(s - m_new[0:1]).astype(BF16)

        def accumulate(h, alpha, pv):
            acc_scr[slot, h] = alpha * acc_scr[slot, h] + pv

        k_first = load_k(j_diag)
        for n, h in enumerate(heads_b):
            sc_scr[n] = _dot(k_first, qts[h])

        def body(t, carry):
            j = j_diag - t
            kj = load_k(j)
            k_next = load_k(jnp.maximum(j - 1, 0))
            vt = vt_ref[j]
            vts = [jnp.concatenate([vt[g * HEAD_DIM:(g + 1) * HEAD_DIM, :], ones_rows], axis=0)
                   for g in range(G_NSA)]
            scores_a = [_dot(kj, qts[h]) for h in heads_a]
            mask_b = mask_fn(1, j)
            soft_b = [softmax_step(h, sc_scr[n], mask_b) for n, h in enumerate(heads_b)]
            pv_b = [_dot(vts[1], p) for _, p in soft_b]
            next_b = [_dot(k_next, qts[h]) for h in heads_b]
            mask_a = mask_fn(0, j)
            soft_a = [softmax_step(h, scores_a[n], mask_a) for n, h in enumerate(heads_a)]
            pv_a = [_dot(vts[0], p) for _, p in soft_a]
            for n, h in enumerate(heads_b):
                accumulate(h, soft_b[n][0], pv_b[n])
                sc_scr[n] = next_b[n]
            for n, h in enumerate(heads_a):
                accumulate(h, soft_a[n][0], pv_a[n])
            return carry

        lax.fori_loop(0, n_tiles, body, 0)
        outs = []
        for h in range(H_NSA):
            acc = acc_scr[slot, h]
            outs.append(acc[:HEAD_DIM] / acc[HEAD_DIM:HEAD_DIM + 1])
        return outs

    def causal(j):
        return j * kt + k_in <= q_pos

    def sel_mask(g, j):
        rows = [jnp.broadcast_to(chosen_scr[g, blocks_per_tile * j + r][0:1], (SEL_BLOCK, tq))
                for r in range(blocks_per_tile)]
        return (jnp.concatenate(rows, axis=0) > 0.5) & causal(j)

    def win_mask(g, j):
        return causal(j) & (j * kt + k_in > q_pos - WINDOW)

    o_sel = flash(ks_ref, vst_ref, 0, j_diag + 1, sel_mask)
    j_low = jnp.maximum(i * tq - (WINDOW - 1), 0) // kt
    o_win = flash(kw_ref, vwt_ref, 1, j_diag - j_low + 1, win_mask)

    bgt = _transpose_tiles(bg_ref[...])

    def gate_row(h, br):
        c = h * 3 + br
        return bgt[c:c + 1, :]

    for p in range(H_NSA // 2):
        halves = []
        for h in (2 * p, 2 * p + 1):
            halves.append(gate_row(h, 0) * o_cmp[h] + gate_row(h, 1) * o_sel[h] + gate_row(h, 2) * o_win[h])
        mixed = _transpose_tiles(jnp.concatenate(halves, axis=0))
        o_ref[:, p * LANES:(p + 1) * LANES] = mixed * gate_ref[:, p * LANES:(p + 1) * LANES]


def _nsa_prompt(qn, ck, cvt, ksb, vst, kwb, vwt, bg, gate, batch, seq):
    tq = NSA_Q_TILE
    kt = NSA_KEY_TILE
    nq = seq // tq
    nb = seq // CMP_BLOCK
    nsb = nb // (SEL_BLOCK // CMP_BLOCK)
    qmap = lambda b, i: (b * nq + i, 0)
    per_b2 = lambda b, i: (b, 0)
    per_b3 = lambda b, i: (b, 0, 0)
    return pl.pallas_call(
        _nsa_prompt_kernel,
        grid=(batch, nq),
        in_specs=[
            pl.BlockSpec((tq, NSA_W), qmap),
            pl.BlockSpec((nb, LANES), per_b2),
            pl.BlockSpec((1, LANES, nb), per_b3),
            pl.BlockSpec((seq, LANES), per_b2),
            pl.BlockSpec((seq // kt, LANES, kt), per_b3),
            pl.BlockSpec((seq, LANES), per_b2),
            pl.BlockSpec((seq // kt, LANES, kt), per_b3),
            pl.BlockSpec((tq, LANES), qmap),
            pl.BlockSpec((tq, NSA_W), qmap),
        ],
        out_specs=pl.BlockSpec((tq, NSA_W), qmap),
        out_shape=jax.ShapeDtypeStruct((batch * seq, NSA_W), F32),
        scratch_shapes=[
            pltpu.VMEM((2, H_NSA, 8, tq), F32),
            pltpu.VMEM((2, H_NSA, HEAD_DIM + 16, tq), F32),
            pltpu.VMEM((tq // LANES, nb, LANES), F32),
            pltpu.VMEM((nsb, tq), F32),
            pltpu.VMEM((G_NSA, nsb, 8, tq), F32),
            pltpu.VMEM((HPG, kt, tq), F32),
        ],
        compiler_params=_params(("parallel", "arbitrary")),
        name="nsa_prompt",
    )(qn, ck, cvt, ksb, vst, kwb, vwt, bg, gate)


def _layer1_prompt(x2d, batch, seq, norm_g, w_in_bf, qn_g, cmp_kn_g, sel_kn_g, win_kn_g, pe128, wk_bd, wv_bd,
                   w_out_bf, tm):
    (qn, gate, bg, kc_t, vc_t, ks_t, vs_t, kw_t, vw_t, kc, vc, ksb, kwb, vst, vwt) = _inproj1(
        x2d, norm_g, w_in_bf, _pair_gain(qn_g), _pair_gain(sel_kn_g), _pair_gain(win_kn_g), tm, seq, True)
    ck, cvt = _compress_prompt(kc, vc, pe128, wk_bd, wv_bd, _pair_gain(cmp_kn_g), batch, seq)
    mix = _nsa_prompt(qn, ck, cvt, ksb, vst, kwb, vwt, bg, gate, batch, seq)
    y = _outproj(x2d, [mix], [w_out_bf], tm)
    return y, kc_t, vc_t, ks_t, vs_t, kw_t, vw_t


def _ret_sample_kernel(ret_ref, s_ref, qdec_ref, cdec_ref, gn_ref, mix_ref, so_ref):
    width = H_RET * HEAD_DIM
    rowi = lax.broadcasted_iota(I32, (8, HEAD_DIM), 0)
    eye = (lax.broadcasted_iota(I32, (HEAD_DIM, HEAD_DIM), 0)
           == lax.broadcasted_iota(I32, (HEAD_DIM, HEAD_DIM), 1)).astype(BF16)
    row = ret_ref[0]
    for h in range(H_RET):
        sl = slice(h * HEAD_DIM, (h + 1) * HEAD_DIM)
        qb = row[:, sl].astype(BF16)
        kb = row[:, width + h * HEAD_DIM:width + (h + 1) * HEAD_DIM].astype(BF16)
        vb = row[:, 2 * width + h * HEAD_DIM:2 * width + (h + 1) * HEAD_DIM].astype(BF16)
        gate = row[:, 3 * width + h * HEAD_DIM:3 * width + (h + 1) * HEAD_DIM]
        s = s_ref[0, h]
        qs = _dot(jnp.broadcast_to(qb.astype(F32), (8, HEAD_DIM)).astype(BF16), s.astype(BF16))[0:1]
        att = jnp.sum(qb.astype(F32) * kb.astype(F32), axis=-1, keepdims=True)
        o = att.astype(BF16).astype(F32) * vb.astype(F32) + qs * qdec_ref[h:h + 1, :]
        k8 = jnp.where(rowi == 0, jnp.broadcast_to(kb.astype(F32), (8, HEAD_DIM)), 0.0).astype(BF16)
        kcol = _dot_nt(eye, k8)[:, 0:1]
        so_ref[0, h] = s * cdec_ref[h:h + 1, :] + kcol * vb.astype(F32)
        mu = jnp.mean(o, axis=-1, keepdims=True)
        oc = o - mu
        var = jnp.mean(oc * oc, axis=-1, keepdims=True)
        on = oc * lax.rsqrt(var + EPS) * gn_ref[h:h + 1, :]
        mix_ref[0, :, sl] = on * _silu(gate)


def _ret_sample(ret, state, gn_g):
    nb, cols = ret.shape
    width = H_RET * HEAD_DIM
    h = jnp.arange(H_RET, dtype=F32)
    lg = jnp.log1p(-jnp.exp2(-5.0 - h))
    qdec = jnp.broadcast_to(jnp.exp(lg * 1.0)[:, None], (H_RET, HEAD_DIM))
    cdec = jnp.broadcast_to(jnp.exp(lg * 1.0)[:, None], (H_RET, HEAD_DIM))
    m3 = lambda b: (b, 0, 0)
    tab = pl.BlockSpec((H_RET, HEAD_DIM), lambda b: (0, 0))
    st = pl.BlockSpec((1, H_RET, HEAD_DIM, HEAD_DIM), lambda b: (b, 0, 0, 0))
    mix, state_new = pl.pallas_call(
        _ret_sample_kernel,
        grid=(nb,),
        in_specs=[pl.BlockSpec((1, 1, cols), m3), st, tab, tab, tab],
        out_specs=[pl.BlockSpec((1, 1, width), m3), st],
        out_shape=[jax.ShapeDtypeStruct((nb, 1, width), F32),
                   jax.ShapeDtypeStruct(state.shape, F32)],
        compiler_params=_params(("parallel",)),
        name="retention_sample",
    )(ret.reshape(nb, 1, cols), state, qdec, cdec, gn_g.astype(F32))
    return mix.reshape(nb, width), state_new


def _sb_sample_kernel(pt_ref, q_ref, g_ref, k_hbm, v_hbm, o_ref, kbuf, vbuf, sem):
    b = pl.program_id(0)
    n_pages = pt_ref.shape[1]
    page = PAGE_SIZE

    def copies(t, slot):
        pid = pt_ref[b, n_pages - 1 - t]
        return (pltpu.make_async_copy(k_hbm.at[pid], kbuf.at[slot], sem.at[0, slot]),
                pltpu.make_async_copy(v_hbm.at[pid], vbuf.at[slot], sem.at[1, slot]))

    def start(t, slot):
        for c in copies(t, slot):
            c.start()

    def wait(t, slot):
        for c in copies(t, slot):
            c.wait()

    q = q_ref[0].astype(F32)
    rowq = lax.broadcasted_iota(I32, (H_SB, HEAD_DIM), 0)
    rowa = lax.broadcasted_iota(I32, (H_SB, page), 0)
    upper = (lax.broadcasted_iota(I32, (page, page), 0) >= lax.broadcasted_iota(I32, (page, page), 1)).astype(BF16)
    q_rows = [jnp.where(rowq == h, q, 0.0).astype(BF16) for h in range(H_SB)]

    start(0, 0)

    def cond(carry):
        t, alive = carry[0], carry[1]
        return (t < n_pages) & alive

    def body(carry):
        t, _, r, acc = carry
        slot = t % 2
        wait(t, slot)

        @pl.when(t + 1 < n_pages)
        def _():
            start(t + 1, 1 - slot)

        z = jnp.zeros((H_SB, page), F32)
        for h in range(H_SB):
            z = z + _dot(q_rows[h], kbuf[slot, h].astype(BF16))
        ls_pos = jnp.minimum(z, 0.0) - jnp.log1p(jnp.exp(-jnp.abs(z)))
        lm = ls_pos - z
        hi, lo_part = _split_bf16(lm)
        s_incl = _dot(hi, upper) + _dot(lo_part, upper)
        a = jnp.exp(ls_pos + (s_incl - lm) + r)
        for h in range(H_SB):
            acc = acc + _dot_nt(jnp.where(rowa == h, a, 0.0).astype(BF16), vbuf[slot, h].astype(BF16))
        r = r + s_incl[:, 0:1]
        return t + 1, jnp.max(r) > SB_LOG_ZERO, r, acc

    init = (jnp.int32(0), jnp.bool_(True), jnp.zeros((H_SB, 1), F32), jnp.zeros((H_SB, HEAD_DIM), F32))
    t_end, _, _, acc = lax.while_loop(cond, body, init)

    @pl.when(t_end < n_pages)
    def _():
        wait(t_end, t_end % 2)

    o_ref[0] = acc * g_ref[0]


def _sb_sample(q8, g8, cache_kt, cache_vt, page_table):
    nb = q8.shape[0]
    head = lambda b, pt: (b, 0, 0)
    grid_spec = pltpu.PrefetchScalarGridSpec(
        num_scalar_prefetch=1,
        grid=(nb,),
        in_specs=[pl.BlockSpec((1, H_SB, HEAD_DIM), head), pl.BlockSpec((1, H_SB, HEAD_DIM), head),
                  pl.BlockSpec(memory_space=pl.ANY), pl.BlockSpec(memory_space=pl.ANY)],
        out_specs=pl.BlockSpec((1, H_SB, HEAD_DIM), head),
        scratch_shapes=[pltpu.VMEM((2, H_SB, HEAD_DIM, PAGE_SIZE), F32),
                        pltpu.VMEM((2, H_SB, HEAD_DIM, PAGE_SIZE), F32),
                        pltpu.SemaphoreType.DMA((2, 2))],
    )
    return pl.pallas_call(
        _sb_sample_kernel,
        grid_spec=grid_spec,
        out_shape=jax.ShapeDtypeStruct((nb, H_SB, HEAD_DIM), F32),
        compiler_params=_params(("arbitrary",)),
        name="sb_sample",
    )(page_table, q8, g8, cache_kt, cache_vt)


CMP_PAGES_PER_STEP = 16


def _compress_sample_kernel(pt_ref, *refs, n_pp):
    k_refs = refs[:n_pp]
    v_refs = refs[n_pp:2 * n_pp]
    pe_ref, wk_ref, wv_ref, kg_ref, ck_ref, cv_ref, kbuf, vbuf = refs[2 * n_pp:]
    j = pl.program_id(1)
    for r in range(n_pp):
        kbuf[r * PAGE_SIZE:(r + 1) * PAGE_SIZE, :] = k_refs[r][0].T
        vbuf[r * PAGE_SIZE:(r + 1) * PAGE_SIZE, :] = v_refs[r][0].T
    nblk = n_pp * PAGE_SIZE // CMP_BLOCK
    lo = _lane_lo((1, LANES))
    start = pl.multiple_of(j * nblk, nblk)
    ck_ref[0, pl.ds(start, nblk), :] = _pair_rmsnorm(_compress_rows(kbuf, pe_ref, wk_ref, 0, nblk), kg_ref[...], lo)
    cv_ref[0, pl.ds(start, nblk), :] = _compress_rows(vbuf, pe_ref, wv_ref, 0, nblk)


def _compress_sample(cache_k, cache_v, page_table, pe128, wk_bd, wv_bd, kg):
    nb, n_pages = page_table.shape
    n_pp = CMP_PAGES_PER_STEP
    nblk_total = n_pages * PAGE_SIZE // CMP_BLOCK

    def page_map(r):
        return lambda b, j, pt: (pt[b, j * n_pp + r], 0, 0)

    fixed2 = lambda b, j, pt: (0, 0)
    fixed3 = lambda b, j, pt: (0, 0, 0)
    out_map = lambda b, j, pt: (b, 0, 0)
    grid_spec = pltpu.PrefetchScalarGridSpec(
        num_scalar_prefetch=1,
        grid=(nb, n_pages // n_pp),
        in_specs=[pl.BlockSpec((1, PAGE_SIZE, LANES), page_map(r)) for r in range(n_pp)] * 2
        + [pl.BlockSpec((CMP_BLOCK, LANES), fixed2),
           pl.BlockSpec((CMP_BLOCK, LANES, LANES), fixed3),
           pl.BlockSpec((CMP_BLOCK, LANES, LANES), fixed3),
           pl.BlockSpec((1, LANES), fixed2)],
        out_specs=[pl.BlockSpec((1, nblk_total, LANES), out_map)] * 2,
        scratch_shapes=[pltpu.VMEM((n_pp * PAGE_SIZE, LANES), F32)] * 2,
    )
    o = jax.ShapeDtypeStruct((nb, nblk_total, LANES), F32)
    return pl.pallas_call(
        functools.partial(_compress_sample_kernel, n_pp=n_pp),
        grid_spec=grid_spec,
        out_shape=[o, o],
        compiler_params=_params(("parallel", "arbitrary")),
        name="compress_sample",
    )(page_table, *([cache_k] * n_pp), *([cache_v] * n_pp), pe128, wk_bd, wv_bd, kg)


def _pages_t(cache):
    return cache.transpose(0, 2, 3, 1)


def _group_q(q, g):
    z = jnp.zeros_like(q)
    return jnp.concatenate([q, z], axis=1) if g == 0 else jnp.concatenate([z, q], axis=1)


def _nsa_sample_cmp_kernel(q_ref, ck_ref, cv_ref, ocmp_ref, score_ref, *, q_pos):
    q = q_ref[0]
    ck = ck_ref[0].astype(BF16)
    cv = cv_ref[0].astype(BF16)
    nb = ck.shape[0]
    blk_n = lax.broadcasted_iota(I32, (1, nb), 1)
    cmask = blk_n * CMP_BLOCK + (CMP_BLOCK - 1) <= q_pos
    sblk = blk_n // (SEL_BLOCK // CMP_BLOCK)
    cur = q_pos // SEL_BLOCK
    row8 = lax.broadcasted_iota(I32, (8, nb), 0)
    neg_inf = float("-inf")
    scores = jnp.full((8, nb), neg_inf, F32)
    for g in range(G_NSA):
        qg = _group_q(q[g * HPG:(g + 1) * HPG], g)
        s = jnp.where(cmask, _dot_nt(qg, ck), NEG_INF)
        m = jnp.max(s, axis=-1, keepdims=True)
        e = jnp.where(cmask, jnp.exp2(s - m), 0.0)
        den = jnp.sum(e, axis=-1, keepdims=True)
        pr = e / jnp.where(den > 0.0, den, 1.0)
        o = _dot(pr.astype(BF16), cv)
        ocmp_ref[0, g * HPG:(g + 1) * HPG, :] = o[:, g * HEAD_DIM:(g + 1) * HEAD_DIM]
        imp = jnp.sum(pr, axis=0, keepdims=True)
        even = (blk_n & 1) == 0
        pooled = imp + jnp.where(even, pltpu.roll(imp, nb - 1, 1), pltpu.roll(imp, 1, 1))
        forced = (sblk == 0) | (sblk == cur) | (sblk == cur - 1)
        score = jnp.where(sblk < cur, pooled + FORCE_BONUS * forced.astype(F32), neg_inf)
        scores = jnp.where(row8 == g, jnp.broadcast_to(score, (8, nb)), scores)
    score_ref[0] = scores


def _nsa_sample_topk_kernel(score_ref, idx_ref):
    score = score_ref[...]
    rows, nb = score.shape
    ratio = SEL_BLOCK // CMP_BLOCK
    lane = lax.broadcasted_iota(I32, (rows, nb), 1)
    lane_f = lane.astype(F32)
    sblk = lane // ratio
    slot = lax.broadcasted_iota(I32, (rows, LANES), 1)
    neg_inf = float("-inf")
    idx = jnp.full((rows, LANES), -1, I32)
    for t in range(TOP_N - 1):
        mx = jnp.max(score, axis=-1, keepdims=True)
        first = jnp.min(jnp.where(score == mx, lane_f, float(4 * nb)), axis=-1, keepdims=True)
        blk = first.astype(I32) // ratio
        idx = jnp.where((slot == t) & (mx > neg_inf), blk, idx)
        score = jnp.where(sblk == blk, neg_inf, score)
    idx_ref[...] = idx


def _nsa_sample_select(q16, ck, cv, q_pos):
    nb = q16.shape[0]
    nblk = ck.shape[1]
    m3 = lambda b: (b, 0, 0)
    ocmp, scores = pl.pallas_call(
        functools.partial(_nsa_sample_cmp_kernel, q_pos=q_pos),
        grid=(nb,),
        in_specs=[pl.BlockSpec((1, H_NSA, HEAD_DIM), m3),
                  pl.BlockSpec((1, nblk, LANES), m3),
                  pl.BlockSpec((1, nblk, LANES), m3)],
        out_specs=[pl.BlockSpec((1, H_NSA, HEAD_DIM), m3), pl.BlockSpec((1, 8, nblk), m3)],
        out_shape=[jax.ShapeDtypeStruct((nb, H_NSA, HEAD_DIM), F32),
                   jax.ShapeDtypeStruct((nb, 8, nblk), F32)],
        compiler_params=_params(("parallel",)),
        name="nsa_sample_cmp",
    )(q16, ck, cv)
    rows = scores[:, :G_NSA].reshape(nb * G_NSA, nblk)
    idx = pl.pallas_call(
        _nsa_sample_topk_kernel,
        out_shape=jax.ShapeDtypeStruct((nb * G_NSA, LANES), I32),
        compiler_params=pltpu.CompilerParams(vmem_limit_bytes=VMEM_LIMIT),
        name="nsa_sample_topk",
    )(rows)
    return ocmp, idx


def _nsa_sample_attend_kernel(pt_ref, idx_ref, q_ref, k_hbm, v_hbm, kn_ref, vn_ref, kwn_ref, vwn_ref,
                              wk_ref, wv_ref, ocmp_ref, bg_ref, gate_ref, o_ref, kbuf, vbuf, sem,
                              *, q_pos):
    b = pl.program_id(0)
    n_slots = kbuf.shape[1]
    n_pages = pt_ref.shape[1]

    def copies(g, s):
        blk = jnp.maximum(idx_ref[b * G_NSA + g, s], 0)
        pid = pt_ref[b, jnp.minimum(blk // 2, n_pages - 1)]
        rows = pl.ds(g * HEAD_DIM, HEAD_DIM)
        return (pltpu.make_async_copy(k_hbm.at[pid, rows], kbuf.at[g, s], sem.at[0, g]),
                pltpu.make_async_copy(v_hbm.at[pid, rows], vbuf.at[g, s], sem.at[1, g]))

    for g in range(G_NSA):
        for s in range(n_slots):
            for c in copies(g, s):
                c.start()

    q = q_ref[0]
    bg = bg_ref[0]
    col = lax.broadcasted_iota(I32, (HPG, LANES), 1)
    hrow = lax.broadcasted_iota(I32, (HPG, LANES), 0)
    key_half = lax.broadcasted_iota(I32, (1, PAGE_SIZE), 1) // SEL_BLOCK
    wlen = wk_ref.shape[3]
    w_pos = q_pos - wlen + lax.broadcasted_iota(I32, (1, wlen), 1)
    wmask = (w_pos > q_pos - WINDOW) & (w_pos >= 0)

    def new_token(x_ref, g):
        return x_ref[0][:, g * HEAD_DIM:(g + 1) * HEAD_DIM].astype(BF16).astype(F32)

    for g in range(G_NSA):
        qg = q[g * HPG:(g + 1) * HPG]
        qf = qg.astype(F32)
        sw = jnp.where(wmask, _dot(qg, wk_ref[0, g].astype(BF16)), NEG_INF)
        sw_new = jnp.sum(qf * new_token(kwn_ref, g), axis=-1, keepdims=True)
        mw = jnp.maximum(jnp.max(sw, axis=-1, keepdims=True), sw_new)
        pw = jnp.where(wmask, jnp.exp2(sw - mw), 0.0)
        pw_new = jnp.exp2(sw_new - mw)
        o_win = _dot_nt(pw.astype(BF16), wv_ref[0, g].astype(BF16))
        o_win = o_win + pw_new.astype(BF16).astype(F32) * new_token(vwn_ref, g)
        o_win = o_win / (jnp.sum(pw, axis=-1, keepdims=True) + pw_new)
        for s in range(n_slots):
            for c in copies(g, s):
                c.wait()
        scores, masks = [], []
        for s in range(n_slots):
            blk = idx_ref[b * G_NSA + g, s]
            ok = (key_half == jnp.maximum(blk, 0) % 2) & (blk >= 0)
            scores.append(jnp.where(ok, _dot(qg, kbuf[g, s].astype(BF16)), NEG_INF))
            masks.append(ok)
        s_new = jnp.sum(qf * new_token(kn_ref, g), axis=-1, keepdims=True)
        m = s_new
        for sc in scores:
            m = jnp.maximum(m, jnp.max(sc, axis=-1, keepdims=True))
        p_new = jnp.exp2(s_new - m)
        den = p_new
        acc = p_new.astype(BF16).astype(F32) * new_token(vn_ref, g)
        for s in range(n_slots):
            p = jnp.where(masks[s], jnp.exp2(scores[s] - m), 0.0)
            den = den + jnp.sum(p, axis=-1, keepdims=True)
            acc = acc + _dot_nt(p.astype(BF16), vbuf[g, s].astype(BF16))
        o_sel = acc / den
        gates = []
        for br in range(3):
            pick = col == g * HPG * 3 + hrow * 3 + br
            gates.append(jnp.sum(jnp.where(pick, jnp.broadcast_to(bg, (HPG, LANES)), 0.0), axis=-1, keepdims=True))
        rows = slice(g * HPG, (g + 1) * HPG)
        mixed = gates[0] * ocmp_ref[0, rows, :] + gates[1] * o_sel + gates[2] * o_win
        o_ref[0, rows, :] = mixed * gate_ref[0, rows, :]


def _nsa_sample_attend(page_table, idx, q16, sel_kt, sel_vt, ks_new, vs_new, kw_new, vw_new, win_kt, win_vt, ocmp,
                       bg, gate16, q_pos):
    nb = q16.shape[0]
    n_slots = TOP_N - 1
    wlen = win_kt.shape[3]
    m3 = lambda b, pt, ix: (b, 0, 0)
    m4 = lambda b, pt, ix: (b, 0, 0, 0)
    row = pl.BlockSpec((1, 1, LANES), m3)
    heads = pl.BlockSpec((1, H_NSA, HEAD_DIM), m3)
    win = pl.BlockSpec((1, G_NSA, HEAD_DIM, wlen), m4)
    grid_spec = pltpu.PrefetchScalarGridSpec(
        num_scalar_prefetch=2,
        grid=(nb,),
        in_specs=[heads, pl.BlockSpec(memory_space=pl.ANY), pl.BlockSpec(memory_space=pl.ANY),
                  row, row, row, row, win, win, heads, row, heads],
        out_specs=heads,
        scratch_shapes=[pltpu.VMEM((G_NSA, n_slots, HEAD_DIM, PAGE_SIZE), F32),
                        pltpu.VMEM((G_NSA, n_slots, HEAD_DIM, PAGE_SIZE), F32),
                        pltpu.SemaphoreType.DMA((2, G_NSA))],
    )
    return pl.pallas_call(
        functools.partial(_nsa_sample_attend_kernel, q_pos=q_pos),
        grid_spec=grid_spec,
        out_shape=jax.ShapeDtypeStruct((nb, H_NSA, HEAD_DIM), F32),
        compiler_params=_params(("arbitrary",)),
        name="nsa_sample_attend",
    )(page_table, idx, q16, sel_kt, sel_vt, ks_new, vs_new, kw_new, vw_new, win_kt, win_vt, ocmp, bg, gate16)


def _layer0_sample(x2d, past_len, state, cache_k, cache_v, page_table, norm_g, w_in_bf, ret_gn_g, sb_qn_g, sb_kn_g,
                   w_out_bf):
    nb = x2d.shape[0]
    cos, sin = _rope_tables(jnp.full((nb,), past_len, I32))
    ret, sbq, sbg, sbk, sbv = _inproj0(x2d, norm_g, w_in_bf, cos, sin, _pair_gain(sb_qn_g), _pair_gain(sb_kn_g),
                                       nb, nb, False)
    mix_ret, state_new = _ret_sample(ret, state, ret_gn_g)
    mix_sb = _sb_sample(sbq.reshape(nb, H_SB, HEAD_DIM), sbg.reshape(nb, H_SB, HEAD_DIM),
                        _pages_t(cache_k), _pages_t(cache_v), page_table)
    mix_sb = mix_sb.reshape(nb, H_SB * HEAD_DIM)
    half = w_out_bf.shape[0] // 2
    y = _outproj(x2d, [mix_ret, mix_sb], [w_out_bf[:half], w_out_bf[half:]], nb)
    return y, state_new, sbk, sbv


def _layer1_sample(x2d, past_len, cmp_k, cmp_v, sel_k, sel_v, win_k, win_v, page_table, norm_g, w_in_bf, qn_g,
                   cmp_kn_g, sel_kn_g, win_kn_g, pe128, wk_bd, wv_bd, w_out_bf):
    nb = x2d.shape[0]
    (qn, gate, bg, kc, vc, ks, vs, kw, vw) = _inproj1(
        x2d, norm_g, w_in_bf, _pair_gain(qn_g), _pair_gain(sel_kn_g), _pair_gain(win_kn_g), nb, nb, False)
    n_phys = cmp_k.shape[0]
    pool = lambda t: _pages_t(t).reshape(n_phys, KV_W, PAGE_SIZE)
    ck, cv = _compress_sample(pool(cmp_k), pool(cmp_v), page_table, pe128, wk_bd, wv_bd, _pair_gain(cmp_kn_g))
    q16 = qn.reshape(nb, H_NSA, HEAD_DIM)
    ocmp, idx = _nsa_sample_select(q16, ck, cv, past_len)
    idx2 = idx[:, :TOP_N]
    mix = _nsa_sample_attend(page_table, idx2, q16, pool(sel_k), pool(sel_v), ks[:, None, :], vs[:, None, :],
                             kw[:, None, :], vw[:, None, :], _pages_t(win_k), _pages_t(win_v), ocmp,
                             bg[:, None, :], gate.reshape(nb, H_NSA, HEAD_DIM), past_len)
    y = _outproj(x2d, [mix.reshape(nb, NSA_W)], [w_out_bf], nb)
    gd = (nb, 1, G_NSA, HEAD_DIM)
    wk_new = jnp.concatenate([win_k[:, 1:], kw.reshape(gd)], axis=1)
    wv_new = jnp.concatenate([win_v[:, 1:], vw.reshape(gd)], axis=1)
    return y, kc, vc, ks, vs, wk_new, wv_new


def kernel(x_prompt, x_sample, state_ret, cache_sb_k, cache_sb_v, cache_cmp_k, cache_cmp_v, cache_sel_k,
           cache_sel_v, cache_win_k, cache_win_v, page_table, norm0_g, w_in0, ret_gn_g, sb_qn_g, sb_kn_g,
           w_out0, norm1_g, w_in1, nsa_qn_g, cmp_kn_g, sel_kn_g, win_kn_g, cmp_pe, w_cmp_k, w_cmp_v, w_out1):
    batch, seq, d = x_prompt.shape
    nb = x_sample.shape[0]
    past_len = page_table.shape[1] * PAGE_SIZE
    tm = 256
    w_in0_bf = w_in0.astype(BF16)
    w_out0_bf = w_out0.astype(BF16)
    w_in1_bf = _pad_w_in1(w_in1)
    w_out1_bf = w_out1.astype(BF16)
    pe128 = jnp.tile(cmp_pe, (1, G_NSA))
    wk_bd = _blockdiag_w(w_cmp_k)
    wv_bd = _blockdiag_w(w_cmp_v)

    xp = x_prompt.reshape(batch * seq, d)
    y1p, ret_p, sbk_p, sbv_p = _layer0_prompt(xp, batch, seq, norm0_g, w_in0_bf, ret_gn_g, sb_qn_g, sb_kn_g,
                                              w_out0_bf, tm)
    y2p, ckp, cvp, skp, svp, wkp, wvp = _layer1_prompt(y1p, batch, seq, norm1_g, w_in1_bf, nsa_qn_g, cmp_kn_g,
                                                        sel_kn_g, win_kn_g, pe128, wk_bd, wv_bd, w_out1_bf, tm)
    xs = x_sample.reshape(nb, d)
    y1s, ret_s, sbk_s, sbv_s = _layer0_sample(xs, past_len, state_ret, cache_sb_k, cache_sb_v, page_table, norm0_g,
                                              w_in0_bf, ret_gn_g, sb_qn_g, sb_kn_g, w_out0_bf)
    y2s, cks, cvs, sks, svs, wks, wvs = _layer1_sample(y1s, past_len, cache_cmp_k, cache_cmp_v, cache_sel_k,
                                                        cache_sel_v, cache_win_k, cache_win_v, page_table, norm1_g,
                                                        w_in1_bf, nsa_qn_g, cmp_kn_g, sel_kn_g, win_kn_g, pe128,
                                                        wk_bd, wv_bd, w_out1_bf)
    keep = min(WINDOW, seq)
    gd = (nb, 1, G_NSA, HEAD_DIM)

    def rows_major(t):
        return t.transpose(0, 3, 1, 2)

    def groups(t):
        return rows_major(t.reshape(batch, G_NSA, HEAD_DIM, seq))

    return (y2p.reshape(batch, seq, d), y2s.reshape(nb, 1, d), ret_p, ret_s,
            rows_major(sbk_p), rows_major(sbv_p),
            sbk_s.reshape(nb, 1, H_SB, HEAD_DIM), sbv_s.reshape(nb, 1, H_SB, HEAD_DIM),
            groups(ckp), groups(cvp), groups(skp), groups(svp),
            groups(wkp)[:, seq - keep:], groups(wvp)[:, seq - keep:],
            cks.reshape(gd), cvs.reshape(gd), sks.reshape(gd), svs.reshape(gd), wks, wvs)
```

```python
import functools

import jax
import jax.numpy as jnp
import numpy as np
from jax import lax
from jax.experimental import pallas as pl
from jax.experimental.pallas import tpu as pltpu

F32 = jnp.float32
BF16 = jnp.bfloat16
I32 = jnp.int32

HEAD_DIM = 64
LANES = 128
H_RET = 8
H_SB = 8
RET_CHUNK = 128
ROPE_BASE = 10000.0
H_NSA = 16
G_NSA = 2
HPG = H_NSA // G_NSA
CMP_BLOCK = 32
SEL_BLOCK = 64
TOP_N = 16
WINDOW = 512
FORCE_BONUS = 1000.0
NEG_INF = -1e30
EPS = 1e-6
PAGE_SIZE = 128
NSA_KEY_TILE = 128
NSA_Q_TILE = 256
SCALE = HEAD_DIM ** -0.5
LOG2E = 1.4426950408889634
VMEM_LIMIT = 56 * 1024 * 1024


def _dot(a, b):
    return jnp.dot(a, b, preferred_element_type=F32)


def _dot_nt(a, b):
    return lax.dot_general(a, b, (((1,), (1,)), ((), ())), preferred_element_type=F32)


def _dot_tn(a, b):
    return lax.dot_general(a, b, (((0,), (0,)), ((), ())), preferred_element_type=F32)


def _split_bf16(x):
    hi = x.astype(BF16)
    lo = (x - hi.astype(F32)).astype(BF16)
    return hi, lo


def _sigmoid(x):
    return 1.0 / (1.0 + jnp.exp(-x))


def _silu(x):
    return x * _sigmoid(x)


def _lane_lo(shape):
    return lax.broadcasted_iota(I32, shape, len(shape) - 1) < HEAD_DIM


def _pair_sum(x, lo):
    s_lo = jnp.sum(jnp.where(lo, x, 0.0), axis=-1, keepdims=True)
    s_hi = jnp.sum(jnp.where(lo, 0.0, x), axis=-1, keepdims=True)
    return jnp.where(lo, s_lo, s_hi)


def _pair_rmsnorm(x, g, lo):
    ms = _pair_sum(x * x, lo) * (1.0 / HEAD_DIM)
    return x * lax.rsqrt(ms + EPS) * g


def _rope_pair(x, cos, sin_signed):
    lane = lax.broadcasted_iota(I32, x.shape, 1)
    first = (lane & 32) == 0
    partner = jnp.where(first, pltpu.roll(x, 96, 1), pltpu.roll(x, 32, 1))
    return x * cos + partner * sin_signed


def _transpose_tiles(x):
    r, c = x.shape
    if r <= LANES and c <= LANES:
        return x.T
    rows = [jnp.concatenate([x[ri * LANES:(ri + 1) * LANES, cj * LANES:(cj + 1) * LANES].T
                             for ri in range(r // LANES)], axis=1) for cj in range(c // LANES)]
    return jnp.concatenate(rows, axis=0)


def _params(sem):
    return pltpu.CompilerParams(dimension_semantics=sem, vmem_limit_bytes=VMEM_LIMIT)


def _store_head_major(dst_ref, t, pair, r):
    cols = slice(r * LANES, (r + 1) * LANES)
    dst_ref[0, 2 * pair, :, cols] = t[:HEAD_DIM]
    dst_ref[0, 2 * pair + 1, :, cols] = t[HEAD_DIM:]


def _inproj0_kernel(x_ref, g_ref, w_ref, cos_ref, sin_ref, qg_ref, kg_ref,
                    ret_ref, sbq_ref, sbg_ref, *kv_refs, prompt):
    x = x_ref[...]
    xn = x * lax.rsqrt(jnp.mean(x * x, axis=-1, keepdims=True) + EPS) * g_ref[...]
    xb = xn.astype(BF16)
    cos = cos_ref[...]
    sin = sin_ref[...]
    lo = _lane_lo((1, LANES))
    width = H_RET * HEAD_DIM

    def group(i):
        return _dot(xb, w_ref[:, i * width:(i + 1) * width])

    def tiles(a):
        return [a[:, t * LANES:(t + 1) * LANES] for t in range(width // LANES)]

    for t, v in enumerate(tiles(group(0))):
        ret_ref[:, t * LANES:(t + 1) * LANES] = _rope_pair(v, cos, sin)
    for t, v in enumerate(tiles(group(1))):
        ret_ref[:, width + t * LANES:width + (t + 1) * LANES] = _rope_pair(v, cos, sin) * SCALE
    ret_ref[:, 2 * width:3 * width] = group(2)
    ret_ref[:, 3 * width:4 * width] = group(3)
    for t, v in enumerate(tiles(group(4))):
        sbq_ref[:, t * LANES:(t + 1) * LANES] = (_pair_rmsnorm(v, qg_ref[...], lo) * (SCALE * LOG2E)).astype(BF16)
    sk = [_pair_rmsnorm(v, kg_ref[...], lo) for v in tiles(group(5))]
    sv = tiles(group(6))
    if prompt:
        kb_ref, vt_ref, kt_out, vt_out = kv_refs
        for t in range(width // LANES):
            kb_ref[:, t * LANES:(t + 1) * LANES] = sk[t].astype(BF16)
            for r in range(x.shape[0] // LANES):
                rows = slice(r * LANES, (r + 1) * LANES)
                vt = sv[t][rows].T
                _store_head_major(kt_out, sk[t][rows].T, t, r)
                _store_head_major(vt_out, vt, t, r)
                vt_ref[r, t] = vt.astype(BF16)
    else:
        k_ref, v_ref = kv_refs
        for t in range(width // LANES):
            k_ref[:, t * LANES:(t + 1) * LANES] = sk[t]
            v_ref[:, t * LANES:(t + 1) * LANES] = sv[t]
    sbg_ref[...] = _silu(group(7))


def _inproj0(x2d, norm_g, w_bf, cos, sin, qg, kg, tm, seq, prompt):
    rows, d = x2d.shape
    width = H_RET * HEAD_DIM
    n = w_bf.shape[1]
    grid = (rows // tm,)
    per_seq = seq // tm
    row = lambda i: (i, 0)
    fixed = lambda i: (0, 0)
    pos = lambda i: (i % per_seq, 0)
    if prompt:
        npair = width // LANES
        head_major = pl.BlockSpec((1, H_SB, HEAD_DIM, tm), lambda i: (i // per_seq, 0, 0, i % per_seq))
        hm_shape = jax.ShapeDtypeStruct((rows // seq, H_SB, HEAD_DIM, seq), F32)
        extra_specs = [pl.BlockSpec((tm, width), row),
                       pl.BlockSpec((tm // LANES, npair, LANES, LANES), lambda i: (i, 0, 0, 0)),
                       head_major, head_major]
        extra_shapes = [jax.ShapeDtypeStruct((rows, width), BF16),
                        jax.ShapeDtypeStruct((rows // LANES, npair, LANES, LANES), BF16),
                        hm_shape, hm_shape]
    else:
        extra_specs = [pl.BlockSpec((tm, width), row)] * 2
        extra_shapes = [jax.ShapeDtypeStruct((rows, width), F32)] * 2
    return pl.pallas_call(
        functools.partial(_inproj0_kernel, prompt=prompt),
        grid=grid,
        in_specs=[
            pl.BlockSpec((tm, d), row),
            pl.BlockSpec((1, d), fixed),
            pl.BlockSpec((d, n), fixed),
            pl.BlockSpec((tm, LANES), pos),
            pl.BlockSpec((tm, LANES), pos),
            pl.BlockSpec((1, LANES), fixed),
            pl.BlockSpec((1, LANES), fixed),
        ],
        out_specs=[
            pl.BlockSpec((tm, 4 * width), row),
            pl.BlockSpec((tm, width), row),
            pl.BlockSpec((tm, width), row),
        ] + extra_specs,
        out_shape=[
            jax.ShapeDtypeStruct((rows, 4 * width), F32),
            jax.ShapeDtypeStruct((rows, width), BF16),
            jax.ShapeDtypeStruct((rows, width), F32),
        ] + extra_shapes,
        compiler_params=_params(("parallel",)),
        name="inproj0",
    )(x2d, norm_g.reshape(1, d), w_bf, cos, sin, qg, kg)


def _ret_prompt_kernel(ret_ref, inner_ref, qdec_ref, kdec_ref, cdec_ref, gn_ref, mix_ref, state_ref, s_scr):
    c = pl.program_id(1)
    npair = H_RET // 2
    width = H_RET * HEAD_DIM

    @pl.when(c == 0)
    def _():
        s_scr[...] = jnp.zeros_like(s_scr)

    def tile(group, p):
        return ret_ref[:, group * width + p * LANES:group * width + (p + 1) * LANES]

    lo = _lane_lo((ret_ref.shape[0], LANES))
    same_head = ((lax.broadcasted_iota(I32, (LANES, LANES), 0) < HEAD_DIM)
                 == (lax.broadcasted_iota(I32, (LANES, LANES), 1) < HEAD_DIM))
    qs = [tile(0, p).astype(BF16) for p in range(npair)]
    ks = [tile(1, p) for p in range(npair)]
    kbs = [k.astype(BF16) for k in ks]
    vbs = [tile(2, p).astype(BF16) for p in range(npair)]
    states = [s_scr[p] for p in range(npair)]
    zero = jnp.zeros_like(qs[0])
    att_lo = [_dot_nt(jnp.where(lo, qs[p], zero), kbs[p]) for p in range(npair)]
    att_hi = [_dot_nt(jnp.where(lo, zero, qs[p]), kbs[p]) for p in range(npair)]
    cross = [_dot(qs[p], states[p].astype(BF16)) for p in range(npair)]
    kv = [_dot_tn((ks[p] * kdec_ref[p]).astype(BF16), vbs[p]) for p in range(npair)]
    a_lo = [(att_lo[p] * inner_ref[2 * p]).astype(BF16) for p in range(npair)]
    a_hi = [(att_hi[p] * inner_ref[2 * p + 1]).astype(BF16) for p in range(npair)]
    o_lo = [_dot(a_lo[p], vbs[p]) for p in range(npair)]
    o_hi = [_dot(a_hi[p], vbs[p]) for p in range(npair)]
    for p in range(npair):
        s_new = states[p] * cdec_ref[p] + jnp.where(same_head, kv[p], 0.0)
        s_scr[p] = s_new
        state_ref[0, p] = s_new
        o = jnp.where(lo, o_lo[p], o_hi[p]) + cross[p] * qdec_ref[p]
        mu = _pair_sum(o, lo) * (1.0 / HEAD_DIM)
        oc = o - mu
        var = _pair_sum(oc * oc, lo) * (1.0 / HEAD_DIM)
        on = oc * lax.rsqrt(var + EPS) * gn_ref[p]
        mix_ref[:, p * LANES:(p + 1) * LANES] = on * _silu(tile(3, p))


def _ret_tables(chunk):
    h = jnp.arange(H_RET, dtype=F32)
    lg = jnp.log1p(-jnp.exp2(-5.0 - h))
    idx = jnp.arange(chunk, dtype=F32)
    diff = idx[:, None] - idx[None, :]
    inner = jnp.where(diff >= 0, jnp.exp(lg[:, None, None] * jnp.maximum(diff, 0.0)), 0.0)
    q_dec = jnp.exp(lg[:, None] * (idx + 1.0))
    k_dec = jnp.exp(lg[:, None] * (chunk - 1.0 - idx))
    c_dec = jnp.exp(lg * chunk)

    def lanes(t):
        t = jnp.repeat(t[:, :, None], HEAD_DIM, axis=2)
        return jnp.concatenate([t[0::2], t[1::2]], axis=2)

    cd = jnp.repeat(c_dec[:, None, None], HEAD_DIM, axis=2)
    cd = jnp.concatenate([cd[0::2], cd[1::2]], axis=2)
    return inner, lanes(q_dec), lanes(k_dec), cd


def _ret_prompt(ret, gn_g, batch, seq):
    width = H_RET * HEAD_DIM
    npair = H_RET // 2
    c = RET_CHUNK
    nc = seq // c
    inner, qdec, kdec, cdec = _ret_tables(c)
    gn = gn_g.reshape(npair, 1, LANES)
    tab = lambda b, i: (0, 0, 0)
    return pl.pallas_call(
        _ret_prompt_kernel,
        grid=(batch, nc),
        in_specs=[
            pl.BlockSpec((c, 4 * width), lambda b, i: (b * nc + i, 0)),
            pl.BlockSpec((H_RET, c, c), tab),
            pl.BlockSpec((npair, c, LANES), tab),
            pl.BlockSpec((npair, c, LANES), tab),
            pl.BlockSpec((npair, 1, LANES), tab),
            pl.BlockSpec((npair, 1, LANES), tab),
        ],
        out_specs=[
            pl.BlockSpec((c, width), lambda b, i: (b * nc + i, 0)),
            pl.BlockSpec((1, npair, LANES, LANES), lambda b, i: (b, 0, 0, 0)),
        ],
        out_shape=[
            jax.ShapeDtypeStruct((batch * seq, width), F32),
            jax.ShapeDtypeStruct((batch, npair, LANES, LANES), F32),
        ],
        scratch_shapes=[pltpu.VMEM((npair, LANES, LANES), F32)],
        compiler_params=_params(("parallel", "arbitrary")),
        name="retention_prompt",
    )(ret, inner, qdec, kdec, cdec, gn)


def _unpair_state(sp):
    a = sp[:, :, :HEAD_DIM, :HEAD_DIM]
    b = sp[:, :, HEAD_DIM:, HEAD_DIM:]
    return jnp.stack([a, b], axis=2).reshape(sp.shape[0], -1, HEAD_DIM, HEAD_DIM)


SB_LOG_ZERO = -151.0


def _log2_sigmoid(z2):
    return jnp.minimum(z2, 0.0) - jnp.log2(1.0 + jnp.exp2(-jnp.abs(z2)))
SB_Q_TILE = 256
SB_KEY_TILE = 128


def _sb_prompt_kernel(q_ref, k_ref, vt_ref, g_ref, o_ref, acc_scr):
    i = pl.program_id(1)
    tq = q_ref.shape[0]
    kt = SB_KEY_TILE
    npair = H_SB // 2
    lower = (lax.broadcasted_iota(I32, (kt, kt), 1)
             >= lax.broadcasted_iota(I32, (kt, kt), 0)).astype(BF16)
    k_in = lax.broadcasted_iota(I32, (kt, tq), 0)
    q_pos = i * tq + lax.broadcasted_iota(I32, (kt, tq), 1)
    zeros = jnp.zeros((HEAD_DIM, tq), F32)
    qts = []
    for p in range(npair):
        t = _transpose_tiles(q_ref[:, p * LANES:(p + 1) * LANES].astype(F32))
        qts.append(jnp.concatenate([t[:HEAD_DIM], zeros], axis=0).astype(BF16))
        qts.append(jnp.concatenate([zeros, t[HEAD_DIM:]], axis=0).astype(BF16))
    acc_scr[...] = jnp.zeros_like(acc_scr)

    def sweep(j, rs, masked):
        start = pl.multiple_of(j * kt, kt)
        diag_mask = (j * kt + k_in < q_pos) if masked else None
        kjs = [k_ref[pl.ds(start, kt), p * LANES:(p + 1) * LANES] for p in range(npair)]
        vts = [vt_ref[j, p] for p in range(npair)]
        zs = [_dot(kjs[h // 2], qt) for h, qt in enumerate(qts)]
        ls_pos, lms, parts = [], [], []
        for z in zs:
            lsp = _log2_sigmoid(z)
            lm = lsp - z
            if diag_mask is not None:
                lm = jnp.where(diag_mask, lm, 0.0)
            ls_pos.append(lsp)
            lms.append(lm)
            parts.append(_split_bf16(lm))
        s_incl = [_dot(lower, hi) + _dot(lower, lo_part) for hi, lo_part in parts]
        probs = []
        for h in range(H_SB):
            a = jnp.exp2(ls_pos[h] + (s_incl[h] - lms[h]) + rs[h])
            if diag_mask is not None:
                a = jnp.where(diag_mask, a, 0.0)
            probs.append(a.astype(BF16))
        for h in range(H_SB):
            half = h % 2
            acc_scr[h] = acc_scr[h] + _dot(vts[h // 2][half * HEAD_DIM:(half + 1) * HEAD_DIM, :], probs[h])
        return [rs[h] + s_incl[h][0:1, :] for h in range(H_SB)]

    def alive_of(rs):
        worst = rs[0]
        for r in rs[1:]:
            worst = jnp.maximum(worst, r)
        return jnp.max(worst) > SB_LOG_ZERO

    n_diag = tq // kt
    j_top = (i + 1) * n_diag - 1
    rs = [jnp.zeros((1, tq), F32)] * H_SB
    for d in range(n_diag):
        rs = sweep(j_top - d, rs, True)
    n_old = i * n_diag

    def cond(carry):
        t, alive = carry[0], carry[1]
        return (t < n_old) & alive

    def body(carry):
        t = carry[0]
        rs = sweep(n_old - 1 - t, list(carry[2:]), False)
        return (t + 1, alive_of(rs)) + tuple(rs)

    lax.while_loop(cond, body, (jnp.int32(0), alive_of(rs)) + tuple(rs))
    for p in range(npair):
        both = _transpose_tiles(jnp.concatenate([acc_scr[2 * p], acc_scr[2 * p + 1]], axis=0))
        o_ref[:, p * LANES:(p + 1) * LANES] = both * g_ref[:, p * LANES:(p + 1) * LANES]


def _sb_prompt(sbq, sbk_b, sbvt, sbg, batch, seq):
    width = H_SB * HEAD_DIM
    npair = H_SB // 2
    tq = SB_Q_TILE
    kt = SB_KEY_TILE
    nq = seq // tq
    qmap = lambda b, i: (b * nq + i, 0)
    return pl.pallas_call(
        _sb_prompt_kernel,
        grid=(batch, nq),
        in_specs=[
            pl.BlockSpec((tq, width), qmap),
            pl.BlockSpec((seq, width), lambda b, i: (b, 0)),
            pl.BlockSpec((seq // kt, npair, LANES, kt), lambda b, i: (b, 0, 0, 0)),
            pl.BlockSpec((tq, width), qmap),
        ],
        out_specs=pl.BlockSpec((tq, width), qmap),
        out_shape=jax.ShapeDtypeStruct((batch * seq, width), F32),
        scratch_shapes=[pltpu.VMEM((H_SB, HEAD_DIM, tq), F32)],
        compiler_params=_params(("parallel", "arbitrary")),
        name="sb_prompt",
    )(sbq, sbk_b, sbvt, sbg)


def _outproj_kernel(*refs, n_mix):
    x_ref = refs[0]
    mix_refs = refs[1:1 + n_mix]
    w_refs = refs[1 + n_mix:1 + 2 * n_mix]
    y_ref = refs[1 + 2 * n_mix]
    acc = x_ref[...]
    for m_ref, w_ref in zip(mix_refs, w_refs):
        acc = acc + _dot(m_ref[...].astype(BF16), w_ref[...])
    y_ref[...] = acc


def _outproj(x2d, mixes, ws, tm):
    rows, d = x2d.shape
    n_mix = len(mixes)
    row = lambda i: (i, 0)
    fixed = lambda i: (0, 0)
    in_specs = [pl.BlockSpec((tm, d), row)]
    in_specs += [pl.BlockSpec((tm, m.shape[1]), row) for m in mixes]
    in_specs += [pl.BlockSpec(w.shape, fixed) for w in ws]
    return pl.pallas_call(
        functools.partial(_outproj_kernel, n_mix=n_mix),
        grid=(rows // tm,),
        in_specs=in_specs,
        out_specs=pl.BlockSpec((tm, d), row),
        out_shape=jax.ShapeDtypeStruct((rows, d), F32),
        compiler_params=_params(("parallel",)),
        name="outproj",
    )(x2d, *mixes, *ws)


def _rope_tables(pos):
    half = HEAD_DIM // 2
    inv = ROPE_BASE ** (-jnp.arange(half, dtype=F32) / half)
    ang = pos.astype(F32)[:, None] * inv[None, :]
    cos, sin = jnp.cos(ang), jnp.sin(ang)
    cos128 = jnp.tile(cos, (1, LANES // half))
    sin128 = jnp.tile(jnp.concatenate([-sin, sin], axis=1), (1, LANES // HEAD_DIM))
    return cos128, sin128


def _pair_gain(g):
    return jnp.tile(g.astype(F32), LANES // HEAD_DIM).reshape(1, LANES)


def _layer0_prompt(x2d, batch, seq, norm_g, w_in_bf, ret_gn_g, sb_qn_g, sb_kn_g, w_out_bf, tm):
    cos, sin = _rope_tables(jnp.arange(seq, dtype=I32))
    ret, sbq, sbg, sbk_b, sbvt, sbk_hm, sbv_hm = _inproj0(x2d, norm_g, w_in_bf, cos, sin, _pair_gain(sb_qn_g),
                                                          _pair_gain(sb_kn_g), tm, seq, True)
    mix_ret, state_pairs = _ret_prompt(ret, ret_gn_g, batch, seq)
    mix_sb = _sb_prompt(sbq, sbk_b, sbvt, sbg, batch, seq)
    half = w_out_bf.shape[0] // 2
    y = _outproj(x2d, [mix_ret, mix_sb], [w_out_bf[:half], w_out_bf[half:]], tm)
    return y, _unpair_state(state_pairs), sbk_hm, sbv_hm


NSA_W = H_NSA * HEAD_DIM
KV_W = G_NSA * HEAD_DIM
IN1_COLS = 2 * NSA_W + 6 * KV_W + LANES


def _inproj1_kernel(x_ref, g_ref, w_ref, qg_ref, skg_ref, wkg_ref, q_ref, gate_ref, bg_ref, *kv_refs, prompt):
    x = x_ref[...]
    xn = x * lax.rsqrt(jnp.mean(x * x, axis=-1, keepdims=True) + EPS) * g_ref[...]
    xb = xn.astype(BF16)
    lo = _lane_lo((1, LANES))
    a = _dot(xb, w_ref[:, 0:NSA_W])
    for t in range(NSA_W // LANES):
        v = a[:, t * LANES:(t + 1) * LANES]
        q_ref[:, t * LANES:(t + 1) * LANES] = (_pair_rmsnorm(v, qg_ref[...], lo) * (SCALE * LOG2E)).astype(BF16)
    a = _dot(xb, w_ref[:, NSA_W:NSA_W + 6 * KV_W])
    kc, vc, ks, vs, kw, vw = [a[:, t * KV_W:(t + 1) * KV_W] for t in range(6)]
    ks = _pair_rmsnorm(ks, skg_ref[...], lo)
    kw = _pair_rmsnorm(kw, wkg_ref[...], lo)
    six = (kc, vc, ks, vs, kw, vw)
    if prompt:
        t_out = kv_refs[:6]
        kc_ref, vc_ref, ksb_ref, kwb_ref, vst_ref, vwt_ref = kv_refs[6:]
        kc_ref[...] = kc
        vc_ref[...] = vc
        ksb_ref[...] = ks.astype(BF16)
        kwb_ref[...] = kw.astype(BF16)
        per_tile = NSA_KEY_TILE // LANES
        for r in range(x.shape[0] // LANES):
            rows = slice(r * LANES, (r + 1) * LANES)
            cols = slice((r % per_tile) * LANES, (r % per_tile + 1) * LANES)
            for src, dst in zip(six, t_out):
                t = src[rows].T
                dst[0, :, rows] = t
                if src is vs:
                    vst_ref[r // per_tile, :, cols] = t.astype(BF16)
                if src is vw:
                    vwt_ref[r // per_tile, :, cols] = t.astype(BF16)
    else:
        for src, dst in zip(six, kv_refs):
            dst[...] = src
    off = NSA_W + 6 * KV_W
    gate_ref[...] = _silu(_dot(xb, w_ref[:, off:off + NSA_W]))
    bg_ref[...] = _sigmoid(_dot(xb, w_ref[:, off + NSA_W:off + NSA_W + LANES]))


def _inproj1(x2d, norm_g, w_bf, qg, skg, wkg, tm, seq, prompt):
    rows, d = x2d.shape
    row = lambda i: (i, 0)
    fixed = lambda i: (0, 0)
    out_specs = [pl.BlockSpec((tm, NSA_W), row), pl.BlockSpec((tm, NSA_W), row), pl.BlockSpec((tm, LANES), row)]
    out_shape = [jax.ShapeDtypeStruct((rows, NSA_W), BF16), jax.ShapeDtypeStruct((rows, NSA_W), F32),
                 jax.ShapeDtypeStruct((rows, LANES), F32)]
    if prompt:
        per_seq = seq // tm
        kt = NSA_KEY_TILE
        out_specs += ([pl.BlockSpec((1, KV_W, tm), lambda i: (i // per_seq, 0, i % per_seq))] * 6
                      + [pl.BlockSpec((tm, KV_W), row)] * 4
                      + [pl.BlockSpec((tm // kt, LANES, kt), lambda i: (i, 0, 0))] * 2)
        out_shape += ([jax.ShapeDtypeStruct((rows // seq, KV_W, seq), F32)] * 6
                      + [jax.ShapeDtypeStruct((rows, KV_W), F32)] * 2
                      + [jax.ShapeDtypeStruct((rows, KV_W), BF16)] * 2
                      + [jax.ShapeDtypeStruct((rows // kt, LANES, kt), BF16)] * 2)
    else:
        out_specs += [pl.BlockSpec((tm, KV_W), row)] * 6
        out_shape += [jax.ShapeDtypeStruct((rows, KV_W), F32)] * 6
    return pl.pallas_call(
        functools.partial(_inproj1_kernel, prompt=prompt),
        grid=(rows // tm,),
        in_specs=[
            pl.BlockSpec((tm, d), row),
            pl.BlockSpec((1, d), fixed),
            pl.BlockSpec((d, IN1_COLS), fixed),
            pl.BlockSpec((1, LANES), fixed),
            pl.BlockSpec((1, LANES), fixed),
            pl.BlockSpec((1, LANES), fixed),
        ],
        out_specs=out_specs,
        out_shape=out_shape,
        compiler_params=_params(("parallel",)),
        name="inproj1",
    )(x2d, norm_g.reshape(1, d), w_bf, qg, skg, wkg)


def _pad_w_in1(w_in1):
    pad = IN1_COLS - w_in1.shape[1]
    return jnp.pad(w_in1, ((0, 0), (0, pad))).astype(BF16)


def _compress_rows(t_ref, pe_ref, w_ref, row0, nblk):
    acc = jnp.zeros((nblk, LANES), F32)
    for l in range(CMP_BLOCK):
        rows = t_ref[pl.ds(row0 + l, nblk, stride=CMP_BLOCK), :]
        acc = acc + _dot((rows + pe_ref[l:l + 1, :]).astype(BF16), w_ref[l])
    return acc


def _compress_prompt_kernel(kc_ref, vc_ref, pe_ref, wk_ref, wv_ref, kg_ref, ck_ref, cvt_ref):
    nblk = ck_ref.shape[0]
    lo = _lane_lo((1, LANES))
    ck = _pair_rmsnorm(_compress_rows(kc_ref, pe_ref, wk_ref, 0, nblk), kg_ref[...], lo)
    cv = _compress_rows(vc_ref, pe_ref, wv_ref, 0, nblk)
    ck_ref[...] = ck.astype(BF16)
    cvt_ref[0] = cv.T.astype(BF16)


def _blockdiag_w(w):
    z = jnp.zeros_like(w)
    top = jnp.concatenate([w, z], axis=2)
    bot = jnp.concatenate([z, w], axis=2)
    return jnp.concatenate([top, bot], axis=1).astype(BF16)


def _compress_prompt(kc, vc, pe128, wk_bd, wv_bd, kg, batch, seq):
    nblk = seq // CMP_BLOCK
    rowb = lambda b: (b, 0)
    fixed2 = lambda b: (0, 0)
    fixed3 = lambda b: (0, 0, 0)
    return pl.pallas_call(
        _compress_prompt_kernel,
        grid=(batch,),
        in_specs=[
            pl.BlockSpec((seq, LANES), rowb),
            pl.BlockSpec((seq, LANES), rowb),
            pl.BlockSpec((CMP_BLOCK, LANES), fixed2),
            pl.BlockSpec((CMP_BLOCK, LANES, LANES), fixed3),
            pl.BlockSpec((CMP_BLOCK, LANES, LANES), fixed3),
            pl.BlockSpec((1, LANES), fixed2),
        ],
        out_specs=[pl.BlockSpec((nblk, LANES), rowb), pl.BlockSpec((1, LANES, nblk), lambda b: (b, 0, 0))],
        out_shape=[jax.ShapeDtypeStruct((batch * nblk, LANES), BF16),
                   jax.ShapeDtypeStruct((batch, LANES, nblk), BF16)],
        compiler_params=_params(("parallel",)),
        name="compress_prompt",
    )(kc, vc, pe128, wk_bd, wv_bd, kg)


def _nsa_prompt_kernel(q_ref, ck_ref, cvt_ref, ks_ref, vst_ref, kw_ref, vwt_ref, bg_ref, gate_ref,
                       o_ref, m_scr, acc_scr, imp_scr, score_scr, chosen_scr, sc_scr):
    i = pl.program_id(1)
    tq = q_ref.shape[0]
    nb = ck_ref.shape[0]
    kt = NSA_KEY_TILE
    ratio = SEL_BLOCK // CMP_BLOCK
    nsb = nb // ratio
    q_pos = i * tq + lax.broadcasted_iota(I32, (1, tq), 1)

    zeros = jnp.zeros((HEAD_DIM, tq), F32)
    qts = []
    for p in range(H_NSA // 2):
        t = _transpose_tiles(q_ref[:, p * LANES:(p + 1) * LANES].astype(F32))
        for ht in (t[:HEAD_DIM], t[HEAD_DIM:]):
            parts = [zeros] * G_NSA
            parts[(2 * p) // HPG] = ht
            qts.append(jnp.concatenate(parts, axis=0).astype(BF16))
    group_of = [h // HPG for h in range(H_NSA)]

    ck = ck_ref[...]
    blk_n = lax.broadcasted_iota(I32, (nb, tq), 0)
    cmask = blk_n * CMP_BLOCK + (CMP_BLOCK - 1) <= q_pos
    cmp_scores = [_dot(ck, qt) for qt in qts]
    cmp_probs = []
    imps = [jnp.zeros((nb, tq), F32) for _ in range(G_NSA)]
    for h, s in enumerate(cmp_scores):
        s = jnp.where(cmask, s, NEG_INF)
        m = jnp.max(s, axis=0, keepdims=True)
        e = jnp.where(cmask, jnp.exp2(s - m), 0.0)
        den = jnp.sum(e, axis=0, keepdims=True)
        pr = e / jnp.where(den > 0.0, den, 1.0)
        imps[group_of[h]] = imps[group_of[h]] + pr
        cmp_probs.append(pr.astype(BF16))
    o_cmp = [_dot(cvt_ref[0, group_of[h] * HEAD_DIM:(group_of[h] + 1) * HEAD_DIM, :], pr)
             for h, pr in enumerate(cmp_probs)]

    sblk = lax.broadcasted_iota(I32, (nsb, tq), 0)
    cur = q_pos // SEL_BLOCK
    forced = (sblk == 0) | (sblk == cur) | (sblk == cur - 1)
    for g in range(G_NSA):
        pooled_parts = []
        for c in range(tq // LANES):
            part_ref = imp_scr.at[c]
            part_ref[...] = imps[g][:, c * LANES:(c + 1) * LANES]
            part = part_ref[pl.ds(0, nsb, stride=ratio), :]
            for r in range(1, ratio):
                part = part + part_ref[pl.ds(r, nsb, stride=ratio), :]
            pooled_parts.append(part)
        pooled = jnp.concatenate(pooled_parts, axis=1)
        score = jnp.where(sblk <= cur, pooled + FORCE_BONUS * forced.astype(F32), NEG_INF)
        score_scr[...] = score
        slab = 8
        cnts = []
        for r0 in range(0, nsb, slab):
            rows = score_scr[r0:min(r0 + slab, nsb), :]
            idx = r0 + lax.broadcasted_iota(I32, rows.shape, 0)
            cnt = jnp.zeros(rows.shape, F32)
            for mblk in range(nsb):
                c = score_scr[mblk:mblk + 1, :]
                if mblk < r0:
                    ahead = c >= rows
                elif mblk >= r0 + rows.shape[0]:
                    ahead = c > rows
                else:
                    ahead = (c > rows) | ((c == rows) & (mblk < idx))
                cnt = cnt + jnp.where(ahead, 1.0, 0.0)
            cnts.append(cnt)
        chosen = (jnp.concatenate(cnts, axis=0) < float(TOP_N)).astype(F32)
        for mblk in range(nsb):
            chosen_scr[g, mblk] = jnp.broadcast_to(chosen[mblk:mblk + 1, :], (8, tq))

    k_in = lax.broadcasted_iota(I32, (kt, tq), 0)
    blocks_per_tile = kt // SEL_BLOCK
    j_diag = ((i + 1) * tq - 1) // kt

    ones_rows = jnp.ones((acc_scr.shape[2] - HEAD_DIM, kt), BF16)

    heads_a = list(range(HPG))
    heads_b = list(range(HPG, H_NSA))

    def flash(k_ref, vt_ref, slot, n_tiles, mask_fn):
        m_scr[slot] = jnp.full(m_scr.shape[1:], NEG_INF, F32)
        acc_scr[slot] = jnp.zeros(acc_scr.shape[1:], F32)

        def load_k(j):
            return k_ref[pl.ds(pl.multiple_of(j * kt, kt), kt), :]

        def softmax_step(h, s, mask):
            s = jnp.where(mask, s, NEG_INF)
            m_old = m_scr[slot, h]
            m_new = jnp.maximum(m_old, jnp.max(s, axis=0, keepdims=True))
            m_scr[slot, h] = m_new
            return jnp.exp2(m_old - m_new)[0:1], jnp.exp2(s - m_new[0:1]).astype(BF16)

        def accumulate(h, alpha, pv):
            acc_scr[slot, h] = alpha * acc_scr[slot, h] + pv

        k_first = load_k(j_diag)
        for n, h in enumerate(heads_b):
            sc_scr[n] = _dot(k_first, qts[h])

        def body(t, carry):
            j = j_diag - t
            kj = load_k(j)
            k_next = load_k(jnp.maximum(j - 1, 0))
            vt = vt_ref[j]
            vts = [jnp.concatenate([vt[g * HEAD_DIM:(g + 1) * HEAD_DIM, :], ones_rows], axis=0)
                   for g in range(G_NSA)]
            scores_a = [_dot(kj, qts[h]) for h in heads_a]
            mask_b = mask_fn(1, j)
            soft_b = [softmax_step(h, sc_scr[n], mask_b) for n, h in enumerate(heads_b)]
            pv_b = [_dot(vts[1], p) for _, p in soft_b]
            next_b = [_dot(k_next, qts[h]) for h in heads_b]
            mask_a = mask_fn(0, j)
            soft_a = [softmax_step(h, scores_a[n], mask_a) for n, h in enumerate(heads_a)]
            pv_a = [_dot(vts[0], p) for _, p in soft_a]
            for n, h in enumerate(heads_b):
                accumulate(h, soft_b[n][0], pv_b[n])
                sc_scr[n] = next_b[n]
            for n, h in enumerate(heads_a):
                accumulate(h, soft_a[n][0], pv_a[n])
            return carry

        lax.fori_loop(0, n_tiles, body, 0)
        outs = []
        for h in range(H_NSA):
            acc = acc_scr[slot, h]
            outs.append(acc[:HEAD_DIM] / acc[HEAD_DIM:HEAD_DIM + 1])
        return outs

    def causal(j):
        return j * kt + k_in <= q_pos

    def sel_mask(g, j):
        rows = [jnp.broadcast_to(chosen_scr[g, blocks_per_tile * j + r][0:1], (SEL_BLOCK, tq))
                for r in range(blocks_per_tile)]
        return (jnp.concatenate(rows, axis=0) > 0.5) & causal(j)

    def win_mask(g, j):
        return causal(j) & (j * kt + k_in > q_pos - WINDOW)

    o_sel = flash(ks_ref, vst_ref, 0, j_diag + 1, sel_mask)
    j_low = jnp.maximum(i * tq - (WINDOW - 1), 0) // kt
    o_win = flash(kw_ref, vwt_ref, 1, j_diag - j_low + 1, win_mask)

    bgt = _transpose_tiles(bg_ref[...])

    def gate_row(h, br):
        c = h * 3 + br
        return bgt[c:c + 1, :]

    for p in range(H_NSA // 2):
        halves = []
        for h in (2 * p, 2 * p + 1):
            halves.append(gate_row(h, 0) * o_cmp[h] + gate_row(h, 1) * o_sel[h] + gate_row(h, 2) * o_win[h])
        mixed = _transpose_tiles(jnp.concatenate(halves, axis=0))
        o_ref[:, p * LANES:(p + 1) * LANES] = mixed * gate_ref[:, p * LANES:(p + 1) * LANES]


def _nsa_prompt(qn, ck, cvt, ksb, vst, kwb, vwt, bg, gate, batch, seq):
    tq = NSA_Q_TILE
    kt = NSA_KEY_TILE
    nq = seq // tq
    nb = seq // CMP_BLOCK
    nsb = nb // (SEL_BLOCK // CMP_BLOCK)
    qmap = lambda b, i: (b * nq + i, 0)
    per_b2 = lambda b, i: (b, 0)
    per_b3 = lambda b, i: (b, 0, 0)
    return pl.pallas_call(
        _nsa_prompt_kernel,
        grid=(batch, nq),
        in_specs=[
            pl.BlockSpec((tq, NSA_W), qmap),
            pl.BlockSpec((nb, LANES), per_b2),
            pl.BlockSpec((1, LANES, nb), per_b3),
            pl.BlockSpec((seq, LANES), per_b2),
            pl.BlockSpec((seq // kt, LANES, kt), per_b3),
            pl.BlockSpec((seq, LANES), per_b2),
            pl.BlockSpec((seq // kt, LANES, kt), per_b3),
            pl.BlockSpec((tq, LANES), qmap),
            pl.BlockSpec((tq, NSA_W), qmap),
        ],
        out_specs=pl.BlockSpec((tq, NSA_W), qmap),
        out_shape=jax.ShapeDtypeStruct((batch * seq, NSA_W), F32),
        scratch_shapes=[
            pltpu.VMEM((2, H_NSA, 8, tq), F32),
            pltpu.VMEM((2, H_NSA, HEAD_DIM + 16, tq), F32),
            pltpu.VMEM((tq // LANES, nb, LANES), F32),
            pltpu.VMEM((nsb, tq), F32),
            pltpu.VMEM((G_NSA, nsb, 8, tq), F32),
            pltpu.VMEM((HPG, kt, tq), F32),
        ],
        compiler_params=_params(("parallel", "arbitrary")),
        name="nsa_prompt",
    )(qn, ck, cvt, ksb, vst, kwb, vwt, bg, gate)


def _layer1_prompt(x2d, batch, seq, norm_g, w_in_bf, qn_g, cmp_kn_g, sel_kn_g, win_kn_g, pe128, wk_bd, wv_bd,
                   w_out_bf, tm):
    (qn, gate, bg, kc_t, vc_t, ks_t, vs_t, kw_t, vw_t, kc, vc, ksb, kwb, vst, vwt) = _inproj1(
        x2d, norm_g, w_in_bf, _pair_gain(qn_g), _pair_gain(sel_kn_g), _pair_gain(win_kn_g), tm, seq, True)
    ck, cvt = _compress_prompt(kc, vc, pe128, wk_bd, wv_bd, _pair_gain(cmp_kn_g), batch, seq)
    mix = _nsa_prompt(qn, ck, cvt, ksb, vst, kwb, vwt, bg, gate, batch, seq)
    y = _outproj(x2d, [mix], [w_out_bf], tm)
    return y, kc_t, vc_t, ks_t, vs_t, kw_t, vw_t


def _ret_sample_kernel(ret_ref, s_ref, qdec_ref, cdec_ref, gn_ref, mix_ref, so_ref):
    width = H_RET * HEAD_DIM
    rowi = lax.broadcasted_iota(I32, (8, HEAD_DIM), 0)
    eye = (lax.broadcasted_iota(I32, (HEAD_DIM, HEAD_DIM), 0)
           == lax.broadcasted_iota(I32, (HEAD_DIM, HEAD_DIM), 1)).astype(BF16)
    row = ret_ref[0]
    for h in range(H_RET):
        sl = slice(h * HEAD_DIM, (h + 1) * HEAD_DIM)
        qb = row[:, sl].astype(BF16)
        kb = row[:, width + h * HEAD_DIM:width + (h + 1) * HEAD_DIM].astype(BF16)
        vb = row[:, 2 * width + h * HEAD_DIM:2 * width + (h + 1) * HEAD_DIM].astype(BF16)
        gate = row[:, 3 * width + h * HEAD_DIM:3 * width + (h + 1) * HEAD_DIM]
        s = s_ref[0, h]
        qs = _dot(jnp.broadcast_to(qb.astype(F32), (8, HEAD_DIM)).astype(BF16), s.astype(BF16))[0:1]
        att = jnp.sum(qb.astype(F32) * kb.astype(F32), axis=-1, keepdims=True)
        o = att.astype(BF16).astype(F32) * vb.astype(F32) + qs * qdec_ref[h:h + 1, :]
        k8 = jnp.where(rowi == 0, jnp.broadcast_to(kb.astype(F32), (8, HEAD_DIM)), 0.0).astype(BF16)
        kcol = _dot_nt(eye, k8)[:, 0:1]
        so_ref[0, h] = s * cdec_ref[h:h + 1, :] + kcol * vb.astype(F32)
        mu = jnp.mean(o, axis=-1, keepdims=True)
        oc = o - mu
        var = jnp.mean(oc * oc, axis=-1, keepdims=True)
        on = oc * lax.rsqrt(var + EPS) * gn_ref[h:h + 1, :]
        mix_ref[0, :, sl] = on * _silu(gate)


def _ret_sample(ret, state, gn_g):
    nb, cols = ret.shape
    width = H_RET * HEAD_DIM
    h = jnp.arange(H_RET, dtype=F32)
    lg = jnp.log1p(-jnp.exp2(-5.0 - h))
    qdec = jnp.broadcast_to(jnp.exp(lg * 1.0)[:, None], (H_RET, HEAD_DIM))
    cdec = jnp.broadcast_to(jnp.exp(lg * 1.0)[:, None], (H_RET, HEAD_DIM))
    m3 = lambda b: (b, 0, 0)
    tab = pl.BlockSpec((H_RET, HEAD_DIM), lambda b: (0, 0))
    st = pl.BlockSpec((1, H_RET, HEAD_DIM, HEAD_DIM), lambda b: (b, 0, 0, 0))
    mix, state_new = pl.pallas_call(
        _ret_sample_kernel,
        grid=(nb,),
        in_specs=[pl.BlockSpec((1, 1, cols), m3), st, tab, tab, tab],
        out_specs=[pl.BlockSpec((1, 1, width), m3), st],
        out_shape=[jax.ShapeDtypeStruct((nb, 1, width), F32),
                   jax.ShapeDtypeStruct(state.shape, F32)],
        compiler_params=_params(("parallel",)),
        name="retention_sample",
    )(ret.reshape(nb, 1, cols), state, qdec, cdec, gn_g.astype(F32))
    return mix.reshape(nb, width), state_new


def _sb_sample_kernel(pt_ref, q_ref, g_ref, k_hbm, v_hbm, o_ref, kbuf, vbuf, sem):
    b = pl.program_id(0)
    n_pages = pt_ref.shape[1]
    page = PAGE_SIZE

    def copies(t, slot):
        pid = pt_ref[b, n_pages - 1 - t]
        return (pltpu.make_async_copy(k_hbm.at[pid], kbuf.at[slot], sem.at[0, slot]),
                pltpu.make_async_copy(v_hbm.at[pid], vbuf.at[slot], sem.at[1, slot]))

    def start(t, slot):
        for c in copies(t, slot):
            c.start()

    def wait(t, slot):
        for c in copies(t, slot):
            c.wait()

    q = q_ref[0].astype(F32)
    rowq = lax.broadcasted_iota(I32, (H_SB, HEAD_DIM), 0)
    rowa = lax.broadcasted_iota(I32, (H_SB, page), 0)
    upper = (lax.broadcasted_iota(I32, (page, page), 0) >= lax.broadcasted_iota(I32, (page, page), 1)).astype(BF16)
    q_rows = [jnp.where(rowq == h, q, 0.0).astype(BF16) for h in range(H_SB)]

    start(0, 0)

    def cond(carry):
        t, alive = carry[0], carry[1]
        return (t < n_pages) & alive

    def body(carry):
        t, _, r, acc = carry
        slot = t % 2
        wait(t, slot)

        @pl.when(t + 1 < n_pages)
        def _():
            start(t + 1, 1 - slot)

        z = jnp.zeros((H_SB, page), F32)
        for h in range(H_SB):
            z = z + _dot(q_rows[h], kbuf[slot, h].astype(BF16))
        ls_pos = _log2_sigmoid(z)
        lm = ls_pos - z
        hi, lo_part = _split_bf16(lm)
        s_incl = _dot(hi, upper) + _dot(lo_part, upper)
        a = jnp.exp2(ls_pos + (s_incl - lm) + r)
        for h in range(H_SB):
            acc = acc + _dot_nt(jnp.where(rowa == h, a, 0.0).astype(BF16), vbuf[slot, h].astype(BF16))
        r = r + s_incl[:, 0:1]
        return t + 1, jnp.max(r) > SB_LOG_ZERO, r, acc

    init = (jnp.int32(0), jnp.bool_(True), jnp.zeros((H_SB, 1), F32), jnp.zeros((H_SB, HEAD_DIM), F32))
    t_end, _, _, acc = lax.while_loop(cond, body, init)

    @pl.when(t_end < n_pages)
    def _():
        wait(t_end, t_end % 2)

    o_ref[0] = acc * g_ref[0]


def _sb_sample(q8, g8, cache_kt, cache_vt, page_table):
    nb = q8.shape[0]
    head = lambda b, pt: (b, 0, 0)
    grid_spec = pltpu.PrefetchScalarGridSpec(
        num_scalar_prefetch=1,
        grid=(nb,),
        in_specs=[pl.BlockSpec((1, H_SB, HEAD_DIM), head), pl.BlockSpec((1, H_SB, HEAD_DIM), head),
                  pl.BlockSpec(memory_space=pl.ANY), pl.BlockSpec(memory_space=pl.ANY)],
        out_specs=pl.BlockSpec((1, H_SB, HEAD_DIM), head),
        scratch_shapes=[pltpu.VMEM((2, H_SB, HEAD_DIM, PAGE_SIZE), F32),
                        pltpu.VMEM((2, H_SB, HEAD_DIM, PAGE_SIZE), F32),
                        pltpu.SemaphoreType.DMA((2, 2))],
    )
    return pl.pallas_call(
        _sb_sample_kernel,
        grid_spec=grid_spec,
        out_shape=jax.ShapeDtypeStruct((nb, H_SB, HEAD_DIM), F32),
        compiler_params=_params(("arbitrary",)),
        name="sb_sample",
    )(page_table, q8, g8, cache_kt, cache_vt)


CMP_PAGES_PER_STEP = 16


def _regroup_matrix():
    r = np.zeros((2 * PAGE_SIZE, 2 * PAGE_SIZE), np.float32)
    per_page = PAGE_SIZE // CMP_BLOCK
    for l in range(CMP_BLOCK):
        for p in range(2):
            for n in range(per_page):
                r[l * 2 * per_page + p * per_page + n, p * PAGE_SIZE + CMP_BLOCK * n + l] = 1.0
    return jnp.asarray(r, BF16)


def _compress_sample_kernel(pt_ref, *refs, n_pp):
    k_refs = refs[:n_pp]
    v_refs = refs[n_pp:2 * n_pp]
    pe_ref, wk_ref, wv_ref, kg_ref, rg_ref, ck_ref, cv_ref, kbuf, vbuf = refs[2 * n_pp:]
    j = pl.program_id(1)
    per_pair = 2 * PAGE_SIZE // CMP_BLOCK
    regroup = rg_ref[...]
    pe_t = pe_ref[...]
    for src_refs, buf in ((k_refs, kbuf), (v_refs, vbuf)):
        for a in range(n_pp // 2):
            both = jnp.concatenate([src_refs[2 * a][0], src_refs[2 * a + 1][0]], axis=1) + pe_t
            rows = _dot_nt(regroup, both.astype(BF16))
            for l in range(CMP_BLOCK):
                buf[l, a * per_pair:(a + 1) * per_pair, :] = rows[l * per_pair:(l + 1) * per_pair]
    nblk = n_pp * PAGE_SIZE // CMP_BLOCK

    def compress(buf, w_ref):
        acc = jnp.zeros((nblk, LANES), F32)
        for l in range(CMP_BLOCK):
            acc = acc + _dot(buf[l].astype(BF16), w_ref[l])
        return acc

    lo = _lane_lo((1, LANES))
    start = pl.multiple_of(j * nblk, nblk)
    ck_ref[0, pl.ds(start, nblk), :] = _pair_rmsnorm(compress(kbuf, wk_ref), kg_ref[...], lo)
    cv_ref[0, pl.ds(start, nblk), :] = compress(vbuf, wv_ref)


def _compress_sample(cache_k, cache_v, page_table, pe128, wk_bd, wv_bd, kg):
    nb, n_pages = page_table.shape
    n_pp = CMP_PAGES_PER_STEP
    nblk_total = n_pages * PAGE_SIZE // CMP_BLOCK
    pe_t = jnp.tile(pe128.T, (1, 2 * PAGE_SIZE // CMP_BLOCK))

    def page_map(r):
        return lambda b, j, pt: (pt[b, j * n_pp + r], 0, 0)

    fixed2 = lambda b, j, pt: (0, 0)
    fixed3 = lambda b, j, pt: (0, 0, 0)
    out_map = lambda b, j, pt: (b, 0, 0)
    grid_spec = pltpu.PrefetchScalarGridSpec(
        num_scalar_prefetch=1,
        grid=(nb, n_pages // n_pp),
        in_specs=[pl.BlockSpec((1, PAGE_SIZE, LANES), page_map(r)) for r in range(n_pp)] * 2
        + [pl.BlockSpec((LANES, 2 * PAGE_SIZE), fixed2),
           pl.BlockSpec((CMP_BLOCK, LANES, LANES), fixed3),
           pl.BlockSpec((CMP_BLOCK, LANES, LANES), fixed3),
           pl.BlockSpec((1, LANES), fixed2),
           pl.BlockSpec((2 * PAGE_SIZE, 2 * PAGE_SIZE), fixed2)],
        out_specs=[pl.BlockSpec((1, nblk_total, LANES), out_map)] * 2,
        scratch_shapes=[pltpu.VMEM((CMP_BLOCK, n_pp * PAGE_SIZE // CMP_BLOCK, LANES), F32)] * 2,
    )
    o = jax.ShapeDtypeStruct((nb, nblk_total, LANES), F32)
    return pl.pallas_call(
        functools.partial(_compress_sample_kernel, n_pp=n_pp),
        grid_spec=grid_spec,
        out_shape=[o, o],
        compiler_params=_params(("parallel", "arbitrary")),
        name="compress_sample",
    )(page_table, *([cache_k] * n_pp), *([cache_v] * n_pp), pe_t, wk_bd, wv_bd, kg, _regroup_matrix())


def _pages_t(cache):
    return cache.transpose(0, 2, 3, 1)


def _group_q(q, g):
    z = jnp.zeros_like(q)
    return jnp.concatenate([q, z], axis=1) if g == 0 else jnp.concatenate([z, q], axis=1)


def _nsa_sample_cmp_kernel(q_ref, ck_ref, cv_ref, ocmp_ref, score_ref, *, q_pos):
    q = q_ref[0]
    ck = ck_ref[0].astype(BF16)
    cv = cv_ref[0].astype(BF16)
    nb = ck.shape[0]
    blk_n = lax.broadcasted_iota(I32, (1, nb), 1)
    cmask = blk_n * CMP_BLOCK + (CMP_BLOCK - 1) <= q_pos
    sblk = blk_n // (SEL_BLOCK // CMP_BLOCK)
    cur = q_pos // SEL_BLOCK
    row8 = lax.broadcasted_iota(I32, (8, nb), 0)
    neg_inf = float("-inf")
    scores = jnp.full((8, nb), neg_inf, F32)
    for g in range(G_NSA):
        qg = _group_q(q[g * HPG:(g + 1) * HPG], g)
        s = jnp.where(cmask, _dot_nt(qg, ck), NEG_INF)
        m = jnp.max(s, axis=-1, keepdims=True)
        e = jnp.where(cmask, jnp.exp2(s - m), 0.0)
        den = jnp.sum(e, axis=-1, keepdims=True)
        pr = e / jnp.where(den > 0.0, den, 1.0)
        o = _dot(pr.astype(BF16), cv)
        ocmp_ref[0, g * HPG:(g + 1) * HPG, :] = o[:, g * HEAD_DIM:(g + 1) * HEAD_DIM]
        imp = jnp.sum(pr, axis=0, keepdims=True)
        even = (blk_n & 1) == 0
        pooled = imp + jnp.where(even, pltpu.roll(imp, nb - 1, 1), pltpu.roll(imp, 1, 1))
        forced = (sblk == 0) | (sblk == cur) | (sblk == cur - 1)
        score = jnp.where(sblk < cur, pooled + FORCE_BONUS * forced.astype(F32), neg_inf)
        scores = jnp.where(row8 == g, jnp.broadcast_to(score, (8, nb)), scores)
    score_ref[0] = scores


def _nsa_sample_topk_kernel(score_ref, idx_ref):
    score = score_ref[...]
    rows, nb = score.shape
    ratio = SEL_BLOCK // CMP_BLOCK
    lane = lax.broadcasted_iota(I32, (rows, nb), 1)
    lane_f = lane.astype(F32)
    sblk = lane // ratio
    slot = lax.broadcasted_iota(I32, (rows, LANES), 1)
    neg_inf = float("-inf")
    idx = jnp.full((rows, LANES), -1, I32)
    for t in range(TOP_N - 1):
        mx = jnp.max(score, axis=-1, keepdims=True)
        first = jnp.min(jnp.where(score == mx, lane_f, float(4 * nb)), axis=-1, keepdims=True)
        blk = first.astype(I32) // ratio
        idx = jnp.where((slot == t) & (mx > neg_inf), blk, idx)
        score = jnp.where(sblk == blk, neg_inf, score)
    idx_ref[...] = idx


def _nsa_sample_select(q16, ck, cv, q_pos):
    nb = q16.shape[0]
    nblk = ck.shape[1]
    m3 = lambda b: (b, 0, 0)
    ocmp, scores = pl.pallas_call(
        functools.partial(_nsa_sample_cmp_kernel, q_pos=q_pos),
        grid=(nb,),
        in_specs=[pl.BlockSpec((1, H_NSA, HEAD_DIM), m3),
                  pl.BlockSpec((1, nblk, LANES), m3),
                  pl.BlockSpec((1, nblk, LANES), m3)],
        out_specs=[pl.BlockSpec((1, H_NSA, HEAD_DIM), m3), pl.BlockSpec((1, 8, nblk), m3)],
        out_shape=[jax.ShapeDtypeStruct((nb, H_NSA, HEAD_DIM), F32),
                   jax.ShapeDtypeStruct((nb, 8, nblk), F32)],
        compiler_params=_params(("parallel",)),
        name="nsa_sample_cmp",
    )(q16, ck, cv)
    rows = scores[:, :G_NSA].reshape(nb * G_NSA, nblk)
    idx = pl.pallas_call(
        _nsa_sample_topk_kernel,
        out_shape=jax.ShapeDtypeStruct((nb * G_NSA, LANES), I32),
        compiler_params=pltpu.CompilerParams(vmem_limit_bytes=VMEM_LIMIT),
        name="nsa_sample_topk",
    )(rows)
    return ocmp, idx


def _nsa_sample_attend_kernel(pt_ref, idx_ref, q_ref, k_hbm, v_hbm, kn_ref, vn_ref, kwn_ref, vwn_ref,
                              wk_ref, wv_ref, ocmp_ref, bg_ref, gate_ref, o_ref, kbuf, vbuf, sem,
                              *, q_pos):
    b = pl.program_id(0)
    n_slots = kbuf.shape[1]
    n_pages = pt_ref.shape[1]

    def copies(g, s):
        blk = jnp.maximum(idx_ref[b * G_NSA + g, s], 0)
        pid = pt_ref[b, jnp.minimum(blk // 2, n_pages - 1)]
        rows = pl.ds(g * HEAD_DIM, HEAD_DIM)
        return (pltpu.make_async_copy(k_hbm.at[pid, rows], kbuf.at[g, s], sem.at[0, g]),
                pltpu.make_async_copy(v_hbm.at[pid, rows], vbuf.at[g, s], sem.at[1, g]))

    for g in range(G_NSA):
        for s in range(n_slots):
            for c in copies(g, s):
                c.start()

    q = q_ref[0]
    bg = bg_ref[0]
    col = lax.broadcasted_iota(I32, (HPG, LANES), 1)
    hrow = lax.broadcasted_iota(I32, (HPG, LANES), 0)
    key_half = lax.broadcasted_iota(I32, (1, PAGE_SIZE), 1) // SEL_BLOCK
    wlen = wk_ref.shape[3]
    w_pos = q_pos - wlen + lax.broadcasted_iota(I32, (1, wlen), 1)
    wmask = (w_pos > q_pos - WINDOW) & (w_pos >= 0)

    def new_token(x_ref, g):
        return x_ref[0][:, g * HEAD_DIM:(g + 1) * HEAD_DIM].astype(BF16).astype(F32)

    for g in range(G_NSA):
        qg = q[g * HPG:(g + 1) * HPG]
        qf = qg.astype(F32)
        sw = jnp.where(wmask, _dot(qg, wk_ref[0, g].astype(BF16)), NEG_INF)
        sw_new = jnp.sum(qf * new_token(kwn_ref, g), axis=-1, keepdims=True)
        mw = jnp.maximum(jnp.max(sw, axis=-1, keepdims=True), sw_new)
        pw = jnp.where(wmask, jnp.exp2(sw - mw), 0.0)
        pw_new = jnp.exp2(sw_new - mw)
        o_win = _dot_nt(pw.astype(BF16), wv_ref[0, g].astype(BF16))
        o_win = o_win + pw_new.astype(BF16).astype(F32) * new_token(vwn_ref, g)
        o_win = o_win / (jnp.sum(pw, axis=-1, keepdims=True) + pw_new)
        for s in range(n_slots):
            for c in copies(g, s):
                c.wait()
        scores, masks = [], []
        for s in range(n_slots):
            blk = idx_ref[b * G_NSA + g, s]
            ok = (key_half == jnp.maximum(blk, 0) % 2) & (blk >= 0)
            scores.append(jnp.where(ok, _dot(qg, kbuf[g, s].astype(BF16)), NEG_INF))
            masks.append(ok)
        s_new = jnp.sum(qf * new_token(kn_ref, g), axis=-1, keepdims=True)
        m = s_new
        for sc in scores:
            m = jnp.maximum(m, jnp.max(sc, axis=-1, keepdims=True))
        p_new = jnp.exp2(s_new - m)
        den = p_new
        acc = p_new.astype(BF16).astype(F32) * new_token(vn_ref, g)
        for s in range(n_slots):
            p = jnp.where(masks[s], jnp.exp2(scores[s] - m), 0.0)
            den = den + jnp.sum(p, axis=-1, keepdims=True)
            acc = acc + _dot_nt(p.astype(BF16), vbuf[g, s].astype(BF16))
        o_sel = acc / den
        gates = []
        for br in range(3):
            pick = col == g * HPG * 3 + hrow * 3 + br
            gates.append(jnp.sum(jnp.where(pick, jnp.broadcast_to(bg, (HPG, LANES)), 0.0), axis=-1, keepdims=True))
        rows = slice(g * HPG, (g + 1) * HPG)
        mixed = gates[0] * ocmp_ref[0, rows, :] + gates[1] * o_sel + gates[2] * o_win
        o_ref[0, rows, :] = mixed * gate_ref[0, rows, :]


def _nsa_sample_attend(page_table, idx, q16, sel_kt, sel_vt, ks_new, vs_new, kw_new, vw_new, win_kt, win_vt, ocmp,
                       bg, gate16, q_pos):
    nb = q16.shape[0]
    n_slots = TOP_N - 1
    wlen = win_kt.shape[3]
    m3 = lambda b, pt, ix: (b, 0, 0)
    m4 = lambda b, pt, ix: (b, 0, 0, 0)
    row = pl.BlockSpec((1, 1, LANES), m3)
    heads = pl.BlockSpec((1, H_NSA, HEAD_DIM), m3)
    win = pl.BlockSpec((1, G_NSA, HEAD_DIM, wlen), m4)
    grid_spec = pltpu.PrefetchScalarGridSpec(
        num_scalar_prefetch=2,
        grid=(nb,),
        in_specs=[heads, pl.BlockSpec(memory_space=pl.ANY), pl.BlockSpec(memory_space=pl.ANY),
                  row, row, row, row, win, win, heads, row, heads],
        out_specs=heads,
        scratch_shapes=[pltpu.VMEM((G_NSA, n_slots, HEAD_DIM, PAGE_SIZE), F32),
                        pltpu.VMEM((G_NSA, n_slots, HEAD_DIM, PAGE_SIZE), F32),
                        pltpu.SemaphoreType.DMA((2, G_NSA))],
    )
    return pl.pallas_call(
        functools.partial(_nsa_sample_attend_kernel, q_pos=q_pos),
        grid_spec=grid_spec,
        out_shape=jax.ShapeDtypeStruct((nb, H_NSA, HEAD_DIM), F32),
        compiler_params=_params(("arbitrary",)),
        name="nsa_sample_attend",
    )(page_table, idx, q16, sel_kt, sel_vt, ks_new, vs_new, kw_new, vw_new, win_kt, win_vt, ocmp, bg, gate16)


def _layer0_sample(x2d, past_len, state, cache_k, cache_v, page_table, norm_g, w_in_bf, ret_gn_g, sb_qn_g, sb_kn_g,
                   w_out_bf):
    nb = x2d.shape[0]
    cos, sin = _rope_tables(jnp.full((nb,), past_len, I32))
    ret, sbq, sbg, sbk, sbv = _inproj0(x2d, norm_g, w_in_bf, cos, sin, _pair_gain(sb_qn_g), _pair_gain(sb_kn_g),
                                       nb, nb, False)
    mix_ret, state_new = _ret_sample(ret, state, ret_gn_g)
    mix_sb = _sb_sample(sbq.reshape(nb, H_SB, HEAD_DIM), sbg.reshape(nb, H_SB, HEAD_DIM),
                        _pages_t(cache_k), _pages_t(cache_v), page_table)
    mix_sb = mix_sb.reshape(nb, H_SB * HEAD_DIM)
    half = w_out_bf.shape[0] // 2
    y = _outproj(x2d, [mix_ret, mix_sb], [w_out_bf[:half], w_out_bf[half:]], nb)
    return y, state_new, sbk, sbv


def _layer1_sample(x2d, past_len, cmp_k, cmp_v, sel_k, sel_v, win_k, win_v, page_table, norm_g, w_in_bf, qn_g,
                   cmp_kn_g, sel_kn_g, win_kn_g, pe128, wk_bd, wv_bd, w_out_bf):
    nb = x2d.shape[0]
    (qn, gate, bg, kc, vc, ks, vs, kw, vw) = _inproj1(
        x2d, norm_g, w_in_bf, _pair_gain(qn_g), _pair_gain(sel_kn_g), _pair_gain(win_kn_g), nb, nb, False)
    n_phys = cmp_k.shape[0]
    pool = lambda t: _pages_t(t).reshape(n_phys, KV_W, PAGE_SIZE)
    ck, cv = _compress_sample(pool(cmp_k), pool(cmp_v), page_table, pe128, wk_bd, wv_bd, _pair_gain(cmp_kn_g))
    q16 = qn.reshape(nb, H_NSA, HEAD_DIM)
    ocmp, idx = _nsa_sample_select(q16, ck, cv, past_len)
    idx2 = idx[:, :TOP_N]
    mix = _nsa_sample_attend(page_table, idx2, q16, pool(sel_k), pool(sel_v), ks[:, None, :], vs[:, None, :],
                             kw[:, None, :], vw[:, None, :], _pages_t(win_k), _pages_t(win_v), ocmp,
                             bg[:, None, :], gate.reshape(nb, H_NSA, HEAD_DIM), past_len)
    y = _outproj(x2d, [mix.reshape(nb, NSA_W)], [w_out_bf], nb)
    gd = (nb, 1, G_NSA, HEAD_DIM)
    wk_new = jnp.concatenate([win_k[:, 1:], kw.reshape(gd)], axis=1)
    wv_new = jnp.concatenate([win_v[:, 1:], vw.reshape(gd)], axis=1)
    return y, kc, vc, ks, vs, wk_new, wv_new


def kernel(x_prompt, x_sample, state_ret, cache_sb_k, cache_sb_v, cache_cmp_k, cache_cmp_v, cache_sel_k,
           cache_sel_v, cache_win_k, cache_win_v, page_table, norm0_g, w_in0, ret_gn_g, sb_qn_g, sb_kn_g,
           w_out0, norm1_g, w_in1, nsa_qn_g, cmp_kn_g, sel_kn_g, win_kn_g, cmp_pe, w_cmp_k, w_cmp_v, w_out1):
    batch, seq, d = x_prompt.shape
    nb = x_sample.shape[0]
    past_len = page_table.shape[1] * PAGE_SIZE
    tm = 256
    w_in0_bf = w_in0.astype(BF16)
    w_out0_bf = w_out0.astype(BF16)
    w_in1_bf = _pad_w_in1(w_in1)
    w_out1_bf = w_out1.astype(BF16)
    pe128 = jnp.tile(cmp_pe, (1, G_NSA))
    wk_bd = _blockdiag_w(w_cmp_k)
    wv_bd = _blockdiag_w(w_cmp_v)

    xp = x_prompt.reshape(batch * seq, d)
    y1p, ret_p, sbk_p, sbv_p = _layer0_prompt(xp, batch, seq, norm0_g, w_in0_bf, ret_gn_g, sb_qn_g, sb_kn_g,
                                              w_out0_bf, tm)
    y2p, ckp, cvp, skp, svp, wkp, wvp = _layer1_prompt(y1p, batch, seq, norm1_g, w_in1_bf, nsa_qn_g, cmp_kn_g,
                                                        sel_kn_g, win_kn_g, pe128, wk_bd, wv_bd, w_out1_bf, tm)
    xs = x_sample.reshape(nb, d)
    y1s, ret_s, sbk_s, sbv_s = _layer0_sample(xs, past_len, state_ret, cache_sb_k, cache_sb_v, page_table, norm0_g,
                                              w_in0_bf, ret_gn_g, sb_qn_g, sb_kn_g, w_out0_bf)
    y2s, cks, cvs, sks, svs, wks, wvs = _layer1_sample(y1s, past_len, cache_cmp_k, cache_cmp_v, cache_sel_k,
                                                        cache_sel_v, cache_win_k, cache_win_v, page_table, norm1_g,
                                                        w_in1_bf, nsa_qn_g, cmp_kn_g, sel_kn_g, win_kn_g, pe128,
                                                        wk_bd, wv_bd, w_out1_bf)
    keep = min(WINDOW, seq)
    gd = (nb, 1, G_NSA, HEAD_DIM)

    def rows_major(t):
        return t.transpose(0, 3, 1, 2)

    def groups(t):
        return rows_major(t.reshape(batch, G_NSA, HEAD_DIM, seq))

    return (y2p.reshape(batch, seq, d), y2s.reshape(nb, 1, d), ret_p, ret_s,
            rows_major(sbk_p), rows_major(sbv_p),
            sbk_s.reshape(nb, 1, H_SB, HEAD_DIM), sbv_s.reshape(nb, 1, H_SB, HEAD_DIM),
            groups(ckp), groups(cvp), groups(skp), groups(svp),
            groups(wkp)[:, seq - keep:], groups(wvp)[:, seq - keep:],
            cks.reshape(gd), cvs.reshape(gd), sks.reshape(gd), svs.reshape(gd), wks, wvs)
```

```python
import functools

import jax
import jax.numpy as jnp
import numpy as np
from jax import lax
from jax.experimental import pallas as pl
from jax.experimental.pallas import tpu as pltpu

F32 = jnp.float32
BF16 = jnp.bfloat16
I32 = jnp.int32

HEAD_DIM = 64
LANES = 128
H_RET = 8
H_SB = 8
RET_CHUNK = 128
ROPE_BASE = 10000.0
H_NSA = 16
G_NSA = 2
HPG = H_NSA // G_NSA
CMP_BLOCK = 32
SEL_BLOCK = 64
TOP_N = 16
WINDOW = 512
FORCE_BONUS = 1000.0
NEG_INF = -1e30
EPS = 1e-6
PAGE_SIZE = 128
NSA_KEY_TILE = 128
NSA_Q_TILE = 256
SCALE = HEAD_DIM ** -0.5
LOG2E = 1.4426950408889634
VMEM_LIMIT = 56 * 1024 * 1024


def _dot(a, b):
    return jnp.dot(a, b, preferred_element_type=F32)


def _dot_nt(a, b):
    return lax.dot_general(a, b, (((1,), (1,)), ((), ())), preferred_element_type=F32)


def _dot_tn(a, b):
    return lax.dot_general(a, b, (((0,), (0,)), ((), ())), preferred_element_type=F32)


def _split_bf16(x):
    hi = x.astype(BF16)
    lo = (x - hi.astype(F32)).astype(BF16)
    return hi, lo


def _sigmoid(x):
    return 1.0 / (1.0 + jnp.exp(-x))


def _silu(x):
    return x * _sigmoid(x)


def _lane_lo(shape):
    return lax.broadcasted_iota(I32, shape, len(shape) - 1) < HEAD_DIM


def _pair_sum(x, lo):
    s_lo = jnp.sum(jnp.where(lo, x, 0.0), axis=-1, keepdims=True)
    s_hi = jnp.sum(jnp.where(lo, 0.0, x), axis=-1, keepdims=True)
    return jnp.where(lo, s_lo, s_hi)


def _pair_rmsnorm(x, g, lo):
    ms = _pair_sum(x * x, lo) * (1.0 / HEAD_DIM)
    return x * lax.rsqrt(ms + EPS) * g


def _rope_pair(x, cos, sin_signed):
    lane = lax.broadcasted_iota(I32, x.shape, 1)
    first = (lane & 32) == 0
    partner = jnp.where(first, pltpu.roll(x, 96, 1), pltpu.roll(x, 32, 1))
    return x * cos + partner * sin_signed


def _transpose_tiles(x):
    r, c = x.shape
    if r <= LANES and c <= LANES:
        return x.T
    rows = [jnp.concatenate([x[ri * LANES:(ri + 1) * LANES, cj * LANES:(cj + 1) * LANES].T
                             for ri in range(r // LANES)], axis=1) for cj in range(c // LANES)]
    return jnp.concatenate(rows, axis=0)


def _params(sem):
    return pltpu.CompilerParams(dimension_semantics=sem, vmem_limit_bytes=VMEM_LIMIT)


def _store_head_major(dst_ref, t, pair, r):
    cols = slice(r * LANES, (r + 1) * LANES)
    dst_ref[0, 2 * pair, :, cols] = t[:HEAD_DIM]
    dst_ref[0, 2 * pair + 1, :, cols] = t[HEAD_DIM:]


def _inproj0_kernel(x_ref, g_ref, w_ref, cos_ref, sin_ref, qg_ref, kg_ref,
                    ret_ref, sbq_ref, sbg_ref, *kv_refs, prompt):
    x = x_ref[...]
    xn = x * lax.rsqrt(jnp.mean(x * x, axis=-1, keepdims=True) + EPS) * g_ref[...]
    xb = xn.astype(BF16)
    cos = cos_ref[...]
    sin = sin_ref[...]
    lo = _lane_lo((1, LANES))
    width = H_RET * HEAD_DIM

    def group(i):
        return _dot(xb, w_ref[:, i * width:(i + 1) * width])

    def tiles(a):
        return [a[:, t * LANES:(t + 1) * LANES] for t in range(width // LANES)]

    for t, v in enumerate(tiles(group(0))):
        ret_ref[:, t * LANES:(t + 1) * LANES] = _rope_pair(v, cos, sin)
    for t, v in enumerate(tiles(group(1))):
        ret_ref[:, width + t * LANES:width + (t + 1) * LANES] = _rope_pair(v, cos, sin) * SCALE
    ret_ref[:, 2 * width:3 * width] = group(2)
    ret_ref[:, 3 * width:4 * width] = group(3)
    for t, v in enumerate(tiles(group(4))):
        sbq_ref[:, t * LANES:(t + 1) * LANES] = (_pair_rmsnorm(v, qg_ref[...], lo) * (SCALE * LOG2E)).astype(BF16)
    sk = [_pair_rmsnorm(v, kg_ref[...], lo) for v in tiles(group(5))]
    sv = tiles(group(6))
    if prompt:
        kb_ref, vt_ref, kt_out, vt_out = kv_refs
        for t in range(width // LANES):
            kb_ref[:, t * LANES:(t + 1) * LANES] = sk[t].astype(BF16)
            for r in range(x.shape[0] // LANES):
                rows = slice(r * LANES, (r + 1) * LANES)
                vt = sv[t][rows].T
                _store_head_major(kt_out, sk[t][rows].T, t, r)
                _store_head_major(vt_out, vt, t, r)
                vt_ref[r, t] = vt.astype(BF16)
    else:
        k_ref, v_ref = kv_refs
        for t in range(width // LANES):
            k_ref[:, t * LANES:(t + 1) * LANES] = sk[t]
            v_ref[:, t * LANES:(t + 1) * LANES] = sv[t]
    sbg_ref[...] = _silu(group(7))


def _inproj0(x2d, norm_g, w_bf, cos, sin, qg, kg, tm, seq, prompt):
    rows, d = x2d.shape
    width = H_RET * HEAD_DIM
    n = w_bf.shape[1]
    grid = (rows // tm,)
    per_seq = seq // tm
    row = lambda i: (i, 0)
    fixed = lambda i: (0, 0)
    pos = lambda i: (i % per_seq, 0)
    if prompt:
        npair = width // LANES
        head_major = pl.BlockSpec((1, H_SB, HEAD_DIM, tm), lambda i: (i // per_seq, 0, 0, i % per_seq))
        hm_shape = jax.ShapeDtypeStruct((rows // seq, H_SB, HEAD_DIM, seq), F32)
        extra_specs = [pl.BlockSpec((tm, width), row),
                       pl.BlockSpec((tm // LANES, npair, LANES, LANES), lambda i: (i, 0, 0, 0)),
                       head_major, head_major]
        extra_shapes = [jax.ShapeDtypeStruct((rows, width), BF16),
                        jax.ShapeDtypeStruct((rows // LANES, npair, LANES, LANES), BF16),
                        hm_shape, hm_shape]
    else:
        extra_specs = [pl.BlockSpec((tm, width), row)] * 2
        extra_shapes = [jax.ShapeDtypeStruct((rows, width), F32)] * 2
    return pl.pallas_call(
        functools.partial(_inproj0_kernel, prompt=prompt),
        grid=grid,
        in_specs=[
            pl.BlockSpec((tm, d), row),
            pl.BlockSpec((1, d), fixed),
            pl.BlockSpec((d, n), fixed),
            pl.BlockSpec((tm, LANES), pos),
            pl.BlockSpec((tm, LANES), pos),
            pl.BlockSpec((1, LANES), fixed),
            pl.BlockSpec((1, LANES), fixed),
        ],
        out_specs=[
            pl.BlockSpec((tm, 4 * width), row),
            pl.BlockSpec((tm, width), row),
            pl.BlockSpec((tm, width), row),
        ] + extra_specs,
        out_shape=[
            jax.ShapeDtypeStruct((rows, 4 * width), F32),
            jax.ShapeDtypeStruct((rows, width), BF16),
            jax.ShapeDtypeStruct((rows, width), F32),
        ] + extra_shapes,
        compiler_params=_params(("parallel",)),
        name="inproj0",
    )(x2d, norm_g.reshape(1, d), w_bf, cos, sin, qg, kg)


def _ret_prompt_kernel(ret_ref, inner_ref, qdec_ref, kdec_ref, cdec_ref, gn_ref, mix_ref, state_ref, s_scr):
    c = pl.program_id(1)
    npair = H_RET // 2
    width = H_RET * HEAD_DIM

    @pl.when(c == 0)
    def _():
        s_scr[...] = jnp.zeros_like(s_scr)

    def tile(group, p):
        return ret_ref[:, group * width + p * LANES:group * width + (p + 1) * LANES]

    lo = _lane_lo((ret_ref.shape[0], LANES))
    same_head = ((lax.broadcasted_iota(I32, (LANES, LANES), 0) < HEAD_DIM)
                 == (lax.broadcasted_iota(I32, (LANES, LANES), 1) < HEAD_DIM))
    qs = [tile(0, p).astype(BF16) for p in range(npair)]
    ks = [tile(1, p) for p in range(npair)]
    kbs = [k.astype(BF16) for k in ks]
    vbs = [tile(2, p).astype(BF16) for p in range(npair)]
    states = [s_scr[p] for p in range(npair)]
    zero = jnp.zeros_like(qs[0])
    att_lo = [_dot_nt(jnp.where(lo, qs[p], zero), kbs[p]) for p in range(npair)]
    att_hi = [_dot_nt(jnp.where(lo, zero, qs[p]), kbs[p]) for p in range(npair)]
    cross = [_dot(qs[p], states[p].astype(BF16)) for p in range(npair)]
    kv = [_dot_tn((ks[p] * kdec_ref[p]).astype(BF16), vbs[p]) for p in range(npair)]
    a_lo = [(att_lo[p] * inner_ref[2 * p]).astype(BF16) for p in range(npair)]
    a_hi = [(att_hi[p] * inner_ref[2 * p + 1]).astype(BF16) for p in range(npair)]
    o_lo = [_dot(a_lo[p], vbs[p]) for p in range(npair)]
    o_hi = [_dot(a_hi[p], vbs[p]) for p in range(npair)]
    for p in range(npair):
        s_new = states[p] * cdec_ref[p] + jnp.where(same_head, kv[p], 0.0)
        s_scr[p] = s_new
        state_ref[0, p] = s_new
        o = jnp.where(lo, o_lo[p], o_hi[p]) + cross[p] * qdec_ref[p]
        mu = _pair_sum(o, lo) * (1.0 / HEAD_DIM)
        oc = o - mu
        var = _pair_sum(oc * oc, lo) * (1.0 / HEAD_DIM)
        on = oc * lax.rsqrt(var + EPS) * gn_ref[p]
        mix_ref[:, p * LANES:(p + 1) * LANES] = on * _silu(tile(3, p))


def _ret_tables(chunk):
    h = jnp.arange(H_RET, dtype=F32)
    lg = jnp.log1p(-jnp.exp2(-5.0 - h))
    idx = jnp.arange(chunk, dtype=F32)
    diff = idx[:, None] - idx[None, :]
    inner = jnp.where(diff >= 0, jnp.exp(lg[:, None, None] * jnp.maximum(diff, 0.0)), 0.0)
    q_dec = jnp.exp(lg[:, None] * (idx + 1.0))
    k_dec = jnp.exp(lg[:, None] * (chunk - 1.0 - idx))
    c_dec = jnp.exp(lg * chunk)

    def lanes(t):
        t = jnp.repeat(t[:, :, None], HEAD_DIM, axis=2)
        return jnp.concatenate([t[0::2], t[1::2]], axis=2)

    cd = jnp.repeat(c_dec[:, None, None], HEAD_DIM, axis=2)
    cd = jnp.concatenate([cd[0::2], cd[1::2]], axis=2)
    return inner, lanes(q_dec), lanes(k_dec), cd


def _ret_prompt(ret, gn_g, batch, seq):
    width = H_RET * HEAD_DIM
    npair = H_RET // 2
    c = RET_CHUNK
    nc = seq // c
    inner, qdec, kdec, cdec = _ret_tables(c)
    gn = gn_g.reshape(npair, 1, LANES)
    tab = lambda b, i: (0, 0, 0)
    return pl.pallas_call(
        _ret_prompt_kernel,
        grid=(batch, nc),
        in_specs=[
            pl.BlockSpec((c, 4 * width), lambda b, i: (b * nc + i, 0)),
            pl.BlockSpec((H_RET, c, c), tab),
            pl.BlockSpec((npair, c, LANES), tab),
            pl.BlockSpec((npair, c, LANES), tab),
            pl.BlockSpec((npair, 1, LANES), tab),
            pl.BlockSpec((npair, 1, LANES), tab),
        ],
        out_specs=[
            pl.BlockSpec((c, width), lambda b, i: (b * nc + i, 0)),
            pl.BlockSpec((1, npair, LANES, LANES), lambda b, i: (b, 0, 0, 0)),
        ],
        out_shape=[
            jax.ShapeDtypeStruct((batch * seq, width), F32),
            jax.ShapeDtypeStruct((batch, npair, LANES, LANES), F32),
        ],
        scratch_shapes=[pltpu.VMEM((npair, LANES, LANES), F32)],
        compiler_params=_params(("parallel", "arbitrary")),
        name="retention_prompt",
    )(ret, inner, qdec, kdec, cdec, gn)


def _unpair_state(sp):
    a = sp[:, :, :HEAD_DIM, :HEAD_DIM]
    b = sp[:, :, HEAD_DIM:, HEAD_DIM:]
    return jnp.stack([a, b], axis=2).reshape(sp.shape[0], -1, HEAD_DIM, HEAD_DIM)


SB_LOG_ZERO = -151.0


def _log2_sigmoid(z2):
    return jnp.minimum(z2, 0.0) - jnp.log2(1.0 + jnp.exp2(-jnp.abs(z2)))
SB_Q_TILE = 256
SB_KEY_TILE = 128


def _sb_prompt_kernel(q_ref, k_ref, vt_ref, g_ref, o_ref, acc_scr):
    i = pl.program_id(1)
    tq = q_ref.shape[0]
    kt = SB_KEY_TILE
    npair = H_SB // 2
    lower = (lax.broadcasted_iota(I32, (kt, kt), 1)
             >= lax.broadcasted_iota(I32, (kt, kt), 0)).astype(BF16)
    k_in = lax.broadcasted_iota(I32, (kt, tq), 0)
    q_pos = i * tq + lax.broadcasted_iota(I32, (kt, tq), 1)
    zeros = jnp.zeros((HEAD_DIM, tq), F32)
    qts = []
    for p in range(npair):
        t = _transpose_tiles(q_ref[:, p * LANES:(p + 1) * LANES].astype(F32))
        qts.append(jnp.concatenate([t[:HEAD_DIM], zeros], axis=0).astype(BF16))
        qts.append(jnp.concatenate([zeros, t[HEAD_DIM:]], axis=0).astype(BF16))
    acc_scr[...] = jnp.zeros_like(acc_scr)

    def sweep(j, rs, masked):
        start = pl.multiple_of(j * kt, kt)
        diag_mask = (j * kt + k_in < q_pos) if masked else None
        kjs = [k_ref[pl.ds(start, kt), p * LANES:(p + 1) * LANES] for p in range(npair)]
        vts = [vt_ref[j, p] for p in range(npair)]
        zs = [_dot(kjs[h // 2], qt) for h, qt in enumerate(qts)]
        ls_pos, lms, parts = [], [], []
        for z in zs:
            lsp = _log2_sigmoid(z)
            lm = lsp - z
            if diag_mask is not None:
                lm = jnp.where(diag_mask, lm, 0.0)
            ls_pos.append(lsp)
            lms.append(lm)
            parts.append(_split_bf16(lm))
        s_incl = [_dot(lower, hi) + _dot(lower, lo_part) for hi, lo_part in parts]
        probs = []
        for h in range(H_SB):
            a = jnp.exp2(ls_pos[h] + (s_incl[h] - lms[h]) + rs[h])
            if diag_mask is not None:
                a = jnp.where(diag_mask, a, 0.0)
            probs.append(a.astype(BF16))
        for h in range(H_SB):
            half = h % 2
            acc_scr[h] = acc_scr[h] + _dot(vts[h // 2][half * HEAD_DIM:(half + 1) * HEAD_DIM, :], probs[h])
        return [rs[h] + s_incl[h][0:1, :] for h in range(H_SB)]

    def alive_of(rs):
        worst = rs[0]
        for r in rs[1:]:
            worst = jnp.maximum(worst, r)
        return jnp.max(worst) > SB_LOG_ZERO

    n_diag = tq // kt
    j_top = (i + 1) * n_diag - 1
    rs = [jnp.zeros((1, tq), F32)] * H_SB
    for d in range(n_diag):
        rs = sweep(j_top - d, rs, True)
    n_old = i * n_diag

    def cond(carry):
        t, alive = carry[0], carry[1]
        return (t < n_old) & alive

    def body(carry):
        t = carry[0]
        rs = sweep(n_old - 1 - t, list(carry[2:]), False)
        return (t + 1, alive_of(rs)) + tuple(rs)

    lax.while_loop(cond, body, (jnp.int32(0), alive_of(rs)) + tuple(rs))
    for p in range(npair):
        both = _transpose_tiles(jnp.concatenate([acc_scr[2 * p], acc_scr[2 * p + 1]], axis=0))
        o_ref[:, p * LANES:(p + 1) * LANES] = both * g_ref[:, p * LANES:(p + 1) * LANES]


def _sb_prompt(sbq, sbk_b, sbvt, sbg, batch, seq):
    width = H_SB * HEAD_DIM
    npair = H_SB // 2
    tq = SB_Q_TILE
    kt = SB_KEY_TILE
    nq = seq // tq
    qmap = lambda b, i: (b * nq + i, 0)
    return pl.pallas_call(
        _sb_prompt_kernel,
        grid=(batch, nq),
        in_specs=[
            pl.BlockSpec((tq, width), qmap),
            pl.BlockSpec((seq, width), lambda b, i: (b, 0)),
            pl.BlockSpec((seq // kt, npair, LANES, kt), lambda b, i: (b, 0, 0, 0)),
            pl.BlockSpec((tq, width), qmap),
        ],
        out_specs=pl.BlockSpec((tq, width), qmap),
        out_shape=jax.ShapeDtypeStruct((batch * seq, width), F32),
        scratch_shapes=[pltpu.VMEM((H_SB, HEAD_DIM, tq), F32)],
        compiler_params=_params(("parallel", "arbitrary")),
        name="sb_prompt",
    )(sbq, sbk_b, sbvt, sbg)


def _outproj_kernel(*refs, n_mix):
    x_ref = refs[0]
    mix_refs = refs[1:1 + n_mix]
    w_refs = refs[1 + n_mix:1 + 2 * n_mix]
    y_ref = refs[1 + 2 * n_mix]
    acc = x_ref[...]
    for m_ref, w_ref in zip(mix_refs, w_refs):
        acc = acc + _dot(m_ref[...].astype(BF16), w_ref[...])
    y_ref[...] = acc


def _outproj(x2d, mixes, ws, tm):
    rows, d = x2d.shape
    n_mix = len(mixes)
    row = lambda i: (i, 0)
    fixed = lambda i: (0, 0)
    in_specs = [pl.BlockSpec((tm, d), row)]
    in_specs += [pl.BlockSpec((tm, m.shape[1]), row) for m in mixes]
    in_specs += [pl.BlockSpec(w.shape, fixed) for w in ws]
    return pl.pallas_call(
        functools.partial(_outproj_kernel, n_mix=n_mix),
        grid=(rows // tm,),
        in_specs=in_specs,
        out_specs=pl.BlockSpec((tm, d), row),
        out_shape=jax.ShapeDtypeStruct((rows, d), F32),
        compiler_params=_params(("parallel",)),
        name="outproj",
    )(x2d, *mixes, *ws)


def _rope_tables(pos):
    half = HEAD_DIM // 2
    inv = ROPE_BASE ** (-jnp.arange(half, dtype=F32) / half)
    ang = pos.astype(F32)[:, None] * inv[None, :]
    cos, sin = jnp.cos(ang), jnp.sin(ang)
    cos128 = jnp.tile(cos, (1, LANES // half))
    sin128 = jnp.tile(jnp.concatenate([-sin, sin], axis=1), (1, LANES // HEAD_DIM))
    return cos128, sin128


def _pair_gain(g):
    return jnp.tile(g.astype(F32), LANES // HEAD_DIM).reshape(1, LANES)


def _layer0_prompt(x2d, batch, seq, norm_g, w_in_bf, ret_gn_g, sb_qn_g, sb_kn_g, w_out_bf, tm):
    cos, sin = _rope_tables(jnp.arange(seq, dtype=I32))
    ret, sbq, sbg, sbk_b, sbvt, sbk_hm, sbv_hm = _inproj0(x2d, norm_g, w_in_bf, cos, sin, _pair_gain(sb_qn_g),
                                                          _pair_gain(sb_kn_g), tm, seq, True)
    mix_ret, state_pairs = _ret_prompt(ret, ret_gn_g, batch, seq)
    mix_sb = _sb_prompt(sbq, sbk_b, sbvt, sbg, batch, seq)
    half = w_out_bf.shape[0] // 2
    y = _outproj(x2d, [mix_ret, mix_sb], [w_out_bf[:half], w_out_bf[half:]], tm)
    return y, _unpair_state(state_pairs), sbk_hm, sbv_hm


NSA_W = H_NSA * HEAD_DIM
KV_W = G_NSA * HEAD_DIM
IN1_COLS = 2 * NSA_W + 6 * KV_W + LANES


def _inproj1_kernel(x_ref, g_ref, w_ref, qg_ref, skg_ref, wkg_ref, q_ref, gate_ref, bg_ref, *kv_refs, prompt):
    x = x_ref[...]
    xn = x * lax.rsqrt(jnp.mean(x * x, axis=-1, keepdims=True) + EPS) * g_ref[...]
    xb = xn.astype(BF16)
    lo = _lane_lo((1, LANES))
    a = _dot(xb, w_ref[:, 0:NSA_W])
    for t in range(NSA_W // LANES):
        v = a[:, t * LANES:(t + 1) * LANES]
        q_ref[:, t * LANES:(t + 1) * LANES] = (_pair_rmsnorm(v, qg_ref[...], lo) * (SCALE * LOG2E)).astype(BF16)
    a = _dot(xb, w_ref[:, NSA_W:NSA_W + 6 * KV_W])
    kc, vc, ks, vs, kw, vw = [a[:, t * KV_W:(t + 1) * KV_W] for t in range(6)]
    ks = _pair_rmsnorm(ks, skg_ref[...], lo)
    kw = _pair_rmsnorm(kw, wkg_ref[...], lo)
    six = (kc, vc, ks, vs, kw, vw)
    if prompt:
        t_out = kv_refs[:6]
        kc_ref, vc_ref, ksb_ref, kwb_ref, vst_ref, vwt_ref = kv_refs[6:]
        kc_ref[...] = kc
        vc_ref[...] = vc
        ksb_ref[...] = ks.astype(BF16)
        kwb_ref[...] = kw.astype(BF16)
        per_tile = NSA_KEY_TILE // LANES
        for r in range(x.shape[0] // LANES):
            rows = slice(r * LANES, (r + 1) * LANES)
            cols = slice((r % per_tile) * LANES, (r % per_tile + 1) * LANES)
            for src, dst in zip(six, t_out):
                t = src[rows].T
                dst[0, :, rows] = t
                if src is vs:
                    vst_ref[r // per_tile, :, cols] = t.astype(BF16)
                if src is vw:
                    vwt_ref[r // per_tile, :, cols] = t.astype(BF16)
    else:
        for src, dst in zip(six, kv_refs):
            dst[...] = src
    off = NSA_W + 6 * KV_W
    gate_ref[...] = _silu(_dot(xb, w_ref[:, off:off + NSA_W]))
    bg_ref[...] = _sigmoid(_dot(xb, w_ref[:, off + NSA_W:off + NSA_W + LANES]))


def _inproj1(x2d, norm_g, w_bf, qg, skg, wkg, tm, seq, prompt):
    rows, d = x2d.shape
    row = lambda i: (i, 0)
    fixed = lambda i: (0, 0)
    out_specs = [pl.BlockSpec((tm, NSA_W), row), pl.BlockSpec((tm, NSA_W), row), pl.BlockSpec((tm, LANES), row)]
    out_shape = [jax.ShapeDtypeStruct((rows, NSA_W), BF16), jax.ShapeDtypeStruct((rows, NSA_W), F32),
                 jax.ShapeDtypeStruct((rows, LANES), F32)]
    if prompt:
        per_seq = seq // tm
        kt = NSA_KEY_TILE
        out_specs += ([pl.BlockSpec((1, KV_W, tm), lambda i: (i // per_seq, 0, i % per_seq))] * 6
                      + [pl.BlockSpec((tm, KV_W), row)] * 4
                      + [pl.BlockSpec((tm // kt, LANES, kt), lambda i: (i, 0, 0))] * 2)
        out_shape += ([jax.ShapeDtypeStruct((rows // seq, KV_W, seq), F32)] * 6
                      + [jax.ShapeDtypeStruct((rows, KV_W), F32)] * 2
                      + [jax.ShapeDtypeStruct((rows, KV_W), BF16)] * 2
                      + [jax.ShapeDtypeStruct((rows // kt, LANES, kt), BF16)] * 2)
    else:
        out_specs += [pl.BlockSpec((tm, KV_W), row)] * 6
        out_shape += [jax.ShapeDtypeStruct((rows, KV_W), F32)] * 6
    return pl.pallas_call(
        functools.partial(_inproj1_kernel, prompt=prompt),
        grid=(rows // tm,),
        in_specs=[
            pl.BlockSpec((tm, d), row),
            pl.BlockSpec((1, d), fixed),
            pl.BlockSpec((d, IN1_COLS), fixed),
            pl.BlockSpec((1, LANES), fixed),
            pl.BlockSpec((1, LANES), fixed),
            pl.BlockSpec((1, LANES), fixed),
        ],
        out_specs=out_specs,
        out_shape=out_shape,
        compiler_params=_params(("parallel",)),
        name="inproj1",
    )(x2d, norm_g.reshape(1, d), w_bf, qg, skg, wkg)


def _pad_w_in1(w_in1):
    pad = IN1_COLS - w_in1.shape[1]
    return jnp.pad(w_in1, ((0, 0), (0, pad))).astype(BF16)


def _compress_rows(t_ref, pe_ref, w_ref, row0, nblk):
    acc = jnp.zeros((nblk, LANES), F32)
    for l in range(CMP_BLOCK):
        rows = t_ref[pl.ds(row0 + l, nblk, stride=CMP_BLOCK), :]
        acc = acc + _dot((rows + pe_ref[l:l + 1, :]).astype(BF16), w_ref[l])
    return acc


def _compress_prompt_kernel(kc_ref, vc_ref, pe_ref, wk_ref, wv_ref, kg_ref, ck_ref, cvt_ref):
    nblk = ck_ref.shape[0]
    lo = _lane_lo((1, LANES))
    ck = _pair_rmsnorm(_compress_rows(kc_ref, pe_ref, wk_ref, 0, nblk), kg_ref[...], lo)
    cv = _compress_rows(vc_ref, pe_ref, wv_ref, 0, nblk)
    ck_ref[...] = ck.astype(BF16)
    cvt_ref[0] = cv.T.astype(BF16)


def _blockdiag_w(w):
    z = jnp.zeros_like(w)
    top = jnp.concatenate([w, z], axis=2)
    bot = jnp.concatenate([z, w], axis=2)
    return jnp.concatenate([top, bot], axis=1).astype(BF16)


def _compress_prompt(kc, vc, pe128, wk_bd, wv_bd, kg, batch, seq):
    nblk = seq // CMP_BLOCK
    rowb = lambda b: (b, 0)
    fixed2 = lambda b: (0, 0)
    fixed3 = lambda b: (0, 0, 0)
    return pl.pallas_call(
        _compress_prompt_kernel,
        grid=(batch,),
        in_specs=[
            pl.BlockSpec((seq, LANES), rowb),
            pl.BlockSpec((seq, LANES), rowb),
            pl.BlockSpec((CMP_BLOCK, LANES), fixed2),
            pl.BlockSpec((CMP_BLOCK, LANES, LANES), fixed3),
            pl.BlockSpec((CMP_BLOCK, LANES, LANES), fixed3),
            pl.BlockSpec((1, LANES), fixed2),
        ],
        out_specs=[pl.BlockSpec((nblk, LANES), rowb), pl.BlockSpec((1, LANES, nblk), lambda b: (b, 0, 0))],
        out_shape=[jax.ShapeDtypeStruct((batch * nblk, LANES), BF16),
                   jax.ShapeDtypeStruct((batch, LANES, nblk), BF16)],
        compiler_params=_params(("parallel",)),
        name="compress_prompt",
    )(kc, vc, pe128, wk_bd, wv_bd, kg)


def _nsa_prompt_kernel(q_ref, ck_ref, cvt_ref, ks_ref, vst_ref, kw_ref, vwt_ref, bg_ref, gate_ref,
                       o_ref, m_scr, acc_scr, imp_scr, score_scr, chosen_scr, sc_scr):
    i = pl.program_id(1)
    tq = q_ref.shape[0]
    nb = ck_ref.shape[0]
    kt = NSA_KEY_TILE
    ratio = SEL_BLOCK // CMP_BLOCK
    nsb = nb // ratio
    q_pos = i * tq + lax.broadcasted_iota(I32, (1, tq), 1)

    zeros = jnp.zeros((HEAD_DIM, tq), F32)
    qts = []
    for p in range(H_NSA // 2):
        t = _transpose_tiles(q_ref[:, p * LANES:(p + 1) * LANES].astype(F32))
        for ht in (t[:HEAD_DIM], t[HEAD_DIM:]):
            parts = [zeros] * G_NSA
            parts[(2 * p) // HPG] = ht
            qts.append(jnp.concatenate(parts, axis=0).astype(BF16))
    group_of = [h // HPG for h in range(H_NSA)]

    ck = ck_ref[...]
    blk_n = lax.broadcasted_iota(I32, (nb, tq), 0)
    cmask = blk_n * CMP_BLOCK + (CMP_BLOCK - 1) <= q_pos
    cmp_scores = [_dot(ck, qt) for qt in qts]
    cmp_probs = []
    imps = [jnp.zeros((nb, tq), F32) for _ in range(G_NSA)]
    for h, s in enumerate(cmp_scores):
        s = jnp.where(cmask, s, NEG_INF)
        m = jnp.max(s, axis=0, keepdims=True)
        e = jnp.where(cmask, jnp.exp2(s - m), 0.0)
        den = jnp.sum(e, axis=0, keepdims=True)
        pr = e / jnp.where(den > 0.0, den, 1.0)
        imps[group_of[h]] = imps[group_of[h]] + pr
        cmp_probs.append(pr.astype(BF16))
    o_cmp = [_dot(cvt_ref[0, group_of[h] * HEAD_DIM:(group_of[h] + 1) * HEAD_DIM, :], pr)
             for h, pr in enumerate(cmp_probs)]

    sblk = lax.broadcasted_iota(I32, (nsb, tq), 0)
    cur = q_pos // SEL_BLOCK
    forced = (sblk == 0) | (sblk == cur) | (sblk == cur - 1)
    for g in range(G_NSA):
        pooled_parts = []
        for c in range(tq // LANES):
            part_ref = imp_scr.at[c]
            part_ref[...] = imps[g][:, c * LANES:(c + 1) * LANES]
            part = part_ref[pl.ds(0, nsb, stride=ratio), :]
            for r in range(1, ratio):
                part = part + part_ref[pl.ds(r, nsb, stride=ratio), :]
            pooled_parts.append(part)
        pooled = jnp.concatenate(pooled_parts, axis=1)
        score = jnp.where(sblk <= cur, pooled + FORCE_BONUS * forced.astype(F32), NEG_INF)
        score_scr[...] = score
        slab = 8
        cnts = []
        for r0 in range(0, nsb, slab):
            rows = score_scr[r0:min(r0 + slab, nsb), :]
            idx = r0 + lax.broadcasted_iota(I32, rows.shape, 0)
            cnt = jnp.zeros(rows.shape, F32)
            for mblk in range(nsb):
                c = score_scr[mblk:mblk + 1, :]
                if mblk < r0:
                    ahead = c >= rows
                elif mblk >= r0 + rows.shape[0]:
                    ahead = c > rows
                else:
                    ahead = (c > rows) | ((c == rows) & (mblk < idx))
                cnt = cnt + jnp.where(ahead, 1.0, 0.0)
            cnts.append(cnt)
        chosen = (jnp.concatenate(cnts, axis=0) < float(TOP_N)).astype(F32)
        for mblk in range(nsb):
            chosen_scr[g, mblk] = jnp.broadcast_to(chosen[mblk:mblk + 1, :], (8, tq))

    k_in = lax.broadcasted_iota(I32, (kt, tq), 0)
    blocks_per_tile = kt // SEL_BLOCK
    j_diag = ((i + 1) * tq - 1) // kt

    ones_rows = jnp.ones((acc_scr.shape[2] - HEAD_DIM, kt), BF16)

    heads_a = list(range(HPG))
    heads_b = list(range(HPG, H_NSA))

    def flash(k_ref, vt_ref, slot, n_tiles, mask_fn):
        m_scr[slot] = jnp.full(m_scr.shape[1:], NEG_INF, F32)
        acc_scr[slot] = jnp.zeros(acc_scr.shape[1:], F32)

        def load_k(j):
            return k_ref[pl.ds(pl.multiple_of(j * kt, kt), kt), :]

        def softmax_step(h, s, mask):
            s = jnp.where(mask, s, NEG_INF)
            m_old = m_scr[slot, h]
            m_new = jnp.maximum(m_old, jnp.max(s, axis=0, keepdims=True))
            m_scr[slot, h] = m_new
            return jnp.exp2(m_old - m_new)[0:1], jnp.exp2(s - m_new[0:1]).astype(BF16)

        def accumulate(h, alpha, pv):
            acc_scr[slot, h] = alpha * acc_scr[slot, h] + pv

        k_first = load_k(j_diag)
        for n, h in enumerate(heads_b):
            sc_scr[n] = _dot(k_first, qts[h])

        def body(t, carry):
            j = j_diag - t
            kj = load_k(j)
            k_next = load_k(jnp.maximum(j - 1, 0))
            vt = vt_ref[j]
            vts = [jnp.concatenate([vt[g * HEAD_DIM:(g + 1) * HEAD_DIM, :], ones_rows], axis=0)
                   for g in range(G_NSA)]
            scores_a = [_dot(kj, qts[h]) for h in heads_a]
            mask_b = mask_fn(1, j)
            soft_b = [softmax_step(h, sc_scr[n], mask_b) for n, h in enumerate(heads_b)]
            pv_b = [_dot(vts[1], p) for _, p in soft_b]
            next_b = [_dot(k_next, qts[h]) for h in heads_b]
            mask_a = mask_fn(0, j)
            soft_a = [softmax_step(h, scores_a[n], mask_a) for n, h in enumerate(heads_a)]
            pv_a = [_dot(vts[0], p) for _, p in soft_a]
            for n, h in enumerate(heads_b):
                accumulate(h, soft_b[n][0], pv_b[n])
                sc_scr[n] = next_b[n]
            for n, h in enumerate(heads_a):
                accumulate(h, soft_a[n][0], pv_a[n])
            return carry

        lax.fori_loop(0, n_tiles, body, 0)
        outs = []
        for h in range(H_NSA):
            acc = acc_scr[slot, h]
            outs.append(acc[:HEAD_DIM] / acc[HEAD_DIM:HEAD_DIM + 1])
        return outs

    def causal(j):
        return j * kt + k_in <= q_pos

    def sel_mask(g, j):
        rows = [jnp.broadcast_to(chosen_scr[g, blocks_per_tile * j + r][0:1], (SEL_BLOCK, tq))
                for r in range(blocks_per_tile)]
        return (jnp.concatenate(rows, axis=0) > 0.5) & causal(j)

    def win_mask(g, j):
        return causal(j) & (j * kt + k_in > q_pos - WINDOW)

    o_sel = flash(ks_ref, vst_ref, 0, j_diag + 1, sel_mask)
    j_low = jnp.maximum(i * tq - (WINDOW - 1), 0) // kt
    o_win = flash(kw_ref, vwt_ref, 1, j_diag - j_low + 1, win_mask)

    bgt = _transpose_tiles(bg_ref[...])

    def gate_row(h, br):
        c = h * 3 + br
        return bgt[c:c + 1, :]

    for p in range(H_NSA // 2):
        halves = []
        for h in (2 * p, 2 * p + 1):
            halves.append(gate_row(h, 0) * o_cmp[h] + gate_row(h, 1) * o_sel[h] + gate_row(h, 2) * o_win[h])
        mixed = _transpose_tiles(jnp.concatenate(halves, axis=0))
        o_ref[:, p * LANES:(p + 1) * LANES] = mixed * gate_ref[:, p * LANES:(p + 1) * LANES]


def _nsa_prompt(qn, ck, cvt, ksb, vst, kwb, vwt, bg, gate, batch, seq):
    tq = NSA_Q_TILE
    kt = NSA_KEY_TILE
    nq = seq // tq
    nb = seq // CMP_BLOCK
    nsb = nb // (SEL_BLOCK // CMP_BLOCK)
    qmap = lambda b, i: (b * nq + i, 0)
    per_b2 = lambda b, i: (b, 0)
    per_b3 = lambda b, i: (b, 0, 0)
    return pl.pallas_call(
        _nsa_prompt_kernel,
        grid=(batch, nq),
        in_specs=[
            pl.BlockSpec((tq, NSA_W), qmap),
            pl.BlockSpec((nb, LANES), per_b2),
            pl.BlockSpec((1, LANES, nb), per_b3),
            pl.BlockSpec((seq, LANES), per_b2),
            pl.BlockSpec((seq // kt, LANES, kt), per_b3),
            pl.BlockSpec((seq, LANES), per_b2),
            pl.BlockSpec((seq // kt, LANES, kt), per_b3),
            pl.BlockSpec((tq, LANES), qmap),
            pl.BlockSpec((tq, NSA_W), qmap),
        ],
        out_specs=pl.BlockSpec((tq, NSA_W), qmap),
        out_shape=jax.ShapeDtypeStruct((batch * seq, NSA_W), F32),
        scratch_shapes=[
            pltpu.VMEM((2, H_NSA, 8, tq), F32),
            pltpu.VMEM((2, H_NSA, HEAD_DIM + 16, tq), F32),
            pltpu.VMEM((tq // LANES, nb, LANES), F32),
            pltpu.VMEM((nsb, tq), F32),
            pltpu.VMEM((G_NSA, nsb, 8, tq), F32),
            pltpu.VMEM((HPG, kt, tq), F32),
        ],
        compiler_params=_params(("parallel", "arbitrary")),
        name="nsa_prompt",
    )(qn, ck, cvt, ksb, vst, kwb, vwt, bg, gate)


def _layer1_prompt(x2d, batch, seq, norm_g, w_in_bf, qn_g, cmp_kn_g, sel_kn_g, win_kn_g, pe128, wk_bd, wv_bd,
                   w_out_bf, tm):
    (qn, gate, bg, kc_t, vc_t, ks_t, vs_t, kw_t, vw_t, kc, vc, ksb, kwb, vst, vwt) = _inproj1(
        x2d, norm_g, w_in_bf, _pair_gain(qn_g), _pair_gain(sel_kn_g), _pair_gain(win_kn_g), tm, seq, True)
    ck, cvt = _compress_prompt(kc, vc, pe128, wk_bd, wv_bd, _pair_gain(cmp_kn_g), batch, seq)
    mix = _nsa_prompt(qn, ck, cvt, ksb, vst, kwb, vwt, bg, gate, batch, seq)
    y = _outproj(x2d, [mix], [w_out_bf], tm)
    return y, kc_t, vc_t, ks_t, vs_t, kw_t, vw_t


def _ret_sample_kernel(ret_ref, s_ref, qdec_ref, cdec_ref, gn_ref, mix_ref, so_ref):
    width = H_RET * HEAD_DIM
    rowi = lax.broadcasted_iota(I32, (8, HEAD_DIM), 0)
    eye = (lax.broadcasted_iota(I32, (HEAD_DIM, HEAD_DIM), 0)
           == lax.broadcasted_iota(I32, (HEAD_DIM, HEAD_DIM), 1)).astype(BF16)
    row = ret_ref[0]
    for h in range(H_RET):
        sl = slice(h * HEAD_DIM, (h + 1) * HEAD_DIM)
        qb = row[:, sl].astype(BF16)
        kb = row[:, width + h * HEAD_DIM:width + (h + 1) * HEAD_DIM].astype(BF16)
        vb = row[:, 2 * width + h * HEAD_DIM:2 * width + (h + 1) * HEAD_DIM].astype(BF16)
        gate = row[:, 3 * width + h * HEAD_DIM:3 * width + (h + 1) * HEAD_DIM]
        s = s_ref[0, h]
        qs = _dot(jnp.broadcast_to(qb.astype(F32), (8, HEAD_DIM)).astype(BF16), s.astype(BF16))[0:1]
        att = jnp.sum(qb.astype(F32) * kb.astype(F32), axis=-1, keepdims=True)
        o = att.astype(BF16).astype(F32) * vb.astype(F32) + qs * qdec_ref[h:h + 1, :]
        k8 = jnp.where(rowi == 0, jnp.broadcast_to(kb.astype(F32), (8, HEAD_DIM)), 0.0).astype(BF16)
        kcol = _dot_nt(eye, k8)[:, 0:1]
        so_ref[0, h] = s * cdec_ref[h:h + 1, :] + kcol * vb.astype(F32)
        mu = jnp.mean(o, axis=-1, keepdims=True)
        oc = o - mu
        var = jnp.mean(oc * oc, axis=-1, keepdims=True)
        on = oc * lax.rsqrt(var + EPS) * gn_ref[h:h + 1, :]
        mix_ref[0, :, sl] = on * _silu(gate)


def _ret_sample(ret, state, gn_g):
    nb, cols = ret.shape
    width = H_RET * HEAD_DIM
    h = jnp.arange(H_RET, dtype=F32)
    lg = jnp.log1p(-jnp.exp2(-5.0 - h))
    qdec = jnp.broadcast_to(jnp.exp(lg * 1.0)[:, None], (H_RET, HEAD_DIM))
    cdec = jnp.broadcast_to(jnp.exp(lg * 1.0)[:, None], (H_RET, HEAD_DIM))
    m3 = lambda b: (b, 0, 0)
    tab = pl.BlockSpec((H_RET, HEAD_DIM), lambda b: (0, 0))
    st = pl.BlockSpec((1, H_RET, HEAD_DIM, HEAD_DIM), lambda b: (b, 0, 0, 0))
    mix, state_new = pl.pallas_call(
        _ret_sample_kernel,
        grid=(nb,),
        in_specs=[pl.BlockSpec((1, 1, cols), m3), st, tab, tab, tab],
        out_specs=[pl.BlockSpec((1, 1, width), m3), st],
        out_shape=[jax.ShapeDtypeStruct((nb, 1, width), F32),
                   jax.ShapeDtypeStruct(state.shape, F32)],
        compiler_params=_params(("parallel",)),
        name="retention_sample",
    )(ret.reshape(nb, 1, cols), state, qdec, cdec, gn_g.astype(F32))
    return mix.reshape(nb, width), state_new


def _sb_sample_kernel(pt_ref, q_ref, g_ref, k_hbm, v_hbm, o_ref, kbuf, vbuf, sem):
    b = pl.program_id(0)
    n_pages = pt_ref.shape[1]
    page = PAGE_SIZE

    def copies(sample, t, slot):
        pid = pt_ref[sample, n_pages - 1 - t]
        return (pltpu.make_async_copy(k_hbm.at[pid], kbuf.at[slot], sem.at[0, slot]),
                pltpu.make_async_copy(v_hbm.at[pid], vbuf.at[slot], sem.at[1, slot]))

    def start(t, slot, sample=b):
        for c in copies(sample, t, slot):
            c.start()

    def wait(t, slot):
        for c in copies(b, t, slot):
            c.wait()

    q = q_ref[0].astype(F32)
    rowq = lax.broadcasted_iota(I32, (H_SB, HEAD_DIM), 0)
    rowa = lax.broadcasted_iota(I32, (H_SB, page), 0)
    upper = (lax.broadcasted_iota(I32, (page, page), 0) >= lax.broadcasted_iota(I32, (page, page), 1)).astype(BF16)
    q_rows = [jnp.where(rowq == h, q, 0.0).astype(BF16) for h in range(H_SB)]

    @pl.when(b == 0)
    def _():
        start(0, 0)

    def cond(carry):
        t, alive = carry[0], carry[1]
        return (t < n_pages) & alive

    def body(carry):
        t, _, r, acc = carry
        slot = t % 2
        wait(t, slot)

        @pl.when(t + 1 < n_pages)
        def _():
            start(t + 1, 1 - slot)

        z = jnp.zeros((H_SB, page), F32)
        for h in range(H_SB):
            z = z + _dot(q_rows[h], kbuf[slot, h].astype(BF16))
        ls_pos = _log2_sigmoid(z)
        lm = ls_pos - z
        hi, lo_part = _split_bf16(lm)
        s_incl = _dot(hi, upper) + _dot(lo_part, upper)
        a = jnp.exp2(ls_pos + (s_incl - lm) + r)
        for h in range(H_SB):
            acc = acc + _dot_nt(jnp.where(rowa == h, a, 0.0).astype(BF16), vbuf[slot, h].astype(BF16))
        r = r + s_incl[:, 0:1]
        return t + 1, jnp.max(r) > SB_LOG_ZERO, r, acc

    init = (jnp.int32(0), jnp.bool_(True), jnp.zeros((H_SB, 1), F32), jnp.zeros((H_SB, HEAD_DIM), F32))
    t_end, _, _, acc = lax.while_loop(cond, body, init)

    @pl.when(t_end < n_pages)
    def _():
        wait(t_end, t_end % 2)

    @pl.when(b + 1 < pl.num_programs(0))
    def _():
        start(0, 0, sample=b + 1)

    o_ref[0] = acc * g_ref[0]


def _sb_sample(q8, g8, cache_kt, cache_vt, page_table):
    nb = q8.shape[0]
    head = lambda b, pt: (b, 0, 0)
    grid_spec = pltpu.PrefetchScalarGridSpec(
        num_scalar_prefetch=1,
        grid=(nb,),
        in_specs=[pl.BlockSpec((1, H_SB, HEAD_DIM), head), pl.BlockSpec((1, H_SB, HEAD_DIM), head),
                  pl.BlockSpec(memory_space=pl.ANY), pl.BlockSpec(memory_space=pl.ANY)],
        out_specs=pl.BlockSpec((1, H_SB, HEAD_DIM), head),
        scratch_shapes=[pltpu.VMEM((2, H_SB, HEAD_DIM, PAGE_SIZE), F32),
                        pltpu.VMEM((2, H_SB, HEAD_DIM, PAGE_SIZE), F32),
                        pltpu.SemaphoreType.DMA((2, 2))],
    )
    return pl.pallas_call(
        _sb_sample_kernel,
        grid_spec=grid_spec,
        out_shape=jax.ShapeDtypeStruct((nb, H_SB, HEAD_DIM), F32),
        compiler_params=_params(("arbitrary",)),
        name="sb_sample",
    )(page_table, q8, g8, cache_kt, cache_vt)


CMP_PAGES_PER_STEP = 16


def _regroup_matrix():
    r = np.zeros((2 * PAGE_SIZE, 2 * PAGE_SIZE), np.float32)
    per_page = PAGE_SIZE // CMP_BLOCK
    for l in range(CMP_BLOCK):
        for p in range(2):
            for n in range(per_page):
                r[l * 2 * per_page + p * per_page + n, p * PAGE_SIZE + CMP_BLOCK * n + l] = 1.0
    return jnp.asarray(r, BF16)


def _compress_sample_kernel(pt_ref, *refs, n_pp):
    k_refs = refs[:n_pp]
    v_refs = refs[n_pp:2 * n_pp]
    pe_ref, wk_ref, wv_ref, kg_ref, rg_ref, ck_ref, cv_ref, kbuf, vbuf = refs[2 * n_pp:]
    j = pl.program_id(1)
    per_pair = 2 * PAGE_SIZE // CMP_BLOCK
    regroup = rg_ref[...]
    pe_t = pe_ref[...]
    for src_refs, buf in ((k_refs, kbuf), (v_refs, vbuf)):
        for a in range(n_pp // 2):
            both = jnp.concatenate([src_refs[2 * a][0], src_refs[2 * a + 1][0]], axis=1) + pe_t
            rows = _dot_nt(regroup, both.astype(BF16))
            for l in range(CMP_BLOCK):
                buf[l, a * per_pair:(a + 1) * per_pair, :] = rows[l * per_pair:(l + 1) * per_pair]
    nblk = n_pp * PAGE_SIZE // CMP_BLOCK

    def compress(buf, w_ref):
        acc = jnp.zeros((nblk, LANES), F32)
        for l in range(CMP_BLOCK):
            acc = acc + _dot(buf[l].astype(BF16), w_ref[l])
        return acc

    lo = _lane_lo((1, LANES))
    start = pl.multiple_of(j * nblk, nblk)
    ck_ref[0, pl.ds(start, nblk), :] = _pair_rmsnorm(compress(kbuf, wk_ref), kg_ref[...], lo)
    cv_ref[0, pl.ds(start, nblk), :] = compress(vbuf, wv_ref)


def _compress_sample(cache_k, cache_v, page_table, pe128, wk_bd, wv_bd, kg):
    nb, n_pages = page_table.shape
    n_pp = CMP_PAGES_PER_STEP
    nblk_total = n_pages * PAGE_SIZE // CMP_BLOCK
    pe_t = jnp.tile(pe128.T, (1, 2 * PAGE_SIZE // CMP_BLOCK))

    def page_map(r):
        return lambda b, j, pt: (pt[b, j * n_pp + r], 0, 0)

    fixed2 = lambda b, j, pt: (0, 0)
    fixed3 = lambda b, j, pt: (0, 0, 0)
    out_map = lambda b, j, pt: (b, 0, 0)
    grid_spec = pltpu.PrefetchScalarGridSpec(
        num_scalar_prefetch=1,
        grid=(nb, n_pages // n_pp),
        in_specs=[pl.BlockSpec((1, PAGE_SIZE, LANES), page_map(r)) for r in range(n_pp)] * 2
        + [pl.BlockSpec((LANES, 2 * PAGE_SIZE), fixed2),
           pl.BlockSpec((CMP_BLOCK, LANES, LANES), fixed3),
           pl.BlockSpec((CMP_BLOCK, LANES, LANES), fixed3),
           pl.BlockSpec((1, LANES), fixed2),
           pl.BlockSpec((2 * PAGE_SIZE, 2 * PAGE_SIZE), fixed2)],
        out_specs=[pl.BlockSpec((1, nblk_total, LANES), out_map)] * 2,
        scratch_shapes=[pltpu.VMEM((CMP_BLOCK, n_pp * PAGE_SIZE // CMP_BLOCK, LANES), F32)] * 2,
    )
    o = jax.ShapeDtypeStruct((nb, nblk_total, LANES), F32)
    return pl.pallas_call(
        functools.partial(_compress_sample_kernel, n_pp=n_pp),
        grid_spec=grid_spec,
        out_shape=[o, o],
        compiler_params=_params(("parallel", "arbitrary")),
        name="compress_sample",
    )(page_table, *([cache_k] * n_pp), *([cache_v] * n_pp), pe_t, wk_bd, wv_bd, kg, _regroup_matrix())


def _pages_t(cache):
    return cache.transpose(0, 2, 3, 1)


def _group_q(q, g):
    z = jnp.zeros_like(q)
    return jnp.concatenate([q, z], axis=1) if g == 0 else jnp.concatenate([z, q], axis=1)


def _nsa_sample_cmp_kernel(q_ref, ck_ref, cv_ref, ocmp_ref, score_ref, *, q_pos):
    q = q_ref[0]
    ck = ck_ref[0].astype(BF16)
    cv = cv_ref[0].astype(BF16)
    nb = ck.shape[0]
    blk_n = lax.broadcasted_iota(I32, (1, nb), 1)
    cmask = blk_n * CMP_BLOCK + (CMP_BLOCK - 1) <= q_pos
    sblk = blk_n // (SEL_BLOCK // CMP_BLOCK)
    cur = q_pos // SEL_BLOCK
    row8 = lax.broadcasted_iota(I32, (8, nb), 0)
    neg_inf = float("-inf")
    scores = jnp.full((8, nb), neg_inf, F32)
    for g in range(G_NSA):
        qg = _group_q(q[g * HPG:(g + 1) * HPG], g)
        s = jnp.where(cmask, _dot_nt(qg, ck), NEG_INF)
        m = jnp.max(s, axis=-1, keepdims=True)
        e = jnp.where(cmask, jnp.exp2(s - m), 0.0)
        den = jnp.sum(e, axis=-1, keepdims=True)
        pr = e / jnp.where(den > 0.0, den, 1.0)
        o = _dot(pr.astype(BF16), cv)
        ocmp_ref[0, g * HPG:(g + 1) * HPG, :] = o[:, g * HEAD_DIM:(g + 1) * HEAD_DIM]
        imp = jnp.sum(pr, axis=0, keepdims=True)
        even = (blk_n & 1) == 0
        pooled = imp + jnp.where(even, pltpu.roll(imp, nb - 1, 1), pltpu.roll(imp, 1, 1))
        forced = (sblk == 0) | (sblk == cur) | (sblk == cur - 1)
        score = jnp.where(sblk < cur, pooled + FORCE_BONUS * forced.astype(F32), neg_inf)
        scores = jnp.where(row8 == g, jnp.broadcast_to(score, (8, nb)), scores)
    score_ref[0] = scores


def _nsa_sample_topk_kernel(score_ref, idx_ref):
    score = score_ref[...]
    rows, nb = score.shape
    ratio = SEL_BLOCK // CMP_BLOCK
    lane = lax.broadcasted_iota(I32, (rows, nb), 1)
    lane_f = lane.astype(F32)
    sblk = lane // ratio
    slot = lax.broadcasted_iota(I32, (rows, LANES), 1)
    neg_inf = float("-inf")
    idx = jnp.full((rows, LANES), -1, I32)
    for t in range(TOP_N - 1):
        mx = jnp.max(score, axis=-1, keepdims=True)
        first = jnp.min(jnp.where(score == mx, lane_f, float(4 * nb)), axis=-1, keepdims=True)
        blk = first.astype(I32) // ratio
        idx = jnp.where((slot == t) & (mx > neg_inf), blk, idx)
        score = jnp.where(sblk == blk, neg_inf, score)
    idx_ref[...] = idx


def _nsa_sample_select(q16, ck, cv, q_pos):
    nb = q16.shape[0]
    nblk = ck.shape[1]
    m3 = lambda b: (b, 0, 0)
    ocmp, scores = pl.pallas_call(
        functools.partial(_nsa_sample_cmp_kernel, q_pos=q_pos),
        grid=(nb,),
        in_specs=[pl.BlockSpec((1, H_NSA, HEAD_DIM), m3),
                  pl.BlockSpec((1, nblk, LANES), m3),
                  pl.BlockSpec((1, nblk, LANES), m3)],
        out_specs=[pl.BlockSpec((1, H_NSA, HEAD_DIM), m3), pl.BlockSpec((1, 8, nblk), m3)],
        out_shape=[jax.ShapeDtypeStruct((nb, H_NSA, HEAD_DIM), F32),
                   jax.ShapeDtypeStruct((nb, 8, nblk), F32)],
        compiler_params=_params(("parallel",)),
        name="nsa_sample_cmp",
    )(q16, ck, cv)
    rows = scores[:, :G_NSA].reshape(nb * G_NSA, nblk)
    idx = pl.pallas_call(
        _nsa_sample_topk_kernel,
        out_shape=jax.ShapeDtypeStruct((nb * G_NSA, LANES), I32),
        compiler_params=pltpu.CompilerParams(vmem_limit_bytes=VMEM_LIMIT),
        name="nsa_sample_topk",
    )(rows)
    return ocmp, idx


def _nsa_sample_attend_kernel(pt_ref, idx_ref, q_ref, k_hbm, v_hbm, kn_ref, vn_ref, kwn_ref, vwn_ref,
                              wk_ref, wv_ref, ocmp_ref, bg_ref, gate_ref, o_ref, kbuf, vbuf, sem,
                              *, q_pos):
    b = pl.program_id(0)
    n_slots = kbuf.shape[2]
    n_pages = pt_ref.shape[1]
    buf = b % 2

    def copies(sample, which, g, s):
        blk = jnp.maximum(idx_ref[sample * G_NSA + g, s], 0)
        pid = pt_ref[sample, jnp.minimum(blk // 2, n_pages - 1)]
        rows = pl.ds(g * HEAD_DIM, HEAD_DIM)
        return (pltpu.make_async_copy(k_hbm.at[pid, rows], kbuf.at[which, g, s], sem.at[which, 0, g]),
                pltpu.make_async_copy(v_hbm.at[pid, rows], vbuf.at[which, g, s], sem.at[which, 1, g]))

    def start_all(sample, which):
        for g in range(G_NSA):
            for s in range(n_slots):
                for c in copies(sample, which, g, s):
                    c.start()

    @pl.when(b == 0)
    def _():
        start_all(0, 0)

    @pl.when(b + 1 < pl.num_programs(0))
    def _():
        start_all(b + 1, 1 - buf)

    q = q_ref[0]
    bg = bg_ref[0]
    col = lax.broadcasted_iota(I32, (HPG, LANES), 1)
    hrow = lax.broadcasted_iota(I32, (HPG, LANES), 0)
    key_half = lax.broadcasted_iota(I32, (1, PAGE_SIZE), 1) // SEL_BLOCK
    wlen = wk_ref.shape[3]
    w_pos = q_pos - wlen + lax.broadcasted_iota(I32, (1, wlen), 1)
    wmask = (w_pos > q_pos - WINDOW) & (w_pos >= 0)

    def new_token(x_ref, g):
        return x_ref[0][:, g * HEAD_DIM:(g + 1) * HEAD_DIM].astype(BF16).astype(F32)

    for g in range(G_NSA):
        qg = q[g * HPG:(g + 1) * HPG]
        qf = qg.astype(F32)
        sw = jnp.where(wmask, _dot(qg, wk_ref[0, g].astype(BF16)), NEG_INF)
        sw_new = jnp.sum(qf * new_token(kwn_ref, g), axis=-1, keepdims=True)
        mw = jnp.maximum(jnp.max(sw, axis=-1, keepdims=True), sw_new)
        pw = jnp.where(wmask, jnp.exp2(sw - mw), 0.0)
        pw_new = jnp.exp2(sw_new - mw)
        o_win = _dot_nt(pw.astype(BF16), wv_ref[0, g].astype(BF16))
        o_win = o_win + pw_new.astype(BF16).astype(F32) * new_token(vwn_ref, g)
        o_win = o_win / (jnp.sum(pw, axis=-1, keepdims=True) + pw_new)
        for s in range(n_slots):
            for c in copies(b, buf, g, s):
                c.wait()
        scores, masks = [], []
        for s in range(n_slots):
            blk = idx_ref[b * G_NSA + g, s]
            ok = (key_half == jnp.maximum(blk, 0) % 2) & (blk >= 0)
            scores.append(jnp.where(ok, _dot(qg, kbuf[buf, g, s].astype(BF16)), NEG_INF))
            masks.append(ok)
        s_new = jnp.sum(qf * new_token(kn_ref, g), axis=-1, keepdims=True)
        m = s_new
        for sc in scores:
            m = jnp.maximum(m, jnp.max(sc, axis=-1, keepdims=True))
        p_new = jnp.exp2(s_new - m)
        den = p_new
        acc = p_new.astype(BF16).astype(F32) * new_token(vn_ref, g)
        for s in range(n_slots):
            p = jnp.where(masks[s], jnp.exp2(scores[s] - m), 0.0)
            den = den + jnp.sum(p, axis=-1, keepdims=True)
            acc = acc + _dot_nt(p.astype(BF16), vbuf[buf, g, s].astype(BF16))
        o_sel = acc / den
        gates = []
        for br in range(3):
            pick = col == g * HPG * 3 + hrow * 3 + br
            gates.append(jnp.sum(jnp.where(pick, jnp.broadcast_to(bg, (HPG, LANES)), 0.0), axis=-1, keepdims=True))
        rows = slice(g * HPG, (g + 1) * HPG)
        mixed = gates[0] * ocmp_ref[0, rows, :] + gates[1] * o_sel + gates[2] * o_win
        o_ref[0, rows, :] = mixed * gate_ref[0, rows, :]


def _nsa_sample_attend(page_table, idx, q16, sel_kt, sel_vt, ks_new, vs_new, kw_new, vw_new, win_kt, win_vt, ocmp,
                       bg, gate16, q_pos):
    nb = q16.shape[0]
    n_slots = TOP_N - 1
    wlen = win_kt.shape[3]
    m3 = lambda b, pt, ix: (b, 0, 0)
    m4 = lambda b, pt, ix: (b, 0, 0, 0)
    row = pl.BlockSpec((1, 1, LANES), m3)
    heads = pl.BlockSpec((1, H_NSA, HEAD_DIM), m3)
    win = pl.BlockSpec((1, G_NSA, HEAD_DIM, wlen), m4)
    grid_spec = pltpu.PrefetchScalarGridSpec(
        num_scalar_prefetch=2,
        grid=(nb,),
        in_specs=[heads, pl.BlockSpec(memory_space=pl.ANY), pl.BlockSpec(memory_space=pl.ANY),
                  row, row, row, row, win, win, heads, row, heads],
        out_specs=heads,
        scratch_shapes=[pltpu.VMEM((2, G_NSA, n_slots, HEAD_DIM, PAGE_SIZE), F32),
                        pltpu.VMEM((2, G_NSA, n_slots, HEAD_DIM, PAGE_SIZE), F32),
                        pltpu.SemaphoreType.DMA((2, 2, G_NSA))],
    )
    return pl.pallas_call(
        functools.partial(_nsa_sample_attend_kernel, q_pos=q_pos),
        grid_spec=grid_spec,
        out_shape=jax.ShapeDtypeStruct((nb, H_NSA, HEAD_DIM), F32),
        compiler_params=_params(("arbitrary",)),
        name="nsa_sample_attend",
    )(page_table, idx, q16, sel_kt, sel_vt, ks_new, vs_new, kw_new, vw_new, win_kt, win_vt, ocmp, bg, gate16)


def _layer0_sample(x2d, past_len, state, cache_k, cache_v, page_table, norm_g, w_in_bf, ret_gn_g, sb_qn_g, sb_kn_g,
                   w_out_bf):
    nb = x2d.shape[0]
    cos, sin = _rope_tables(jnp.full((nb,), past_len, I32))
    ret, sbq, sbg, sbk, sbv = _inproj0(x2d, norm_g, w_in_bf, cos, sin, _pair_gain(sb_qn_g), _pair_gain(sb_kn_g),
                                       nb, nb, False)
    mix_ret, state_new = _ret_sample(ret, state, ret_gn_g)
    mix_sb = _sb_sample(sbq.reshape(nb, H_SB, HEAD_DIM), sbg.reshape(nb, H_SB, HEAD_DIM),
                        _pages_t(cache_k), _pages_t(cache_v), page_table)
    mix_sb = mix_sb.reshape(nb, H_SB * HEAD_DIM)
    half = w_out_bf.shape[0] // 2
    y = _outproj(x2d, [mix_ret, mix_sb], [w_out_bf[:half], w_out_bf[half:]], nb)
    return y, state_new, sbk, sbv


def _layer1_sample(x2d, past_len, cmp_k, cmp_v, sel_k, sel_v, win_k, win_v, page_table, norm_g, w_in_bf, qn_g,
                   cmp_kn_g, sel_kn_g, win_kn_g, pe128, wk_bd, wv_bd, w_out_bf):
    nb = x2d.shape[0]
    (qn, gate, bg, kc, vc, ks, vs, kw, vw) = _inproj1(
        x2d, norm_g, w_in_bf, _pair_gain(qn_g), _pair_gain(sel_kn_g), _pair_gain(win_kn_g), nb, nb, False)
    n_phys = cmp_k.shape[0]
    pool = lambda t: _pages_t(t).reshape(n_phys, KV_W, PAGE_SIZE)
    ck, cv = _compress_sample(pool(cmp_k), pool(cmp_v), page_table, pe128, wk_bd, wv_bd, _pair_gain(cmp_kn_g))
    q16 = qn.reshape(nb, H_NSA, HEAD_DIM)
    ocmp, idx = _nsa_sample_select(q16, ck, cv, past_len)
    idx2 = idx[:, :TOP_N]
    mix = _nsa_sample_attend(page_table, idx2, q16, pool(sel_k), pool(sel_v), ks[:, None, :], vs[:, None, :],
                             kw[:, None, :], vw[:, None, :], _pages_t(win_k), _pages_t(win_v), ocmp,
                             bg[:, None, :], gate.reshape(nb, H_NSA, HEAD_DIM), past_len)
    y = _outproj(x2d, [mix.reshape(nb, NSA_W)], [w_out_bf], nb)
    gd = (nb, 1, G_NSA, HEAD_DIM)
    wk_new = jnp.concatenate([win_k[:, 1:], kw.reshape(gd)], axis=1)
    wv_new = jnp.concatenate([win_v[:, 1:], vw.reshape(gd)], axis=1)
    return y, kc, vc, ks, vs, wk_new, wv_new


def kernel(x_prompt, x_sample, state_ret, cache_sb_k, cache_sb_v, cache_cmp_k, cache_cmp_v, cache_sel_k,
           cache_sel_v, cache_win_k, cache_win_v, page_table, norm0_g, w_in0, ret_gn_g, sb_qn_g, sb_kn_g,
           w_out0, norm1_g, w_in1, nsa_qn_g, cmp_kn_g, sel_kn_g, win_kn_g, cmp_pe, w_cmp_k, w_cmp_v, w_out1):
    batch, seq, d = x_prompt.shape
    nb = x_sample.shape[0]
    past_len = page_table.shape[1] * PAGE_SIZE
    tm = 256
    w_in0_bf = w_in0.astype(BF16)
    w_out0_bf = w_out0.astype(BF16)
    w_in1_bf = _pad_w_in1(w_in1)
    w_out1_bf = w_out1.astype(BF16)
    pe128 = jnp.tile(cmp_pe, (1, G_NSA))
    wk_bd = _blockdiag_w(w_cmp_k)
    wv_bd = _blockdiag_w(w_cmp_v)

    xp = x_prompt.reshape(batch * seq, d)
    y1p, ret_p, sbk_p, sbv_p = _layer0_prompt(xp, batch, seq, norm0_g, w_in0_bf, ret_gn_g, sb_qn_g, sb_kn_g,
                                              w_out0_bf, tm)
    y2p, ckp, cvp, skp, svp, wkp, wvp = _layer1_prompt(y1p, batch, seq, norm1_g, w_in1_bf, nsa_qn_g, cmp_kn_g,
                                                        sel_kn_g, win_kn_g, pe128, wk_bd, wv_bd, w_out1_bf, tm)
    xs = x_sample.reshape(nb, d)
    y1s, ret_s, sbk_s, sbv_s = _layer0_sample(xs, past_len, state_ret, cache_sb_k, cache_sb_v, page_table, norm0_g,
                                              w_in0_bf, ret_gn_g, sb_qn_g, sb_kn_g, w_out0_bf)
    y2s, cks, cvs, sks, svs, wks, wvs = _layer1_sample(y1s, past_len, cache_cmp_k, cache_cmp_v, cache_sel_k,
                                                        cache_sel_v, cache_win_k, cache_win_v, page_table, norm1_g,
                                                        w_in1_bf, nsa_qn_g, cmp_kn_g, sel_kn_g, win_kn_g, pe128,
                                                        wk_bd, wv_bd, w_out1_bf)
    keep = min(WINDOW, seq)
    gd = (nb, 1, G_NSA, HEAD_DIM)

    def rows_major(t):
        return t.transpose(0, 3, 1, 2)

    def groups(t):
        return rows_major(t.reshape(batch, G_NSA, HEAD_DIM, seq))

    return (y2p.reshape(batch, seq, d), y2s.reshape(nb, 1, d), ret_p, ret_s,
            rows_major(sbk_p), rows_major(sbv_p),
            sbk_s.reshape(nb, 1, H_SB, HEAD_DIM), sbv_s.reshape(nb, 1, H_SB, HEAD_DIM),
            groups(ckp), groups(cvp), groups(skp), groups(svp),
            groups(wkp)[:, seq - keep:], groups(wvp)[:, seq - keep:],
            cks.reshape(gd), cvs.reshape(gd), sks.reshape(gd), svs.reshape(gd), wks, wvs)
```

```python
import functools

import jax
import jax.numpy as jnp
import numpy as np
from jax import lax
from jax.experimental import pallas as pl
from jax.experimental.pallas import tpu as pltpu

F32 = jnp.float32
BF16 = jnp.bfloat16
I32 = jnp.int32

HEAD_DIM = 64
LANES = 128
H_RET = 8
H_SB = 8
RET_CHUNK = 128
ROPE_BASE = 10000.0
H_NSA = 16
G_NSA = 2
HPG = H_NSA // G_NSA
CMP_BLOCK = 32
SEL_BLOCK = 64
TOP_N = 16
WINDOW = 512
FORCE_BONUS = 1000.0
NEG_INF = -1e30
EPS = 1e-6
PAGE_SIZE = 128
NSA_KEY_TILE = 128
NSA_Q_TILE = 256
SCALE = HEAD_DIM ** -0.5
LOG2E = 1.4426950408889634
VMEM_LIMIT = 56 * 1024 * 1024


def _dot(a, b):
    return jnp.dot(a, b, preferred_element_type=F32)


def _dot_nt(a, b):
    return lax.dot_general(a, b, (((1,), (1,)), ((), ())), preferred_element_type=F32)


def _dot_tn(a, b):
    return lax.dot_general(a, b, (((0,), (0,)), ((), ())), preferred_element_type=F32)


def _split_bf16(x):
    hi = x.astype(BF16)
    lo = (x - hi.astype(F32)).astype(BF16)
    return hi, lo


def _sigmoid(x):
    return 1.0 / (1.0 + jnp.exp(-x))


def _silu(x):
    return x * _sigmoid(x)


def _lane_lo(shape):
    return lax.broadcasted_iota(I32, shape, len(shape) - 1) < HEAD_DIM


def _pair_sum(x, lo):
    s_lo = jnp.sum(jnp.where(lo, x, 0.0), axis=-1, keepdims=True)
    s_hi = jnp.sum(jnp.where(lo, 0.0, x), axis=-1, keepdims=True)
    return jnp.where(lo, s_lo, s_hi)


def _pair_rmsnorm(x, g, lo):
    ms = _pair_sum(x * x, lo) * (1.0 / HEAD_DIM)
    return x * lax.rsqrt(ms + EPS) * g


def _rope_pair(x, cos, sin_signed):
    lane = lax.broadcasted_iota(I32, x.shape, 1)
    first = (lane & 32) == 0
    partner = jnp.where(first, pltpu.roll(x, 96, 1), pltpu.roll(x, 32, 1))
    return x * cos + partner * sin_signed


def _transpose_tiles(x):
    r, c = x.shape
    if r <= LANES and c <= LANES:
        return x.T
    rows = [jnp.concatenate([x[ri * LANES:(ri + 1) * LANES, cj * LANES:(cj + 1) * LANES].T
                             for ri in range(r // LANES)], axis=1) for cj in range(c // LANES)]
    return jnp.concatenate(rows, axis=0)


def _params(sem):
    return pltpu.CompilerParams(dimension_semantics=sem, vmem_limit_bytes=VMEM_LIMIT)


def _store_head_major(dst_ref, t, pair, r):
    cols = slice(r * LANES, (r + 1) * LANES)
    dst_ref[0, 2 * pair, :, cols] = t[:HEAD_DIM]
    dst_ref[0, 2 * pair + 1, :, cols] = t[HEAD_DIM:]


def _inproj0_kernel(x_ref, g_ref, w_ref, cos_ref, sin_ref, qg_ref, kg_ref,
                    ret_ref, sbq_ref, sbg_ref, *kv_refs, prompt):
    x = x_ref[...]
    xn = x * lax.rsqrt(jnp.mean(x * x, axis=-1, keepdims=True) + EPS) * g_ref[...]
    xb = xn.astype(BF16)
    cos = cos_ref[...]
    sin = sin_ref[...]
    lo = _lane_lo((1, LANES))
    width = H_RET * HEAD_DIM

    def group(i):
        return _dot(xb, w_ref[:, i * width:(i + 1) * width])

    def tiles(a):
        return [a[:, t * LANES:(t + 1) * LANES] for t in range(width // LANES)]

    for t, v in enumerate(tiles(group(0))):
        ret_ref[:, t * LANES:(t + 1) * LANES] = _rope_pair(v, cos, sin)
    for t, v in enumerate(tiles(group(1))):
        ret_ref[:, width + t * LANES:width + (t + 1) * LANES] = _rope_pair(v, cos, sin) * SCALE
    ret_ref[:, 2 * width:3 * width] = group(2)
    ret_ref[:, 3 * width:4 * width] = group(3)
    for t, v in enumerate(tiles(group(4))):
        sbq_ref[:, t * LANES:(t + 1) * LANES] = (_pair_rmsnorm(v, qg_ref[...], lo) * (SCALE * LOG2E)).astype(BF16)
    sk = [_pair_rmsnorm(v, kg_ref[...], lo) for v in tiles(group(5))]
    sv = tiles(group(6))
    if prompt:
        kb_ref, vt_ref, kt_out, vt_out = kv_refs
        for t in range(width // LANES):
            kb_ref[:, t * LANES:(t + 1) * LANES] = sk[t].astype(BF16)
            for r in range(x.shape[0] // LANES):
                rows = slice(r * LANES, (r + 1) * LANES)
                vt = sv[t][rows].T
                _store_head_major(kt_out, sk[t][rows].T, t, r)
                _store_head_major(vt_out, vt, t, r)
                vt_ref[r, t] = vt.astype(BF16)
    else:
        k_ref, v_ref = kv_refs
        for t in range(width // LANES):
            k_ref[:, t * LANES:(t + 1) * LANES] = sk[t]
            v_ref[:, t * LANES:(t + 1) * LANES] = sv[t]
    sbg_ref[...] = _silu(group(7))


def _inproj0(x2d, norm_g, w_bf, cos, sin, qg, kg, tm, seq, prompt):
    rows, d = x2d.shape
    width = H_RET * HEAD_DIM
    n = w_bf.shape[1]
    grid = (rows // tm,)
    per_seq = seq // tm
    row = lambda i: (i, 0)
    fixed = lambda i: (0, 0)
    pos = lambda i: (i % per_seq, 0)
    if prompt:
        npair = width // LANES
        head_major = pl.BlockSpec((1, H_SB, HEAD_DIM, tm), lambda i: (i // per_seq, 0, 0, i % per_seq))
        hm_shape = jax.ShapeDtypeStruct((rows // seq, H_SB, HEAD_DIM, seq), F32)
        extra_specs = [pl.BlockSpec((tm, width), row),
                       pl.BlockSpec((tm // LANES, npair, LANES, LANES), lambda i: (i, 0, 0, 0)),
                       head_major, head_major]
        extra_shapes = [jax.ShapeDtypeStruct((rows, width), BF16),
                        jax.ShapeDtypeStruct((rows // LANES, npair, LANES, LANES), BF16),
                        hm_shape, hm_shape]
    else:
        extra_specs = [pl.BlockSpec((tm, width), row)] * 2
        extra_shapes = [jax.ShapeDtypeStruct((rows, width), F32)] * 2
    return pl.pallas_call(
        functools.partial(_inproj0_kernel, prompt=prompt),
        grid=grid,
        in_specs=[
            pl.BlockSpec((tm, d), row),
            pl.BlockSpec((1, d), fixed),
            pl.BlockSpec((d, n), fixed),
            pl.BlockSpec((tm, LANES), pos),
            pl.BlockSpec((tm, LANES), pos),
            pl.BlockSpec((1, LANES), fixed),
            pl.BlockSpec((1, LANES), fixed),
        ],
        out_specs=[
            pl.BlockSpec((tm, 4 * width), row),
            pl.BlockSpec((tm, width), row),
            pl.BlockSpec((tm, width), row),
        ] + extra_specs,
        out_shape=[
            jax.ShapeDtypeStruct((rows, 4 * width), F32),
            jax.ShapeDtypeStruct((rows, width), BF16),
            jax.ShapeDtypeStruct((rows, width), F32),
        ] + extra_shapes,
        compiler_params=_params(("parallel",)),
        name="inproj0",
    )(x2d, norm_g.reshape(1, d), w_bf, cos, sin, qg, kg)


def _ret_prompt_kernel(ret_ref, inner_ref, qdec_ref, kdec_ref, cdec_ref, gn_ref, mix_ref, state_ref, s_scr):
    c = pl.program_id(1)
    npair = H_RET // 2
    width = H_RET * HEAD_DIM

    @pl.when(c == 0)
    def _():
        s_scr[...] = jnp.zeros_like(s_scr)

    def tile(group, p):
        return ret_ref[:, group * width + p * LANES:group * width + (p + 1) * LANES]

    lo = _lane_lo((ret_ref.shape[0], LANES))
    same_head = ((lax.broadcasted_iota(I32, (LANES, LANES), 0) < HEAD_DIM)
                 == (lax.broadcasted_iota(I32, (LANES, LANES), 1) < HEAD_DIM))
    qs = [tile(0, p).astype(BF16) for p in range(npair)]
    ks = [tile(1, p) for p in range(npair)]
    kbs = [k.astype(BF16) for k in ks]
    vbs = [tile(2, p).astype(BF16) for p in range(npair)]
    states = [s_scr[p] for p in range(npair)]
    zero = jnp.zeros_like(qs[0])
    att_lo = [_dot_nt(jnp.where(lo, qs[p], zero), kbs[p]) for p in range(npair)]
    att_hi = [_dot_nt(jnp.where(lo, zero, qs[p]), kbs[p]) for p in range(npair)]
    cross = [_dot(qs[p], states[p].astype(BF16)) for p in range(npair)]
    kv = [_dot_tn((ks[p] * kdec_ref[p]).astype(BF16), vbs[p]) for p in range(npair)]
    a_lo = [(att_lo[p] * inner_ref[2 * p]).astype(BF16) for p in range(npair)]
    a_hi = [(att_hi[p] * inner_ref[2 * p + 1]).astype(BF16) for p in range(npair)]
    o_lo = [_dot(a_lo[p], vbs[p]) for p in range(npair)]
    o_hi = [_dot(a_hi[p], vbs[p]) for p in range(npair)]
    for p in range(npair):
        s_new = states[p] * cdec_ref[p] + jnp.where(same_head, kv[p], 0.0)
        s_scr[p] = s_new
        state_ref[0, p] = s_new
        o = jnp.where(lo, o_lo[p], o_hi[p]) + cross[p] * qdec_ref[p]
        mu = _pair_sum(o, lo) * (1.0 / HEAD_DIM)
        oc = o - mu
        var = _pair_sum(oc * oc, lo) * (1.0 / HEAD_DIM)
        on = oc * lax.rsqrt(var + EPS) * gn_ref[p]
        mix_ref[:, p * LANES:(p + 1) * LANES] = on * _silu(tile(3, p))


def _ret_tables(chunk):
    h = jnp.arange(H_RET, dtype=F32)
    lg = jnp.log1p(-jnp.exp2(-5.0 - h))
    idx = jnp.arange(chunk, dtype=F32)
    diff = idx[:, None] - idx[None, :]
    inner = jnp.where(diff >= 0, jnp.exp(lg[:, None, None] * jnp.maximum(diff, 0.0)), 0.0)
    q_dec = jnp.exp(lg[:, None] * (idx + 1.0))
    k_dec = jnp.exp(lg[:, None] * (chunk - 1.0 - idx))
    c_dec = jnp.exp(lg * chunk)

    def lanes(t):
        t = jnp.repeat(t[:, :, None], HEAD_DIM, axis=2)
        return jnp.concatenate([t[0::2], t[1::2]], axis=2)

    cd = jnp.repeat(c_dec[:, None, None], HEAD_DIM, axis=2)
    cd = jnp.concatenate([cd[0::2], cd[1::2]], axis=2)
    return inner, lanes(q_dec), lanes(k_dec), cd


def _ret_prompt(ret, gn_g, batch, seq):
    width = H_RET * HEAD_DIM
    npair = H_RET // 2
    c = RET_CHUNK
    nc = seq // c
    inner, qdec, kdec, cdec = _ret_tables(c)
    gn = gn_g.reshape(npair, 1, LANES)
    tab = lambda b, i: (0, 0, 0)
    return pl.pallas_call(
        _ret_prompt_kernel,
        grid=(batch, nc),
        in_specs=[
            pl.BlockSpec((c, 4 * width), lambda b, i: (b * nc + i, 0)),
            pl.BlockSpec((H_RET, c, c), tab),
            pl.BlockSpec((npair, c, LANES), tab),
            pl.BlockSpec((npair, c, LANES), tab),
            pl.BlockSpec((npair, 1, LANES), tab),
            pl.BlockSpec((npair, 1, LANES), tab),
        ],
        out_specs=[
            pl.BlockSpec((c, width), lambda b, i: (b * nc + i, 0)),
            pl.BlockSpec((1, npair, LANES, LANES), lambda b, i: (b, 0, 0, 0)),
        ],
        out_shape=[
            jax.ShapeDtypeStruct((batch * seq, width), F32),
            jax.ShapeDtypeStruct((batch, npair, LANES, LANES), F32),
        ],
        scratch_shapes=[pltpu.VMEM((npair, LANES, LANES), F32)],
        compiler_params=_params(("parallel", "arbitrary")),
        name="retention_prompt",
    )(ret, inner, qdec, kdec, cdec, gn)


def _unpair_state(sp):
    a = sp[:, :, :HEAD_DIM, :HEAD_DIM]
    b = sp[:, :, HEAD_DIM:, HEAD_DIM:]
    return jnp.stack([a, b], axis=2).reshape(sp.shape[0], -1, HEAD_DIM, HEAD_DIM)


SB_LOG_ZERO = -151.0


def _log2_sigmoid(z2):
    return jnp.minimum(z2, 0.0) - jnp.log2(1.0 + jnp.exp2(-jnp.abs(z2)))
SB_Q_TILE = 256
SB_KEY_TILE = 128


def _sb_prompt_kernel(q_ref, k_ref, vt_ref, g_ref, o_ref, acc_scr):
    i = pl.program_id(1)
    tq = q_ref.shape[0]
    kt = SB_KEY_TILE
    npair = H_SB // 2
    lower = (lax.broadcasted_iota(I32, (kt, kt), 1)
             >= lax.broadcasted_iota(I32, (kt, kt), 0)).astype(BF16)
    k_in = lax.broadcasted_iota(I32, (kt, tq), 0)
    q_pos = i * tq + lax.broadcasted_iota(I32, (kt, tq), 1)
    zeros = jnp.zeros((HEAD_DIM, tq), F32)
    qts = []
    for p in range(npair):
        t = _transpose_tiles(q_ref[:, p * LANES:(p + 1) * LANES].astype(F32))
        qts.append(jnp.concatenate([t[:HEAD_DIM], zeros], axis=0).astype(BF16))
        qts.append(jnp.concatenate([zeros, t[HEAD_DIM:]], axis=0).astype(BF16))
    acc_scr[...] = jnp.zeros_like(acc_scr)

    def sweep(j, rs, masked):
        start = pl.multiple_of(j * kt, kt)
        diag_mask = (j * kt + k_in < q_pos) if masked else None
        kjs = [k_ref[pl.ds(start, kt), p * LANES:(p + 1) * LANES] for p in range(npair)]
        vts = [vt_ref[j, p] for p in range(npair)]
        zs = [_dot(kjs[h // 2], qt) for h, qt in enumerate(qts)]
        ls_pos, lms, parts = [], [], []
        for z in zs:
            lsp = _log2_sigmoid(z)
            lm = lsp - z
            if diag_mask is not None:
                lm = jnp.where(diag_mask, lm, 0.0)
            ls_pos.append(lsp)
            lms.append(lm)
            parts.append(_split_bf16(lm))
        s_incl = [_dot(lower, hi) + _dot(lower, lo_part) for hi, lo_part in parts]
        probs = []
        for h in range(H_SB):
            a = jnp.exp2(ls_pos[h] + (s_incl[h] - lms[h]) + rs[h])
            if diag_mask is not None:
                a = jnp.where(diag_mask, a, 0.0)
            probs.append(a.astype(BF16))
        for h in range(H_SB):
            half = h % 2
            acc_scr[h] = acc_scr[h] + _dot(vts[h // 2][half * HEAD_DIM:(half + 1) * HEAD_DIM, :], probs[h])
        return [rs[h] + s_incl[h][0:1, :] for h in range(H_SB)]

    def alive_of(rs):
        worst = rs[0]
        for r in rs[1:]:
            worst = jnp.maximum(worst, r)
        return jnp.max(worst) > SB_LOG_ZERO

    n_diag = tq // kt
    j_top = (i + 1) * n_diag - 1
    rs = [jnp.zeros((1, tq), F32)] * H_SB
    for d in range(n_diag):
        rs = sweep(j_top - d, rs, True)
    n_old = i * n_diag

    def cond(carry):
        t, alive = carry[0], carry[1]
        return (t < n_old) & alive

    def body(carry):
        t = carry[0]
        rs = sweep(n_old - 1 - t, list(carry[2:]), False)
        return (t + 1, alive_of(rs)) + tuple(rs)

    lax.while_loop(cond, body, (jnp.int32(0), alive_of(rs)) + tuple(rs))
    for p in range(npair):
        both = _transpose_tiles(jnp.concatenate([acc_scr[2 * p], acc_scr[2 * p + 1]], axis=0))
        o_ref[:, p * LANES:(p + 1) * LANES] = both * g_ref[:, p * LANES:(p + 1) * LANES]


def _sb_prompt(sbq, sbk_b, sbvt, sbg, batch, seq):
    width = H_SB * HEAD_DIM
    npair = H_SB // 2
    tq = SB_Q_TILE
    kt = SB_KEY_TILE
    nq = seq // tq
    qmap = lambda b, i: (b * nq + i, 0)
    return pl.pallas_call(
        _sb_prompt_kernel,
        grid=(batch, nq),
        in_specs=[
            pl.BlockSpec((tq, width), qmap),
            pl.BlockSpec((seq, width), lambda b, i: (b, 0)),
            pl.BlockSpec((seq // kt, npair, LANES, kt), lambda b, i: (b, 0, 0, 0)),
            pl.BlockSpec((tq, width), qmap),
        ],
        out_specs=pl.BlockSpec((tq, width), qmap),
        out_shape=jax.ShapeDtypeStruct((batch * seq, width), F32),
        scratch_shapes=[pltpu.VMEM((H_SB, HEAD_DIM, tq), F32)],
        compiler_params=_params(("parallel", "arbitrary")),
        name="sb_prompt",
    )(sbq, sbk_b, sbvt, sbg)


def _outproj_kernel(*refs, n_mix):
    x_ref = refs[0]
    mix_refs = refs[1:1 + n_mix]
    w_refs = refs[1 + n_mix:1 + 2 * n_mix]
    y_ref = refs[1 + 2 * n_mix]
    acc = x_ref[...]
    for m_ref, w_ref in zip(mix_refs, w_refs):
        acc = acc + _dot(m_ref[...].astype(BF16), w_ref[...])
    y_ref[...] = acc


def _outproj(x2d, mixes, ws, tm):
    rows, d = x2d.shape
    n_mix = len(mixes)
    row = lambda i: (i, 0)
    fixed = lambda i: (0, 0)
    in_specs = [pl.BlockSpec((tm, d), row)]
    in_specs += [pl.BlockSpec((tm, m.shape[1]), row) for m in mixes]
    in_specs += [pl.BlockSpec(w.shape, fixed) for w in ws]
    return pl.pallas_call(
        functools.partial(_outproj_kernel, n_mix=n_mix),
        grid=(rows // tm,),
        in_specs=in_specs,
        out_specs=pl.BlockSpec((tm, d), row),
        out_shape=jax.ShapeDtypeStruct((rows, d), F32),
        compiler_params=_params(("parallel",)),
        name="outproj",
    )(x2d, *mixes, *ws)


def _rope_tables(pos):
    half = HEAD_DIM // 2
    inv = ROPE_BASE ** (-jnp.arange(half, dtype=F32) / half)
    ang = pos.astype(F32)[:, None] * inv[None, :]
    cos, sin = jnp.cos(ang), jnp.sin(ang)
    cos128 = jnp.tile(cos, (1, LANES // half))
    sin128 = jnp.tile(jnp.concatenate([-sin, sin], axis=1), (1, LANES // HEAD_DIM))
    return cos128, sin128


def _pair_gain(g):
    return jnp.tile(g.astype(F32), LANES // HEAD_DIM).reshape(1, LANES)


def _layer0_prompt(x2d, batch, seq, norm_g, w_in_bf, ret_gn_g, sb_qn_g, sb_kn_g, w_out_bf, tm):
    cos, sin = _rope_tables(jnp.arange(seq, dtype=I32))
    ret, sbq, sbg, sbk_b, sbvt, sbk_hm, sbv_hm = _inproj0(x2d, norm_g, w_in_bf, cos, sin, _pair_gain(sb_qn_g),
                                                          _pair_gain(sb_kn_g), tm, seq, True)
    mix_ret, state_pairs = _ret_prompt(ret, ret_gn_g, batch, seq)
    mix_sb = _sb_prompt(sbq, sbk_b, sbvt, sbg, batch, seq)
    half = w_out_bf.shape[0] // 2
    y = _outproj(x2d, [mix_ret, mix_sb], [w_out_bf[:half], w_out_bf[half:]], tm)
    return y, _unpair_state(state_pairs), sbk_hm, sbv_hm


NSA_W = H_NSA * HEAD_DIM
KV_W = G_NSA * HEAD_DIM
IN1_COLS = 2 * NSA_W + 6 * KV_W + LANES


def _inproj1_kernel(x_ref, g_ref, w_ref, qg_ref, skg_ref, wkg_ref, q_ref, gate_ref, bg_ref, *kv_refs, prompt):
    x = x_ref[...]
    xn = x * lax.rsqrt(jnp.mean(x * x, axis=-1, keepdims=True) + EPS) * g_ref[...]
    xb = xn.astype(BF16)
    lo = _lane_lo((1, LANES))
    a = _dot(xb, w_ref[:, 0:NSA_W])
    for t in range(NSA_W // LANES):
        v = a[:, t * LANES:(t + 1) * LANES]
        q_ref[:, t * LANES:(t + 1) * LANES] = (_pair_rmsnorm(v, qg_ref[...], lo) * (SCALE * LOG2E)).astype(BF16)
    a = _dot(xb, w_ref[:, NSA_W:NSA_W + 6 * KV_W])
    kc, vc, ks, vs, kw, vw = [a[:, t * KV_W:(t + 1) * KV_W] for t in range(6)]
    ks = _pair_rmsnorm(ks, skg_ref[...], lo)
    kw = _pair_rmsnorm(kw, wkg_ref[...], lo)
    six = (kc, vc, ks, vs, kw, vw)
    if prompt:
        t_out = kv_refs[:6]
        kc_ref, vc_ref, ksb_ref, kwb_ref, vst_ref, vwt_ref = kv_refs[6:]
        kc_ref[...] = kc
        vc_ref[...] = vc
        ksb_ref[...] = ks.astype(BF16)
        kwb_ref[...] = kw.astype(BF16)
        per_tile = NSA_KEY_TILE // LANES
        for r in range(x.shape[0] // LANES):
            rows = slice(r * LANES, (r + 1) * LANES)
            cols = slice((r % per_tile) * LANES, (r % per_tile + 1) * LANES)
            for src, dst in zip(six, t_out):
                t = src[rows].T
                dst[0, :, rows] = t
                if src is vs:
                    vst_ref[r // per_tile, :, cols] = t.astype(BF16)
                if src is vw:
                    vwt_ref[r // per_tile, :, cols] = t.astype(BF16)
    else:
        for src, dst in zip(six, kv_refs):
            dst[...] = src
    off = NSA_W + 6 * KV_W
    gate_ref[...] = _silu(_dot(xb, w_ref[:, off:off + NSA_W]))
    bg_ref[...] = _sigmoid(_dot(xb, w_ref[:, off + NSA_W:off + NSA_W + LANES]))


def _inproj1(x2d, norm_g, w_bf, qg, skg, wkg, tm, seq, prompt):
    rows, d = x2d.shape
    row = lambda i: (i, 0)
    fixed = lambda i: (0, 0)
    out_specs = [pl.BlockSpec((tm, NSA_W), row), pl.BlockSpec((tm, NSA_W), row), pl.BlockSpec((tm, LANES), row)]
    out_shape = [jax.ShapeDtypeStruct((rows, NSA_W), BF16), jax.ShapeDtypeStruct((rows, NSA_W), F32),
                 jax.ShapeDtypeStruct((rows, LANES), F32)]
    if prompt:
        per_seq = seq // tm
        kt = NSA_KEY_TILE
        out_specs += ([pl.BlockSpec((1, KV_W, tm), lambda i: (i // per_seq, 0, i % per_seq))] * 6
                      + [pl.BlockSpec((tm, KV_W), row)] * 4
                      + [pl.BlockSpec((tm // kt, LANES, kt), lambda i: (i, 0, 0))] * 2)
        out_shape += ([jax.ShapeDtypeStruct((rows // seq, KV_W, seq), F32)] * 6
                      + [jax.ShapeDtypeStruct((rows, KV_W), F32)] * 2
                      + [jax.ShapeDtypeStruct((rows, KV_W), BF16)] * 2
                      + [jax.ShapeDtypeStruct((rows // kt, LANES, kt), BF16)] * 2)
    else:
        out_specs += [pl.BlockSpec((tm, KV_W), row)] * 6
        out_shape += [jax.ShapeDtypeStruct((rows, KV_W), F32)] * 6
    return pl.pallas_call(
        functools.partial(_inproj1_kernel, prompt=prompt),
        grid=(rows // tm,),
        in_specs=[
            pl.BlockSpec((tm, d), row),
            pl.BlockSpec((1, d), fixed),
            pl.BlockSpec((d, IN1_COLS), fixed),
            pl.BlockSpec((1, LANES), fixed),
            pl.BlockSpec((1, LANES), fixed),
            pl.BlockSpec((1, LANES), fixed),
        ],
        out_specs=out_specs,
        out_shape=out_shape,
        compiler_params=_params(("parallel",)),
        name="inproj1",
    )(x2d, norm_g.reshape(1, d), w_bf, qg, skg, wkg)


def _pad_w_in1(w_in1):
    pad = IN1_COLS - w_in1.shape[1]
    return jnp.pad(w_in1, ((0, 0), (0, pad))).astype(BF16)


def _compress_rows(t_ref, pe_ref, w_ref, row0, nblk):
    acc = jnp.zeros((nblk, LANES), F32)
    for l in range(CMP_BLOCK):
        rows = t_ref[pl.ds(row0 + l, nblk, stride=CMP_BLOCK), :]
        acc = acc + _dot((rows + pe_ref[l:l + 1, :]).astype(BF16), w_ref[l])
    return acc


def _compress_prompt_kernel(kc_ref, vc_ref, pe_ref, wk_ref, wv_ref, kg_ref, ck_ref, cvt_ref):
    nblk = ck_ref.shape[0]
    lo = _lane_lo((1, LANES))
    ck = _pair_rmsnorm(_compress_rows(kc_ref, pe_ref, wk_ref, 0, nblk), kg_ref[...], lo)
    cv = _compress_rows(vc_ref, pe_ref, wv_ref, 0, nblk)
    ck_ref[...] = ck.astype(BF16)
    cvt_ref[0] = cv.T.astype(BF16)


def _blockdiag_w(w):
    z = jnp.zeros_like(w)
    top = jnp.concatenate([w, z], axis=2)
    bot = jnp.concatenate([z, w], axis=2)
    return jnp.concatenate([top, bot], axis=1).astype(BF16)


def _compress_prompt(kc, vc, pe128, wk_bd, wv_bd, kg, batch, seq):
    nblk = seq // CMP_BLOCK
    rowb = lambda b: (b, 0)
    fixed2 = lambda b: (0, 0)
    fixed3 = lambda b: (0, 0, 0)
    return pl.pallas_call(
        _compress_prompt_kernel,
        grid=(batch,),
        in_specs=[
            pl.BlockSpec((seq, LANES), rowb),
            pl.BlockSpec((seq, LANES), rowb),
            pl.BlockSpec((CMP_BLOCK, LANES), fixed2),
            pl.BlockSpec((CMP_BLOCK, LANES, LANES), fixed3),
            pl.BlockSpec((CMP_BLOCK, LANES, LANES), fixed3),
            pl.BlockSpec((1, LANES), fixed2),
        ],
        out_specs=[pl.BlockSpec((nblk, LANES), rowb), pl.BlockSpec((1, LANES, nblk), lambda b: (b, 0, 0))],
        out_shape=[jax.ShapeDtypeStruct((batch * nblk, LANES), BF16),
                   jax.ShapeDtypeStruct((batch, LANES, nblk), BF16)],
        compiler_params=_params(("parallel",)),
        name="compress_prompt",
    )(kc, vc, pe128, wk_bd, wv_bd, kg)


def _nsa_prompt_kernel(q_ref, ck_ref, cvt_ref, ks_ref, vst_ref, kw_ref, vwt_ref, bg_ref, gate_ref,
                       o_ref, m_scr, acc_scr, imp_scr, score_scr, chosen_scr, sc_scr):
    i = pl.program_id(1)
    tq = q_ref.shape[0]
    nb = ck_ref.shape[0]
    kt = NSA_KEY_TILE
    ratio = SEL_BLOCK // CMP_BLOCK
    nsb = nb // ratio
    q_pos = i * tq + lax.broadcasted_iota(I32, (1, tq), 1)

    zeros = jnp.zeros((HEAD_DIM, tq), F32)
    qts = []
    for p in range(H_NSA // 2):
        t = _transpose_tiles(q_ref[:, p * LANES:(p + 1) * LANES].astype(F32))
        for ht in (t[:HEAD_DIM], t[HEAD_DIM:]):
            parts = [zeros] * G_NSA
            parts[(2 * p) // HPG] = ht
            qts.append(jnp.concatenate(parts, axis=0).astype(BF16))
    group_of = [h // HPG for h in range(H_NSA)]

    ck = ck_ref[...]
    blk_n = lax.broadcasted_iota(I32, (nb, tq), 0)
    cmask = blk_n * CMP_BLOCK + (CMP_BLOCK - 1) <= q_pos
    cmp_scores = [_dot(ck, qt) for qt in qts]
    cmp_probs = []
    imps = [jnp.zeros((nb, tq), F32) for _ in range(G_NSA)]
    for h, s in enumerate(cmp_scores):
        s = jnp.where(cmask, s, NEG_INF)
        m = jnp.max(s, axis=0, keepdims=True)
        e = jnp.where(cmask, jnp.exp2(s - m), 0.0)
        den = jnp.sum(e, axis=0, keepdims=True)
        pr = e / jnp.where(den > 0.0, den, 1.0)
        imps[group_of[h]] = imps[group_of[h]] + pr
        cmp_probs.append(pr.astype(BF16))
    o_cmp = [_dot(cvt_ref[0, group_of[h] * HEAD_DIM:(group_of[h] + 1) * HEAD_DIM, :], pr)
             for h, pr in enumerate(cmp_probs)]

    sblk = lax.broadcasted_iota(I32, (nsb, tq), 0)
    cur = q_pos // SEL_BLOCK
    forced = (sblk == 0) | (sblk == cur) | (sblk == cur - 1)
    for g in range(G_NSA):
        pooled_parts = []
        for c in range(tq // LANES):
            part_ref = imp_scr.at[c]
            part_ref[...] = imps[g][:, c * LANES:(c + 1) * LANES]
            part = part_ref[pl.ds(0, nsb, stride=ratio), :]
            for r in range(1, ratio):
                part = part + part_ref[pl.ds(r, nsb, stride=ratio), :]
            pooled_parts.append(part)
        pooled = jnp.concatenate(pooled_parts, axis=1)
        score = jnp.where(sblk <= cur, pooled + FORCE_BONUS * forced.astype(F32), NEG_INF)
        score_scr[...] = score
        slab = 8
        cnts = []
        for r0 in range(0, nsb, slab):
            rows = score_scr[r0:min(r0 + slab, nsb), :]
            idx = r0 + lax.broadcasted_iota(I32, rows.shape, 0)
            cnt = jnp.zeros(rows.shape, F32)
            for mblk in range(nsb):
                c = score_scr[mblk:mblk + 1, :]
                if mblk < r0:
                    ahead = c >= rows
                elif mblk >= r0 + rows.shape[0]:
                    ahead = c > rows
                else:
                    ahead = (c > rows) | ((c == rows) & (mblk < idx))
                cnt = cnt + jnp.where(ahead, 1.0, 0.0)
            cnts.append(cnt)
        chosen = (jnp.concatenate(cnts, axis=0) < float(TOP_N)).astype(F32)
        for mblk in range(nsb):
            chosen_scr[g, mblk] = jnp.broadcast_to(chosen[mblk:mblk + 1, :], (8, tq))

    k_in = lax.broadcasted_iota(I32, (kt, tq), 0)
    blocks_per_tile = kt // SEL_BLOCK
    j_diag = ((i + 1) * tq - 1) // kt

    ones_rows = jnp.ones((acc_scr.shape[2] - HEAD_DIM, kt), BF16)

    heads_a = list(range(HPG))
    heads_b = list(range(HPG, H_NSA))

    def flash(k_ref, vt_ref, slot, n_tiles, mask_fn):
        m_scr[slot] = jnp.full(m_scr.shape[1:], NEG_INF, F32)
        acc_scr[slot] = jnp.zeros(acc_scr.shape[1:], F32)

        def load_k(j):
            return k_ref[pl.ds(pl.multiple_of(j * kt, kt), kt), :]

        def softmax_step(h, s, mask):
            s = jnp.where(mask, s, NEG_INF)
            m_old = m_scr[slot, h]
            m_new = jnp.maximum(m_old, jnp.max(s, axis=0, keepdims=True))
            m_scr[slot, h] = m_new
            return jnp.exp2(m_old - m_new)[0:1], jnp.exp2(s - m_new[0:1]).astype(BF16)

        def accumulate(h, alpha, pv):
            acc_scr[slot, h] = alpha * acc_scr[slot, h] + pv

        k_first = load_k(j_diag)
        for n, h in enumerate(heads_b):
            sc_scr[n] = _dot(k_first, qts[h])

        def body(t, carry):
            j = j_diag - t
            kj = load_k(j)
            k_next = load_k(jnp.maximum(j - 1, 0))
            vt = vt_ref[j]
            vts = [jnp.concatenate([vt[g * HEAD_DIM:(g + 1) * HEAD_DIM, :], ones_rows], axis=0)
                   for g in range(G_NSA)]
            scores_a = [_dot(kj, qts[h]) for h in heads_a]
            mask_b = mask_fn(1, j)
            soft_b = [softmax_step(h, sc_scr[n], mask_b) for n, h in enumerate(heads_b)]
            pv_b = [_dot(vts[1], p) for _, p in soft_b]
            next_b = [_dot(k_next, qts[h]) for h in heads_b]
            mask_a = mask_fn(0, j)
            soft_a = [softmax_step(h, scores_a[n], mask_a) for n, h in enumerate(heads_a)]
            pv_a = [_dot(vts[0], p) for _, p in soft_a]
            for n, h in enumerate(heads_b):
                accumulate(h, soft_b[n][0], pv_b[n])
                sc_scr[n] = next_b[n]
            for n, h in enumerate(heads_a):
                accumulate(h, soft_a[n][0], pv_a[n])
            return carry

        lax.fori_loop(0, n_tiles, body, 0)
        outs = []
        for h in range(H_NSA):
            acc = acc_scr[slot, h]
            outs.append(acc[:HEAD_DIM] / acc[HEAD_DIM:HEAD_DIM + 1])
        return outs

    def causal(j):
        return j * kt + k_in <= q_pos

    def sel_mask(g, j):
        rows = [jnp.broadcast_to(chosen_scr[g, blocks_per_tile * j + r][0:1], (SEL_BLOCK, tq))
                for r in range(blocks_per_tile)]
        return (jnp.concatenate(rows, axis=0) > 0.5) & causal(j)

    def win_mask(g, j):
        return causal(j) & (j * kt + k_in > q_pos - WINDOW)

    o_sel = flash(ks_ref, vst_ref, 0, j_diag + 1, sel_mask)
    j_low = jnp.maximum(i * tq - (WINDOW - 1), 0) // kt
    o_win = flash(kw_ref, vwt_ref, 1, j_diag - j_low + 1, win_mask)

    bgt = _transpose_tiles(bg_ref[...])

    def gate_row(h, br):
        c = h * 3 + br
        return bgt[c:c + 1, :]

    for p in range(H_NSA // 2):
        halves = []
        for h in (2 * p, 2 * p + 1):
            halves.append(gate_row(h, 0) * o_cmp[h] + gate_row(h, 1) * o_sel[h] + gate_row(h, 2) * o_win[h])
        mixed = _transpose_tiles(jnp.concatenate(halves, axis=0))
        o_ref[:, p * LANES:(p + 1) * LANES] = mixed * gate_ref[:, p * LANES:(p + 1) * LANES]


def _nsa_prompt(qn, ck, cvt, ksb, vst, kwb, vwt, bg, gate, batch, seq):
    tq = NSA_Q_TILE
    kt = NSA_KEY_TILE
    nq = seq // tq
    nb = seq // CMP_BLOCK
    nsb = nb // (SEL_BLOCK // CMP_BLOCK)
    qmap = lambda b, i: (b * nq + i, 0)
    per_b2 = lambda b, i: (b, 0)
    per_b3 = lambda b, i: (b, 0, 0)
    return pl.pallas_call(
        _nsa_prompt_kernel,
        grid=(batch, nq),
        in_specs=[
            pl.BlockSpec((tq, NSA_W), qmap),
            pl.BlockSpec((nb, LANES), per_b2),
            pl.BlockSpec((1, LANES, nb), per_b3),
            pl.BlockSpec((seq, LANES), per_b2),
            pl.BlockSpec((seq // kt, LANES, kt), per_b3),
            pl.BlockSpec((seq, LANES), per_b2),
            pl.BlockSpec((seq // kt, LANES, kt), per_b3),
            pl.BlockSpec((tq, LANES), qmap),
            pl.BlockSpec((tq, NSA_W), qmap),
        ],
        out_specs=pl.BlockSpec((tq, NSA_W), qmap),
        out_shape=jax.ShapeDtypeStruct((batch * seq, NSA_W), F32),
        scratch_shapes=[
            pltpu.VMEM((2, H_NSA, 8, tq), F32),
            pltpu.VMEM((2, H_NSA, HEAD_DIM + 16, tq), F32),
            pltpu.VMEM((tq // LANES, nb, LANES), F32),
            pltpu.VMEM((nsb, tq), F32),
            pltpu.VMEM((G_NSA, nsb, 8, tq), F32),
            pltpu.VMEM((HPG, kt, tq), F32),
        ],
        compiler_params=_params(("parallel", "arbitrary")),
        name="nsa_prompt",
    )(qn, ck, cvt, ksb, vst, kwb, vwt, bg, gate)


def _layer1_prompt(x2d, batch, seq, norm_g, w_in_bf, qn_g, cmp_kn_g, sel_kn_g, win_kn_g, pe128, wk_bd, wv_bd,
                   w_out_bf, tm):
    (qn, gate, bg, kc_t, vc_t, ks_t, vs_t, kw_t, vw_t, kc, vc, ksb, kwb, vst, vwt) = _inproj1(
        x2d, norm_g, w_in_bf, _pair_gain(qn_g), _pair_gain(sel_kn_g), _pair_gain(win_kn_g), tm, seq, True)
    ck, cvt = _compress_prompt(kc, vc, pe128, wk_bd, wv_bd, _pair_gain(cmp_kn_g), batch, seq)
    mix = _nsa_prompt(qn, ck, cvt, ksb, vst, kwb, vwt, bg, gate, batch, seq)
    y = _outproj(x2d, [mix], [w_out_bf], tm)
    return y, kc_t, vc_t, ks_t, vs_t, kw_t, vw_t


def _ret_sample_kernel(ret_ref, s_ref, qdec_ref, cdec_ref, gn_ref, mix_ref, so_ref):
    width = H_RET * HEAD_DIM
    rowi = lax.broadcasted_iota(I32, (8, HEAD_DIM), 0)
    eye = (lax.broadcasted_iota(I32, (HEAD_DIM, HEAD_DIM), 0)
           == lax.broadcasted_iota(I32, (HEAD_DIM, HEAD_DIM), 1)).astype(BF16)
    row = ret_ref[0]
    for h in range(H_RET):
        sl = slice(h * HEAD_DIM, (h + 1) * HEAD_DIM)
        qb = row[:, sl].astype(BF16)
        kb = row[:, width + h * HEAD_DIM:width + (h + 1) * HEAD_DIM].astype(BF16)
        vb = row[:, 2 * width + h * HEAD_DIM:2 * width + (h + 1) * HEAD_DIM].astype(BF16)
        gate = row[:, 3 * width + h * HEAD_DIM:3 * width + (h + 1) * HEAD_DIM]
        s = s_ref[0, h]
        qs = _dot(jnp.broadcast_to(qb.astype(F32), (8, HEAD_DIM)).astype(BF16), s.astype(BF16))[0:1]
        att = jnp.sum(qb.astype(F32) * kb.astype(F32), axis=-1, keepdims=True)
        o = att.astype(BF16).astype(F32) * vb.astype(F32) + qs * qdec_ref[h:h + 1, :]
        k8 = jnp.where(rowi == 0, jnp.broadcast_to(kb.astype(F32), (8, HEAD_DIM)), 0.0).astype(BF16)
        kcol = _dot_nt(eye, k8)[:, 0:1]
        so_ref[0, h] = s * cdec_ref[h:h + 1, :] + kcol * vb.astype(F32)
        mu = jnp.mean(o, axis=-1, keepdims=True)
        oc = o - mu
        var = jnp.mean(oc * oc, axis=-1, keepdims=True)
        on = oc * lax.rsqrt(var + EPS) * gn_ref[h:h + 1, :]
        mix_ref[0, :, sl] = on * _silu(gate)


def _ret_sample(ret, state, gn_g):
    nb, cols = ret.shape
    width = H_RET * HEAD_DIM
    h = jnp.arange(H_RET, dtype=F32)
    lg = jnp.log1p(-jnp.exp2(-5.0 - h))
    qdec = jnp.broadcast_to(jnp.exp(lg * 1.0)[:, None], (H_RET, HEAD_DIM))
    cdec = jnp.broadcast_to(jnp.exp(lg * 1.0)[:, None], (H_RET, HEAD_DIM))
    m3 = lambda b: (b, 0, 0)
    tab = pl.BlockSpec((H_RET, HEAD_DIM), lambda b: (0, 0))
    st = pl.BlockSpec((1, H_RET, HEAD_DIM, HEAD_DIM), lambda b: (b, 0, 0, 0))
    mix, state_new = pl.pallas_call(
        _ret_sample_kernel,
        grid=(nb,),
        in_specs=[pl.BlockSpec((1, 1, cols), m3), st, tab, tab, tab],
        out_specs=[pl.BlockSpec((1, 1, width), m3), st],
        out_shape=[jax.ShapeDtypeStruct((nb, 1, width), F32),
                   jax.ShapeDtypeStruct(state.shape, F32)],
        compiler_params=_params(("parallel",)),
        name="retention_sample",
    )(ret.reshape(nb, 1, cols), state, qdec, cdec, gn_g.astype(F32))
    return mix.reshape(nb, width), state_new


def _sb_sample_kernel(pt_ref, q_ref, g_ref, k_hbm, v_hbm, o_ref, kbuf, vbuf, sem):
    b = pl.program_id(0)
    n_pages = pt_ref.shape[1]
    page = PAGE_SIZE

    def copies(sample, t, slot):
        pid = pt_ref[sample, n_pages - 1 - t]
        return (pltpu.make_async_copy(k_hbm.at[pid], kbuf.at[slot], sem.at[0, slot]),
                pltpu.make_async_copy(v_hbm.at[pid], vbuf.at[slot], sem.at[1, slot]))

    def start(t, slot, sample=b):
        for c in copies(sample, t, slot):
            c.start()

    def wait(t, slot):
        for c in copies(b, t, slot):
            c.wait()

    q = q_ref[0].astype(F32)
    rowq = lax.broadcasted_iota(I32, (H_SB, HEAD_DIM), 0)
    rowa = lax.broadcasted_iota(I32, (H_SB, page), 0)
    upper = (lax.broadcasted_iota(I32, (page, page), 0) >= lax.broadcasted_iota(I32, (page, page), 1)).astype(BF16)
    q_rows = [jnp.where(rowq == h, q, 0.0).astype(BF16) for h in range(H_SB)]

    @pl.when(b == 0)
    def _():
        start(0, 0)

    def cond(carry):
        t, alive = carry[0], carry[1]
        return (t < n_pages) & alive

    def body(carry):
        t, _, r, acc = carry
        slot = t % 2
        wait(t, slot)

        @pl.when(t + 1 < n_pages)
        def _():
            start(t + 1, 1 - slot)

        z = jnp.zeros((H_SB, page), F32)
        for h in range(H_SB):
            z = z + _dot(q_rows[h], kbuf[slot, h].astype(BF16))
        ls_pos = _log2_sigmoid(z)
        lm = ls_pos - z
        hi, lo_part = _split_bf16(lm)
        s_incl = _dot(hi, upper) + _dot(lo_part, upper)
        a = jnp.exp2(ls_pos + (s_incl - lm) + r)
        for h in range(H_SB):
            acc = acc + _dot_nt(jnp.where(rowa == h, a, 0.0).astype(BF16), vbuf[slot, h].astype(BF16))
        r = r + s_incl[:, 0:1]
        return t + 1, jnp.max(r) > SB_LOG_ZERO, r, acc

    init = (jnp.int32(0), jnp.bool_(True), jnp.zeros((H_SB, 1), F32), jnp.zeros((H_SB, HEAD_DIM), F32))
    t_end, _, _, acc = lax.while_loop(cond, body, init)

    @pl.when(t_end < n_pages)
    def _():
        wait(t_end, t_end % 2)

    @pl.when(b + 1 < pl.num_programs(0))
    def _():
        start(0, 0, sample=b + 1)

    o_ref[0] = acc * g_ref[0]


def _sb_sample(q8, g8, cache_kt, cache_vt, page_table):
    nb = q8.shape[0]
    head = lambda b, pt: (b, 0, 0)
    grid_spec = pltpu.PrefetchScalarGridSpec(
        num_scalar_prefetch=1,
        grid=(nb,),
        in_specs=[pl.BlockSpec((1, H_SB, HEAD_DIM), head), pl.BlockSpec((1, H_SB, HEAD_DIM), head),
                  pl.BlockSpec(memory_space=pl.ANY), pl.BlockSpec(memory_space=pl.ANY)],
        out_specs=pl.BlockSpec((1, H_SB, HEAD_DIM), head),
        scratch_shapes=[pltpu.VMEM((2, H_SB, HEAD_DIM, PAGE_SIZE), F32),
                        pltpu.VMEM((2, H_SB, HEAD_DIM, PAGE_SIZE), F32),
                        pltpu.SemaphoreType.DMA((2, 2))],
    )
    return pl.pallas_call(
        _sb_sample_kernel,
        grid_spec=grid_spec,
        out_shape=jax.ShapeDtypeStruct((nb, H_SB, HEAD_DIM), F32),
        compiler_params=_params(("arbitrary",)),
        name="sb_sample",
    )(page_table, q8, g8, cache_kt, cache_vt)


CMP_PAGES_PER_STEP = 16


def _regroup_matrix():
    r = np.zeros((2 * PAGE_SIZE, 2 * PAGE_SIZE), np.float32)
    per_page = PAGE_SIZE // CMP_BLOCK
    for l in range(CMP_BLOCK):
        for p in range(2):
            for n in range(per_page):
                r[l * 2 * per_page + p * per_page + n, p * PAGE_SIZE + CMP_BLOCK * n + l] = 1.0
    return jnp.asarray(r, BF16)


def _compress_sample_kernel(pt_ref, *refs, n_pp):
    k_refs = refs[:n_pp]
    v_refs = refs[n_pp:2 * n_pp]
    pe_ref, wk_ref, wv_ref, kg_ref, rg_ref, ck_ref, cv_ref, kbuf, vbuf = refs[2 * n_pp:]
    j = pl.program_id(1)
    per_pair = 2 * PAGE_SIZE // CMP_BLOCK
    regroup = rg_ref[...]
    pe_t = pe_ref[...]
    for src_refs, buf in ((k_refs, kbuf), (v_refs, vbuf)):
        for a in range(n_pp // 2):
            both = jnp.concatenate([src_refs[2 * a][0], src_refs[2 * a + 1][0]], axis=1) + pe_t
            rows = _dot_nt(regroup, both.astype(BF16))
            for l in range(CMP_BLOCK):
                buf[l, a * per_pair:(a + 1) * per_pair, :] = rows[l * per_pair:(l + 1) * per_pair]
    nblk = n_pp * PAGE_SIZE // CMP_BLOCK

    def compress(buf, w_ref):
        acc = jnp.zeros((nblk, LANES), F32)
        for l in range(CMP_BLOCK):
            acc = acc + _dot(buf[l].astype(BF16), w_ref[l])
        return acc

    lo = _lane_lo((1, LANES))
    start = pl.multiple_of(j * nblk, nblk)
    ck_ref[0, pl.ds(start, nblk), :] = _pair_rmsnorm(compress(kbuf, wk_ref), kg_ref[...], lo)
    cv_ref[0, pl.ds(start, nblk), :] = compress(vbuf, wv_ref)


def _compress_sample(cache_k, cache_v, page_table, pe128, wk_bd, wv_bd, kg):
    nb, n_pages = page_table.shape
    n_pp = CMP_PAGES_PER_STEP
    nblk_total = n_pages * PAGE_SIZE // CMP_BLOCK
    pe_t = jnp.tile(pe128.T, (1, 2 * PAGE_SIZE // CMP_BLOCK))

    def page_map(r):
        return lambda b, j, pt: (pt[b, j * n_pp + r], 0, 0)

    fixed2 = lambda b, j, pt: (0, 0)
    fixed3 = lambda b, j, pt: (0, 0, 0)
    out_map = lambda b, j, pt: (b, 0, 0)
    grid_spec = pltpu.PrefetchScalarGridSpec(
        num_scalar_prefetch=1,
        grid=(nb, n_pages // n_pp),
        in_specs=[pl.BlockSpec((1, PAGE_SIZE, LANES), page_map(r)) for r in range(n_pp)] * 2
        + [pl.BlockSpec((LANES, 2 * PAGE_SIZE), fixed2),
           pl.BlockSpec((CMP_BLOCK, LANES, LANES), fixed3),
           pl.BlockSpec((CMP_BLOCK, LANES, LANES), fixed3),
           pl.BlockSpec((1, LANES), fixed2),
           pl.BlockSpec((2 * PAGE_SIZE, 2 * PAGE_SIZE), fixed2)],
        out_specs=[pl.BlockSpec((1, nblk_total, LANES), out_map)] * 2,
        scratch_shapes=[pltpu.VMEM((CMP_BLOCK, n_pp * PAGE_SIZE // CMP_BLOCK, LANES), F32)] * 2,
    )
    o = jax.ShapeDtypeStruct((nb, nblk_total, LANES), F32)
    return pl.pallas_call(
        functools.partial(_compress_sample_kernel, n_pp=n_pp),
        grid_spec=grid_spec,
        out_shape=[o, o],
        compiler_params=_params(("parallel", "arbitrary")),
        name="compress_sample",
    )(page_table, *([cache_k] * n_pp), *([cache_v] * n_pp), pe_t, wk_bd, wv_bd, kg, _regroup_matrix())


def _pages_t(cache):
    return cache.transpose(0, 2, 3, 1)


def _group_q(q, g):
    z = jnp.zeros_like(q)
    return jnp.concatenate([q, z], axis=1) if g == 0 else jnp.concatenate([z, q], axis=1)


def _nsa_sample_cmp_kernel(q_ref, ck_ref, cv_ref, ocmp_ref, score_ref, *, q_pos):
    q = q_ref[0]
    ck = ck_ref[0].astype(BF16)
    cv = cv_ref[0].astype(BF16)
    nb = ck.shape[0]
    blk_n = lax.broadcasted_iota(I32, (1, nb), 1)
    cmask = blk_n * CMP_BLOCK + (CMP_BLOCK - 1) <= q_pos
    sblk = blk_n // (SEL_BLOCK // CMP_BLOCK)
    cur = q_pos // SEL_BLOCK
    row8 = lax.broadcasted_iota(I32, (8, nb), 0)
    neg_inf = float("-inf")
    scores = jnp.full((8, nb), neg_inf, F32)
    for g in range(G_NSA):
        qg = _group_q(q[g * HPG:(g + 1) * HPG], g)
        s = jnp.where(cmask, _dot_nt(qg, ck), NEG_INF)
        m = jnp.max(s, axis=-1, keepdims=True)
        e = jnp.where(cmask, jnp.exp2(s - m), 0.0)
        den = jnp.sum(e, axis=-1, keepdims=True)
        pr = e / jnp.where(den > 0.0, den, 1.0)
        o = _dot(pr.astype(BF16), cv)
        ocmp_ref[0, g * HPG:(g + 1) * HPG, :] = o[:, g * HEAD_DIM:(g + 1) * HEAD_DIM]
        imp = jnp.sum(pr, axis=0, keepdims=True)
        even = (blk_n & 1) == 0
        pooled = imp + jnp.where(even, pltpu.roll(imp, nb - 1, 1), pltpu.roll(imp, 1, 1))
        forced = (sblk == 0) | (sblk == cur) | (sblk == cur - 1)
        score = jnp.where(sblk < cur, pooled + FORCE_BONUS * forced.astype(F32), neg_inf)
        scores = jnp.where(row8 == g, jnp.broadcast_to(score, (8, nb)), scores)
    score_ref[0] = scores


def _nsa_sample_topk_kernel(score_ref, idx_ref):
    score = score_ref[...]
    rows, nb = score.shape
    ratio = SEL_BLOCK // CMP_BLOCK
    lane = lax.broadcasted_iota(I32, (rows, nb), 1)
    lane_f = lane.astype(F32)
    sblk = lane // ratio
    slot = lax.broadcasted_iota(I32, (rows, LANES), 1)
    neg_inf = float("-inf")
    idx = jnp.full((rows, LANES), -1, I32)
    for t in range(TOP_N - 1):
        mx = jnp.max(score, axis=-1, keepdims=True)
        first = jnp.min(jnp.where(score == mx, lane_f, float(4 * nb)), axis=-1, keepdims=True)
        blk = first.astype(I32) // ratio
        idx = jnp.where((slot == t) & (mx > neg_inf), blk, idx)
        score = jnp.where(sblk == blk, neg_inf, score)
    idx_ref[...] = idx


def _nsa_sample_select(q16, ck, cv, q_pos):
    nb = q16.shape[0]
    nblk = ck.shape[1]
    m3 = lambda b: (b, 0, 0)
    ocmp, scores = pl.pallas_call(
        functools.partial(_nsa_sample_cmp_kernel, q_pos=q_pos),
        grid=(nb,),
        in_specs=[pl.BlockSpec((1, H_NSA, HEAD_DIM), m3),
                  pl.BlockSpec((1, nblk, LANES), m3),
                  pl.BlockSpec((1, nblk, LANES), m3)],
        out_specs=[pl.BlockSpec((1, H_NSA, HEAD_DIM), m3), pl.BlockSpec((1, 8, nblk), m3)],
        out_shape=[jax.ShapeDtypeStruct((nb, H_NSA, HEAD_DIM), F32),
                   jax.ShapeDtypeStruct((nb, 8, nblk), F32)],
        compiler_params=_params(("parallel",)),
        name="nsa_sample_cmp",
    )(q16, ck, cv)
    rows = scores[:, :G_NSA].reshape(nb * G_NSA, nblk)
    idx = pl.pallas_call(
        _nsa_sample_topk_kernel,
        out_shape=jax.ShapeDtypeStruct((nb * G_NSA, LANES), I32),
        compiler_params=pltpu.CompilerParams(vmem_limit_bytes=VMEM_LIMIT),
        name="nsa_sample_topk",
    )(rows)
    return ocmp, idx


def _nsa_sample_attend_kernel(pt_ref, idx_ref, q_ref, k_hbm, v_hbm, kn_ref, vn_ref, kwn_ref, vwn_ref,
                              wk_ref, wv_ref, ocmp_ref, bg_ref, gate_ref, o_ref, kbuf, vbuf, sem,
                              *, q_pos):
    b = pl.program_id(0)
    n_slots = kbuf.shape[2]
    n_pages = pt_ref.shape[1]
    buf = b % 2

    def copies(sample, which, g, s):
        blk = jnp.maximum(idx_ref[sample * G_NSA + g, s], 0)
        pid = pt_ref[sample, jnp.minimum(blk // 2, n_pages - 1)]
        rows = pl.ds(g * HEAD_DIM, HEAD_DIM)
        return (pltpu.make_async_copy(k_hbm.at[pid, rows], kbuf.at[which, g, s], sem.at[which, 0, g]),
                pltpu.make_async_copy(v_hbm.at[pid, rows], vbuf.at[which, g, s], sem.at[which, 1, g]))

    def start_all(sample, which):
        for g in range(G_NSA):
            for s in range(n_slots):
                for c in copies(sample, which, g, s):
                    c.start()

    @pl.when(b == 0)
    def _():
        start_all(0, 0)

    @pl.when(b + 1 < pl.num_programs(0))
    def _():
        start_all(b + 1, 1 - buf)

    q = q_ref[0]
    bg = bg_ref[0]
    col = lax.broadcasted_iota(I32, (HPG, LANES), 1)
    hrow = lax.broadcasted_iota(I32, (HPG, LANES), 0)
    key_half = lax.broadcasted_iota(I32, (1, PAGE_SIZE), 1) // SEL_BLOCK
    wlen = wk_ref.shape[3]
    w_pos = q_pos - wlen + lax.broadcasted_iota(I32, (1, wlen), 1)
    wmask = (w_pos > q_pos - WINDOW) & (w_pos >= 0)

    def new_token(x_ref, g):
        return x_ref[0][:, g * HEAD_DIM:(g + 1) * HEAD_DIM].astype(BF16).astype(F32)

    for g in range(G_NSA):
        qg = q[g * HPG:(g + 1) * HPG]
        qf = qg.astype(F32)
        sw = jnp.where(wmask, _dot(qg, wk_ref[0, g].astype(BF16)), NEG_INF)
        sw_new = jnp.sum(qf * new_token(kwn_ref, g), axis=-1, keepdims=True)
        mw = jnp.maximum(jnp.max(sw, axis=-1, keepdims=True), sw_new)
        pw = jnp.where(wmask, jnp.exp2(sw - mw), 0.0)
        pw_new = jnp.exp2(sw_new - mw)
        o_win = _dot_nt(pw.astype(BF16), wv_ref[0, g].astype(BF16))
        o_win = o_win + pw_new.astype(BF16).astype(F32) * new_token(vwn_ref, g)
        o_win = o_win / (jnp.sum(pw, axis=-1, keepdims=True) + pw_new)
        for s in range(n_slots):
            for c in copies(b, buf, g, s):
                c.wait()
        scores, masks = [], []
        for s in range(n_slots):
            blk = idx_ref[b * G_NSA + g, s]
            ok = (key_half == jnp.maximum(blk, 0) % 2) & (blk >= 0)
            scores.append(jnp.where(ok, _dot(qg, kbuf[buf, g, s].astype(BF16)), NEG_INF))
            masks.append(ok)
        s_new = jnp.sum(qf * new_token(kn_ref, g), axis=-1, keepdims=True)
        m = s_new
        for sc in scores:
            m = jnp.maximum(m, jnp.max(sc, axis=-1, keepdims=True))
        p_new = jnp.exp2(s_new - m)
        den = p_new
        acc = p_new.astype(BF16).astype(F32) * new_token(vn_ref, g)
        for s in range(n_slots):
            p = jnp.where(masks[s], jnp.exp2(scores[s] - m), 0.0)
            den = den + jnp.sum(p, axis=-1, keepdims=True)
            acc = acc + _dot_nt(p.astype(BF16), vbuf[buf, g, s].astype(BF16))
        o_sel = acc / den
        gates = []
        for br in range(3):
            pick = col == g * HPG * 3 + hrow * 3 + br
            gates.append(jnp.sum(jnp.where(pick, jnp.broadcast_to(bg, (HPG, LANES)), 0.0), axis=-1, keepdims=True))
        rows = slice(g * HPG, (g + 1) * HPG)
        mixed = gates[0] * ocmp_ref[0, rows, :] + gates[1] * o_sel + gates[2] * o_win
        o_ref[0, rows, :] = mixed * gate_ref[0, rows, :]


def _nsa_sample_attend(page_table, idx, q16, sel_kt, sel_vt, ks_new, vs_new, kw_new, vw_new, win_kt, win_vt, ocmp,
                       bg, gate16, q_pos):
    nb = q16.shape[0]
    n_slots = TOP_N - 1
    wlen = win_kt.shape[3]
    m3 = lambda b, pt, ix: (b, 0, 0)
    m4 = lambda b, pt, ix: (b, 0, 0, 0)
    row = pl.BlockSpec((1, 1, LANES), m3)
    heads = pl.BlockSpec((1, H_NSA, HEAD_DIM), m3)
    win = pl.BlockSpec((1, G_NSA, HEAD_DIM, wlen), m4)
    grid_spec = pltpu.PrefetchScalarGridSpec(
        num_scalar_prefetch=2,
        grid=(nb,),
        in_specs=[heads, pl.BlockSpec(memory_space=pl.ANY), pl.BlockSpec(memory_space=pl.ANY),
                  row, row, row, row, win, win, heads, row, heads],
        out_specs=heads,
        scratch_shapes=[pltpu.VMEM((2, G_NSA, n_slots, HEAD_DIM, PAGE_SIZE), F32),
                        pltpu.VMEM((2, G_NSA, n_slots, HEAD_DIM, PAGE_SIZE), F32),
                        pltpu.SemaphoreType.DMA((2, 2, G_NSA))],
    )
    return pl.pallas_call(
        functools.partial(_nsa_sample_attend_kernel, q_pos=q_pos),
        grid_spec=grid_spec,
        out_shape=jax.ShapeDtypeStruct((nb, H_NSA, HEAD_DIM), F32),
        compiler_params=_params(("arbitrary",)),
        name="nsa_sample_attend",
    )(page_table, idx, q16, sel_kt, sel_vt, ks_new, vs_new, kw_new, vw_new, win_kt, win_vt, ocmp, bg, gate16)


def _layer0_sample(x2d, past_len, state, cache_k, cache_v, page_table, norm_g, w_in_bf, ret_gn_g, sb_qn_g, sb_kn_g,
                   w_out_bf):
    nb = x2d.shape[0]
    cos, sin = _rope_tables(jnp.full((nb,), past_len, I32))
    ret, sbq, sbg, sbk, sbv = _inproj0(x2d, norm_g, w_in_bf, cos, sin, _pair_gain(sb_qn_g), _pair_gain(sb_kn_g),
                                       nb, nb, False)
    mix_ret, state_new = _ret_sample(ret, state, ret_gn_g)
    mix_sb = _sb_sample(sbq.reshape(nb, H_SB, HEAD_DIM), sbg.reshape(nb, H_SB, HEAD_DIM),
                        _pages_t(cache_k), _pages_t(cache_v), page_table)
    mix_sb = mix_sb.reshape(nb, H_SB * HEAD_DIM)
    half = w_out_bf.shape[0] // 2
    y = _outproj(x2d, [mix_ret, mix_sb], [w_out_bf[:half], w_out_bf[half:]], nb)
    return y, state_new, sbk, sbv


def _layer1_sample(x2d, past_len, cmp_k, cmp_v, sel_k, sel_v, win_k, win_v, page_table, norm_g, w_in_bf, qn_g,
                   cmp_kn_g, sel_kn_g, win_kn_g, pe128, wk_bd, wv_bd, w_out_bf):
    nb = x2d.shape[0]
    (qn, gate, bg, kc, vc, ks, vs, kw, vw) = _inproj1(
        x2d, norm_g, w_in_bf, _pair_gain(qn_g), _pair_gain(sel_kn_g), _pair_gain(win_kn_g), nb, nb, False)
    n_phys = cmp_k.shape[0]
    pool = lambda t: _pages_t(t).reshape(n_phys, KV_W, PAGE_SIZE)
    ck, cv = _compress_sample(pool(cmp_k), pool(cmp_v), page_table, pe128, wk_bd, wv_bd, _pair_gain(cmp_kn_g))
    q16 = qn.reshape(nb, H_NSA, HEAD_DIM)
    ocmp, idx = _nsa_sample_select(q16, ck, cv, past_len)
    idx2 = idx[:, :TOP_N]
    mix = _nsa_sample_attend(page_table, idx2, q16, pool(sel_k), pool(sel_v), ks[:, None, :], vs[:, None, :],
                             kw[:, None, :], vw[:, None, :], _pages_t(win_k), _pages_t(win_v), ocmp,
                             bg[:, None, :], gate.reshape(nb, H_NSA, HEAD_DIM), past_len)
    y = _outproj(x2d, [mix.reshape(nb, NSA_W)], [w_out_bf], nb)
    gd = (nb, 1, G_NSA, HEAD_DIM)
    wk_new = jnp.concatenate([win_k[:, 1:], kw.reshape(gd)], axis=1)
    wv_new = jnp.concatenate([win_v[:, 1:], vw.reshape(gd)], axis=1)
    return y, kc, vc, ks, vs, wk_new, wv_new


def kernel(x_prompt, x_sample, state_ret, cache_sb_k, cache_sb_v, cache_cmp_k, cache_cmp_v, cache_sel_k,
           cache_sel_v, cache_win_k, cache_win_v, page_table, norm0_g, w_in0, ret_gn_g, sb_qn_g, sb_kn_g,
           w_out0, norm1_g, w_in1, nsa_qn_g, cmp_kn_g, sel_kn_g, win_kn_g, cmp_pe, w_cmp_k, w_cmp_v, w_out1):
    batch, seq, d = x_prompt.shape
    nb = x_sample.shape[0]
    past_len = page_table.shape[1] * PAGE_SIZE
    tm = 512
    w_in0_bf = w_in0.astype(BF16)
    w_out0_bf = w_out0.astype(BF16)
    w_in1_bf = _pad_w_in1(w_in1)
    w_out1_bf = w_out1.astype(BF16)
    pe128 = jnp.tile(cmp_pe, (1, G_NSA))
    wk_bd = _blockdiag_w(w_cmp_k)
    wv_bd = _blockdiag_w(w_cmp_v)

    xp = x_prompt.reshape(batch * seq, d)
    y1p, ret_p, sbk_p, sbv_p = _layer0_prompt(xp, batch, seq, norm0_g, w_in0_bf, ret_gn_g, sb_qn_g, sb_kn_g,
                                              w_out0_bf, tm)
    y2p, ckp, cvp, skp, svp, wkp, wvp = _layer1_prompt(y1p, batch, seq, norm1_g, w_in1_bf, nsa_qn_g, cmp_kn_g,
                                                        sel_kn_g, win_kn_g, pe128, wk_bd, wv_bd, w_out1_bf, tm)
    xs = x_sample.reshape(nb, d)
    y1s, ret_s, sbk_s, sbv_s = _layer0_sample(xs, past_len, state_ret, cache_sb_k, cache_sb_v, page_table, norm0_g,
                                              w_in0_bf, ret_gn_g, sb_qn_g, sb_kn_g, w_out0_bf)
    y2s, cks, cvs, sks, svs, wks, wvs = _layer1_sample(y1s, past_len, cache_cmp_k, cache_cmp_v, cache_sel_k,
                                                        cache_sel_v, cache_win_k, cache_win_v, page_table, norm1_g,
                                                        w_in1_bf, nsa_qn_g, cmp_kn_g, sel_kn_g, win_kn_g, pe128,
                                                        wk_bd, wv_bd, w_out1_bf)
    keep = min(WINDOW, seq)
    gd = (nb, 1, G_NSA, HEAD_DIM)

    def rows_major(t):
        return t.transpose(0, 3, 1, 2)

    def groups(t):
        return rows_major(t.reshape(batch, G_NSA, HEAD_DIM, seq))

    return (y2p.reshape(batch, seq, d), y2s.reshape(nb, 1, d), ret_p, ret_s,
            rows_major(sbk_p), rows_major(sbv_p),
            sbk_s.reshape(nb, 1, H_SB, HEAD_DIM), sbv_s.reshape(nb, 1, H_SB, HEAD_DIM),
            groups(ckp), groups(cvp), groups(skp), groups(svp),
            groups(wkp)[:, seq - keep:], groups(wvp)[:, seq - keep:],
            cks.reshape(gd), cvs.reshape(gd), sks.reshape(gd), svs.reshape(gd), wks, wvs)
```
